```python
import math
import jax, jax.numpy as jnp
from jax import lax
import numpy as np

D_MODEL = 2048
BATCH = 4
SEQ = 4096
DEPTH = 1

HEAD_DIM_A = D_MODEL // 16
ATTN_GROUPS = ((128, 1), (512, 4), (2048, 16))
N_GROUPS = len(ATTN_GROUPS)
HEADS_PER_GROUP = 4
N_HEADS_A = N_GROUPS * HEADS_PER_GROUP
WIDTH_A = N_HEADS_A * HEAD_DIM_A
OUT_WIDTH_A = HEADS_PER_GROUP * HEAD_DIM_A
BLK = 64
N_BUCKETS = 32
MAX_DISTANCE = 1024
LRU_WIDTH = 3 * D_MODEL // 4
LRU_BLOCKS = 12
LRU_BW = LRU_WIDTH // LRU_BLOCKS
CONV_WIDTH = 4
LRU_C = 8.0
N_MEM = 256
MEM_HEADS = 4
MEM_HEAD_DIM = D_MODEL // 8
MEM_WIDTH = MEM_HEADS * MEM_HEAD_DIM
D_FF = 4 * D_MODEL
N_BRANCH = 3
EPS = 1e-6

N_IN = 3 * WIDTH_A + 2 * LRU_WIDTH + MEM_WIDTH
IN_SPLITS = (WIDTH_A, 2 * WIDTH_A, 3 * WIDTH_A, 3 * WIDTH_A + LRU_WIDTH, 3 * WIDTH_A + 2 * LRU_WIDTH)

kernel_name = "hybrid_dilated_rglru_memxattn_block"


def rms_norm(t, gain):
    tf = t.astype(jnp.float32)
    y = tf * lax.rsqrt(jnp.mean(tf * tf, axis=-1, keepdims=True) + EPS) * gain.astype(jnp.float32)
    return y.astype(t.dtype)


def t5_bucket(rel):
    nb = N_BUCKETS // 2
    max_exact = nb // 2
    sign = (rel > 0).astype(np.int32) * nb
    n = np.abs(rel)
    large = max_exact + (np.log(np.maximum(n, 1) / max_exact)
                         / np.log(MAX_DISTANCE / max_exact) * (nb - max_exact)).astype(np.int32)
    large = np.minimum(large, nb - 1)
    return (sign + np.where(n < max_exact, n, large)).astype(np.int32)


def dilated_window_attention(q, k, v, bias_table, window, dilation):
    B, S, H, C = q.shape
    d = dilation
    L = S // d
    radius = window // (2 * d)
    nblk = -(-L // BLK)
    Lp = nblk * BLK
    pad = Lp - L

    def to_strided(t):
        return t.reshape(B, L, d, H, C).transpose(0, 2, 1, 3, 4)

    qs, ks, vs = to_strided(q), to_strided(k), to_strided(v)
    qb = jnp.pad(qs, ((0, 0), (0, 0), (0, pad), (0, 0), (0, 0))).reshape(B, d, nblk, BLK, H, C)

    def windows(t):
        tp = jnp.pad(t, ((0, 0), (0, 0), (BLK, pad + BLK), (0, 0), (0, 0)))
        tb = tp.reshape(B, d, nblk + 2, BLK, H, C)
        return jnp.concatenate([tb[:, :, :-2], tb[:, :, 1:-1], tb[:, :, 2:]], axis=3)

    kw, vw = windows(ks), windows(vs)

    qq = np.arange(BLK)[:, None]
    kk = np.arange(3 * BLK)[None, :]
    rel = kk - BLK - qq
    band = np.abs(rel) <= radius
    key_pos = (np.arange(nblk)[:, None, None] - 1) * BLK + kk[None]
    valid = band[None] & (key_pos >= 0) & (key_pos < L)
    bias = bias_table.astype(jnp.float32)[t5_bucket(rel * d)]
    bias = jnp.transpose(bias, (2, 0, 1))

    scale = 1.0 / math.sqrt(C)
    logits = jnp.einsum('brnqhc,brnkhc->brnhqk', qb, kw).astype(jnp.float32) * scale
    logits = logits + bias[None, None, None]
    logits = jnp.where(valid[None, None, :, None], logits, -1e30)
    m = jnp.max(logits, axis=-1, keepdims=True)
    p = jnp.exp(logits - m)
    s = jnp.sum(p, axis=-1, keepdims=True)
    o = jnp.einsum('brnhqk,brnkhc->brnqhc', p.astype(vw.dtype), vw).astype(jnp.float32)
    o = o / jnp.swapaxes(s, 3, 4)
    lse = jnp.swapaxes((m + jnp.log(s))[..., 0], 3, 4)

    o = o.reshape(B, d, Lp, H, C)[:, :, :L].transpose(0, 2, 1, 3, 4).reshape(B, S, H, C)
    lse = lse.reshape(B, d, Lp, H)[:, :, :L].transpose(0, 2, 1, 3).reshape(B, S, H)
    return o, lse


def _lin_combine(left, right):
    a1, b1 = left
    a2, b2 = right
    return a1 * a2, a2 * b1 + b2


def rg_lru_forward(xc, wa, ba, wi, bi, lam):
    B, S, W = xc.shape
    xb = xc.reshape(B, S, LRU_BLOCKS, LRU_BW)
    r = jax.nn.sigmoid(jnp.einsum('bsnc,ncd->bsnd', xb, wa.astype(jnp.float32)) + ba.astype(jnp.float32)).reshape(B, S, W)
    i = jax.nn.sigmoid(jnp.einsum('bsnc,ncd->bsnd', xb, wi.astype(jnp.float32)) + bi.astype(jnp.float32)).reshape(B, S, W)
    log_a = -LRU_C * jax.nn.softplus(-lam.astype(jnp.float32)) * r
    a = jnp.exp(log_a)
    is_start = (jnp.arange(S) == 0)[None, :, None]
    mult = jnp.where(is_start, 1.0, jnp.sqrt(-jnp.expm1(2.0 * log_a)))
    b = mult * (i * xc)
    _, h = lax.associative_scan(_lin_combine, (a, b), axis=1)
    return h


def setup_inputs(seed: int = 0) -> dict:
    key = jax.random.key(seed)
    ks = jax.random.split(key, 32)
    f32 = jnp.float32

    def dense(k, shape, fan_in):
        return jax.random.normal(k, shape, f32) * (fan_in ** -0.5)

    def gain(k, shape):
        return 1.0 + 0.02 * jax.random.normal(k, shape, f32)

    def small(k, shape):
        return 0.01 * jax.random.normal(k, shape, f32)

    u = jax.random.uniform(ks[14], (DEPTH, 2, LRU_WIDTH), f32, 0.9, 0.999)
    a_base = u ** (1.0 / LRU_C)
    lru_lambda = jnp.log(a_base) - jnp.log1p(-a_base)

    return {
        "x": jax.random.normal(ks[0], (BATCH, SEQ, D_MODEL), f32),
        "mem": jax.random.normal(ks[1], (BATCH, N_MEM, D_MODEL), f32),
        "rel_bias": 0.1 * jax.random.normal(ks[2], (N_BUCKETS, N_HEADS_A), f32),
        "norm_mix": gain(ks[3], (DEPTH, D_MODEL)),
        "norm_mem": gain(ks[4], (DEPTH, D_MODEL)),
        "norm_mlp": gain(ks[5], (DEPTH, D_MODEL)),
        "norm_final": gain(ks[6], (D_MODEL,)),
        "w_in": dense(ks[7], (DEPTH, D_MODEL, N_IN), D_MODEL),
        "w_gate": dense(ks[8], (DEPTH, D_MODEL, N_BRANCH * D_MODEL), D_MODEL),
        "b_gate": small(ks[9], (DEPTH, N_BRANCH * D_MODEL)),
        "conv_w": dense(ks[10], (DEPTH, CONV_WIDTH, LRU_WIDTH), CONV_WIDTH),
        "conv_b": small(ks[11], (DEPTH, LRU_WIDTH)),
        "lru_wa": dense(ks[12], (DEPTH, 2, LRU_BLOCKS, LRU_BW, LRU_BW), LRU_BW),
        "lru_ba": small(ks[13], (DEPTH, 2, LRU_BLOCKS, LRU_BW)),
        "lru_wi": dense(ks[15], (DEPTH, 2, LRU_BLOCKS, LRU_BW, LRU_BW), LRU_BW),
        "lru_bi": small(ks[16], (DEPTH, 2, LRU_BLOCKS, LRU_BW)),
        "lru_lambda": lru_lambda,
        "w_mem_kv": dense(ks[17], (DEPTH, D_MODEL, 2 * MEM_WIDTH), D_MODEL),
        "w_o_attn": dense(ks[18], (DEPTH, OUT_WIDTH_A, D_MODEL), OUT_WIDTH_A),
        "w_o_lru": dense(ks[19], (DEPTH, LRU_WIDTH, D_MODEL), LRU_WIDTH),
        "w_o_mem": dense(ks[20], (DEPTH, MEM_WIDTH, D_MODEL), MEM_WIDTH),
        "w_out": dense(ks[21], (DEPTH, D_MODEL, D_MODEL), D_MODEL),
        "w_up": dense(ks[22], (DEPTH, D_MODEL, D_FF), D_MODEL),
        "w_down": dense(ks[23], (DEPTH, D_FF, D_MODEL), D_FF),
    }


def reference(x, mem, rel_bias, norm_mix, norm_mem, norm_mlp, norm_final, w_in, w_gate, b_gate,
              conv_w, conv_b, lru_wa, lru_ba, lru_wi, lru_bi, lru_lambda, w_mem_kv,
              w_o_attn, w_o_lru, w_o_mem, w_out, w_up, w_down):
    B, S, _ = x.shape
    for l in range(DEPTH):
        h = rms_norm(x, norm_mix[l])
        proj = h @ w_in[l]
        q_a, k_a, v_a, x_b, y_b, q_c = jnp.split(proj, IN_SPLITS, axis=-1)

        q_a = q_a.reshape(B, S, N_HEADS_A, HEAD_DIM_A)
        k_a = k_a.reshape(B, S, N_HEADS_A, HEAD_DIM_A)
        v_a = v_a.reshape(B, S, N_HEADS_A, HEAD_DIM_A)
        outs, lses = [], []
        for g, (window, dil) in enumerate(ATTN_GROUPS):
            hs = slice(g * HEADS_PER_GROUP, (g + 1) * HEADS_PER_GROUP)
            o, lse = dilated_window_attention(q_a[:, :, hs], k_a[:, :, hs], v_a[:, :, hs],
                                              rel_bias[:, hs], window, dil)
            outs.append(o)
            lses.append(lse)
        wts = jax.nn.softmax(jnp.stack(lses), axis=0)
        y_a = jnp.einsum('gbsh,gbshc->bshc', wts, jnp.stack(outs))
        y_a = y_a.reshape(B, S, OUT_WIDTH_A).astype(h.dtype) @ w_o_attn[l]

        kern = conv_w[l].reshape(CONV_WIDTH, 1, LRU_WIDTH).astype(x_b.dtype)
        xc = lax.conv_general_dilated(x_b, kern, window_strides=(1,), padding=[(1, 2)],
                                      dimension_numbers=('NWC', 'WIO', 'NWC'),
                                      feature_group_count=LRU_WIDTH) + conv_b[l]
        xc = xc.astype(jnp.float32)
        h_fwd = rg_lru_forward(xc, lru_wa[l, 0], lru_ba[l, 0], lru_wi[l, 0], lru_bi[l, 0], lru_lambda[l, 0])
        h_bwd = jnp.flip(rg_lru_forward(jnp.flip(xc, axis=1), lru_wa[l, 1], lru_ba[l, 1],
                                        lru_wi[l, 1], lru_bi[l, 1], lru_lambda[l, 1]), axis=1)
        y_lru = (h_fwd + h_bwd).astype(h.dtype) * jax.nn.gelu(y_b)
        y_lru = y_lru @ w_o_lru[l]

        mem_n = rms_norm(mem, norm_mem[l])
        k_c, v_c = jnp.split(mem_n @ w_mem_kv[l], 2, axis=-1)
        q_c = q_c.reshape(B, S, MEM_HEADS, MEM_HEAD_DIM)
        k_c = k_c.reshape(B, N_MEM, MEM_HEADS, MEM_HEAD_DIM)
        v_c = v_c.reshape(B, N_MEM, MEM_HEADS, MEM_HEAD_DIM)
        logits_c = jnp.einsum('bshc,bmhc->bhsm', q_c, k_c).astype(jnp.float32) * (1.0 / math.sqrt(MEM_HEAD_DIM))
        p_c = jax.nn.softmax(logits_c, axis=-1)
        y_c = jnp.einsum('bhsm,bmhc->bshc', p_c.astype(v_c.dtype), v_c).reshape(B, S, MEM_WIDTH)
        y_c = y_c @ w_o_mem[l]

        gates = jax.nn.sigmoid((h @ w_gate[l] + b_gate[l]).astype(jnp.float32)).astype(h.dtype)
        g_a, g_b, g_c = jnp.split(gates, N_BRANCH, axis=-1)
        mixed = g_a * y_a + g_b * y_lru + g_c * y_c
        x = x + mixed @ w_out[l]

        h2 = rms_norm(x, norm_mlp[l])
        x = x + jnp.square(jax.nn.relu(h2 @ w_up[l])) @ w_down[l]

    return rms_norm(x, norm_final)
```

```python
import functools
import math

import jax
import jax.numpy as jnp
import numpy as np
from jax import lax
from jax.experimental import pallas as pl
from jax.experimental.pallas import tpu as pltpu

D_MODEL = 2048
HEAD_DIM_A = 128
ATTN_GROUPS = ((128, 1), (512, 4), (2048, 16))
HEADS_PER_GROUP = 4
GROUP_WIDTH = HEADS_PER_GROUP * HEAD_DIM_A
WIDTH_A = len(ATTN_GROUPS) * GROUP_WIDTH
ATTN_RADIUS = 64
N_BUCKETS = 32
MAX_DISTANCE = 1024
LRU_WIDTH = 1536
LRU_BLOCKS = 12
LRU_BW = 128
LRU_C = 8.0
N_MEM = 256
MEM_HEADS = 4
MEM_HEAD_DIM = 256
MEM_WIDTH = MEM_HEADS * MEM_HEAD_DIM
D_FF = 4 * D_MODEL
EPS = 1e-6
N_IN = 3 * WIDTH_A + 2 * LRU_WIDTH + MEM_WIDTH
COL_K = WIDTH_A
COL_V = 2 * WIDTH_A
COL_XB = 3 * WIDTH_A
COL_YB = 3 * WIDTH_A + LRU_WIDTH
COL_QC = 3 * WIDTH_A + 2 * LRU_WIDTH
NEG_INF = -1e30

SUB_Q = 128
SUB_K = SUB_Q + 2 * ATTN_RADIUS
LSE_LANES = 128
LSE_REP = LSE_LANES // HEADS_PER_GROUP

VMEM_LIMIT = 56 * 1024 * 1024

f32 = jnp.float32
bf16 = jnp.bfloat16


def _cparams(*sem):
    return pltpu.CompilerParams(dimension_semantics=sem, vmem_limit_bytes=VMEM_LIMIT)


def _rms(x, gain):
    return x * lax.rsqrt(jnp.mean(x * x, axis=-1, keepdims=True) + EPS) * gain


def _in_proj_kernel(x_ref, g_ref, w_ref, o_ref, h_scr):
    @pl.when(pl.program_id(1) == 0)
    def _():
        h_scr[...] = _rms(x_ref[...], g_ref[...]).astype(bf16)

    o_ref[...] = jnp.dot(h_scr[...], w_ref[...], preferred_element_type=f32).astype(o_ref.dtype)


def _in_proj(x2, gain, w, tm=512, tn=2176):
    T = x2.shape[0]
    return pl.pallas_call(
        _in_proj_kernel,
        grid=(T // tm, N_IN // tn),
        in_specs=[
            pl.BlockSpec((tm, D_MODEL), lambda i, j: (i, 0)),
            pl.BlockSpec((1, D_MODEL), lambda i, j: (0, 0)),
            pl.BlockSpec((D_MODEL, tn), lambda i, j: (0, j)),
        ],
        out_specs=pl.BlockSpec((tm, tn), lambda i, j: (i, j)),
        out_shape=jax.ShapeDtypeStruct((T, N_IN), bf16),
        scratch_shapes=[pltpu.VMEM((tm, D_MODEL), bf16)],
        compiler_params=_cparams("parallel", "arbitrary"),
        name="in_proj",
    )(x2, gain, w)


def _t5_bucket(rel):
    nb = N_BUCKETS // 2
    max_exact = nb // 2
    sign = (rel > 0).astype(np.int32) * nb
    n = np.abs(rel)
    large = max_exact + (np.log(np.maximum(n, 1) / max_exact)
                         / np.log(MAX_DISTANCE / max_exact) * (nb - max_exact)).astype(np.int32)
    large = np.minimum(large, nb - 1)
    return (sign + np.where(n < max_exact, n, large)).astype(np.int32)


def _band_bias(rel_bias_g, dilation):
    qq = np.arange(SUB_Q)[:, None]
    kk = np.arange(SUB_K)[None, :]
    rel = kk - ATTN_RADIUS - qq
    band = np.abs(rel) <= ATTN_RADIUS
    bias = rel_bias_g.astype(f32)[_t5_bucket(rel * dilation)]
    bias = jnp.where(band[:, :, None], bias, NEG_INF)
    return jnp.transpose(bias, (2, 0, 1))


def _attn_kernel(q_ref, kp_ref, km_ref, kn_ref, vp_ref, vm_ref, vn_ref, bias_ref,
                 o_ref, lse_ref, kbuf, vbuf, *, tq, seq):
    R = ATTN_RADIUS
    kbuf[0:R] = kp_ref[0]
    kbuf[R:R + tq] = km_ref[0]
    kbuf[R + tq:] = kn_ref[0]
    vbuf[0:R] = vp_ref[0]
    vbuf[R:R + tq] = vm_ref[0]
    vbuf[R + tq:] = vn_ref[0]
    q0 = pl.program_id(2) * tq
    scale = 1.0 / math.sqrt(HEAD_DIM_A)
    lane = lax.broadcasted_iota(jnp.int32, (SUB_Q, LSE_LANES), 1)
    for s in range(tq // SUB_Q):
        r0 = s * SUB_Q
        pos = q0 + (r0 - R) + lax.broadcasted_iota(jnp.int32, (1, SUB_K), 1)
        edge = jnp.where(pos >= 0, jnp.where(pos < seq, 0.0, NEG_INF), NEG_INF)
        lse_tile = None
        for h in range(HEADS_PER_GROUP):
            c0 = h * HEAD_DIM_A
            q = q_ref[0, r0:r0 + SUB_Q, c0:c0 + HEAD_DIM_A]
            k = kbuf[r0:r0 + SUB_K, c0:c0 + HEAD_DIM_A]
            v = vbuf[r0:r0 + SUB_K, c0:c0 + HEAD_DIM_A]
            logits = lax.dot_general(q, k, (((1,), (1,)), ((), ())), preferred_element_type=f32)
            logits = logits * scale + bias_ref[h] + edge
            m = jnp.max(logits, axis=-1, keepdims=True)
            p = jnp.exp(logits - m)
            ssum = jnp.sum(p, axis=-1, keepdims=True)
            o = jnp.dot(p.astype(bf16), v, preferred_element_type=f32) * (1.0 / ssum)
            o_ref[0, r0:r0 + SUB_Q, c0:c0 + HEAD_DIM_A] = o.astype(o_ref.dtype)
            lse = m + jnp.log(ssum)
            lse_tile = lse if lse_tile is None else jnp.where(lane >= h * LSE_REP, lse, lse_tile)
        lse_ref[0, r0:r0 + SUB_Q, :] = jnp.broadcast_to(lse_tile, (SUB_Q, LSE_LANES))


def _attn_group(proj, rel_bias, g, batch, seq_tokens):
    _, d = ATTN_GROUPS[g]
    L = seq_tokens // d
    tq = min(512, L)
    R = ATTN_RADIUS
    nb = N_IN // GROUP_WIDTH
    pv = proj.reshape(batch, L, d * N_IN)
    bias = _band_bias(rel_bias[:, g * HEADS_PER_GROUP:(g + 1) * HEADS_PER_GROUP], d)
    rb = tq // R
    last_rb = L // R - 1
    kq, kk, kv = g, COL_K // GROUP_WIDTH + g, COL_V // GROUP_WIDTH + g

    def main(col):
        return pl.BlockSpec((1, tq, GROUP_WIDTH), lambda b, r, t: (b, t, r * nb + col))

    def prev(col):
        return pl.BlockSpec((1, R, GROUP_WIDTH),
                            lambda b, r, t: (b, jnp.maximum(t * rb - 1, 0), r * nb + col))

    def nxt(col):
        return pl.BlockSpec((1, R, GROUP_WIDTH),
                            lambda b, r, t: (b, jnp.minimum((t + 1) * rb, last_rb), r * nb + col))

    o, lse = pl.pallas_call(
        functools.partial(_attn_kernel, tq=tq, seq=L),
        grid=(batch, d, L // tq),
        in_specs=[main(kq), prev(kk), main(kk), nxt(kk), prev(kv), main(kv), nxt(kv),
                  pl.BlockSpec((HEADS_PER_GROUP, SUB_Q, SUB_K), lambda b, r, t: (0, 0, 0))],
        out_specs=[pl.BlockSpec((1, tq, GROUP_WIDTH), lambda b, r, t: (b, t, r)),
                   pl.BlockSpec((1, tq, LSE_LANES), lambda b, r, t: (b, t, r))],
        out_shape=[jax.ShapeDtypeStruct((batch, L, d * GROUP_WIDTH), bf16),
                   jax.ShapeDtypeStruct((batch, L, d * LSE_LANES), f32)],
        scratch_shapes=[pltpu.VMEM((tq + 2 * R, GROUP_WIDTH), bf16),
                        pltpu.VMEM((tq + 2 * R, GROUP_WIDTH), bf16)],
        compiler_params=_cparams("parallel", "parallel", "arbitrary"),
        name=f"attn_g{g}",
    )(pv, pv, pv, pv, pv, pv, pv, bias)
    T = batch * seq_tokens
    return o.reshape(T, GROUP_WIDTH), lse.reshape(T, LSE_LANES)


LRU_CHUNK = 256
LRU_PAD = 8


def _lru_kernel(xb_ref, yb_ref, cw_ref, cb_ref, w_ref, gb_ref, lam_ref, o_ref,
                xpad, af, bf, ab, bb, *, seq):
    R = LRU_CHUNK
    P = LRU_PAD
    xpad[0:P] = jnp.zeros((P, LRU_BW), f32)
    xpad[P + seq:] = jnp.zeros((P, LRU_BW), f32)
    xpad[P:P + seq] = xb_ref[0].astype(f32)
    lam = lam_ref[...]
    log_a_unit = -LRU_C * (jnp.maximum(-lam, 0.0) + jnp.log1p(jnp.exp(-jnp.abs(lam))))
    cw = cw_ref[...]
    cb = cb_ref[...]
    gate_bias = gb_ref[0]
    row = lax.broadcasted_iota(jnp.int32, (R, LRU_BW), 0)

    def chunk(ci, carry):
        c0 = pl.multiple_of(ci * R, R)
        v = xpad[pl.ds(c0, R + 2 * P), :]
        xc = (cw[0:1] * v[P - 1:P - 1 + R] + cw[1:2] * v[P:P + R]
              + cw[2:3] * v[P + 1:P + 1 + R] + cw[3:4] * v[P + 2:P + 2 + R]) + cb
        gates = jax.nn.sigmoid(jnp.dot(xc.astype(bf16), w_ref[0], preferred_element_type=f32) + gate_bias)
        t = row + c0
        for direction, (a_scr, b_scr, start) in enumerate(((af, bf, 0), (ab, bb, seq - 1))):
            base = direction * 2 * LRU_BW
            r = gates[:, base:base + LRU_BW]
            i = gates[:, base + LRU_BW:base + 2 * LRU_BW]
            a = jnp.exp(log_a_unit[direction:direction + 1] * r)
            mult = jnp.where(t == start, 1.0, jnp.sqrt(1.0 - a * a))
            a_scr[pl.ds(c0, R), :] = a
            b_scr[pl.ds(c0, R), :] = mult * (i * xc)
        return carry

    lax.fori_loop(0, seq // R, chunk, 0)

    srow = lax.broadcasted_iota(jnp.int32, (8, LRU_BW), 0)

    def scan(i, carry):
        cf, cbk = carry
        tf = pl.multiple_of(i * 8, 8)
        tb = pl.multiple_of(seq - 8 - i * 8, 8)
        a = af[pl.ds(tf, 8), :]
        b = bf[pl.ds(tf, 8), :]
        for k in (1, 2, 4):
            keep = srow >= k
            b = jnp.where(keep, a * pltpu.roll(b, k, 0) + b, b)
            a = jnp.where(keep, a * pltpu.roll(a, k, 0), a)
        h = a * cf + b
        bf[pl.ds(tf, 8), :] = h
        cf = jnp.broadcast_to(h[7:8], (8, LRU_BW))
        a = ab[pl.ds(tb, 8), :]
        b = bb[pl.ds(tb, 8), :]
        for k in (1, 2, 4):
            keep = srow < 8 - k
            b = jnp.where(keep, a * pltpu.roll(b, 8 - k, 0) + b, b)
            a = jnp.where(keep, a * pltpu.roll(a, 8 - k, 0), a)
        h = a * cbk + b
        bb[pl.ds(tb, 8), :] = h
        cbk = jnp.broadcast_to(h[0:1], (8, LRU_BW))
        return cf, cbk

    zero = jnp.zeros((8, LRU_BW), f32)
    lax.fori_loop(0, seq // 8, scan, (zero, zero), unroll=4)

    def finish(ci, carry):
        c0 = pl.multiple_of(ci * R, R)
        y = yb_ref[0, pl.ds(c0, R), :].astype(f32)
        gelu = y * (0.5 * (1.0 + jnp.tanh(math.sqrt(2.0 / math.pi) * (y + 0.044715 * (y * y * y)))))
        o_ref[0, pl.ds(c0, R), :] = ((bf[pl.ds(c0, R), :] + bb[pl.ds(c0, R), :]) * gelu).astype(o_ref.dtype)
        return carry

    lax.fori_loop(0, seq // R, finish, 0)


def _lru(proj3, conv_w, conv_b, w_gates, b_gates, lam):
    B, S, _ = proj3.shape
    xb0 = COL_XB // LRU_BW
    yb0 = COL_YB // LRU_BW
    return pl.pallas_call(
        functools.partial(_lru_kernel, seq=S),
        grid=(B, LRU_BLOCKS),
        in_specs=[
            pl.BlockSpec((1, S, LRU_BW), lambda b, n: (b, 0, xb0 + n)),
            pl.BlockSpec((1, S, LRU_BW), lambda b, n: (b, 0, yb0 + n)),
            pl.BlockSpec((4, LRU_BW), lambda b, n: (0, n)),
            pl.BlockSpec((1, LRU_BW), lambda b, n: (0, n)),
            pl.BlockSpec((1, LRU_BW, 4 * LRU_BW), lambda b, n: (n, 0, 0)),
            pl.BlockSpec((1, 1, 4 * LRU_BW), lambda b, n: (n, 0, 0)),
            pl.BlockSpec((2, LRU_BW), lambda b, n: (0, n)),
        ],
        out_specs=pl.BlockSpec((1, S, LRU_BW), lambda b, n: (b, 0, n)),
        out_shape=jax.ShapeDtypeStruct((B, S, LRU_WIDTH), bf16),
        scratch_shapes=[pltpu.VMEM((S + 2 * LRU_PAD, LRU_BW), f32)] + [pltpu.VMEM((S, LRU_BW), f32)] * 4,
        compiler_params=_cparams("parallel", "parallel"),
        name="lru",
    )(proj3, proj3, conv_w, conv_b, w_gates, b_gates, lam)


def _mem_kv_kernel(m_ref, g_ref, w_ref, o_ref, h_scr):
    @pl.when(pl.program_id(0) == 0)
    def _():
        h_scr[...] = _rms(m_ref[...], g_ref[...]).astype(bf16)

    o_ref[...] = jnp.dot(h_scr[...], w_ref[...], preferred_element_type=f32).astype(o_ref.dtype)


def _mem_kv(mem2, gain, w, tn=512):
    M = mem2.shape[0]
    N = w.shape[1]
    return pl.pallas_call(
        _mem_kv_kernel,
        grid=(N // tn,),
        in_specs=[pl.BlockSpec((M, D_MODEL), lambda j: (0, 0)),
                  pl.BlockSpec((1, D_MODEL), lambda j: (0, 0)),
                  pl.BlockSpec((D_MODEL, tn), lambda j: (0, j))],
        out_specs=pl.BlockSpec((M, tn), lambda j: (0, j)),
        out_shape=jax.ShapeDtypeStruct((M, N), bf16),
        scratch_shapes=[pltpu.VMEM((M, D_MODEL), bf16)],
        compiler_params=_cparams("arbitrary"),
        name="mem_kv",
    )(mem2, gain, w)


def _xattn_kernel(q0_ref, q1_ref, q2_ref, q3_ref, kv_ref, o_ref):
    scale = 1.0 / math.sqrt(MEM_HEAD_DIM)
    for h, q_ref in enumerate((q0_ref, q1_ref, q2_ref, q3_ref)):
        c0 = h * MEM_HEAD_DIM
        k = kv_ref[0, :, c0:c0 + MEM_HEAD_DIM]
        v = kv_ref[0, :, MEM_WIDTH + c0:MEM_WIDTH + c0 + MEM_HEAD_DIM]
        logits = lax.dot_general(q_ref[0], k, (((1,), (1,)), ((), ())), preferred_element_type=f32) * scale
        m = jnp.max(logits, axis=-1, keepdims=True)
        p = jnp.exp(logits - m)
        ssum = jnp.sum(p, axis=-1, keepdims=True)
        o = jnp.dot(p.astype(bf16), v, preferred_element_type=f32) * (1.0 / ssum)
        o_ref[0, :, c0:c0 + MEM_HEAD_DIM] = o.astype(o_ref.dtype)


def _xattn(proj3, kv3, tq=1024):
    B, S, _ = proj3.shape
    qb0 = COL_QC // MEM_HEAD_DIM

    def qspec(h):
        return pl.BlockSpec((1, tq, MEM_HEAD_DIM), lambda b, t: (b, t, qb0 + h))

    return pl.pallas_call(
        _xattn_kernel,
        grid=(B, S // tq),
        in_specs=[qspec(0), qspec(1), qspec(2), qspec(3),
                  pl.BlockSpec((1, N_MEM, 2 * MEM_WIDTH), lambda b, t: (b, 0, 0))],
        out_specs=pl.BlockSpec((1, tq, MEM_WIDTH), lambda b, t: (b, t, 0)),
        out_shape=jax.ShapeDtypeStruct((B, S, MEM_WIDTH), bf16),
        compiler_params=_cparams("parallel", "parallel"),
        name="xattn",
    )(proj3, proj3, proj3, proj3, kv3)


def _merge_kernel(x_ref, g_ref, o0_ref, o1_ref, o2_ref, l0_ref, l1_ref, l2_ref, yl_ref, yc_ref,
                  wga_ref, wgb_ref, wgc_ref, bga_ref, bgb_ref, bgc_ref,
                  woa_ref, wol_ref, wom_ref, wout_ref, out_ref, h_scr, ya_scr):
    @pl.when(pl.program_id(1) == 0)
    def _():
        x = x_ref[...]
        h_scr[...] = _rms(x, g_ref[...]).astype(bf16)
        out_ref[...] = x
        l0, l1, l2 = l0_ref[...], l1_ref[...], l2_ref[...]
        m = jnp.maximum(jnp.maximum(l0, l1), l2)
        e0, e1, e2 = jnp.exp(l0 - m), jnp.exp(l1 - m), jnp.exp(l2 - m)
        inv = 1.0 / (e0 + e1 + e2)
        for h in range(HEADS_PER_GROUP):
            c0 = h * HEAD_DIM_A
            y = None
            for e, o_ref in ((e0, o0_ref), (e1, o1_ref), (e2, o2_ref)):
                w = (e * inv)[:, h * LSE_REP:h * LSE_REP + 1]
                term = w * o_ref[:, c0:c0 + HEAD_DIM_A].astype(f32)
                y = term if y is None else y + term
            ya_scr[:, c0:c0 + HEAD_DIM_A] = y.astype(bf16)

    h = h_scr[...]

    def gate(w_ref, b_ref):
        return jax.nn.sigmoid(jnp.dot(h, w_ref[...], preferred_element_type=f32) + b_ref[...])

    mixed = (gate(wga_ref, bga_ref) * jnp.dot(ya_scr[...], woa_ref[...], preferred_element_type=f32)
             + gate(wgb_ref, bgb_ref) * jnp.dot(yl_ref[...], wol_ref[...], preferred_element_type=f32)
             + gate(wgc_ref, bgc_ref) * jnp.dot(yc_ref[...], wom_ref[...], preferred_element_type=f32))
    out_ref[...] += jnp.dot(mixed.astype(bf16), wout_ref[...], preferred_element_type=f32)


def _merge(x2, gain, o_groups, lse_groups, y_lru, y_c, w_gate, b_gate, w_o_attn, w_o_lru, w_o_mem, w_out,
           tm=512, tn=256):
    T = x2.shape[0]
    nj = D_MODEL // tn

    def rows(width):
        return pl.BlockSpec((tm, width), lambda i, j: (i, 0))

    def gate_w(k):
        return pl.BlockSpec((D_MODEL, tn), lambda i, j: (0, k * nj + j))

    def gate_b(k):
        return pl.BlockSpec((1, tn), lambda i, j: (0, k * nj + j))

    def cols(width):
        return pl.BlockSpec((width, tn), lambda i, j: (0, j))

    return pl.pallas_call(
        _merge_kernel,
        grid=(T // tm, nj),
        in_specs=[rows(D_MODEL), pl.BlockSpec((1, D_MODEL), lambda i, j: (0, 0)),
                  rows(GROUP_WIDTH), rows(GROUP_WIDTH), rows(GROUP_WIDTH),
                  rows(LSE_LANES), rows(LSE_LANES), rows(LSE_LANES),
                  rows(LRU_WIDTH), rows(MEM_WIDTH),
                  gate_w(0), gate_w(1), gate_w(2), gate_b(0), gate_b(1), gate_b(2),
                  cols(GROUP_WIDTH), cols(LRU_WIDTH), cols(MEM_WIDTH),
                  pl.BlockSpec((tn, D_MODEL), lambda i, j: (j, 0))],
        out_specs=pl.BlockSpec((tm, D_MODEL), lambda i, j: (i, 0)),
        out_shape=jax.ShapeDtypeStruct((T, D_MODEL), f32),
        scratch_shapes=[pltpu.VMEM((tm, D_MODEL), bf16), pltpu.VMEM((tm, GROUP_WIDTH), bf16)],
        compiler_params=_cparams("parallel", "arbitrary"),
        name="merge",
    )(x2, gain, *o_groups, *lse_groups, y_lru, y_c, w_gate, w_gate, w_gate, b_gate, b_gate, b_gate,
      w_o_attn, w_o_lru, w_o_mem, w_out)


def _mlp_kernel(x_ref, g_ref, gf_ref, wu_ref, wd_ref, out_ref, h_scr):
    j = pl.program_id(1)

    @pl.when(j == 0)
    def _():
        x = x_ref[...]
        h_scr[...] = _rms(x, g_ref[...]).astype(bf16)
        out_ref[...] = x

    u = jnp.maximum(jnp.dot(h_scr[...], wu_ref[...], preferred_element_type=f32), 0.0)
    out_ref[...] += jnp.dot((u * u).astype(bf16), wd_ref[...], preferred_element_type=f32)

    @pl.when(j == pl.num_programs(1) - 1)
    def _():
        out_ref[...] = _rms(out_ref[...], gf_ref[...])


def _mlp(x2, gain, gain_final, w_up, w_down, tm=512, tf=1024):
    T = x2.shape[0]
    return pl.pallas_call(
        _mlp_kernel,
        grid=(T // tm, D_FF // tf),
        in_specs=[pl.BlockSpec((tm, D_MODEL), lambda i, j: (i, 0)),
                  pl.BlockSpec((1, D_MODEL), lambda i, j: (0, 0)),
                  pl.BlockSpec((1, D_MODEL), lambda i, j: (0, 0)),
                  pl.BlockSpec((D_MODEL, tf), lambda i, j: (0, j)),
                  pl.BlockSpec((tf, D_MODEL), lambda i, j: (j, 0))],
        out_specs=pl.BlockSpec((tm, D_MODEL), lambda i, j: (i, 0)),
        out_shape=jax.ShapeDtypeStruct((T, D_MODEL), f32),
        scratch_shapes=[pltpu.VMEM((tm, D_MODEL), bf16)],
        compiler_params=_cparams("parallel", "arbitrary"),
        name="mlp",
    )(x2, gain, gain_final, w_up, w_down)


def kernel(x, mem, rel_bias, norm_mix, norm_mem, norm_mlp, norm_final, w_in, w_gate, b_gate, conv_w, conv_b,
           lru_wa, lru_ba, lru_wi, lru_bi, lru_lambda, w_mem_kv, w_o_attn, w_o_lru, w_o_mem, w_out, w_up, w_down):
    B, S, D = x.shape
    T = B * S
    depth = w_in.shape[0]
    x2 = x.reshape(T, D)
    mem2 = mem.reshape(B * N_MEM, D)
    for l in range(depth):
        gain_mix = norm_mix[l].reshape(1, D)
        proj = _in_proj(x2, gain_mix, w_in[l].astype(bf16))
        proj3 = proj.reshape(B, S, N_IN)

        attn = [_attn_group(proj, rel_bias, g, B, S) for g in range(len(ATTN_GROUPS))]

        w_gates = jnp.concatenate([lru_wa[l, 0], lru_wi[l, 0], lru_wa[l, 1], lru_wi[l, 1]], axis=-1).astype(bf16)
        b_gates = jnp.concatenate([lru_ba[l, 0], lru_bi[l, 0], lru_ba[l, 1], lru_bi[l, 1]], axis=-1)
        y_lru = _lru(proj3, conv_w[l], conv_b[l].reshape(1, LRU_WIDTH), w_gates,
                     b_gates.reshape(LRU_BLOCKS, 1, 4 * LRU_BW), lru_lambda[l])

        kv = _mem_kv(mem2, norm_mem[l].reshape(1, D), w_mem_kv[l].astype(bf16))
        y_c = _xattn(proj3, kv.reshape(B, N_MEM, 2 * MEM_WIDTH))

        x2 = _merge(x2, gain_mix, [a[0] for a in attn], [a[1] for a in attn],
                    y_lru.reshape(T, LRU_WIDTH), y_c.reshape(T, MEM_WIDTH),
                    w_gate[l].astype(bf16), b_gate[l].reshape(1, 3 * D),
                    w_o_attn[l].astype(bf16), w_o_lru[l].astype(bf16), w_o_mem[l].astype(bf16),
                    w_out[l].astype(bf16))
        last = l == depth - 1
        x2 = _mlp(x2, norm_mlp[l].reshape(1, D), norm_final.reshape(1, D), w_up[l].astype(bf16),
                  w_down[l].astype(bf16))
        assert last, "final norm is fused into the MLP kernel: depth must be 1"
    return x2.reshape(B, S, D)
```

```python
import functools
import math

import jax
import jax.numpy as jnp
import numpy as np
from jax import lax
from jax.experimental import pallas as pl
from jax.experimental.pallas import tpu as pltpu

D_MODEL = 2048
HEAD_DIM_A = 128
ATTN_GROUPS = ((128, 1), (512, 4), (2048, 16))
HEADS_PER_GROUP = 4
GROUP_WIDTH = HEADS_PER_GROUP * HEAD_DIM_A
WIDTH_A = len(ATTN_GROUPS) * GROUP_WIDTH
ATTN_RADIUS = 64
N_BUCKETS = 32
MAX_DISTANCE = 1024
LRU_WIDTH = 1536
LRU_BLOCKS = 12
LRU_BW = 128
LRU_C = 8.0
N_MEM = 256
MEM_HEADS = 4
MEM_HEAD_DIM = 256
MEM_WIDTH = MEM_HEADS * MEM_HEAD_DIM
D_FF = 4 * D_MODEL
EPS = 1e-6
N_IN = 3 * WIDTH_A + 2 * LRU_WIDTH + MEM_WIDTH
COL_K = WIDTH_A
COL_V = 2 * WIDTH_A
COL_XB = 3 * WIDTH_A
COL_YB = 3 * WIDTH_A + LRU_WIDTH
COL_QC = 3 * WIDTH_A + 2 * LRU_WIDTH
NEG_INF = -1e30

SUB_Q = 128
SUB_K = SUB_Q + 2 * ATTN_RADIUS
LSE_LANES = 128
LSE_REP = LSE_LANES // HEADS_PER_GROUP

VMEM_LIMIT = 56 * 1024 * 1024

f32 = jnp.float32
bf16 = jnp.bfloat16


def _cparams(*sem):
    return pltpu.CompilerParams(dimension_semantics=sem, vmem_limit_bytes=VMEM_LIMIT)


def _rms(x, gain):
    return x * lax.rsqrt(jnp.mean(x * x, axis=-1, keepdims=True) + EPS) * gain


QKV_W = 3 * GROUP_WIDTH
REST_W = 2 * LRU_WIDTH + MEM_WIDTH
REST_YB = LRU_WIDTH
REST_QC = 2 * LRU_WIDTH
PROJ_TN = GROUP_WIDTH
QKV_TILES = QKV_W // PROJ_TN
LANES = 128
SLABS = PROJ_TN // LANES


def _pack_w_in(w):
    cols = []
    for g in range(len(ATTN_GROUPS)):
        for base in (0, COL_K, COL_V):
            cols.append(w[:, base + g * GROUP_WIDTH:base + (g + 1) * GROUP_WIDTH])
    cols.append(w[:, COL_XB:])
    return jnp.concatenate(cols, axis=1).astype(bf16)


def _in_proj_kernel(x_ref, g_ref, w_ref, q0_ref, q1_ref, q2_ref, rest_ref, h_scr, res_scr, *, tm):
    j = pl.program_id(1)

    @pl.when(j == 0)
    def _():
        h_scr[...] = _rms(x_ref[...], g_ref[...]).astype(bf16)

    res = jnp.dot(h_scr[...], w_ref[...], preferred_element_type=f32)

    @pl.when(j < QKV_TILES)
    def _():
        q0_ref[0] = res.astype(bf16)

    for g, q_ref in ((1, q1_ref), (2, q2_ref)):
        d = ATTN_GROUPS[g][1]

        @pl.when((j >= g * QKV_TILES) & (j < (g + 1) * QKV_TILES))
        def _():
            for c in range(SLABS):
                res_scr[c] = res[:, c * LANES:(c + 1) * LANES]
            for r in range(d):
                for c in range(SLABS):
                    q_ref[0, r, :, c * LANES:(c + 1) * LANES] = (
                        res_scr[c, pl.ds(r, tm // d, stride=d), :].astype(bf16))

    @pl.when(j >= 3 * QKV_TILES)
    def _():
        rest_ref[...] = res.astype(bf16)


def _in_proj(x2, gain, w, batch, seq, tm=1024):
    T = x2.shape[0]
    tn = PROJ_TN
    nt = seq // tm
    d1, d2 = ATTN_GROUPS[1][1], ATTN_GROUPS[2][1]

    def qcol(j, g):
        return jnp.clip(j - g * QKV_TILES, 0, QKV_TILES - 1)

    return pl.pallas_call(
        functools.partial(_in_proj_kernel, tm=tm),
        grid=(T // tm, N_IN // tn),
        in_specs=[
            pl.BlockSpec((tm, D_MODEL), lambda i, j: (i, 0)),
            pl.BlockSpec((1, D_MODEL), lambda i, j: (0, 0)),
            pl.BlockSpec((D_MODEL, tn), lambda i, j: (0, j)),
        ],
        out_specs=[
            pl.BlockSpec((1, tm, tn), lambda i, j: (i // nt, i % nt, qcol(j, 0))),
            pl.BlockSpec((1, d1, tm // d1, tn), lambda i, j: (i // nt, 0, i % nt, qcol(j, 1))),
            pl.BlockSpec((1, d2, tm // d2, tn), lambda i, j: (i // nt, 0, i % nt, qcol(j, 2))),
            pl.BlockSpec((tm, tn), lambda i, j: (i, jnp.maximum(j - 3 * QKV_TILES, 0))),
        ],
        out_shape=[
            jax.ShapeDtypeStruct((batch, seq, QKV_W), bf16),
            jax.ShapeDtypeStruct((batch, d1, seq // d1, QKV_W), bf16),
            jax.ShapeDtypeStruct((batch, d2, seq // d2, QKV_W), bf16),
            jax.ShapeDtypeStruct((T, REST_W), bf16),
        ],
        scratch_shapes=[pltpu.VMEM((tm, D_MODEL), bf16), pltpu.VMEM((SLABS, tm, LANES), f32)],
        compiler_params=_cparams("parallel", "arbitrary"),
        name="in_proj",
    )(x2, gain, w)


def _t5_bucket(rel):
    nb = N_BUCKETS // 2
    max_exact = nb // 2
    sign = (rel > 0).astype(np.int32) * nb
    n = np.abs(rel)
    large = max_exact + (np.log(np.maximum(n, 1) / max_exact)
                         / np.log(MAX_DISTANCE / max_exact) * (nb - max_exact)).astype(np.int32)
    large = np.minimum(large, nb - 1)
    return (sign + np.where(n < max_exact, n, large)).astype(np.int32)


def _band_bias(rel_bias_g, dilation):
    qq = np.arange(SUB_Q)[:, None]
    kk = np.arange(SUB_K)[None, :]
    rel = kk - ATTN_RADIUS - qq
    onehot = (_t5_bucket(rel * dilation)[None] == np.arange(N_BUCKETS)[:, None, None]).astype(np.float32)
    bias = jnp.einsum('nh,nqk->hqk', rel_bias_g.astype(f32), onehot, precision=lax.Precision.HIGHEST)
    return bias + np.where(np.abs(rel) <= ATTN_RADIUS, 0.0, NEG_INF).astype(np.float32)[None]


def _attn_kernel(q_ref, kp_ref, km_ref, kn_ref, vp_ref, vm_ref, vn_ref, bias_ref,
                 o_ref, lse_ref, kbuf, vbuf, *, tq, seq):
    R = ATTN_RADIUS
    kbuf[0:R] = kp_ref[0]
    kbuf[R:R + tq] = km_ref[0]
    kbuf[R + tq:] = kn_ref[0]
    vbuf[0:R] = vp_ref[0]
    vbuf[R:R + tq] = vm_ref[0]
    vbuf[R + tq:] = vn_ref[0]
    q0 = pl.program_id(1) * tq
    scale = 1.0 / math.sqrt(HEAD_DIM_A)
    lane = lax.broadcasted_iota(jnp.int32, (SUB_Q, LSE_LANES), 1)
    for s in range(tq // SUB_Q):
        r0 = s * SUB_Q
        pos = q0 + (r0 - R) + lax.broadcasted_iota(jnp.int32, (1, SUB_K), 1)
        edge = jnp.where(pos >= 0, jnp.where(pos < seq, 0.0, NEG_INF), NEG_INF)
        lse_tile = None
        for h in range(HEADS_PER_GROUP):
            c0 = h * HEAD_DIM_A
            q = q_ref[0, r0:r0 + SUB_Q, c0:c0 + HEAD_DIM_A]
            k = kbuf[r0:r0 + SUB_K, c0:c0 + HEAD_DIM_A]
            v = vbuf[r0:r0 + SUB_K, c0:c0 + HEAD_DIM_A]
            logits = lax.dot_general(q, k, (((1,), (1,)), ((), ())), preferred_element_type=f32)
            logits = logits * scale + bias_ref[h] + edge
            m = jnp.max(logits, axis=-1, keepdims=True)
            p = jnp.exp(logits - m)
            ssum = jnp.sum(p, axis=-1, keepdims=True)
            o = jnp.dot(p.astype(bf16), v, preferred_element_type=f32) * (1.0 / ssum)
            o_ref[0, r0:r0 + SUB_Q, c0:c0 + HEAD_DIM_A] = o.astype(o_ref.dtype)
            lse = m + jnp.log(ssum)
            lse_tile = lse if lse_tile is None else jnp.where(lane >= h * LSE_REP, lse, lse_tile)
        lse_ref[0, r0:r0 + SUB_Q, :] = jnp.broadcast_to(lse_tile, (SUB_Q, LSE_LANES))


def _attn_group(qkv, rel_bias, g):
    _, d = ATTN_GROUPS[g]
    n, L, _ = qkv.shape
    tq = min(512, L)
    R = ATTN_RADIUS
    bias = _band_bias(rel_bias[:, g * HEADS_PER_GROUP:(g + 1) * HEADS_PER_GROUP], d)
    rb = tq // R
    last_rb = L // R - 1

    def main(col):
        return pl.BlockSpec((1, tq, GROUP_WIDTH), lambda b, t: (b, t, col))

    def prev(col):
        return pl.BlockSpec((1, R, GROUP_WIDTH), lambda b, t: (b, jnp.maximum(t * rb - 1, 0), col))

    def nxt(col):
        return pl.BlockSpec((1, R, GROUP_WIDTH), lambda b, t: (b, jnp.minimum((t + 1) * rb, last_rb), col))

    return pl.pallas_call(
        functools.partial(_attn_kernel, tq=tq, seq=L),
        grid=(n, L // tq),
        in_specs=[main(0), prev(1), main(1), nxt(1), prev(2), main(2), nxt(2),
                  pl.BlockSpec((HEADS_PER_GROUP, SUB_Q, SUB_K), lambda b, t: (0, 0, 0))],
        out_specs=[pl.BlockSpec((1, tq, GROUP_WIDTH), lambda b, t: (b, t, 0)),
                   pl.BlockSpec((1, tq, LSE_LANES), lambda b, t: (b, t, 0))],
        out_shape=[jax.ShapeDtypeStruct((n, L, GROUP_WIDTH), bf16),
                   jax.ShapeDtypeStruct((n, L, LSE_LANES), f32)],
        scratch_shapes=[pltpu.VMEM((tq + 2 * R, GROUP_WIDTH), bf16),
                        pltpu.VMEM((tq + 2 * R, GROUP_WIDTH), bf16)],
        compiler_params=_cparams("parallel", "arbitrary"),
        name=f"attn_g{g}",
    )(qkv, qkv, qkv, qkv, qkv, qkv, qkv, bias)


LRU_CHUNK = 256
LRU_PAD = 8


def _lru_kernel(xb_ref, yb_ref, cw_ref, cb_ref, w_ref, gb_ref, lam_ref, o_ref,
                xpad, af, bf, ab, bb, *, seq):
    R = LRU_CHUNK
    P = LRU_PAD
    xpad[0:P] = jnp.zeros((P, LRU_BW), f32)
    xpad[P + seq:] = jnp.zeros((P, LRU_BW), f32)
    xpad[P:P + seq] = xb_ref[0].astype(f32)
    lam = lam_ref[...]
    log_a_unit = -LRU_C * (jnp.maximum(-lam, 0.0) + jnp.log1p(jnp.exp(-jnp.abs(lam))))
    cw = cw_ref[...]
    cb = cb_ref[...]
    gate_bias = gb_ref[0]
    row = lax.broadcasted_iota(jnp.int32, (R, LRU_BW), 0)

    def chunk(ci, carry):
        c0 = pl.multiple_of(ci * R, R)
        v = xpad[pl.ds(c0, R + 2 * P), :]
        xc = (cw[0:1] * v[P - 1:P - 1 + R] + cw[1:2] * v[P:P + R]
              + cw[2:3] * v[P + 1:P + 1 + R] + cw[3:4] * v[P + 2:P + 2 + R]) + cb
        gates = jax.nn.sigmoid(jnp.dot(xc.astype(bf16), w_ref[0], preferred_element_type=f32) + gate_bias)
        t = row + c0
        for direction, (a_scr, b_scr, start) in enumerate(((af, bf, 0), (ab, bb, seq - 1))):
            base = direction * 2 * LRU_BW
            r = gates[:, base:base + LRU_BW]
            i = gates[:, base + LRU_BW:base + 2 * LRU_BW]
            a = jnp.exp(log_a_unit[direction:direction + 1] * r)
            mult = jnp.where(t == start, 1.0, jnp.sqrt(1.0 - a * a))
            a_scr[pl.ds(c0, R), :] = a
            b_scr[pl.ds(c0, R), :] = mult * (i * xc)
        return carry

    lax.fori_loop(0, seq // R, chunk, 0)

    srow = lax.broadcasted_iota(jnp.int32, (8, LRU_BW), 0)

    def scan(i, carry):
        cf, cbk = carry
        tf = pl.multiple_of(i * 8, 8)
        tb = pl.multiple_of(seq - 8 - i * 8, 8)
        a = af[pl.ds(tf, 8), :]
        b = bf[pl.ds(tf, 8), :]
        for k in (1, 2, 4):
            keep = srow >= k
            b = jnp.where(keep, a * pltpu.roll(b, k, 0) + b, b)
            a = jnp.where(keep, a * pltpu.roll(a, k, 0), a)
        h = a * cf + b
        bf[pl.ds(tf, 8), :] = h
        cf = jnp.broadcast_to(h[7:8], (8, LRU_BW))
        a = ab[pl.ds(tb, 8), :]
        b = bb[pl.ds(tb, 8), :]
        for k in (1, 2, 4):
            keep = srow < 8 - k
            b = jnp.where(keep, a * pltpu.roll(b, 8 - k, 0) + b, b)
            a = jnp.where(keep, a * pltpu.roll(a, 8 - k, 0), a)
        h = a * cbk + b
        bb[pl.ds(tb, 8), :] = h
        cbk = jnp.broadcast_to(h[0:1], (8, LRU_BW))
        return cf, cbk

    zero = jnp.zeros((8, LRU_BW), f32)
    lax.fori_loop(0, seq // 8, scan, (zero, zero), unroll=4)

    def finish(ci, carry):
        c0 = pl.multiple_of(ci * R, R)
        y = yb_ref[0, pl.ds(c0, R), :].astype(f32)
        gelu = y * (0.5 * (1.0 + jnp.tanh(math.sqrt(2.0 / math.pi) * (y + 0.044715 * (y * y * y)))))
        o_ref[0, pl.ds(c0, R), :] = ((bf[pl.ds(c0, R), :] + bb[pl.ds(c0, R), :]) * gelu).astype(o_ref.dtype)
        return carry

    lax.fori_loop(0, seq // R, finish, 0)


def _lru(proj3, conv_w, conv_b, w_gates, b_gates, lam):
    B, S, _ = proj3.shape
    xb0 = 0
    yb0 = REST_YB // LRU_BW
    return pl.pallas_call(
        functools.partial(_lru_kernel, seq=S),
        grid=(B, LRU_BLOCKS),
        in_specs=[
            pl.BlockSpec((1, S, LRU_BW), lambda b, n: (b, 0, xb0 + n)),
            pl.BlockSpec((1, S, LRU_BW), lambda b, n: (b, 0, yb0 + n)),
            pl.BlockSpec((4, LRU_BW), lambda b, n: (0, n)),
            pl.BlockSpec((1, LRU_BW), lambda b, n: (0, n)),
            pl.BlockSpec((1, LRU_BW, 4 * LRU_BW), lambda b, n: (n, 0, 0)),
            pl.BlockSpec((1, 1, 4 * LRU_BW), lambda b, n: (n, 0, 0)),
            pl.BlockSpec((2, LRU_BW), lambda b, n: (0, n)),
        ],
        out_specs=pl.BlockSpec((1, S, LRU_BW), lambda b, n: (b, 0, n)),
        out_shape=jax.ShapeDtypeStruct((B, S, LRU_WIDTH), bf16),
        scratch_shapes=[pltpu.VMEM((S + 2 * LRU_PAD, LRU_BW), f32)] + [pltpu.VMEM((S, LRU_BW), f32)] * 4,
        compiler_params=_cparams("parallel", "parallel"),
        name="lru",
    )(proj3, proj3, conv_w, conv_b, w_gates, b_gates, lam)


def _mem_kv_kernel(m_ref, g_ref, w_ref, o_ref, h_scr):
    @pl.when(pl.program_id(0) == 0)
    def _():
        h_scr[...] = _rms(m_ref[...], g_ref[...]).astype(bf16)

    o_ref[...] = jnp.dot(h_scr[...], w_ref[...], preferred_element_type=f32).astype(o_ref.dtype)


def _mem_kv(mem2, gain, w, tn=512):
    M = mem2.shape[0]
    N = w.shape[1]
    return pl.pallas_call(
        _mem_kv_kernel,
        grid=(N // tn,),
        in_specs=[pl.BlockSpec((M, D_MODEL), lambda j: (0, 0)),
                  pl.BlockSpec((1, D_MODEL), lambda j: (0, 0)),
                  pl.BlockSpec((D_MODEL, tn), lambda j: (0, j))],
        out_specs=pl.BlockSpec((M, tn), lambda j: (0, j)),
        out_shape=jax.ShapeDtypeStruct((M, N), bf16),
        scratch_shapes=[pltpu.VMEM((M, D_MODEL), bf16)],
        compiler_params=_cparams("arbitrary"),
        name="mem_kv",
    )(mem2, gain, w)


def _xattn_kernel(q0_ref, q1_ref, q2_ref, q3_ref, kv_ref, o_ref):
    scale = 1.0 / math.sqrt(MEM_HEAD_DIM)
    for h, q_ref in enumerate((q0_ref, q1_ref, q2_ref, q3_ref)):
        c0 = h * MEM_HEAD_DIM
        k = kv_ref[0, :, c0:c0 + MEM_HEAD_DIM]
        v = kv_ref[0, :, MEM_WIDTH + c0:MEM_WIDTH + c0 + MEM_HEAD_DIM]
        logits = lax.dot_general(q_ref[0], k, (((1,), (1,)), ((), ())), preferred_element_type=f32) * scale
        m = jnp.max(logits, axis=-1, keepdims=True)
        p = jnp.exp(logits - m)
        ssum = jnp.sum(p, axis=-1, keepdims=True)
        o = jnp.dot(p.astype(bf16), v, preferred_element_type=f32) * (1.0 / ssum)
        o_ref[0, :, c0:c0 + MEM_HEAD_DIM] = o.astype(o_ref.dtype)


def _xattn(proj3, kv3, tq=1024):
    B, S, _ = proj3.shape
    qb0 = REST_QC // MEM_HEAD_DIM

    def qspec(h):
        return pl.BlockSpec((1, tq, MEM_HEAD_DIM), lambda b, t: (b, t, qb0 + h))

    return pl.pallas_call(
        _xattn_kernel,
        grid=(B, S // tq),
        in_specs=[qspec(0), qspec(1), qspec(2), qspec(3),
                  pl.BlockSpec((1, N_MEM, 2 * MEM_WIDTH), lambda b, t: (b, 0, 0))],
        out_specs=pl.BlockSpec((1, tq, MEM_WIDTH), lambda b, t: (b, t, 0)),
        out_shape=jax.ShapeDtypeStruct((B, S, MEM_WIDTH), bf16),
        compiler_params=_cparams("parallel", "parallel"),
        name="xattn",
    )(proj3, proj3, proj3, proj3, kv3)


def _merge_kernel(x_ref, g_ref, o0_ref, o1_ref, o2_ref, l0_ref, l1_ref, l2_ref, yl_ref, yc_ref,
                  wga_ref, wgb_ref, wgc_ref, bga_ref, bgb_ref, bgc_ref,
                  woa_ref, wol_ref, wom_ref, wout_ref, out_ref, h_scr, ya_scr,
                  o1_scr, o2_scr, l1_scr, l2_scr, *, tm):
    @pl.when(pl.program_id(1) == 0)
    def _():
        x = x_ref[...]
        h_scr[...] = _rms(x, g_ref[...]).astype(bf16)
        out_ref[...] = x
        for g, o_ref, l_ref, o_scr, l_scr in ((1, o1_ref, l1_ref, o1_scr, l1_scr),
                                              (2, o2_ref, l2_ref, o2_scr, l2_scr)):
            d = ATTN_GROUPS[g][1]
            for r in range(d):
                l_scr[pl.ds(r, tm // d, stride=d), :] = l_ref[0, r]
                for h in range(HEADS_PER_GROUP):
                    o_scr[h, pl.ds(r, tm // d, stride=d), :] = (
                        o_ref[0, r, :, h * HEAD_DIM_A:(h + 1) * HEAD_DIM_A].astype(f32))
        l0, l1, l2 = l0_ref[...], l1_scr[...], l2_scr[...]
        m = jnp.maximum(jnp.maximum(l0, l1), l2)
        e0, e1, e2 = jnp.exp(l0 - m), jnp.exp(l1 - m), jnp.exp(l2 - m)
        inv = 1.0 / (e0 + e1 + e2)
        for h in range(HEADS_PER_GROUP):
            c0 = h * HEAD_DIM_A
            lane = slice(h * LSE_REP, h * LSE_REP + 1)
            y = ((e0 * inv)[:, lane] * o0_ref[:, c0:c0 + HEAD_DIM_A].astype(f32)
                 + (e1 * inv)[:, lane] * o1_scr[h] + (e2 * inv)[:, lane] * o2_scr[h])
            ya_scr[:, c0:c0 + HEAD_DIM_A] = y.astype(bf16)

    h = h_scr[...]

    def gate(w_ref, b_ref):
        return jax.nn.sigmoid(jnp.dot(h, w_ref[...], preferred_element_type=f32) + b_ref[...])

    mixed = (gate(wga_ref, bga_ref) * jnp.dot(ya_scr[...], woa_ref[...], preferred_element_type=f32)
             + gate(wgb_ref, bgb_ref) * jnp.dot(yl_ref[...], wol_ref[...], preferred_element_type=f32)
             + gate(wgc_ref, bgc_ref) * jnp.dot(yc_ref[...], wom_ref[...], preferred_element_type=f32))
    out_ref[...] += jnp.dot(mixed.astype(bf16), wout_ref[...], preferred_element_type=f32)


def _merge(x2, gain, o_groups, lse_groups, y_lru, y_c, w_gate, b_gate, w_o_attn, w_o_lru, w_o_mem, w_out,
           seq, tm=512, tn=256):
    T = x2.shape[0]
    nj = D_MODEL // tn
    nt = seq // tm
    d1, d2 = ATTN_GROUPS[1][1], ATTN_GROUPS[2][1]

    def rows(width):
        return pl.BlockSpec((tm, width), lambda i, j: (i, 0))

    def strided_rows(d, width):
        return pl.BlockSpec((1, d, tm // d, width), lambda i, j: (i // nt, 0, i % nt, 0))

    def gate_w(k):
        return pl.BlockSpec((D_MODEL, tn), lambda i, j: (0, k * nj + j))

    def gate_b(k):
        return pl.BlockSpec((1, tn), lambda i, j: (0, k * nj + j))

    def cols(width):
        return pl.BlockSpec((width, tn), lambda i, j: (0, j))

    return pl.pallas_call(
        functools.partial(_merge_kernel, tm=tm),
        grid=(T // tm, nj),
        in_specs=[rows(D_MODEL), pl.BlockSpec((1, D_MODEL), lambda i, j: (0, 0)),
                  rows(GROUP_WIDTH), strided_rows(d1, GROUP_WIDTH), strided_rows(d2, GROUP_WIDTH),
                  rows(LSE_LANES), strided_rows(d1, LSE_LANES), strided_rows(d2, LSE_LANES),
                  rows(LRU_WIDTH), rows(MEM_WIDTH),
                  gate_w(0), gate_w(1), gate_w(2), gate_b(0), gate_b(1), gate_b(2),
                  cols(GROUP_WIDTH), cols(LRU_WIDTH), cols(MEM_WIDTH),
                  pl.BlockSpec((tn, D_MODEL), lambda i, j: (j, 0))],
        out_specs=pl.BlockSpec((tm, D_MODEL), lambda i, j: (i, 0)),
        out_shape=jax.ShapeDtypeStruct((T, D_MODEL), f32),
        scratch_shapes=[pltpu.VMEM((tm, D_MODEL), bf16), pltpu.VMEM((tm, GROUP_WIDTH), bf16),
                        pltpu.VMEM((HEADS_PER_GROUP, tm, HEAD_DIM_A), f32),
                        pltpu.VMEM((HEADS_PER_GROUP, tm, HEAD_DIM_A), f32),
                        pltpu.VMEM((tm, LSE_LANES), f32), pltpu.VMEM((tm, LSE_LANES), f32)],
        compiler_params=_cparams("parallel", "arbitrary"),
        name="merge",
    )(x2, gain, *o_groups, *lse_groups, y_lru, y_c, w_gate, w_gate, w_gate, b_gate, b_gate, b_gate,
      w_o_attn, w_o_lru, w_o_mem, w_out)


def _mlp_kernel(x_ref, g_ref, gf_ref, wu_ref, wd_ref, out_ref, h_scr):
    j = pl.program_id(1)

    @pl.when(j == 0)
    def _():
        x = x_ref[...]
        h_scr[...] = _rms(x, g_ref[...]).astype(bf16)
        out_ref[...] = x

    u = jnp.maximum(jnp.dot(h_scr[...], wu_ref[...], preferred_element_type=f32), 0.0)
    out_ref[...] += jnp.dot((u * u).astype(bf16), wd_ref[...], preferred_element_type=f32)

    @pl.when(j == pl.num_programs(1) - 1)
    def _():
        out_ref[...] = _rms(out_ref[...], gf_ref[...])


def _mlp(x2, gain, gain_final, w_up, w_down, tm=512, tf=1024):
    T = x2.shape[0]
    return pl.pallas_call(
        _mlp_kernel,
        grid=(T // tm, D_FF // tf),
        in_specs=[pl.BlockSpec((tm, D_MODEL), lambda i, j: (i, 0)),
                  pl.BlockSpec((1, D_MODEL), lambda i, j: (0, 0)),
                  pl.BlockSpec((1, D_MODEL), lambda i, j: (0, 0)),
                  pl.BlockSpec((D_MODEL, tf), lambda i, j: (0, j)),
                  pl.BlockSpec((tf, D_MODEL), lambda i, j: (j, 0))],
        out_specs=pl.BlockSpec((tm, D_MODEL), lambda i, j: (i, 0)),
        out_shape=jax.ShapeDtypeStruct((T, D_MODEL), f32),
        scratch_shapes=[pltpu.VMEM((tm, D_MODEL), bf16)],
        compiler_params=_cparams("parallel", "arbitrary"),
        name="mlp",
    )(x2, gain, gain_final, w_up, w_down)


def kernel(x, mem, rel_bias, norm_mix, norm_mem, norm_mlp, norm_final, w_in, w_gate, b_gate, conv_w, conv_b,
           lru_wa, lru_ba, lru_wi, lru_bi, lru_lambda, w_mem_kv, w_o_attn, w_o_lru, w_o_mem, w_out, w_up, w_down):
    B, S, D = x.shape
    T = B * S
    depth = w_in.shape[0]
    assert depth == 1, "the final RMSNorm is fused into the (single) layer's MLP kernel"
    x2 = x.reshape(T, D)
    mem2 = mem.reshape(B * N_MEM, D)
    for l in range(depth):
        gain_mix = norm_mix[l].reshape(1, D)
        qkv0, qkv1, qkv2, rest = _in_proj(x2, gain_mix, _pack_w_in(w_in[l]), B, S)
        proj3 = rest.reshape(B, S, REST_W)

        attn = []
        for g, qkv in enumerate((qkv0, qkv1, qkv2)):
            d = ATTN_GROUPS[g][1]
            o, lse = _attn_group(qkv.reshape(B * d, S // d, QKV_W), rel_bias, g)
            if g == 0:
                attn.append((o.reshape(T, GROUP_WIDTH), lse.reshape(T, LSE_LANES)))
            else:
                attn.append((o.reshape(B, d, S // d, GROUP_WIDTH), lse.reshape(B, d, S // d, LSE_LANES)))

        w_gates = jnp.concatenate([lru_wa[l, 0], lru_wi[l, 0], lru_wa[l, 1], lru_wi[l, 1]], axis=-1).astype(bf16)
        b_gates = jnp.concatenate([lru_ba[l, 0], lru_bi[l, 0], lru_ba[l, 1], lru_bi[l, 1]], axis=-1)
        y_lru = _lru(proj3, conv_w[l], conv_b[l].reshape(1, LRU_WIDTH), w_gates,
                     b_gates.reshape(LRU_BLOCKS, 1, 4 * LRU_BW), lru_lambda[l])

        kv = _mem_kv(mem2, norm_mem[l].reshape(1, D), w_mem_kv[l].astype(bf16))
        y_c = _xattn(proj3, kv.reshape(B, N_MEM, 2 * MEM_WIDTH))

        x2 = _merge(x2, gain_mix, [a[0] for a in attn], [a[1] for a in attn],
                    y_lru.reshape(T, LRU_WIDTH), y_c.reshape(T, MEM_WIDTH),
                    w_gate[l].astype(bf16), b_gate[l].reshape(1, 3 * D),
                    w_o_attn[l].astype(bf16), w_o_lru[l].astype(bf16), w_o_mem[l].astype(bf16),
                    w_out[l].astype(bf16), S)
        x2 = _mlp(x2, norm_mlp[l].reshape(1, D), norm_final.reshape(1, D), w_up[l].astype(bf16),
                  w_down[l].astype(bf16))
    return x2.reshape(B, S, D)
```

```python
import functools
import math

import jax
import jax.numpy as jnp
import numpy as np
from jax import lax
from jax.experimental import pallas as pl
from jax.experimental.pallas import tpu as pltpu

D_MODEL = 2048
HEAD_DIM_A = 128
ATTN_GROUPS = ((128, 1), (512, 4), (2048, 16))
HEADS_PER_GROUP = 4
GROUP_WIDTH = HEADS_PER_GROUP * HEAD_DIM_A
WIDTH_A = len(ATTN_GROUPS) * GROUP_WIDTH
ATTN_RADIUS = 64
N_BUCKETS = 32
MAX_DISTANCE = 1024
LRU_WIDTH = 1536
LRU_BLOCKS = 12
LRU_BW = 128
LRU_C = 8.0
N_MEM = 256
MEM_HEADS = 4
MEM_HEAD_DIM = 256
MEM_WIDTH = MEM_HEADS * MEM_HEAD_DIM
D_FF = 4 * D_MODEL
EPS = 1e-6
N_IN = 3 * WIDTH_A + 2 * LRU_WIDTH + MEM_WIDTH
COL_K = WIDTH_A
COL_V = 2 * WIDTH_A
COL_XB = 3 * WIDTH_A
COL_YB = 3 * WIDTH_A + LRU_WIDTH
COL_QC = 3 * WIDTH_A + 2 * LRU_WIDTH
NEG_INF = -1e30

SUB_Q = 128
SUB_K = SUB_Q + 2 * ATTN_RADIUS
LSE_LANES = 128
LSE_REP = LSE_LANES // HEADS_PER_GROUP

VMEM_LIMIT = 56 * 1024 * 1024

f32 = jnp.float32
bf16 = jnp.bfloat16


def _cparams(*sem):
    return pltpu.CompilerParams(dimension_semantics=sem, vmem_limit_bytes=VMEM_LIMIT)


def _rms(x, gain):
    return x * lax.rsqrt(jnp.mean(x * x, axis=-1, keepdims=True) + EPS) * gain


QKV_W = 3 * GROUP_WIDTH
REST_W = 2 * LRU_WIDTH + MEM_WIDTH
REST_YB = LRU_WIDTH
REST_QC = 2 * LRU_WIDTH
PROJ_TN = GROUP_WIDTH
QKV_TILES = QKV_W // PROJ_TN
LANES = 128
SLABS = PROJ_TN // LANES


def _pack_w_in(w):
    cols = []
    for g in range(len(ATTN_GROUPS)):
        for base in (0, COL_K, COL_V):
            cols.append(w[:, base + g * GROUP_WIDTH:base + (g + 1) * GROUP_WIDTH])
    cols.append(w[:, COL_XB:])
    return jnp.concatenate(cols, axis=1).astype(bf16)


def _in_proj_kernel(x_ref, g_ref, w_ref, q0_ref, q1_ref, q2_ref, rest_ref, h_scr, res_scr, *, tm):
    j = pl.program_id(1)

    @pl.when(j == 0)
    def _():
        h_scr[...] = _rms(x_ref[...], g_ref[...]).astype(bf16)

    res = jnp.dot(h_scr[...], w_ref[...], preferred_element_type=f32)

    @pl.when(j < QKV_TILES)
    def _():
        q0_ref[0] = res.astype(bf16)

    for g, q_ref in ((1, q1_ref), (2, q2_ref)):
        d = ATTN_GROUPS[g][1]

        @pl.when((j >= g * QKV_TILES) & (j < (g + 1) * QKV_TILES))
        def _():
            for c in range(SLABS):
                res_scr[c] = res[:, c * LANES:(c + 1) * LANES]
            for r in range(d):
                for c in range(SLABS):
                    q_ref[0, r, :, c * LANES:(c + 1) * LANES] = (
                        res_scr[c, pl.ds(r, tm // d, stride=d), :].astype(bf16))

    @pl.when(j >= 3 * QKV_TILES)
    def _():
        rest_ref[...] = res.astype(bf16)


def _in_proj(x2, gain, w, batch, seq, tm=1024):
    T = x2.shape[0]
    tn = PROJ_TN
    nt = seq // tm
    d1, d2 = ATTN_GROUPS[1][1], ATTN_GROUPS[2][1]

    def qcol(j, g):
        return jnp.clip(j - g * QKV_TILES, 0, QKV_TILES - 1)

    return pl.pallas_call(
        functools.partial(_in_proj_kernel, tm=tm),
        grid=(T // tm, N_IN // tn),
        in_specs=[
            pl.BlockSpec((tm, D_MODEL), lambda i, j: (i, 0)),
            pl.BlockSpec((1, D_MODEL), lambda i, j: (0, 0)),
            pl.BlockSpec((D_MODEL, tn), lambda i, j: (0, j)),
        ],
        out_specs=[
            pl.BlockSpec((1, tm, tn), lambda i, j: (i // nt, i % nt, qcol(j, 0))),
            pl.BlockSpec((1, d1, tm // d1, tn), lambda i, j: (i // nt, 0, i % nt, qcol(j, 1))),
            pl.BlockSpec((1, d2, tm // d2, tn), lambda i, j: (i // nt, 0, i % nt, qcol(j, 2))),
            pl.BlockSpec((tm, tn), lambda i, j: (i, jnp.maximum(j - 3 * QKV_TILES, 0))),
        ],
        out_shape=[
            jax.ShapeDtypeStruct((batch, seq, QKV_W), bf16),
            jax.ShapeDtypeStruct((batch, d1, seq // d1, QKV_W), bf16),
            jax.ShapeDtypeStruct((batch, d2, seq // d2, QKV_W), bf16),
            jax.ShapeDtypeStruct((T, REST_W), bf16),
        ],
        scratch_shapes=[pltpu.VMEM((tm, D_MODEL), bf16), pltpu.VMEM((SLABS, tm, LANES), f32)],
        compiler_params=_cparams("parallel", "arbitrary"),
        name="in_proj",
    )(x2, gain, w)


def _t5_bucket(rel):
    nb = N_BUCKETS // 2
    max_exact = nb // 2
    sign = (rel > 0).astype(np.int32) * nb
    n = np.abs(rel)
    large = max_exact + (np.log(np.maximum(n, 1) / max_exact)
                         / np.log(MAX_DISTANCE / max_exact) * (nb - max_exact)).astype(np.int32)
    large = np.minimum(large, nb - 1)
    return (sign + np.where(n < max_exact, n, large)).astype(np.int32)


def _band_bias(rel_bias_g, dilation):
    qq = np.arange(SUB_Q)[:, None]
    kk = np.arange(SUB_K)[None, :]
    rel = kk - ATTN_RADIUS - qq
    onehot = (_t5_bucket(rel * dilation)[None] == np.arange(N_BUCKETS)[:, None, None]).astype(np.float32)
    bias = jnp.einsum('nh,nqk->hqk', rel_bias_g.astype(f32), onehot, precision=lax.Precision.HIGHEST)
    return bias + np.where(np.abs(rel) <= ATTN_RADIUS, 0.0, NEG_INF).astype(np.float32)[None]


def _attn_kernel(q_ref, kp_ref, km_ref, kn_ref, vp_ref, vm_ref, vn_ref, bias_ref,
                 o_ref, lse_ref, kbuf, vbuf, *, tq, seq):
    R = ATTN_RADIUS
    kbuf[0:R] = kp_ref[0]
    kbuf[R:R + tq] = km_ref[0]
    kbuf[R + tq:] = kn_ref[0]
    vbuf[0:R] = vp_ref[0]
    vbuf[R:R + tq] = vm_ref[0]
    vbuf[R + tq:] = vn_ref[0]
    q0 = pl.program_id(1) * tq
    scale = 1.0 / math.sqrt(HEAD_DIM_A)
    lane = lax.broadcasted_iota(jnp.int32, (SUB_Q, LSE_LANES), 1)
    for s in range(tq // SUB_Q):
        r0 = s * SUB_Q
        pos = q0 + (r0 - R) + lax.broadcasted_iota(jnp.int32, (1, SUB_K), 1)
        edge = jnp.where(pos >= 0, jnp.where(pos < seq, 0.0, NEG_INF), NEG_INF)
        lse_tile = None
        for h in range(HEADS_PER_GROUP):
            c0 = h * HEAD_DIM_A
            q = q_ref[0, r0:r0 + SUB_Q, c0:c0 + HEAD_DIM_A]
            k = kbuf[r0:r0 + SUB_K, c0:c0 + HEAD_DIM_A]
            v = vbuf[r0:r0 + SUB_K, c0:c0 + HEAD_DIM_A]
            logits = lax.dot_general(q, k, (((1,), (1,)), ((), ())), preferred_element_type=f32)
            logits = logits * scale + bias_ref[h] + edge
            m = jnp.max(logits, axis=-1, keepdims=True)
            p = jnp.exp(logits - m)
            ssum = jnp.sum(p, axis=-1, keepdims=True)
            o = jnp.dot(p.astype(bf16), v, preferred_element_type=f32) * (1.0 / ssum)
            o_ref[0, r0:r0 + SUB_Q, c0:c0 + HEAD_DIM_A] = o.astype(o_ref.dtype)
            lse = m + jnp.log(ssum)
            lse_tile = lse if lse_tile is None else jnp.where(lane >= h * LSE_REP, lse, lse_tile)
        lse_ref[0, r0:r0 + SUB_Q, :] = jnp.broadcast_to(lse_tile, (SUB_Q, LSE_LANES))


def _attn_group(qkv, rel_bias, g):
    _, d = ATTN_GROUPS[g]
    n, L, _ = qkv.shape
    tq = min(512, L)
    R = ATTN_RADIUS
    bias = _band_bias(rel_bias[:, g * HEADS_PER_GROUP:(g + 1) * HEADS_PER_GROUP], d)
    rb = tq // R
    last_rb = L // R - 1

    def main(col):
        return pl.BlockSpec((1, tq, GROUP_WIDTH), lambda b, t: (b, t, col))

    def prev(col):
        return pl.BlockSpec((1, R, GROUP_WIDTH), lambda b, t: (b, jnp.maximum(t * rb - 1, 0), col))

    def nxt(col):
        return pl.BlockSpec((1, R, GROUP_WIDTH), lambda b, t: (b, jnp.minimum((t + 1) * rb, last_rb), col))

    return pl.pallas_call(
        functools.partial(_attn_kernel, tq=tq, seq=L),
        grid=(n, L // tq),
        in_specs=[main(0), prev(1), main(1), nxt(1), prev(2), main(2), nxt(2),
                  pl.BlockSpec((HEADS_PER_GROUP, SUB_Q, SUB_K), lambda b, t: (0, 0, 0))],
        out_specs=[pl.BlockSpec((1, tq, GROUP_WIDTH), lambda b, t: (b, t, 0)),
                   pl.BlockSpec((1, tq, LSE_LANES), lambda b, t: (b, t, 0))],
        out_shape=[jax.ShapeDtypeStruct((n, L, GROUP_WIDTH), bf16),
                   jax.ShapeDtypeStruct((n, L, LSE_LANES), f32)],
        scratch_shapes=[pltpu.VMEM((tq + 2 * R, GROUP_WIDTH), bf16),
                        pltpu.VMEM((tq + 2 * R, GROUP_WIDTH), bf16)],
        compiler_params=_cparams("parallel", "arbitrary"),
        name=f"attn_g{g}",
    )(qkv, qkv, qkv, qkv, qkv, qkv, qkv, bias)


LRU_CHUNK = 256
LRU_PAD = 8
LRU_FINISH_ROWS = 512
LRU_SEGS = 8
SEG_GAP = 8


def _lru_kernel(xb_ref, yb_ref, cw_ref, cb_ref, w_ref, gb_ref, lam_ref, o_ref,
                xpad, af, bf, ab, bb, htf, ptf, htb, ptb, cf_scr, cb_scr, *, seq):
    R = LRU_CHUNK
    P = LRU_PAD
    seg_len = seq // LRU_SEGS
    pitch = seg_len + SEG_GAP
    chunks_per_seg = seg_len // R
    n_chunks = seq // R
    xpad[0:P] = jnp.zeros((P, LRU_BW), f32)
    xpad[P + seq:] = jnp.zeros((P, LRU_BW), f32)
    xpad[P:P + seq] = xb_ref[0].astype(f32)
    lam = lam_ref[...]
    log_a_unit = -LRU_C * (jnp.maximum(-lam, 0.0) + jnp.log1p(jnp.exp(-jnp.abs(lam))))
    cw = cw_ref[...]
    cb = cb_ref[...]
    gate_bias = gb_ref[0]
    row = lax.broadcasted_iota(jnp.int32, (R, LRU_BW), 0)

    def chunk(ci, first=False, last=False):
        c0 = ci * R if isinstance(ci, int) else pl.multiple_of(ci * R, R)
        dst = (ci // chunks_per_seg) * pitch + (ci % chunks_per_seg) * R
        dst = dst if isinstance(ci, int) else pl.multiple_of(dst, 8)
        xc = (cw[0:1] * xpad[pl.ds(c0 + (P - 1), R), :] + cw[1:2] * xpad[pl.ds(c0 + P, R), :]
              + cw[2:3] * xpad[pl.ds(c0 + (P + 1), R), :] + cw[3:4] * xpad[pl.ds(c0 + (P + 2), R), :]) + cb
        th = jnp.tanh(jnp.dot(xc.astype(bf16), w_ref[0], preferred_element_type=f32) + gate_bias)
        half_xc = 0.5 * xc
        for direction, (a_scr, b_scr) in enumerate(((af, bf), (ab, bb))):
            base = direction * 2 * LRU_BW
            half_log_a = 0.5 * log_a_unit[direction:direction + 1]
            a = jnp.exp(half_log_a * th[:, base:base + LRU_BW] + half_log_a)
            gated_x = half_xc * th[:, base + LRU_BW:base + 2 * LRU_BW] + half_xc
            y = 1.0 - a * a
            mult = y * lax.rsqrt(jnp.maximum(y, 1e-30))
            if direction == 0 and first:
                mult = jnp.where(row == 0, 1.0, mult)
            if direction == 1 and last:
                mult = jnp.where(row == R - 1, 1.0, mult)
            a_scr[pl.ds(dst, R), :] = a
            b_scr[pl.ds(dst, R), :] = mult * gated_x

    chunk(0, first=True)
    lax.fori_loop(1, n_chunks - 1, lambda ci, c: (chunk(ci), c)[1], 0, unroll=2)
    chunk(n_chunks - 1, last=True)

    def scan(i, carry):
        hf, pf, hb, pb = carry
        a = af[pl.ds(i, LRU_SEGS, stride=pitch), :]
        hf = a * hf + bf[pl.ds(i, LRU_SEGS, stride=pitch), :]
        pf = a * pf
        htf[pl.ds(pl.multiple_of(i * LRU_SEGS, LRU_SEGS), LRU_SEGS), :] = hf
        ptf[pl.ds(pl.multiple_of(i * LRU_SEGS, LRU_SEGS), LRU_SEGS), :] = pf
        ib = seg_len - 1 - i
        a = ab[pl.ds(ib, LRU_SEGS, stride=pitch), :]
        hb = a * hb + bb[pl.ds(ib, LRU_SEGS, stride=pitch), :]
        pb = a * pb
        htb[pl.ds(pl.multiple_of(ib * LRU_SEGS, LRU_SEGS), LRU_SEGS), :] = hb
        ptb[pl.ds(pl.multiple_of(ib * LRU_SEGS, LRU_SEGS), LRU_SEGS), :] = pb
        return hf, pf, hb, pb

    zero = jnp.zeros((LRU_SEGS, LRU_BW), f32)
    one = jnp.ones((LRU_SEGS, LRU_BW), f32)
    hf, pf, hb, pb = lax.fori_loop(0, seg_len, scan, (zero, one, zero, one), unroll=8)

    c = jnp.zeros((1, LRU_BW), f32)
    cf_scr[0:1] = c
    for j in range(1, LRU_SEGS):
        c = hf[j - 1:j] + pf[j - 1:j] * c
        cf_scr[j:j + 1] = c
    c = jnp.zeros((1, LRU_BW), f32)
    cb_scr[LRU_SEGS - 1:LRU_SEGS] = c
    for j in range(LRU_SEGS - 2, -1, -1):
        c = hb[j + 1:j + 2] + pb[j + 1:j + 2] * c
        cb_scr[j:j + 1] = c

    F = LRU_FINISH_ROWS
    finish_per_seg = seg_len // F

    def finish(ci, carry):
        c0 = pl.multiple_of(ci * F, F)
        seg = ci // finish_per_seg
        src = (ci % finish_per_seg) * (F * LRU_SEGS) + seg
        h = (htf[pl.ds(src, F, stride=LRU_SEGS), :] + ptf[pl.ds(src, F, stride=LRU_SEGS), :] * cf_scr[pl.ds(seg, 1), :]
             + htb[pl.ds(src, F, stride=LRU_SEGS), :] + ptb[pl.ds(src, F, stride=LRU_SEGS), :] * cb_scr[pl.ds(seg, 1), :])
        y = yb_ref[0, pl.ds(c0, F), :].astype(f32)
        gelu = y * (0.5 * (1.0 + jnp.tanh(math.sqrt(2.0 / math.pi) * (y + 0.044715 * (y * y * y)))))
        o_ref[0, pl.ds(c0, F), :] = (h * gelu).astype(o_ref.dtype)
        return carry

    lax.fori_loop(0, seq // F, finish, 0)


def _lru(proj3, conv_w, conv_b, w_gates, b_gates, lam):
    B, S, _ = proj3.shape
    xb0 = 0
    yb0 = REST_YB // LRU_BW
    return pl.pallas_call(
        functools.partial(_lru_kernel, seq=S),
        grid=(B, LRU_BLOCKS),
        in_specs=[
            pl.BlockSpec((1, S, LRU_BW), lambda b, n: (b, 0, xb0 + n)),
            pl.BlockSpec((1, S, LRU_BW), lambda b, n: (b, 0, yb0 + n)),
            pl.BlockSpec((4, LRU_BW), lambda b, n: (0, n)),
            pl.BlockSpec((1, LRU_BW), lambda b, n: (0, n)),
            pl.BlockSpec((1, LRU_BW, 4 * LRU_BW), lambda b, n: (n, 0, 0)),
            pl.BlockSpec((1, 1, 4 * LRU_BW), lambda b, n: (n, 0, 0)),
            pl.BlockSpec((2, LRU_BW), lambda b, n: (0, n)),
        ],
        out_specs=pl.BlockSpec((1, S, LRU_BW), lambda b, n: (b, 0, n)),
        out_shape=jax.ShapeDtypeStruct((B, S, LRU_WIDTH), bf16),
        scratch_shapes=([pltpu.VMEM((S + 2 * LRU_PAD, LRU_BW), f32)]
                        + [pltpu.VMEM((S + LRU_SEGS * SEG_GAP, LRU_BW), f32)] * 4
                        + [pltpu.VMEM((S, LRU_BW), f32)] * 4
                        + [pltpu.VMEM((LRU_SEGS, LRU_BW), f32)] * 2),
        compiler_params=_cparams("parallel", "parallel"),
        name="lru",
    )(proj3, proj3, conv_w, conv_b, w_gates, b_gates, lam)


def _mem_kv_kernel(m_ref, g_ref, w_ref, o_ref, h_scr):
    @pl.when(pl.program_id(0) == 0)
    def _():
        h_scr[...] = _rms(m_ref[...], g_ref[...]).astype(bf16)

    o_ref[...] = jnp.dot(h_scr[...], w_ref[...], preferred_element_type=f32).astype(o_ref.dtype)


def _mem_kv(mem2, gain, w, tn=512):
    M = mem2.shape[0]
    N = w.shape[1]
    return pl.pallas_call(
        _mem_kv_kernel,
        grid=(N // tn,),
        in_specs=[pl.BlockSpec((M, D_MODEL), lambda j: (0, 0)),
                  pl.BlockSpec((1, D_MODEL), lambda j: (0, 0)),
                  pl.BlockSpec((D_MODEL, tn), lambda j: (0, j))],
        out_specs=pl.BlockSpec((M, tn), lambda j: (0, j)),
        out_shape=jax.ShapeDtypeStruct((M, N), bf16),
        scratch_shapes=[pltpu.VMEM((M, D_MODEL), bf16)],
        compiler_params=_cparams("arbitrary"),
        name="mem_kv",
    )(mem2, gain, w)


def _xattn_kernel(q0_ref, q1_ref, q2_ref, q3_ref, kv_ref, o_ref):
    scale = 1.0 / math.sqrt(MEM_HEAD_DIM)
    for h, q_ref in enumerate((q0_ref, q1_ref, q2_ref, q3_ref)):
        c0 = h * MEM_HEAD_DIM
        k = kv_ref[0, :, c0:c0 + MEM_HEAD_DIM]
        v = kv_ref[0, :, MEM_WIDTH + c0:MEM_WIDTH + c0 + MEM_HEAD_DIM]
        logits = lax.dot_general(q_ref[0], k, (((1,), (1,)), ((), ())), preferred_element_type=f32) * scale
        m = jnp.max(logits, axis=-1, keepdims=True)
        p = jnp.exp(logits - m)
        ssum = jnp.sum(p, axis=-1, keepdims=True)
        o = jnp.dot(p.astype(bf16), v, preferred_element_type=f32) * (1.0 / ssum)
        o_ref[0, :, c0:c0 + MEM_HEAD_DIM] = o.astype(o_ref.dtype)


def _xattn(proj3, kv3, tq=1024):
    B, S, _ = proj3.shape
    qb0 = REST_QC // MEM_HEAD_DIM

    def qspec(h):
        return pl.BlockSpec((1, tq, MEM_HEAD_DIM), lambda b, t: (b, t, qb0 + h))

    return pl.pallas_call(
        _xattn_kernel,
        grid=(B, S // tq),
        in_specs=[qspec(0), qspec(1), qspec(2), qspec(3),
                  pl.BlockSpec((1, N_MEM, 2 * MEM_WIDTH), lambda b, t: (b, 0, 0))],
        out_specs=pl.BlockSpec((1, tq, MEM_WIDTH), lambda b, t: (b, t, 0)),
        out_shape=jax.ShapeDtypeStruct((B, S, MEM_WIDTH), bf16),
        compiler_params=_cparams("parallel", "parallel"),
        name="xattn",
    )(proj3, proj3, proj3, proj3, kv3)


def _merge_kernel(x_ref, g_ref, o0_ref, o1_ref, o2_ref, l0_ref, l1_ref, l2_ref, yl_ref, yc_ref,
                  wga_ref, wgb_ref, wgc_ref, bga_ref, bgb_ref, bgc_ref,
                  woa_ref, wol_ref, wom_ref, wout_ref, out_ref, h_scr, ya_scr,
                  o1_scr, o2_scr, l1_scr, l2_scr, *, tm):
    @pl.when(pl.program_id(1) == 0)
    def _():
        x = x_ref[...]
        h_scr[...] = _rms(x, g_ref[...]).astype(bf16)
        out_ref[...] = x
        for g, o_ref, l_ref, o_scr, l_scr in ((1, o1_ref, l1_ref, o1_scr, l1_scr),
                                              (2, o2_ref, l2_ref, o2_scr, l2_scr)):
            d = ATTN_GROUPS[g][1]
            for r in range(d):
                l_scr[pl.ds(r, tm // d, stride=d), :] = l_ref[0, r]
                for h in range(HEADS_PER_GROUP):
                    o_scr[h, pl.ds(r, tm // d, stride=d), :] = (
                        o_ref[0, r, :, h * HEAD_DIM_A:(h + 1) * HEAD_DIM_A].astype(f32))
        l0, l1, l2 = l0_ref[...], l1_scr[...], l2_scr[...]
        m = jnp.maximum(jnp.maximum(l0, l1), l2)
        e0, e1, e2 = jnp.exp(l0 - m), jnp.exp(l1 - m), jnp.exp(l2 - m)
        inv = 1.0 / (e0 + e1 + e2)
        for h in range(HEADS_PER_GROUP):
            c0 = h * HEAD_DIM_A
            lane = slice(h * LSE_REP, h * LSE_REP + 1)
            y = ((e0 * inv)[:, lane] * o0_ref[:, c0:c0 + HEAD_DIM_A].astype(f32)
                 + (e1 * inv)[:, lane] * o1_scr[h] + (e2 * inv)[:, lane] * o2_scr[h])
            ya_scr[:, c0:c0 + HEAD_DIM_A] = y.astype(bf16)

    h = h_scr[...]

    def gate(w_ref, b_ref):
        return jax.nn.sigmoid(jnp.dot(h, w_ref[...], preferred_element_type=f32) + b_ref[...])

    mixed = (gate(wga_ref, bga_ref) * jnp.dot(ya_scr[...], woa_ref[...], preferred_element_type=f32)
             + gate(wgb_ref, bgb_ref) * jnp.dot(yl_ref[...], wol_ref[...], preferred_element_type=f32)
             + gate(wgc_ref, bgc_ref) * jnp.dot(yc_ref[...], wom_ref[...], preferred_element_type=f32))
    out_ref[...] += jnp.dot(mixed.astype(bf16), wout_ref[...], preferred_element_type=f32)


def _merge(x2, gain, o_groups, lse_groups, y_lru, y_c, w_gate, b_gate, w_o_attn, w_o_lru, w_o_mem, w_out,
           seq, tm=512, tn=256):
    T = x2.shape[0]
    nj = D_MODEL // tn
    nt = seq // tm
    d1, d2 = ATTN_GROUPS[1][1], ATTN_GROUPS[2][1]

    def rows(width):
        return pl.BlockSpec((tm, width), lambda i, j: (i, 0))

    def strided_rows(d, width):
        return pl.BlockSpec((1, d, tm // d, width), lambda i, j: (i // nt, 0, i % nt, 0))

    def gate_w(k):
        return pl.BlockSpec((D_MODEL, tn), lambda i, j: (0, k * nj + j))

    def gate_b(k):
        return pl.BlockSpec((1, tn), lambda i, j: (0, k * nj + j))

    def cols(width):
        return pl.BlockSpec((width, tn), lambda i, j: (0, j))

    return pl.pallas_call(
        functools.partial(_merge_kernel, tm=tm),
        grid=(T // tm, nj),
        in_specs=[rows(D_MODEL), pl.BlockSpec((1, D_MODEL), lambda i, j: (0, 0)),
                  rows(GROUP_WIDTH), strided_rows(d1, GROUP_WIDTH), strided_rows(d2, GROUP_WIDTH),
                  rows(LSE_LANES), strided_rows(d1, LSE_LANES), strided_rows(d2, LSE_LANES),
                  rows(LRU_WIDTH), rows(MEM_WIDTH),
                  gate_w(0), gate_w(1), gate_w(2), gate_b(0), gate_b(1), gate_b(2),
                  cols(GROUP_WIDTH), cols(LRU_WIDTH), cols(MEM_WIDTH),
                  pl.BlockSpec((tn, D_MODEL), lambda i, j: (j, 0))],
        out_specs=pl.BlockSpec((tm, D_MODEL), lambda i, j: (i, 0)),
        out_shape=jax.ShapeDtypeStruct((T, D_MODEL), f32),
        scratch_shapes=[pltpu.VMEM((tm, D_MODEL), bf16), pltpu.VMEM((tm, GROUP_WIDTH), bf16),
                        pltpu.VMEM((HEADS_PER_GROUP, tm, HEAD_DIM_A), f32),
                        pltpu.VMEM((HEADS_PER_GROUP, tm, HEAD_DIM_A), f32),
                        pltpu.VMEM((tm, LSE_LANES), f32), pltpu.VMEM((tm, LSE_LANES), f32)],
        compiler_params=_cparams("parallel", "arbitrary"),
        name="merge",
    )(x2, gain, *o_groups, *lse_groups, y_lru, y_c, w_gate, w_gate, w_gate, b_gate, b_gate, b_gate,
      w_o_attn, w_o_lru, w_o_mem, w_out)


def _mlp_kernel(x_ref, g_ref, gf_ref, wu_ref, wd_ref, out_ref, h_scr):
    j = pl.program_id(1)

    @pl.when(j == 0)
    def _():
        x = x_ref[...]
        h_scr[...] = _rms(x, g_ref[...]).astype(bf16)
        out_ref[...] = x

    u = jnp.maximum(jnp.dot(h_scr[...], wu_ref[...], preferred_element_type=f32), 0.0)
    out_ref[...] += jnp.dot((u * u).astype(bf16), wd_ref[...], preferred_element_type=f32)

    @pl.when(j == pl.num_programs(1) - 1)
    def _():
        out_ref[...] = _rms(out_ref[...], gf_ref[...])


def _mlp(x2, gain, gain_final, w_up, w_down, tm=512, tf=1024):
    T = x2.shape[0]
    return pl.pallas_call(
        _mlp_kernel,
        grid=(T // tm, D_FF // tf),
        in_specs=[pl.BlockSpec((tm, D_MODEL), lambda i, j: (i, 0)),
                  pl.BlockSpec((1, D_MODEL), lambda i, j: (0, 0)),
                  pl.BlockSpec((1, D_MODEL), lambda i, j: (0, 0)),
                  pl.BlockSpec((D_MODEL, tf), lambda i, j: (0, j)),
                  pl.BlockSpec((tf, D_MODEL), lambda i, j: (j, 0))],
        out_specs=pl.BlockSpec((tm, D_MODEL), lambda i, j: (i, 0)),
        out_shape=jax.ShapeDtypeStruct((T, D_MODEL), f32),
        scratch_shapes=[pltpu.VMEM((tm, D_MODEL), bf16)],
        compiler_params=_cparams("parallel", "arbitrary"),
        name="mlp",
    )(x2, gain, gain_final, w_up, w_down)


def kernel(x, mem, rel_bias, norm_mix, norm_mem, norm_mlp, norm_final, w_in, w_gate, b_gate, conv_w, conv_b,
           lru_wa, lru_ba, lru_wi, lru_bi, lru_lambda, w_mem_kv, w_o_attn, w_o_lru, w_o_mem, w_out, w_up, w_down):
    B, S, D = x.shape
    T = B * S
    depth = w_in.shape[0]
    assert depth == 1, "the final RMSNorm is fused into the (single) layer's MLP kernel"
    x2 = x.reshape(T, D)
    mem2 = mem.reshape(B * N_MEM, D)
    for l in range(depth):
        gain_mix = norm_mix[l].reshape(1, D)
        qkv0, qkv1, qkv2, rest = _in_proj(x2, gain_mix, _pack_w_in(w_in[l]), B, S)
        proj3 = rest.reshape(B, S, REST_W)

        attn = []
        for g, qkv in enumerate((qkv0, qkv1, qkv2)):
            d = ATTN_GROUPS[g][1]
            o, lse = _attn_group(qkv.reshape(B * d, S // d, QKV_W), rel_bias, g)
            if g == 0:
                attn.append((o.reshape(T, GROUP_WIDTH), lse.reshape(T, LSE_LANES)))
            else:
                attn.append((o.reshape(B, d, S // d, GROUP_WIDTH), lse.reshape(B, d, S // d, LSE_LANES)))

        w_gates = (0.5 * jnp.concatenate([lru_wa[l, 0], lru_wi[l, 0], lru_wa[l, 1], lru_wi[l, 1]], axis=-1)
                   ).astype(bf16)
        b_gates = 0.5 * jnp.concatenate([lru_ba[l, 0], lru_bi[l, 0], lru_ba[l, 1], lru_bi[l, 1]], axis=-1)
        y_lru = _lru(proj3, conv_w[l], conv_b[l].reshape(1, LRU_WIDTH), w_gates,
                     b_gates.reshape(LRU_BLOCKS, 1, 4 * LRU_BW), lru_lambda[l])

        kv = _mem_kv(mem2, norm_mem[l].reshape(1, D), w_mem_kv[l].astype(bf16))
        y_c = _xattn(proj3, kv.reshape(B, N_MEM, 2 * MEM_WIDTH))

        x2 = _merge(x2, gain_mix, [a[0] for a in attn], [a[1] for a in attn],
                    y_lru.reshape(T, LRU_WIDTH), y_c.reshape(T, MEM_WIDTH),
                    w_gate[l].astype(bf16), b_gate[l].reshape(1, 3 * D),
                    w_o_attn[l].astype(bf16), w_o_lru[l].astype(bf16), w_o_mem[l].astype(bf16),
                    w_out[l].astype(bf16), S)
        x2 = _mlp(x2, norm_mlp[l].reshape(1, D), norm_final.reshape(1, D), w_up[l].astype(bf16),
                  w_down[l].astype(bf16))
    return x2.reshape(B, S, D)
```

```python
import functools
import math

import jax
import jax.numpy as jnp
import numpy as np
from jax import lax
from jax.experimental import pallas as pl
from jax.experimental.pallas import tpu as pltpu

D_MODEL = 2048
HEAD_DIM_A = 128
ATTN_GROUPS = ((128, 1), (512, 4), (2048, 16))
HEADS_PER_GROUP = 4
GROUP_WIDTH = HEADS_PER_GROUP * HEAD_DIM_A
WIDTH_A = len(ATTN_GROUPS) * GROUP_WIDTH
ATTN_RADIUS = 64
N_BUCKETS = 32
MAX_DISTANCE = 1024
LRU_WIDTH = 1536
LRU_BLOCKS = 12
LRU_BW = 128
LRU_C = 8.0
N_MEM = 256
MEM_HEADS = 4
MEM_HEAD_DIM = 256
MEM_WIDTH = MEM_HEADS * MEM_HEAD_DIM
D_FF = 4 * D_MODEL
EPS = 1e-6
N_IN = 3 * WIDTH_A + 2 * LRU_WIDTH + MEM_WIDTH
COL_K = WIDTH_A
COL_V = 2 * WIDTH_A
COL_XB = 3 * WIDTH_A
COL_YB = 3 * WIDTH_A + LRU_WIDTH
COL_QC = 3 * WIDTH_A + 2 * LRU_WIDTH
NEG_INF = -1e30

SUB_Q = 128
SUB_K = SUB_Q + 2 * ATTN_RADIUS
LSE_LANES = 128
LSE_REP = LSE_LANES // HEADS_PER_GROUP

VMEM_LIMIT = 56 * 1024 * 1024

f32 = jnp.float32
bf16 = jnp.bfloat16


def _cparams(*sem):
    return pltpu.CompilerParams(dimension_semantics=sem, vmem_limit_bytes=VMEM_LIMIT)


def _rms(x, gain):
    return x * lax.rsqrt(jnp.mean(x * x, axis=-1, keepdims=True) + EPS) * gain


QKV_W = 3 * GROUP_WIDTH
REST_W = 2 * LRU_WIDTH + MEM_WIDTH
REST_YB = LRU_WIDTH
REST_QC = 2 * LRU_WIDTH
PROJ_TN = GROUP_WIDTH
QKV_TILES = QKV_W // PROJ_TN
LANES = 128
SLABS = PROJ_TN // LANES
PROJ_ROW_BLOCKS = 2
DEINTERLEAVE_STEP = 4


def _pack_w_in(w):
    cols = []
    for g in range(len(ATTN_GROUPS)):
        for base in (0, COL_K, COL_V):
            cols.append(w[:, base + g * GROUP_WIDTH:base + (g + 1) * GROUP_WIDTH])
    cols.append(w[:, COL_XB:])
    return jnp.concatenate(cols, axis=1).astype(bf16)


def _in_proj_kernel(x_ref, g_ref, w_ref, q0_ref, q1_ref, q2_ref, rest_ref, h_scr, res_scr, tmp_scr, *, tm):
    j = pl.program_id(1)

    @pl.when(j == 0)
    def _():
        h_scr[...] = _rms(x_ref[...], g_ref[...]).astype(bf16)

    mb = tm // PROJ_ROW_BLOCKS

    def block_dot(k):
        return jnp.dot(h_scr[k * mb:(k + 1) * mb, :], w_ref[...], preferred_element_type=f32)

    @pl.when(j < QKV_TILES)
    def _():
        for k in range(PROJ_ROW_BLOCKS):
            q0_ref[0, k * mb:(k + 1) * mb, :] = block_dot(k).astype(bf16)

    for g, q_ref in ((1, q1_ref), (2, q2_ref)):
        d = ATTN_GROUPS[g][1]

        @pl.when((j >= g * QKV_TILES) & (j < (g + 1) * QKV_TILES))
        def _():
            for k in range(PROJ_ROW_BLOCKS):
                res = block_dot(k)
                for c in range(SLABS):
                    res_scr[k, c] = res[:, c * LANES:(c + 1) * LANES]
                src, step = res_scr, d
                if d == DEINTERLEAVE_STEP ** 2:
                    step = DEINTERLEAVE_STEP
                    for r in range(step):
                        for c in range(SLABS):
                            tmp_scr[k, c, r * (mb // step):(r + 1) * (mb // step), :] = (
                                res_scr[k, c, pl.ds(r, mb // step, stride=step), :])
                    src = tmp_scr
                for r in range(d):
                    start = r if src is res_scr else (r % step) * (mb // step) + r // step
                    for c in range(SLABS):
                        q_ref[0, r, k * (mb // d):(k + 1) * (mb // d), c * LANES:(c + 1) * LANES] = (
                            src[k, c, pl.ds(start, mb // d, stride=step), :].astype(bf16))

    @pl.when(j >= 3 * QKV_TILES)
    def _():
        for k in range(PROJ_ROW_BLOCKS):
            rest_ref[k * mb:(k + 1) * mb, :] = block_dot(k).astype(bf16)


def _in_proj(x2, gain, w, batch, seq, tm=1024):
    T = x2.shape[0]
    tn = PROJ_TN
    nt = seq // tm
    d1, d2 = ATTN_GROUPS[1][1], ATTN_GROUPS[2][1]

    def qcol(j, g):
        return jnp.clip(j - g * QKV_TILES, 0, QKV_TILES - 1)

    return pl.pallas_call(
        functools.partial(_in_proj_kernel, tm=tm),
        grid=(T // tm, N_IN // tn),
        in_specs=[
            pl.BlockSpec((tm, D_MODEL), lambda i, j: (i, 0)),
            pl.BlockSpec((1, D_MODEL), lambda i, j: (0, 0)),
            pl.BlockSpec((D_MODEL, tn), lambda i, j: (0, j)),
        ],
        out_specs=[
            pl.BlockSpec((1, tm, tn), lambda i, j: (i // nt, i % nt, qcol(j, 0))),
            pl.BlockSpec((1, d1, tm // d1, tn), lambda i, j: (i // nt, 0, i % nt, qcol(j, 1))),
            pl.BlockSpec((1, d2, tm // d2, tn), lambda i, j: (i // nt, 0, i % nt, qcol(j, 2))),
            pl.BlockSpec((tm, tn), lambda i, j: (i, jnp.maximum(j - 3 * QKV_TILES, 0))),
        ],
        out_shape=[
            jax.ShapeDtypeStruct((batch, seq, QKV_W), bf16),
            jax.ShapeDtypeStruct((batch, d1, seq // d1, QKV_W), bf16),
            jax.ShapeDtypeStruct((batch, d2, seq // d2, QKV_W), bf16),
            jax.ShapeDtypeStruct((T, REST_W), bf16),
        ],
        scratch_shapes=[pltpu.VMEM((tm, D_MODEL), bf16)]
        + [pltpu.VMEM((PROJ_ROW_BLOCKS, SLABS, tm // PROJ_ROW_BLOCKS, LANES), f32)] * 2,
        compiler_params=_cparams("parallel", "arbitrary"),
        name="in_proj",
    )(x2, gain, w)


def _t5_bucket(rel):
    nb = N_BUCKETS // 2
    max_exact = nb // 2
    sign = (rel > 0).astype(np.int32) * nb
    n = np.abs(rel)
    large = max_exact + (np.log(np.maximum(n, 1) / max_exact)
                         / np.log(MAX_DISTANCE / max_exact) * (nb - max_exact)).astype(np.int32)
    large = np.minimum(large, nb - 1)
    return (sign + np.where(n < max_exact, n, large)).astype(np.int32)


def _band_bias(rel_bias_g, dilation):
    qq = np.arange(SUB_Q)[:, None]
    kk = np.arange(SUB_K)[None, :]
    rel = kk - ATTN_RADIUS - qq
    onehot = (_t5_bucket(rel * dilation)[None] == np.arange(N_BUCKETS)[:, None, None]).astype(np.float32)
    bias = jnp.einsum('nh,nqk->hqk', rel_bias_g.astype(f32), onehot, precision=lax.Precision.HIGHEST)
    return bias + np.where(np.abs(rel) <= ATTN_RADIUS, 0.0, NEG_INF).astype(np.float32)[None]


def _attn_kernel(q_ref, kp_ref, km_ref, kn_ref, vp_ref, vm_ref, vn_ref, bias_ref,
                 o_ref, lse_ref, kbuf, vbuf, *, tq, seq):
    R = ATTN_RADIUS
    kbuf[0:R] = kp_ref[0]
    kbuf[R:R + tq] = km_ref[0]
    kbuf[R + tq:] = kn_ref[0]
    vbuf[0:R] = vp_ref[0]
    vbuf[R:R + tq] = vm_ref[0]
    vbuf[R + tq:] = vn_ref[0]
    q0 = pl.program_id(1) * tq
    scale = 1.0 / math.sqrt(HEAD_DIM_A)
    lane = lax.broadcasted_iota(jnp.int32, (SUB_Q, LSE_LANES), 1)
    for s in range(tq // SUB_Q):
        r0 = s * SUB_Q
        pos = q0 + (r0 - R) + lax.broadcasted_iota(jnp.int32, (1, SUB_K), 1)
        edge = jnp.where(pos >= 0, jnp.where(pos < seq, 0.0, NEG_INF), NEG_INF)
        lse_tile = None
        for h in range(HEADS_PER_GROUP):
            c0 = h * HEAD_DIM_A
            q = q_ref[0, r0:r0 + SUB_Q, c0:c0 + HEAD_DIM_A]
            k = kbuf[r0:r0 + SUB_K, c0:c0 + HEAD_DIM_A]
            v = vbuf[r0:r0 + SUB_K, c0:c0 + HEAD_DIM_A]
            logits = lax.dot_general(q, k, (((1,), (1,)), ((), ())), preferred_element_type=f32)
            logits = logits * scale + bias_ref[h] + edge
            m = jnp.max(logits, axis=-1, keepdims=True)
            p = jnp.exp(logits - m)
            ssum = jnp.sum(p, axis=-1, keepdims=True)
            o = jnp.dot(p.astype(bf16), v, preferred_element_type=f32) * (1.0 / ssum)
            o_ref[0, r0:r0 + SUB_Q, c0:c0 + HEAD_DIM_A] = o.astype(o_ref.dtype)
            lse = m + jnp.log(ssum)
            lse_tile = lse if lse_tile is None else jnp.where(lane >= h * LSE_REP, lse, lse_tile)
        lse_ref[0, r0:r0 + SUB_Q, :] = jnp.broadcast_to(lse_tile, (SUB_Q, LSE_LANES))


def _attn_group(qkv, rel_bias, g):
    _, d = ATTN_GROUPS[g]
    n, L, _ = qkv.shape
    tq = min(512, L)
    R = ATTN_RADIUS
    bias = _band_bias(rel_bias[:, g * HEADS_PER_GROUP:(g + 1) * HEADS_PER_GROUP], d)
    rb = tq // R
    last_rb = L // R - 1

    def main(col):
        return pl.BlockSpec((1, tq, GROUP_WIDTH), lambda b, t: (b, t, col))

    def prev(col):
        return pl.BlockSpec((1, R, GROUP_WIDTH), lambda b, t: (b, jnp.maximum(t * rb - 1, 0), col))

    def nxt(col):
        return pl.BlockSpec((1, R, GROUP_WIDTH), lambda b, t: (b, jnp.minimum((t + 1) * rb, last_rb), col))

    return pl.pallas_call(
        functools.partial(_attn_kernel, tq=tq, seq=L),
        grid=(n, L // tq),
        in_specs=[main(0), prev(1), main(1), nxt(1), prev(2), main(2), nxt(2),
                  pl.BlockSpec((HEADS_PER_GROUP, SUB_Q, SUB_K), lambda b, t: (0, 0, 0))],
        out_specs=[pl.BlockSpec((1, tq, GROUP_WIDTH), lambda b, t: (b, t, 0)),
                   pl.BlockSpec((1, tq, LSE_LANES), lambda b, t: (b, t, 0))],
        out_shape=[jax.ShapeDtypeStruct((n, L, GROUP_WIDTH), bf16),
                   jax.ShapeDtypeStruct((n, L, LSE_LANES), f32)],
        scratch_shapes=[pltpu.VMEM((tq + 2 * R, GROUP_WIDTH), bf16),
                        pltpu.VMEM((tq + 2 * R, GROUP_WIDTH), bf16)],
        compiler_params=_cparams("parallel", "arbitrary"),
        name=f"attn_g{g}",
    )(qkv, qkv, qkv, qkv, qkv, qkv, qkv, bias)


LRU_CHUNK = 256
LRU_PAD = 8
LRU_FINISH_ROWS = 512
LRU_SEGS = 8
SEG_GAP = 8


def _lru_kernel(xb_ref, yb_ref, cw_ref, cb_ref, w_ref, gb_ref, lam_ref, o_ref,
                xpad, af, bf, ab, bb, htf, ptf, htb, ptb, cf_scr, cb_scr, *, seq):
    R = LRU_CHUNK
    P = LRU_PAD
    seg_len = seq // LRU_SEGS
    pitch = seg_len + SEG_GAP
    chunks_per_seg = seg_len // R
    n_chunks = seq // R
    xpad[0:P] = jnp.zeros((P, LRU_BW), f32)
    xpad[P + seq:] = jnp.zeros((P, LRU_BW), f32)
    xpad[P:P + seq] = xb_ref[0].astype(f32)
    lam = lam_ref[...]
    log_a_unit = -LRU_C * (jnp.maximum(-lam, 0.0) + jnp.log1p(jnp.exp(-jnp.abs(lam))))
    cw = cw_ref[...]
    cb = cb_ref[...]
    gate_bias = gb_ref[0]
    row = lax.broadcasted_iota(jnp.int32, (R, LRU_BW), 0)

    def chunk(ci, first=False, last=False):
        c0 = ci * R if isinstance(ci, int) else pl.multiple_of(ci * R, R)
        dst = (ci // chunks_per_seg) * pitch + (ci % chunks_per_seg) * R
        dst = dst if isinstance(ci, int) else pl.multiple_of(dst, 8)
        xc = (cw[0:1] * xpad[pl.ds(c0 + (P - 1), R), :] + cw[1:2] * xpad[pl.ds(c0 + P, R), :]
              + cw[2:3] * xpad[pl.ds(c0 + (P + 1), R), :] + cw[3:4] * xpad[pl.ds(c0 + (P + 2), R), :]) + cb
        th = jnp.tanh(jnp.dot(xc.astype(bf16), w_ref[0], preferred_element_type=f32) + gate_bias)
        half_xc = 0.5 * xc
        for direction, (a_scr, b_scr) in enumerate(((af, bf), (ab, bb))):
            base = direction * 2 * LRU_BW
            half_log_a = 0.5 * log_a_unit[direction:direction + 1]
            a = jnp.exp(half_log_a * th[:, base:base + LRU_BW] + half_log_a)
            gated_x = half_xc * th[:, base + LRU_BW:base + 2 * LRU_BW] + half_xc
            y = 1.0 - a * a
            mult = y * lax.rsqrt(jnp.maximum(y, 1e-30))
            if direction == 0 and first:
                mult = jnp.where(row == 0, 1.0, mult)
            if direction == 1 and last:
                mult = jnp.where(row == R - 1, 1.0, mult)
            a_scr[pl.ds(dst, R), :] = a
            b_scr[pl.ds(dst, R), :] = mult * gated_x

    chunk(0, first=True)
    lax.fori_loop(1, n_chunks - 1, lambda ci, c: (chunk(ci), c)[1], 0, unroll=2)
    chunk(n_chunks - 1, last=True)

    def scan(i, carry):
        hf, pf, hb, pb = carry
        a = af[pl.ds(i, LRU_SEGS, stride=pitch), :]
        hf = a * hf + bf[pl.ds(i, LRU_SEGS, stride=pitch), :]
        pf = a * pf
        htf[pl.ds(pl.multiple_of(i * LRU_SEGS, LRU_SEGS), LRU_SEGS), :] = hf
        ptf[pl.ds(pl.multiple_of(i * LRU_SEGS, LRU_SEGS), LRU_SEGS), :] = pf
        ib = seg_len - 1 - i
        a = ab[pl.ds(ib, LRU_SEGS, stride=pitch), :]
        hb = a * hb + bb[pl.ds(ib, LRU_SEGS, stride=pitch), :]
        pb = a * pb
        htb[pl.ds(pl.multiple_of(ib * LRU_SEGS, LRU_SEGS), LRU_SEGS), :] = hb
        ptb[pl.ds(pl.multiple_of(ib * LRU_SEGS, LRU_SEGS), LRU_SEGS), :] = pb
        return hf, pf, hb, pb

    zero = jnp.zeros((LRU_SEGS, LRU_BW), f32)
    one = jnp.ones((LRU_SEGS, LRU_BW), f32)
    hf, pf, hb, pb = lax.fori_loop(0, seg_len, scan, (zero, one, zero, one), unroll=8)

    c = jnp.zeros((1, LRU_BW), f32)
    cf_scr[0:1] = c
    for j in range(1, LRU_SEGS):
        c = hf[j - 1:j] + pf[j - 1:j] * c
        cf_scr[j:j + 1] = c
    c = jnp.zeros((1, LRU_BW), f32)
    cb_scr[LRU_SEGS - 1:LRU_SEGS] = c
    for j in range(LRU_SEGS - 2, -1, -1):
        c = hb[j + 1:j + 2] + pb[j + 1:j + 2] * c
        cb_scr[j:j + 1] = c

    F = LRU_FINISH_ROWS
    finish_per_seg = seg_len // F

    def finish(ci, carry):
        c0 = pl.multiple_of(ci * F, F)
        seg = ci // finish_per_seg
        src = (ci % finish_per_seg) * (F * LRU_SEGS) + seg
        h = (htf[pl.ds(src, F, stride=LRU_SEGS), :] + ptf[pl.ds(src, F, stride=LRU_SEGS), :] * cf_scr[pl.ds(seg, 1), :]
             + htb[pl.ds(src, F, stride=LRU_SEGS), :] + ptb[pl.ds(src, F, stride=LRU_SEGS), :] * cb_scr[pl.ds(seg, 1), :])
        y = yb_ref[0, pl.ds(c0, F), :].astype(f32)
        gelu = y * (0.5 * (1.0 + jnp.tanh(math.sqrt(2.0 / math.pi) * (y + 0.044715 * (y * y * y)))))
        o_ref[0, pl.ds(c0, F), :] = (h * gelu).astype(o_ref.dtype)
        return carry

    lax.fori_loop(0, seq // F, finish, 0)


def _lru(proj3, conv_w, conv_b, w_gates, b_gates, lam):
    B, S, _ = proj3.shape
    xb0 = 0
    yb0 = REST_YB // LRU_BW
    return pl.pallas_call(
        functools.partial(_lru_kernel, seq=S),
        grid=(B, LRU_BLOCKS),
        in_specs=[
            pl.BlockSpec((1, S, LRU_BW), lambda b, n: (b, 0, xb0 + n)),
            pl.BlockSpec((1, S, LRU_BW), lambda b, n: (b, 0, yb0 + n)),
            pl.BlockSpec((4, LRU_BW), lambda b, n: (0, n)),
            pl.BlockSpec((1, LRU_BW), lambda b, n: (0, n)),
            pl.BlockSpec((1, LRU_BW, 4 * LRU_BW), lambda b, n: (n, 0, 0)),
            pl.BlockSpec((1, 1, 4 * LRU_BW), lambda b, n: (n, 0, 0)),
            pl.BlockSpec((2, LRU_BW), lambda b, n: (0, n)),
        ],
        out_specs=pl.BlockSpec((1, S, LRU_BW), lambda b, n: (b, 0, n)),
        out_shape=jax.ShapeDtypeStruct((B, S, LRU_WIDTH), bf16),
        scratch_shapes=([pltpu.VMEM((S + 2 * LRU_PAD, LRU_BW), f32)]
                        + [pltpu.VMEM((S + LRU_SEGS * SEG_GAP, LRU_BW), f32)] * 4
                        + [pltpu.VMEM((S, LRU_BW), f32)] * 4
                        + [pltpu.VMEM((LRU_SEGS, LRU_BW), f32)] * 2),
        compiler_params=_cparams("parallel", "parallel"),
        name="lru",
    )(proj3, proj3, conv_w, conv_b, w_gates, b_gates, lam)


def _mem_kv_kernel(m_ref, g_ref, w_ref, o_ref, h_scr):
    @pl.when(pl.program_id(0) == 0)
    def _():
        h_scr[...] = _rms(m_ref[...], g_ref[...]).astype(bf16)

    o_ref[...] = jnp.dot(h_scr[...], w_ref[...], preferred_element_type=f32).astype(o_ref.dtype)


def _mem_kv(mem2, gain, w, tn=512):
    M = mem2.shape[0]
    N = w.shape[1]
    return pl.pallas_call(
        _mem_kv_kernel,
        grid=(N // tn,),
        in_specs=[pl.BlockSpec((M, D_MODEL), lambda j: (0, 0)),
                  pl.BlockSpec((1, D_MODEL), lambda j: (0, 0)),
                  pl.BlockSpec((D_MODEL, tn), lambda j: (0, j))],
        out_specs=pl.BlockSpec((M, tn), lambda j: (0, j)),
        out_shape=jax.ShapeDtypeStruct((M, N), bf16),
        scratch_shapes=[pltpu.VMEM((M, D_MODEL), bf16)],
        compiler_params=_cparams("arbitrary"),
        name="mem_kv",
    )(mem2, gain, w)


def _xattn_kernel(q0_ref, q1_ref, q2_ref, q3_ref, kv_ref, o_ref):
    scale = 1.0 / math.sqrt(MEM_HEAD_DIM)
    for h, q_ref in enumerate((q0_ref, q1_ref, q2_ref, q3_ref)):
        c0 = h * MEM_HEAD_DIM
        k = kv_ref[0, :, c0:c0 + MEM_HEAD_DIM]
        v = kv_ref[0, :, MEM_WIDTH + c0:MEM_WIDTH + c0 + MEM_HEAD_DIM]
        logits = lax.dot_general(q_ref[0], k, (((1,), (1,)), ((), ())), preferred_element_type=f32) * scale
        m = jnp.max(logits, axis=-1, keepdims=True)
        p = jnp.exp(logits - m)
        ssum = jnp.sum(p, axis=-1, keepdims=True)
        o = jnp.dot(p.astype(bf16), v, preferred_element_type=f32) * (1.0 / ssum)
        o_ref[0, :, c0:c0 + MEM_HEAD_DIM] = o.astype(o_ref.dtype)


def _xattn(proj3, kv3, tq=1024):
    B, S, _ = proj3.shape
    qb0 = REST_QC // MEM_HEAD_DIM

    def qspec(h):
        return pl.BlockSpec((1, tq, MEM_HEAD_DIM), lambda b, t: (b, t, qb0 + h))

    return pl.pallas_call(
        _xattn_kernel,
        grid=(B, S // tq),
        in_specs=[qspec(0), qspec(1), qspec(2), qspec(3),
                  pl.BlockSpec((1, N_MEM, 2 * MEM_WIDTH), lambda b, t: (b, 0, 0))],
        out_specs=pl.BlockSpec((1, tq, MEM_WIDTH), lambda b, t: (b, t, 0)),
        out_shape=jax.ShapeDtypeStruct((B, S, MEM_WIDTH), bf16),
        compiler_params=_cparams("parallel", "parallel"),
        name="xattn",
    )(proj3, proj3, proj3, proj3, kv3)


MERGE_ROWS = (D_MODEL, D_MODEL, D_MODEL, GROUP_WIDTH, LRU_WIDTH, MEM_WIDTH)
MERGE_OFFS = tuple(int(v) for v in np.cumsum((0,) + MERGE_ROWS))


def _pack_merge_weights(w_gate, b_gate, w_o_attn, w_o_lru, w_o_mem, tn):
    def col_tiles(w):
        k, n = w.shape
        return w.reshape(k, n // tn, tn).transpose(1, 0, 2)

    parts = [col_tiles(w_gate[:, k * D_MODEL:(k + 1) * D_MODEL]) for k in range(3)]
    parts += [col_tiles(w_o_attn), col_tiles(w_o_lru), col_tiles(w_o_mem)]
    wcols = jnp.concatenate(parts, axis=1).astype(bf16)
    bias = b_gate.reshape(3, D_MODEL // tn, tn).transpose(1, 0, 2)
    return wcols, bias


def _merge_kernel(x_ref, g_ref, o0_ref, o1_ref, o2_ref, l0_ref, l1_ref, l2_ref, yl_ref, yc_ref,
                  wcol_ref, bg_ref, wout_ref, out_ref, h_scr, ya_scr,
                  o1_scr, o2_scr, l1_scr, l2_scr, *, tm):
    @pl.when(pl.program_id(1) == 0)
    def _():
        x = x_ref[...]
        h_scr[...] = _rms(x, g_ref[...]).astype(bf16)
        out_ref[...] = x
        for g, o_ref, l_ref, o_scr, l_scr in ((1, o1_ref, l1_ref, o1_scr, l1_scr),
                                              (2, o2_ref, l2_ref, o2_scr, l2_scr)):
            d = ATTN_GROUPS[g][1]
            for r in range(d):
                l_scr[pl.ds(r, tm // d, stride=d), :] = l_ref[0, r]
                for h in range(HEADS_PER_GROUP):
                    o_scr[h, pl.ds(r, tm // d, stride=d), :] = (
                        o_ref[0, r, :, h * HEAD_DIM_A:(h + 1) * HEAD_DIM_A].astype(f32))
        l0, l1, l2 = l0_ref[...], l1_scr[...], l2_scr[...]
        m = jnp.maximum(jnp.maximum(l0, l1), l2)
        e0, e1, e2 = jnp.exp(l0 - m), jnp.exp(l1 - m), jnp.exp(l2 - m)
        inv = 1.0 / (e0 + e1 + e2)
        for h in range(HEADS_PER_GROUP):
            c0 = h * HEAD_DIM_A
            lane = slice(h * LSE_REP, h * LSE_REP + 1)
            y = ((e0 * inv)[:, lane] * o0_ref[:, c0:c0 + HEAD_DIM_A].astype(f32)
                 + (e1 * inv)[:, lane] * o1_scr[h] + (e2 * inv)[:, lane] * o2_scr[h])
            ya_scr[:, c0:c0 + HEAD_DIM_A] = y.astype(bf16)

    h = h_scr[...]

    def w(k):
        return wcol_ref[0, MERGE_OFFS[k]:MERGE_OFFS[k + 1], :]

    def gate(k):
        return jax.nn.sigmoid(jnp.dot(h, w(k), preferred_element_type=f32) + bg_ref[0, k:k + 1, :])

    mixed = (gate(0) * jnp.dot(ya_scr[...], w(3), preferred_element_type=f32)
             + gate(1) * jnp.dot(yl_ref[...], w(4), preferred_element_type=f32)
             + gate(2) * jnp.dot(yc_ref[...], w(5), preferred_element_type=f32))
    out_ref[...] += jnp.dot(mixed.astype(bf16), wout_ref[...], preferred_element_type=f32)


def _merge(x2, gain, o_groups, lse_groups, y_lru, y_c, w_gate, b_gate, w_o_attn, w_o_lru, w_o_mem, w_out,
           seq, tm=512, tn=256):
    T = x2.shape[0]
    nj = D_MODEL // tn
    nt = seq // tm
    d1, d2 = ATTN_GROUPS[1][1], ATTN_GROUPS[2][1]

    def rows(width):
        return pl.BlockSpec((tm, width), lambda i, j: (i, 0))

    def strided_rows(d, width):
        return pl.BlockSpec((1, d, tm // d, width), lambda i, j: (i // nt, 0, i % nt, 0))

    wcols, bias = _pack_merge_weights(w_gate, b_gate, w_o_attn, w_o_lru, w_o_mem, tn)

    return pl.pallas_call(
        functools.partial(_merge_kernel, tm=tm),
        grid=(T // tm, nj),
        in_specs=[rows(D_MODEL), pl.BlockSpec((1, D_MODEL), lambda i, j: (0, 0)),
                  rows(GROUP_WIDTH), strided_rows(d1, GROUP_WIDTH), strided_rows(d2, GROUP_WIDTH),
                  rows(LSE_LANES), strided_rows(d1, LSE_LANES), strided_rows(d2, LSE_LANES),
                  rows(LRU_WIDTH), rows(MEM_WIDTH),
                  pl.BlockSpec((1, MERGE_OFFS[-1], tn), lambda i, j: (j, 0, 0)),
                  pl.BlockSpec((1, 3, tn), lambda i, j: (j, 0, 0)),
                  pl.BlockSpec((tn, D_MODEL), lambda i, j: (j, 0))],
        out_specs=pl.BlockSpec((tm, D_MODEL), lambda i, j: (i, 0)),
        out_shape=jax.ShapeDtypeStruct((T, D_MODEL), f32),
        scratch_shapes=[pltpu.VMEM((tm, D_MODEL), bf16), pltpu.VMEM((tm, GROUP_WIDTH), bf16),
                        pltpu.VMEM((HEADS_PER_GROUP, tm, HEAD_DIM_A), f32),
                        pltpu.VMEM((HEADS_PER_GROUP, tm, HEAD_DIM_A), f32),
                        pltpu.VMEM((tm, LSE_LANES), f32), pltpu.VMEM((tm, LSE_LANES), f32)],
        compiler_params=_cparams("parallel", "arbitrary"),
        name="merge",
    )(x2, gain, *o_groups, *lse_groups, y_lru, y_c, wcols, bias, w_out.astype(bf16))


def _mlp_kernel(x_ref, g_ref, gf_ref, wu_ref, wd_ref, out_ref, h_scr):
    j = pl.program_id(1)

    @pl.when(j == 0)
    def _():
        x = x_ref[...]
        h_scr[...] = _rms(x, g_ref[...]).astype(bf16)
        out_ref[...] = x

    u = jnp.maximum(jnp.dot(h_scr[...], wu_ref[...], preferred_element_type=f32), 0.0)
    out_ref[...] += jnp.dot((u * u).astype(bf16), wd_ref[...], preferred_element_type=f32)

    @pl.when(j == pl.num_programs(1) - 1)
    def _():
        out_ref[...] = _rms(out_ref[...], gf_ref[...])


def _mlp(x2, gain, gain_final, w_up, w_down, tm=512, tf=1024):
    T = x2.shape[0]
    return pl.pallas_call(
        _mlp_kernel,
        grid=(T // tm, D_FF // tf),
        in_specs=[pl.BlockSpec((tm, D_MODEL), lambda i, j: (i, 0)),
                  pl.BlockSpec((1, D_MODEL), lambda i, j: (0, 0)),
                  pl.BlockSpec((1, D_MODEL), lambda i, j: (0, 0)),
                  pl.BlockSpec((D_MODEL, tf), lambda i, j: (0, j)),
                  pl.BlockSpec((tf, D_MODEL), lambda i, j: (j, 0))],
        out_specs=pl.BlockSpec((tm, D_MODEL), lambda i, j: (i, 0)),
        out_shape=jax.ShapeDtypeStruct((T, D_MODEL), f32),
        scratch_shapes=[pltpu.VMEM((tm, D_MODEL), bf16)],
        compiler_params=_cparams("parallel", "arbitrary"),
        name="mlp",
    )(x2, gain, gain_final, w_up, w_down)


def kernel(x, mem, rel_bias, norm_mix, norm_mem, norm_mlp, norm_final, w_in, w_gate, b_gate, conv_w, conv_b,
           lru_wa, lru_ba, lru_wi, lru_bi, lru_lambda, w_mem_kv, w_o_attn, w_o_lru, w_o_mem, w_out, w_up, w_down):
    B, S, D = x.shape
    T = B * S
    depth = w_in.shape[0]
    assert depth == 1, "the final RMSNorm is fused into the (single) layer's MLP kernel"
    x2 = x.reshape(T, D)
    mem2 = mem.reshape(B * N_MEM, D)
    for l in range(depth):
        gain_mix = norm_mix[l].reshape(1, D)
        qkv0, qkv1, qkv2, rest = _in_proj(x2, gain_mix, _pack_w_in(w_in[l]), B, S)
        proj3 = rest.reshape(B, S, REST_W)

        attn = []
        for g, qkv in enumerate((qkv0, qkv1, qkv2)):
            d = ATTN_GROUPS[g][1]
            o, lse = _attn_group(qkv.reshape(B * d, S // d, QKV_W), rel_bias, g)
            if g == 0:
                attn.append((o.reshape(T, GROUP_WIDTH), lse.reshape(T, LSE_LANES)))
            else:
                attn.append((o.reshape(B, d, S // d, GROUP_WIDTH), lse.reshape(B, d, S // d, LSE_LANES)))

        w_gates = (0.5 * jnp.concatenate([lru_wa[l, 0], lru_wi[l, 0], lru_wa[l, 1], lru_wi[l, 1]], axis=-1)
                   ).astype(bf16)
        b_gates = 0.5 * jnp.concatenate([lru_ba[l, 0], lru_bi[l, 0], lru_ba[l, 1], lru_bi[l, 1]], axis=-1)
        y_lru = _lru(proj3, conv_w[l], conv_b[l].reshape(1, LRU_WIDTH), w_gates,
                     b_gates.reshape(LRU_BLOCKS, 1, 4 * LRU_BW), lru_lambda[l])

        kv = _mem_kv(mem2, norm_mem[l].reshape(1, D), w_mem_kv[l].astype(bf16))
        y_c = _xattn(proj3, kv.reshape(B, N_MEM, 2 * MEM_WIDTH))

        x2 = _merge(x2, gain_mix, [a[0] for a in attn], [a[1] for a in attn],
                    y_lru.reshape(T, LRU_WIDTH), y_c.reshape(T, MEM_WIDTH),
                    w_gate[l], b_gate[l], w_o_attn[l], w_o_lru[l], w_o_mem[l], w_out[l], S)
        x2 = _mlp(x2, norm_mlp[l].reshape(1, D), norm_final.reshape(1, D), w_up[l].astype(bf16),
                  w_down[l].astype(bf16))
    return x2.reshape(B, S, D)
```

```python
import functools
import math

import jax
import jax.numpy as jnp
import numpy as np
from jax import lax
from jax.experimental import pallas as pl
from jax.experimental.pallas import tpu as pltpu

D_MODEL = 2048
HEAD_DIM_A = 128
ATTN_GROUPS = ((128, 1), (512, 4), (2048, 16))
HEADS_PER_GROUP = 4
GROUP_WIDTH = HEADS_PER_GROUP * HEAD_DIM_A
WIDTH_A = len(ATTN_GROUPS) * GROUP_WIDTH
ATTN_RADIUS = 64
N_BUCKETS = 32
MAX_DISTANCE = 1024
LRU_WIDTH = 1536
LRU_BLOCKS = 12
LRU_BW = 128
LRU_C = 8.0
N_MEM = 256
MEM_HEADS = 4
MEM_HEAD_DIM = 256
MEM_WIDTH = MEM_HEADS * MEM_HEAD_DIM
D_FF = 4 * D_MODEL
EPS = 1e-6
N_IN = 3 * WIDTH_A + 2 * LRU_WIDTH + MEM_WIDTH
COL_K = WIDTH_A
COL_V = 2 * WIDTH_A
COL_XB = 3 * WIDTH_A
COL_YB = 3 * WIDTH_A + LRU_WIDTH
COL_QC = 3 * WIDTH_A + 2 * LRU_WIDTH
NEG_INF = -1e30

SUB_Q = 128
SUB_K = SUB_Q + 2 * ATTN_RADIUS
LSE_LANES = 128
LSE_REP = LSE_LANES // HEADS_PER_GROUP

VMEM_LIMIT = 56 * 1024 * 1024

f32 = jnp.float32
bf16 = jnp.bfloat16


def _cparams(*sem):
    return pltpu.CompilerParams(dimension_semantics=sem, vmem_limit_bytes=VMEM_LIMIT)


def _rms(x, gain):
    return x * lax.rsqrt(jnp.mean(x * x, axis=-1, keepdims=True) + EPS) * gain


QKV_W = 3 * GROUP_WIDTH
REST_W = 2 * LRU_WIDTH + MEM_WIDTH
REST_YB = LRU_WIDTH
REST_QC = 2 * LRU_WIDTH
PROJ_TN = GROUP_WIDTH
QKV_TILES = QKV_W // PROJ_TN
LANES = 128
SLABS = PROJ_TN // LANES
PROJ_ROW_BLOCKS = 2
DEINTERLEAVE_STEP = 4


def _w_in_tile(j):
    n_groups = len(ATTN_GROUPS)
    return jnp.where(j < n_groups * QKV_TILES, (j % QKV_TILES) * n_groups + j // QKV_TILES, j)


def _in_proj_kernel(x_ref, g_ref, w_ref, q0_ref, q1_ref, q2_ref, rest_ref, h_scr, res_scr, tmp_scr, *, tm):
    j = pl.program_id(1)

    @pl.when(j == 0)
    def _():
        h_scr[...] = _rms(x_ref[...], g_ref[...]).astype(bf16)

    mb = tm // PROJ_ROW_BLOCKS

    def block_dot(k):
        return jnp.dot(h_scr[k * mb:(k + 1) * mb, :], w_ref[...], preferred_element_type=f32)

    @pl.when(j < QKV_TILES)
    def _():
        for k in range(PROJ_ROW_BLOCKS):
            q0_ref[0, k * mb:(k + 1) * mb, :] = block_dot(k).astype(bf16)

    for g, q_ref in ((1, q1_ref), (2, q2_ref)):
        d = ATTN_GROUPS[g][1]

        @pl.when((j >= g * QKV_TILES) & (j < (g + 1) * QKV_TILES))
        def _():
            for k in range(PROJ_ROW_BLOCKS):
                res = block_dot(k)
                for c in range(SLABS):
                    res_scr[k, c] = res[:, c * LANES:(c + 1) * LANES]
                src, step = res_scr, d
                if d == DEINTERLEAVE_STEP ** 2:
                    step = DEINTERLEAVE_STEP
                    for r in range(step):
                        for c in range(SLABS):
                            tmp_scr[k, c, r * (mb // step):(r + 1) * (mb // step), :] = (
                                res_scr[k, c, pl.ds(r, mb // step, stride=step), :])
                    src = tmp_scr
                for r in range(d):
                    start = r if src is res_scr else (r % step) * (mb // step) + r // step
                    for c in range(SLABS):
                        q_ref[0, r, k * (mb // d):(k + 1) * (mb // d), c * LANES:(c + 1) * LANES] = (
                            src[k, c, pl.ds(start, mb // d, stride=step), :].astype(bf16))

    @pl.when(j >= 3 * QKV_TILES)
    def _():
        for k in range(PROJ_ROW_BLOCKS):
            rest_ref[k * mb:(k + 1) * mb, :] = block_dot(k).astype(bf16)


def _in_proj(x2, gain, w, batch, seq, tm=1024):
    T = x2.shape[0]
    tn = PROJ_TN
    nt = seq // tm
    d1, d2 = ATTN_GROUPS[1][1], ATTN_GROUPS[2][1]

    def qcol(j, g):
        return jnp.clip(j - g * QKV_TILES, 0, QKV_TILES - 1)

    return pl.pallas_call(
        functools.partial(_in_proj_kernel, tm=tm),
        grid=(T // tm, N_IN // tn),
        in_specs=[
            pl.BlockSpec((tm, D_MODEL), lambda i, j: (i, 0)),
            pl.BlockSpec((1, D_MODEL), lambda i, j: (0, 0)),
            pl.BlockSpec((D_MODEL, tn), lambda i, j: (0, _w_in_tile(j))),
        ],
        out_specs=[
            pl.BlockSpec((1, tm, tn), lambda i, j: (i // nt, i % nt, qcol(j, 0))),
            pl.BlockSpec((1, d1, tm // d1, tn), lambda i, j: (i // nt, 0, i % nt, qcol(j, 1))),
            pl.BlockSpec((1, d2, tm // d2, tn), lambda i, j: (i // nt, 0, i % nt, qcol(j, 2))),
            pl.BlockSpec((tm, tn), lambda i, j: (i, jnp.maximum(j - 3 * QKV_TILES, 0))),
            pl.BlockSpec((tm, D_MODEL), lambda i, j: (i, 0)),
        ],
        out_shape=[
            jax.ShapeDtypeStruct((batch, seq, QKV_W), bf16),
            jax.ShapeDtypeStruct((batch, d1, seq // d1, QKV_W), bf16),
            jax.ShapeDtypeStruct((batch, d2, seq // d2, QKV_W), bf16),
            jax.ShapeDtypeStruct((T, REST_W), bf16),
            jax.ShapeDtypeStruct((T, D_MODEL), bf16),
        ],
        scratch_shapes=[pltpu.VMEM((PROJ_ROW_BLOCKS, SLABS, tm // PROJ_ROW_BLOCKS, LANES), f32)] * 2,
        compiler_params=_cparams("parallel", "arbitrary"),
        name="in_proj",
    )(x2, gain, w)


def _t5_bucket(rel):
    nb = N_BUCKETS // 2
    max_exact = nb // 2
    sign = (rel > 0).astype(np.int32) * nb
    n = np.abs(rel)
    large = max_exact + (np.log(np.maximum(n, 1) / max_exact)
                         / np.log(MAX_DISTANCE / max_exact) * (nb - max_exact)).astype(np.int32)
    large = np.minimum(large, nb - 1)
    return (sign + np.where(n < max_exact, n, large)).astype(np.int32)


def _band_bias(rel_bias_g, dilation):
    qq = np.arange(SUB_Q)[:, None]
    kk = np.arange(SUB_K)[None, :]
    rel = kk - ATTN_RADIUS - qq
    onehot = (_t5_bucket(rel * dilation)[None] == np.arange(N_BUCKETS)[:, None, None]).astype(np.float32)
    bias = jnp.einsum('nh,nqk->hqk', rel_bias_g.astype(f32), onehot, precision=lax.Precision.HIGHEST)
    return bias + np.where(np.abs(rel) <= ATTN_RADIUS, 0.0, NEG_INF).astype(np.float32)[None]


def _attn_kernel(q_ref, kp_ref, km_ref, kn_ref, vp_ref, vm_ref, vn_ref, bias_ref,
                 o_ref, lse_ref, kbuf, vbuf, *, tq, seq):
    R = ATTN_RADIUS
    kbuf[0:R] = kp_ref[0]
    kbuf[R:R + tq] = km_ref[0]
    kbuf[R + tq:] = kn_ref[0]
    vbuf[0:R] = vp_ref[0]
    vbuf[R:R + tq] = vm_ref[0]
    vbuf[R + tq:] = vn_ref[0]
    q0 = pl.program_id(1) * tq
    scale = 1.0 / math.sqrt(HEAD_DIM_A)
    lane = lax.broadcasted_iota(jnp.int32, (SUB_Q, LSE_LANES), 1)
    for s in range(tq // SUB_Q):
        r0 = s * SUB_Q
        pos = q0 + (r0 - R) + lax.broadcasted_iota(jnp.int32, (1, SUB_K), 1)
        edge = jnp.where(pos >= 0, jnp.where(pos < seq, 0.0, NEG_INF), NEG_INF)
        lse_tile = None
        for h in range(HEADS_PER_GROUP):
            c0 = h * HEAD_DIM_A
            q = q_ref[0, r0:r0 + SUB_Q, c0:c0 + HEAD_DIM_A]
            k = kbuf[r0:r0 + SUB_K, c0:c0 + HEAD_DIM_A]
            v = vbuf[r0:r0 + SUB_K, c0:c0 + HEAD_DIM_A]
            logits = lax.dot_general(q, k, (((1,), (1,)), ((), ())), preferred_element_type=f32)
            logits = logits * scale + bias_ref[h] + edge
            m = jnp.max(logits, axis=-1, keepdims=True)
            p = jnp.exp(logits - m)
            ssum = jnp.sum(p, axis=-1, keepdims=True)
            o = jnp.dot(p.astype(bf16), v, preferred_element_type=f32) * (1.0 / ssum)
            o_ref[0, r0:r0 + SUB_Q, c0:c0 + HEAD_DIM_A] = o.astype(o_ref.dtype)
            lse = m + jnp.log(ssum)
            lse_tile = lse if lse_tile is None else jnp.where(lane >= h * LSE_REP, lse, lse_tile)
        lse_ref[0, r0:r0 + SUB_Q, :] = jnp.broadcast_to(lse_tile, (SUB_Q, LSE_LANES))


def _attn_group(qkv, rel_bias, g):
    _, d = ATTN_GROUPS[g]
    n, L, _ = qkv.shape
    tq = min(512, L)
    R = ATTN_RADIUS
    bias = _band_bias(rel_bias[:, g * HEADS_PER_GROUP:(g + 1) * HEADS_PER_GROUP], d)
    rb = tq // R
    last_rb = L // R - 1

    def main(col):
        return pl.BlockSpec((1, tq, GROUP_WIDTH), lambda b, t: (b, t, col))

    def prev(col):
        return pl.BlockSpec((1, R, GROUP_WIDTH), lambda b, t: (b, jnp.maximum(t * rb - 1, 0), col))

    def nxt(col):
        return pl.BlockSpec((1, R, GROUP_WIDTH), lambda b, t: (b, jnp.minimum((t + 1) * rb, last_rb), col))

    return pl.pallas_call(
        functools.partial(_attn_kernel, tq=tq, seq=L),
        grid=(n, L // tq),
        in_specs=[main(0), prev(1), main(1), nxt(1), prev(2), main(2), nxt(2),
                  pl.BlockSpec((HEADS_PER_GROUP, SUB_Q, SUB_K), lambda b, t: (0, 0, 0))],
        out_specs=[pl.BlockSpec((1, tq, GROUP_WIDTH), lambda b, t: (b, t, 0)),
                   pl.BlockSpec((1, tq, LSE_LANES), lambda b, t: (b, t, 0))],
        out_shape=[jax.ShapeDtypeStruct((n, L, GROUP_WIDTH), bf16),
                   jax.ShapeDtypeStruct((n, L, LSE_LANES), f32)],
        scratch_shapes=[pltpu.VMEM((tq + 2 * R, GROUP_WIDTH), bf16),
                        pltpu.VMEM((tq + 2 * R, GROUP_WIDTH), bf16)],
        compiler_params=_cparams("parallel", "arbitrary"),
        name=f"attn_g{g}",
    )(qkv, qkv, qkv, qkv, qkv, qkv, qkv, bias)


LRU_CHUNK = 256
LRU_PAD = 8
LRU_FINISH_ROWS = 512
LRU_SEGS = 8
SEG_GAP = 8


def _lru_kernel(xb_ref, yb_ref, cw_ref, cb_ref, w_ref, gb_ref, lam_ref, o_ref,
                xpad, af, bf, ab, bb, htf, ptf, htb, ptb, cf_scr, cb_scr, *, seq):
    R = LRU_CHUNK
    P = LRU_PAD
    seg_len = seq // LRU_SEGS
    pitch = seg_len + SEG_GAP
    chunks_per_seg = seg_len // R
    n_chunks = seq // R
    xpad[0:P] = jnp.zeros((P, LRU_BW), f32)
    xpad[P + seq:] = jnp.zeros((P, LRU_BW), f32)
    xpad[P:P + seq] = xb_ref[0].astype(f32)
    lam = lam_ref[...]
    log_a_unit = -LRU_C * (jnp.maximum(-lam, 0.0) + jnp.log1p(jnp.exp(-jnp.abs(lam))))
    cw = cw_ref[...]
    cb = cb_ref[...]
    gate_bias = gb_ref[0]
    row = lax.broadcasted_iota(jnp.int32, (R, LRU_BW), 0)

    def chunk(ci, first=False, last=False):
        c0 = ci * R if isinstance(ci, int) else pl.multiple_of(ci * R, R)
        dst = (ci // chunks_per_seg) * pitch + (ci % chunks_per_seg) * R
        dst = dst if isinstance(ci, int) else pl.multiple_of(dst, 8)
        xc = (cw[0:1] * xpad[pl.ds(c0 + (P - 1), R), :] + cw[1:2] * xpad[pl.ds(c0 + P, R), :]
              + cw[2:3] * xpad[pl.ds(c0 + (P + 1), R), :] + cw[3:4] * xpad[pl.ds(c0 + (P + 2), R), :]) + cb
        th = jnp.tanh(jnp.dot(xc.astype(bf16), w_ref[0], preferred_element_type=f32) + gate_bias)
        half_xc = 0.5 * xc
        for direction, (a_scr, b_scr) in enumerate(((af, bf), (ab, bb))):
            base = direction * 2 * LRU_BW
            half_log_a = 0.5 * log_a_unit[direction:direction + 1]
            a = jnp.exp(half_log_a * th[:, base:base + LRU_BW] + half_log_a)
            gated_x = half_xc * th[:, base + LRU_BW:base + 2 * LRU_BW] + half_xc
            y = 1.0 - a * a
            mult = y * lax.rsqrt(jnp.maximum(y, 1e-30))
            if direction == 0 and first:
                mult = jnp.where(row == 0, 1.0, mult)
            if direction == 1 and last:
                mult = jnp.where(row == R - 1, 1.0, mult)
            a_scr[pl.ds(dst, R), :] = a
            b_scr[pl.ds(dst, R), :] = mult * gated_x

    chunk(0, first=True)
    lax.fori_loop(1, n_chunks - 1, lambda ci, c: (chunk(ci), c)[1], 0, unroll=2)
    chunk(n_chunks - 1, last=True)

    def scan(i, carry):
        hf, pf, hb, pb = carry
        a = af[pl.ds(i, LRU_SEGS, stride=pitch), :]
        hf = a * hf + bf[pl.ds(i, LRU_SEGS, stride=pitch), :]
        pf = a * pf
        htf[pl.ds(pl.multiple_of(i * LRU_SEGS, LRU_SEGS), LRU_SEGS), :] = hf
        ptf[pl.ds(pl.multiple_of(i * LRU_SEGS, LRU_SEGS), LRU_SEGS), :] = pf
        ib = seg_len - 1 - i
        a = ab[pl.ds(ib, LRU_SEGS, stride=pitch), :]
        hb = a * hb + bb[pl.ds(ib, LRU_SEGS, stride=pitch), :]
        pb = a * pb
        htb[pl.ds(pl.multiple_of(ib * LRU_SEGS, LRU_SEGS), LRU_SEGS), :] = hb
        ptb[pl.ds(pl.multiple_of(ib * LRU_SEGS, LRU_SEGS), LRU_SEGS), :] = pb
        return hf, pf, hb, pb

    zero = jnp.zeros((LRU_SEGS, LRU_BW), f32)
    one = jnp.ones((LRU_SEGS, LRU_BW), f32)
    hf, pf, hb, pb = lax.fori_loop(0, seg_len, scan, (zero, one, zero, one), unroll=8)

    c = jnp.zeros((1, LRU_BW), f32)
    cf_scr[0:1] = c
    for j in range(1, LRU_SEGS):
        c = hf[j - 1:j] + pf[j - 1:j] * c
        cf_scr[j:j + 1] = c
    c = jnp.zeros((1, LRU_BW), f32)
    cb_scr[LRU_SEGS - 1:LRU_SEGS] = c
    for j in range(LRU_SEGS - 2, -1, -1):
        c = hb[j + 1:j + 2] + pb[j + 1:j + 2] * c
        cb_scr[j:j + 1] = c

    F = LRU_FINISH_ROWS
    finish_per_seg = seg_len // F

    def finish(ci, carry):
        c0 = pl.multiple_of(ci * F, F)
        seg = ci // finish_per_seg
        src = (ci % finish_per_seg) * (F * LRU_SEGS) + seg
        h = (htf[pl.ds(src, F, stride=LRU_SEGS), :] + ptf[pl.ds(src, F, stride=LRU_SEGS), :] * cf_scr[pl.ds(seg, 1), :]
             + htb[pl.ds(src, F, stride=LRU_SEGS), :] + ptb[pl.ds(src, F, stride=LRU_SEGS), :] * cb_scr[pl.ds(seg, 1), :])
        y = yb_ref[0, pl.ds(c0, F), :].astype(f32)
        gelu = y * (0.5 * (1.0 + jnp.tanh(math.sqrt(2.0 / math.pi) * (y + 0.044715 * (y * y * y)))))
        o_ref[0, pl.ds(c0, F), :] = (h * gelu).astype(o_ref.dtype)
        return carry

    lax.fori_loop(0, seq // F, finish, 0)


def _lru(proj3, conv_w, conv_b, w_gates, b_gates, lam):
    B, S, _ = proj3.shape
    xb0 = 0
    yb0 = REST_YB // LRU_BW
    return pl.pallas_call(
        functools.partial(_lru_kernel, seq=S),
        grid=(B, LRU_BLOCKS),
        in_specs=[
            pl.BlockSpec((1, S, LRU_BW), lambda b, n: (b, 0, xb0 + n)),
            pl.BlockSpec((1, S, LRU_BW), lambda b, n: (b, 0, yb0 + n)),
            pl.BlockSpec((4, LRU_BW), lambda b, n: (0, n)),
            pl.BlockSpec((1, LRU_BW), lambda b, n: (0, n)),
            pl.BlockSpec((1, LRU_BW, 4 * LRU_BW), lambda b, n: (n, 0, 0)),
            pl.BlockSpec((1, 1, 4 * LRU_BW), lambda b, n: (n, 0, 0)),
            pl.BlockSpec((2, LRU_BW), lambda b, n: (0, n)),
        ],
        out_specs=pl.BlockSpec((1, S, LRU_BW), lambda b, n: (b, 0, n)),
        out_shape=jax.ShapeDtypeStruct((B, S, LRU_WIDTH), bf16),
        scratch_shapes=([pltpu.VMEM((S + 2 * LRU_PAD, LRU_BW), f32)]
                        + [pltpu.VMEM((S + LRU_SEGS * SEG_GAP, LRU_BW), f32)] * 4
                        + [pltpu.VMEM((S, LRU_BW), f32)] * 4
                        + [pltpu.VMEM((LRU_SEGS, LRU_BW), f32)] * 2),
        compiler_params=_cparams("parallel", "parallel"),
        name="lru",
    )(proj3, proj3, conv_w, conv_b, w_gates, b_gates, lam)


def _mem_kv_kernel(m_ref, g_ref, w_ref, o_ref, h_scr):
    @pl.when(pl.program_id(0) == 0)
    def _():
        h_scr[...] = _rms(m_ref[...], g_ref[...]).astype(bf16)

    o_ref[...] = jnp.dot(h_scr[...], w_ref[...], preferred_element_type=f32).astype(o_ref.dtype)


def _mem_kv(mem2, gain, w, tn=512):
    M = mem2.shape[0]
    N = w.shape[1]
    return pl.pallas_call(
        _mem_kv_kernel,
        grid=(N // tn,),
        in_specs=[pl.BlockSpec((M, D_MODEL), lambda j: (0, 0)),
                  pl.BlockSpec((1, D_MODEL), lambda j: (0, 0)),
                  pl.BlockSpec((D_MODEL, tn), lambda j: (0, j))],
        out_specs=pl.BlockSpec((M, tn), lambda j: (0, j)),
        out_shape=jax.ShapeDtypeStruct((M, N), bf16),
        scratch_shapes=[pltpu.VMEM((M, D_MODEL), bf16)],
        compiler_params=_cparams("arbitrary"),
        name="mem_kv",
    )(mem2, gain, w)


def _xattn_kernel(q0_ref, q1_ref, q2_ref, q3_ref, kv_ref, o_ref):
    scale = 1.0 / math.sqrt(MEM_HEAD_DIM)
    for h, q_ref in enumerate((q0_ref, q1_ref, q2_ref, q3_ref)):
        c0 = h * MEM_HEAD_DIM
        k = kv_ref[0, :, c0:c0 + MEM_HEAD_DIM]
        v = kv_ref[0, :, MEM_WIDTH + c0:MEM_WIDTH + c0 + MEM_HEAD_DIM]
        logits = lax.dot_general(q_ref[0], k, (((1,), (1,)), ((), ())), preferred_element_type=f32) * scale
        m = jnp.max(logits, axis=-1, keepdims=True)
        p = jnp.exp(logits - m)
        ssum = jnp.sum(p, axis=-1, keepdims=True)
        o = jnp.dot(p.astype(bf16), v, preferred_element_type=f32) * (1.0 / ssum)
        o_ref[0, :, c0:c0 + MEM_HEAD_DIM] = o.astype(o_ref.dtype)


def _xattn(proj3, kv3, tq=1024):
    B, S, _ = proj3.shape
    qb0 = REST_QC // MEM_HEAD_DIM

    def qspec(h):
        return pl.BlockSpec((1, tq, MEM_HEAD_DIM), lambda b, t: (b, t, qb0 + h))

    return pl.pallas_call(
        _xattn_kernel,
        grid=(B, S // tq),
        in_specs=[qspec(0), qspec(1), qspec(2), qspec(3),
                  pl.BlockSpec((1, N_MEM, 2 * MEM_WIDTH), lambda b, t: (b, 0, 0))],
        out_specs=pl.BlockSpec((1, tq, MEM_WIDTH), lambda b, t: (b, t, 0)),
        out_shape=jax.ShapeDtypeStruct((B, S, MEM_WIDTH), bf16),
        compiler_params=_cparams("parallel", "parallel"),
        name="xattn",
    )(proj3, proj3, proj3, proj3, kv3)


def _combine_kernel(o0_ref, o1_ref, o2_ref, l0_ref, l1_ref, l2_ref, ya_ref, o1_scr, o2_scr, l1_scr, l2_scr, *, tm):
    for g, o_ref, l_ref, o_scr, l_scr in ((1, o1_ref, l1_ref, o1_scr, l1_scr),
                                          (2, o2_ref, l2_ref, o2_scr, l2_scr)):
        d = ATTN_GROUPS[g][1]
        for r in range(d):
            l_scr[pl.ds(r, tm // d, stride=d), :] = l_ref[0, r]
            for h in range(HEADS_PER_GROUP):
                o_scr[h, pl.ds(r, tm // d, stride=d), :] = (
                    o_ref[0, r, :, h * HEAD_DIM_A:(h + 1) * HEAD_DIM_A].astype(f32))
    l0, l1, l2 = l0_ref[...], l1_scr[...], l2_scr[...]
    m = jnp.maximum(jnp.maximum(l0, l1), l2)
    e0, e1, e2 = jnp.exp(l0 - m), jnp.exp(l1 - m), jnp.exp(l2 - m)
    inv = 1.0 / (e0 + e1 + e2)
    for h in range(HEADS_PER_GROUP):
        c0 = h * HEAD_DIM_A
        lane = slice(h * LSE_REP, h * LSE_REP + 1)
        y = ((e0 * inv)[:, lane] * o0_ref[:, c0:c0 + HEAD_DIM_A].astype(f32)
             + (e1 * inv)[:, lane] * o1_scr[h] + (e2 * inv)[:, lane] * o2_scr[h])
        ya_ref[:, c0:c0 + HEAD_DIM_A] = y.astype(bf16)


def _combine(o_groups, lse_groups, seq, tm=512):
    T = o_groups[0].shape[0]
    nt = seq // tm
    d1, d2 = ATTN_GROUPS[1][1], ATTN_GROUPS[2][1]

    def rows(width):
        return pl.BlockSpec((tm, width), lambda i: (i, 0))

    def strided_rows(d, width):
        return pl.BlockSpec((1, d, tm // d, width), lambda i: (i // nt, 0, i % nt, 0))

    return pl.pallas_call(
        functools.partial(_combine_kernel, tm=tm),
        grid=(T // tm,),
        in_specs=[rows(GROUP_WIDTH), strided_rows(d1, GROUP_WIDTH), strided_rows(d2, GROUP_WIDTH),
                  rows(LSE_LANES), strided_rows(d1, LSE_LANES), strided_rows(d2, LSE_LANES)],
        out_specs=rows(GROUP_WIDTH),
        out_shape=jax.ShapeDtypeStruct((T, GROUP_WIDTH), bf16),
        scratch_shapes=[pltpu.VMEM((HEADS_PER_GROUP, tm, HEAD_DIM_A), f32),
                        pltpu.VMEM((HEADS_PER_GROUP, tm, HEAD_DIM_A), f32),
                        pltpu.VMEM((tm, LSE_LANES), f32), pltpu.VMEM((tm, LSE_LANES), f32)],
        compiler_params=_cparams("parallel"),
        name="combine",
    )(*o_groups, *lse_groups)


def _gate_mix_kernel(h_ref, ya_ref, yl_ref, yc_ref, wga_ref, wgb_ref, wgc_ref, bga_ref, bgb_ref, bgc_ref,
                     woa_ref, wol_ref, wom_ref, mix_ref):
    h = h_ref[...]

    def gate(w_ref, b_ref):
        return jax.nn.sigmoid(jnp.dot(h, w_ref[...], preferred_element_type=f32) + b_ref[...])

    mixed = (gate(wga_ref, bga_ref) * jnp.dot(ya_ref[...], woa_ref[...], preferred_element_type=f32)
             + gate(wgb_ref, bgb_ref) * jnp.dot(yl_ref[...], wol_ref[...], preferred_element_type=f32)
             + gate(wgc_ref, bgc_ref) * jnp.dot(yc_ref[...], wom_ref[...], preferred_element_type=f32))
    mix_ref[...] = mixed.astype(mix_ref.dtype)


def _gate_mix(h, y_a, y_lru, y_c, w_gate, b_gate, w_o_attn, w_o_lru, w_o_mem, tm=512, tn=512):
    T = h.shape[0]
    nj = D_MODEL // tn

    def rows(width):
        return pl.BlockSpec((tm, width), lambda j, i: (i, 0))

    def gate_w(k):
        return pl.BlockSpec((D_MODEL, tn), lambda j, i: (0, k * nj + j))

    def gate_b(k):
        return pl.BlockSpec((1, tn), lambda j, i: (0, k * nj + j))

    def cols(width):
        return pl.BlockSpec((width, tn), lambda j, i: (0, j))

    return pl.pallas_call(
        _gate_mix_kernel,
        grid=(nj, T // tm),
        in_specs=[rows(D_MODEL), rows(GROUP_WIDTH), rows(LRU_WIDTH), rows(MEM_WIDTH),
                  gate_w(0), gate_w(1), gate_w(2), gate_b(0), gate_b(1), gate_b(2),
                  cols(GROUP_WIDTH), cols(LRU_WIDTH), cols(MEM_WIDTH)],
        out_specs=pl.BlockSpec((tm, tn), lambda j, i: (i, j)),
        out_shape=jax.ShapeDtypeStruct((T, D_MODEL), bf16),
        compiler_params=_cparams("arbitrary", "arbitrary"),
        name="gate_mix",
    )(h, y_a, y_lru, y_c, w_gate, w_gate, w_gate, b_gate, b_gate, b_gate, w_o_attn, w_o_lru, w_o_mem)


def _mlp_kernel(x_ref, mix_ref, wo_ref, g_ref, gf_ref, wu_ref, wd_ref, out_ref, h_scr):
    j = pl.program_id(1)

    @pl.when(j == 0)
    def _():
        x = x_ref[...] + jnp.dot(mix_ref[...], wo_ref[...], preferred_element_type=f32)
        h_scr[...] = _rms(x, g_ref[...]).astype(bf16)
        out_ref[...] = x

    u = jnp.maximum(jnp.dot(h_scr[...], wu_ref[...], preferred_element_type=f32), 0.0)
    out_ref[...] += jnp.dot((u * u).astype(bf16), wd_ref[...], preferred_element_type=f32)

    @pl.when(j == pl.num_programs(1) - 1)
    def _():
        out_ref[...] = _rms(out_ref[...], gf_ref[...])


def _mlp(x2, mixed, w_out, gain, gain_final, w_up, w_down, tm=512, tf=1024):
    T = x2.shape[0]
    return pl.pallas_call(
        _mlp_kernel,
        grid=(T // tm, D_FF // tf),
        in_specs=[pl.BlockSpec((tm, D_MODEL), lambda i, j: (i, 0)),
                  pl.BlockSpec((tm, D_MODEL), lambda i, j: (i, 0)),
                  pl.BlockSpec((D_MODEL, D_MODEL), lambda i, j: (0, 0)),
                  pl.BlockSpec((1, D_MODEL), lambda i, j: (0, 0)),
                  pl.BlockSpec((1, D_MODEL), lambda i, j: (0, 0)),
                  pl.BlockSpec((D_MODEL, tf), lambda i, j: (0, j)),
                  pl.BlockSpec((tf, D_MODEL), lambda i, j: (j, 0))],
        out_specs=pl.BlockSpec((tm, D_MODEL), lambda i, j: (i, 0)),
        out_shape=jax.ShapeDtypeStruct((T, D_MODEL), f32),
        scratch_shapes=[pltpu.VMEM((tm, D_MODEL), bf16)],
        compiler_params=_cparams("parallel", "arbitrary"),
        name="mlp",
    )(x2, mixed, w_out, gain, gain_final, w_up, w_down)


def kernel(x, mem, rel_bias, norm_mix, norm_mem, norm_mlp, norm_final, w_in, w_gate, b_gate, conv_w, conv_b,
           lru_wa, lru_ba, lru_wi, lru_bi, lru_lambda, w_mem_kv, w_o_attn, w_o_lru, w_o_mem, w_out, w_up, w_down):
    B, S, D = x.shape
    T = B * S
    depth = w_in.shape[0]
    assert depth == 1, "the final RMSNorm is fused into the (single) layer's MLP kernel"
    x2 = x.reshape(T, D)
    mem2 = mem.reshape(B * N_MEM, D)
    for l in range(depth):
        gain_mix = norm_mix[l].reshape(1, D)
        qkv0, qkv1, qkv2, rest, h = _in_proj(x2, gain_mix, w_in[l].astype(bf16), B, S)
        proj3 = rest.reshape(B, S, REST_W)

        attn = []
        for g, qkv in enumerate((qkv0, qkv1, qkv2)):
            d = ATTN_GROUPS[g][1]
            o, lse = _attn_group(qkv.reshape(B * d, S // d, QKV_W), rel_bias, g)
            if g == 0:
                attn.append((o.reshape(T, GROUP_WIDTH), lse.reshape(T, LSE_LANES)))
            else:
                attn.append((o.reshape(B, d, S // d, GROUP_WIDTH), lse.reshape(B, d, S // d, LSE_LANES)))

        w_gates = (0.5 * jnp.concatenate([lru_wa[l, 0], lru_wi[l, 0], lru_wa[l, 1], lru_wi[l, 1]], axis=-1)
                   ).astype(bf16)
        b_gates = 0.5 * jnp.concatenate([lru_ba[l, 0], lru_bi[l, 0], lru_ba[l, 1], lru_bi[l, 1]], axis=-1)
        y_lru = _lru(proj3, conv_w[l], conv_b[l].reshape(1, LRU_WIDTH), w_gates,
                     b_gates.reshape(LRU_BLOCKS, 1, 4 * LRU_BW), lru_lambda[l])

        kv = _mem_kv(mem2, norm_mem[l].reshape(1, D), w_mem_kv[l].astype(bf16))
        y_c = _xattn(proj3, kv.reshape(B, N_MEM, 2 * MEM_WIDTH))

        y_a = _combine([a[0] for a in attn], [a[1] for a in attn], S)
        mixed = _gate_mix(h, y_a, y_lru.reshape(T, LRU_WIDTH), y_c.reshape(T, MEM_WIDTH),
                          w_gate[l].astype(bf16), b_gate[l].reshape(1, 3 * D),
                          w_o_attn[l].astype(bf16), w_o_lru[l].astype(bf16), w_o_mem[l].astype(bf16))
        x2 = _mlp(x2, mixed, w_out[l].astype(bf16), norm_mlp[l].reshape(1, D), norm_final.reshape(1, D),
                  w_up[l].astype(bf16), w_down[l].astype(bf16))
    return x2.reshape(B, S, D)
```

```python
import functools
import math

import jax
import jax.numpy as jnp
import numpy as np
from jax import lax
from jax.experimental import pallas as pl
from jax.experimental.pallas import tpu as pltpu

D_MODEL = 2048
HEAD_DIM_A = 128
ATTN_GROUPS = ((128, 1), (512, 4), (2048, 16))
HEADS_PER_GROUP = 4
GROUP_WIDTH = HEADS_PER_GROUP * HEAD_DIM_A
WIDTH_A = len(ATTN_GROUPS) * GROUP_WIDTH
ATTN_RADIUS = 64
N_BUCKETS = 32
MAX_DISTANCE = 1024
LRU_WIDTH = 1536
LRU_BLOCKS = 12
LRU_BW = 128
LRU_C = 8.0
N_MEM = 256
MEM_HEADS = 4
MEM_HEAD_DIM = 256
MEM_WIDTH = MEM_HEADS * MEM_HEAD_DIM
D_FF = 4 * D_MODEL
EPS = 1e-6
N_IN = 3 * WIDTH_A + 2 * LRU_WIDTH + MEM_WIDTH
COL_K = WIDTH_A
COL_V = 2 * WIDTH_A
COL_XB = 3 * WIDTH_A
COL_YB = 3 * WIDTH_A + LRU_WIDTH
COL_QC = 3 * WIDTH_A + 2 * LRU_WIDTH
NEG_INF = -1e30

SUB_Q = 128
SUB_K = SUB_Q + 2 * ATTN_RADIUS
LSE_LANES = 128
LSE_REP = LSE_LANES // HEADS_PER_GROUP

VMEM_LIMIT = 56 * 1024 * 1024

f32 = jnp.float32
bf16 = jnp.bfloat16


def _cparams(*sem):
    return pltpu.CompilerParams(dimension_semantics=sem, vmem_limit_bytes=VMEM_LIMIT)


def _rms(x, gain):
    return x * lax.rsqrt(jnp.mean(x * x, axis=-1, keepdims=True) + EPS) * gain


QKV_W = 3 * GROUP_WIDTH
REST_W = 2 * LRU_WIDTH + MEM_WIDTH
REST_YB = LRU_WIDTH
REST_QC = 2 * LRU_WIDTH
PROJ_TN = GROUP_WIDTH
LANES = 128
SLABS = PROJ_TN // LANES
PROJ_ROW_BLOCKS = 2
DEINTERLEAVE_STEP = 4
N_STAGE = 2


def _norm_kernel(x_ref, g_ref, h_ref):
    h_ref[...] = _rms(x_ref[...], g_ref[...]).astype(h_ref.dtype)


def _norm(x2, gain, tm=1024):
    T = x2.shape[0]
    return pl.pallas_call(
        _norm_kernel,
        grid=(T // tm,),
        in_specs=[pl.BlockSpec((tm, D_MODEL), lambda i: (i, 0)),
                  pl.BlockSpec((1, D_MODEL), lambda i: (0, 0))],
        out_specs=pl.BlockSpec((tm, D_MODEL), lambda i: (i, 0)),
        out_shape=jax.ShapeDtypeStruct((T, D_MODEL), bf16),
        compiler_params=_cparams("parallel"),
        name="norm",
    )(x2, gain)


def _qkv_kernel(h_ref, wq_ref, wk_ref, wv_ref, o_ref, res_scr, tmp_scr, *, d, tm):
    mb = tm // PROJ_ROW_BLOCKS
    n = 0
    for k in range(PROJ_ROW_BLOCKS):
        hk = h_ref[k * mb:(k + 1) * mb, :]
        for t, w_ref in enumerate((wq_ref, wk_ref, wv_ref)):
            res = jnp.dot(hk, w_ref[...], preferred_element_type=f32)
            col = t * PROJ_TN
            if d == 1:
                o_ref[0, k * mb:(k + 1) * mb, col:col + PROJ_TN] = res.astype(bf16)
                continue
            buf = n % N_STAGE
            n += 1
            for c in range(SLABS):
                res_scr[buf, c] = res[:, c * LANES:(c + 1) * LANES]
            src, step = res_scr, d
            if d == DEINTERLEAVE_STEP ** 2:
                step = DEINTERLEAVE_STEP
                for r in range(step):
                    for c in range(SLABS):
                        tmp_scr[buf, c, r * (mb // step):(r + 1) * (mb // step), :] = (
                            res_scr[buf, c, pl.ds(r, mb // step, stride=step), :])
                src = tmp_scr
            for r in range(d):
                start = r if src is res_scr else (r % step) * (mb // step) + r // step
                for c in range(SLABS):
                    o_ref[0, r, k * (mb // d):(k + 1) * (mb // d), col + c * LANES:col + (c + 1) * LANES] = (
                        src[buf, c, pl.ds(start, mb // d, stride=step), :].astype(bf16))


def _qkv_proj(h, w_qkv, g, batch, seq, tm=1024):
    T = h.shape[0]
    d = ATTN_GROUPS[g][1]
    nt = seq // tm
    n_groups = len(ATTN_GROUPS)
    mb = tm // PROJ_ROW_BLOCKS

    def w_spec(which):
        return pl.BlockSpec((D_MODEL, PROJ_TN), lambda i: (0, which * n_groups + g))

    if d == 1:
        out_spec = pl.BlockSpec((1, tm, QKV_W), lambda i: (i // nt, i % nt, 0))
        out_shape = jax.ShapeDtypeStruct((batch, seq, QKV_W), bf16)
    else:
        out_spec = pl.BlockSpec((1, d, tm // d, QKV_W), lambda i: (i // nt, 0, i % nt, 0))
        out_shape = jax.ShapeDtypeStruct((batch, d, seq // d, QKV_W), bf16)
    return pl.pallas_call(
        functools.partial(_qkv_kernel, d=d, tm=tm),
        grid=(T // tm,),
        in_specs=[pl.BlockSpec((tm, D_MODEL), lambda i: (i, 0)), w_spec(0), w_spec(1), w_spec(2)],
        out_specs=out_spec,
        out_shape=out_shape,
        scratch_shapes=[pltpu.VMEM((N_STAGE, SLABS, mb, LANES), f32)] * 2,
        compiler_params=_cparams("parallel"),
        name=f"qkv_g{g}",
    )(h, w_qkv, w_qkv, w_qkv)


def _rest_kernel(h_ref, w_ref, o_ref, *, tm, tn):
    mb = tm // PROJ_ROW_BLOCKS
    for k in range(PROJ_ROW_BLOCKS):
        hk = h_ref[k * mb:(k + 1) * mb, :]
        for c in range(tn // PROJ_TN):
            cols = slice(c * PROJ_TN, (c + 1) * PROJ_TN)
            o_ref[k * mb:(k + 1) * mb, cols] = jnp.dot(hk, w_ref[:, cols], preferred_element_type=f32).astype(bf16)


def _rest_proj(h, w_rest, tm=1024, tn=2048):
    T = h.shape[0]
    return pl.pallas_call(
        functools.partial(_rest_kernel, tm=tm, tn=tn),
        grid=(REST_W // tn, T // tm),
        in_specs=[pl.BlockSpec((tm, D_MODEL), lambda j, i: (i, 0)),
                  pl.BlockSpec((D_MODEL, tn), lambda j, i: (0, j))],
        out_specs=pl.BlockSpec((tm, tn), lambda j, i: (i, j)),
        out_shape=jax.ShapeDtypeStruct((T, REST_W), bf16),
        compiler_params=_cparams("arbitrary", "arbitrary"),
        name="rest_proj",
    )(h, w_rest)


def _t5_bucket(rel):
    nb = N_BUCKETS // 2
    max_exact = nb // 2
    sign = (rel > 0).astype(np.int32) * nb
    n = np.abs(rel)
    large = max_exact + (np.log(np.maximum(n, 1) / max_exact)
                         / np.log(MAX_DISTANCE / max_exact) * (nb - max_exact)).astype(np.int32)
    large = np.minimum(large, nb - 1)
    return (sign + np.where(n < max_exact, n, large)).astype(np.int32)


def _band_bias(rel_bias_g, dilation):
    qq = np.arange(SUB_Q)[:, None]
    kk = np.arange(SUB_K)[None, :]
    rel = kk - ATTN_RADIUS - qq
    onehot = (_t5_bucket(rel * dilation)[None] == np.arange(N_BUCKETS)[:, None, None]).astype(np.float32)
    bias = jnp.einsum('nh,nqk->hqk', rel_bias_g.astype(f32), onehot, precision=lax.Precision.HIGHEST)
    return bias + np.where(np.abs(rel) <= ATTN_RADIUS, 0.0, NEG_INF).astype(np.float32)[None]


def _attn_kernel(q_ref, kp_ref, km_ref, kn_ref, vp_ref, vm_ref, vn_ref, bias_ref,
                 o_ref, lse_ref, kbuf, vbuf, *, tq, seq):
    R = ATTN_RADIUS
    kbuf[0:R] = kp_ref[0]
    kbuf[R:R + tq] = km_ref[0]
    kbuf[R + tq:] = kn_ref[0]
    vbuf[0:R] = vp_ref[0]
    vbuf[R:R + tq] = vm_ref[0]
    vbuf[R + tq:] = vn_ref[0]
    q0 = pl.program_id(1) * tq
    scale = 1.0 / math.sqrt(HEAD_DIM_A)
    lane = lax.broadcasted_iota(jnp.int32, (SUB_Q, LSE_LANES), 1)
    for s in range(tq // SUB_Q):
        r0 = s * SUB_Q
        pos = q0 + (r0 - R) + lax.broadcasted_iota(jnp.int32, (1, SUB_K), 1)
        edge = jnp.where(pos >= 0, jnp.where(pos < seq, 0.0, NEG_INF), NEG_INF)
        lse_tile = None
        for h in range(HEADS_PER_GROUP):
            c0 = h * HEAD_DIM_A
            q = q_ref[0, r0:r0 + SUB_Q, c0:c0 + HEAD_DIM_A]
            k = kbuf[r0:r0 + SUB_K, c0:c0 + HEAD_DIM_A]
            v = vbuf[r0:r0 + SUB_K, c0:c0 + HEAD_DIM_A]
            logits = lax.dot_general(q, k, (((1,), (1,)), ((), ())), preferred_element_type=f32)
            logits = logits * scale + bias_ref[h] + edge
            m = jnp.max(logits, axis=-1, keepdims=True)
            p = jnp.exp(logits - m)
            ssum = jnp.sum(p, axis=-1, keepdims=True)
            o = jnp.dot(p.astype(bf16), v, preferred_element_type=f32) * (1.0 / ssum)
            o_ref[0, r0:r0 + SUB_Q, c0:c0 + HEAD_DIM_A] = o.astype(o_ref.dtype)
            lse = m + jnp.log(ssum)
            lse_tile = lse if lse_tile is None else jnp.where(lane >= h * LSE_REP, lse, lse_tile)
        lse_ref[0, r0:r0 + SUB_Q, :] = jnp.broadcast_to(lse_tile, (SUB_Q, LSE_LANES))


def _attn_group(qkv, rel_bias, g):
    _, d = ATTN_GROUPS[g]
    n, L, _ = qkv.shape
    tq = min(512, L)
    R = ATTN_RADIUS
    bias = _band_bias(rel_bias[:, g * HEADS_PER_GROUP:(g + 1) * HEADS_PER_GROUP], d)
    rb = tq // R
    last_rb = L // R - 1

    def main(col):
        return pl.BlockSpec((1, tq, GROUP_WIDTH), lambda b, t: (b, t, col))

    def prev(col):
        return pl.BlockSpec((1, R, GROUP_WIDTH), lambda b, t: (b, jnp.maximum(t * rb - 1, 0), col))

    def nxt(col):
        return pl.BlockSpec((1, R, GROUP_WIDTH), lambda b, t: (b, jnp.minimum((t + 1) * rb, last_rb), col))

    return pl.pallas_call(
        functools.partial(_attn_kernel, tq=tq, seq=L),
        grid=(n, L // tq),
        in_specs=[main(0), prev(1), main(1), nxt(1), prev(2), main(2), nxt(2),
                  pl.BlockSpec((HEADS_PER_GROUP, SUB_Q, SUB_K), lambda b, t: (0, 0, 0))],
        out_specs=[pl.BlockSpec((1, tq, GROUP_WIDTH), lambda b, t: (b, t, 0)),
                   pl.BlockSpec((1, tq, LSE_LANES), lambda b, t: (b, t, 0))],
        out_shape=[jax.ShapeDtypeStruct((n, L, GROUP_WIDTH), bf16),
                   jax.ShapeDtypeStruct((n, L, LSE_LANES), f32)],
        scratch_shapes=[pltpu.VMEM((tq + 2 * R, GROUP_WIDTH), bf16),
                        pltpu.VMEM((tq + 2 * R, GROUP_WIDTH), bf16)],
        compiler_params=_cparams("parallel", "arbitrary"),
        name=f"attn_g{g}",
    )(qkv, qkv, qkv, qkv, qkv, qkv, qkv, bias)


LRU_CHUNK = 256
LRU_PAD = 8
LRU_FINISH_ROWS = 512
LRU_SEGS = 8
SEG_GAP = 8


def _lru_kernel(xb_ref, yb_ref, cw_ref, cb_ref, w_ref, gb_ref, lam_ref, o_ref,
                xpad, af, bf, ab, bb, htf, ptf, htb, ptb, cf_scr, cb_scr, *, seq):
    R = LRU_CHUNK
    P = LRU_PAD
    seg_len = seq // LRU_SEGS
    pitch = seg_len + SEG_GAP
    chunks_per_seg = seg_len // R
    n_chunks = seq // R
    xpad[0:P] = jnp.zeros((P, LRU_BW), f32)
    xpad[P + seq:] = jnp.zeros((P, LRU_BW), f32)
    xpad[P:P + seq] = xb_ref[0].astype(f32)
    lam = lam_ref[...]
    log_a_unit = -LRU_C * (jnp.maximum(-lam, 0.0) + jnp.log1p(jnp.exp(-jnp.abs(lam))))
    cw = cw_ref[...]
    cb = cb_ref[...]
    gate_bias = gb_ref[0]
    row = lax.broadcasted_iota(jnp.int32, (R, LRU_BW), 0)

    def chunk(ci, first=False, last=False):
        c0 = ci * R if isinstance(ci, int) else pl.multiple_of(ci * R, R)
        dst = (ci // chunks_per_seg) * pitch + (ci % chunks_per_seg) * R
        dst = dst if isinstance(ci, int) else pl.multiple_of(dst, 8)
        xc = (cw[0:1] * xpad[pl.ds(c0 + (P - 1), R), :] + cw[1:2] * xpad[pl.ds(c0 + P, R), :]
              + cw[2:3] * xpad[pl.ds(c0 + (P + 1), R), :] + cw[3:4] * xpad[pl.ds(c0 + (P + 2), R), :]) + cb
        th = jnp.tanh(jnp.dot(xc.astype(bf16), w_ref[0], preferred_element_type=f32) + gate_bias)
        half_xc = 0.5 * xc
        for direction, (a_scr, b_scr) in enumerate(((af, bf), (ab, bb))):
            base = direction * 2 * LRU_BW
            half_log_a = 0.5 * log_a_unit[direction:direction + 1]
            a = jnp.exp(half_log_a * th[:, base:base + LRU_BW] + half_log_a)
            gated_x = half_xc * th[:, base + LRU_BW:base + 2 * LRU_BW] + half_xc
            y = 1.0 - a * a
            mult = y * lax.rsqrt(jnp.maximum(y, 1e-30))
            if direction == 0 and first:
                mult = jnp.where(row == 0, 1.0, mult)
            if direction == 1 and last:
                mult = jnp.where(row == R - 1, 1.0, mult)
            a_scr[pl.ds(dst, R), :] = a
            b_scr[pl.ds(dst, R), :] = mult * gated_x

    chunk(0, first=True)
    lax.fori_loop(1, n_chunks - 1, lambda ci, c: (chunk(ci), c)[1], 0, unroll=2)
    chunk(n_chunks - 1, last=True)

    def scan(i, carry):
        hf, pf, hb, pb = carry
        a = af[pl.ds(i, LRU_SEGS, stride=pitch), :]
        hf = a * hf + bf[pl.ds(i, LRU_SEGS, stride=pitch), :]
        pf = a * pf
        htf[pl.ds(pl.multiple_of(i * LRU_SEGS, LRU_SEGS), LRU_SEGS), :] = hf
        ptf[pl.ds(pl.multiple_of(i * LRU_SEGS, LRU_SEGS), LRU_SEGS), :] = pf
        ib = seg_len - 1 - i
        a = ab[pl.ds(ib, LRU_SEGS, stride=pitch), :]
        hb = a * hb + bb[pl.ds(ib, LRU_SEGS, stride=pitch), :]
        pb = a * pb
        htb[pl.ds(pl.multiple_of(ib * LRU_SEGS, LRU_SEGS), LRU_SEGS), :] = hb
        ptb[pl.ds(pl.multiple_of(ib * LRU_SEGS, LRU_SEGS), LRU_SEGS), :] = pb
        return hf, pf, hb, pb

    zero = jnp.zeros((LRU_SEGS, LRU_BW), f32)
    one = jnp.ones((LRU_SEGS, LRU_BW), f32)
    hf, pf, hb, pb = lax.fori_loop(0, seg_len, scan, (zero, one, zero, one), unroll=8)

    c = jnp.zeros((1, LRU_BW), f32)
    cf_scr[0:1] = c
    for j in range(1, LRU_SEGS):
        c = hf[j - 1:j] + pf[j - 1:j] * c
        cf_scr[j:j + 1] = c
    c = jnp.zeros((1, LRU_BW), f32)
    cb_scr[LRU_SEGS - 1:LRU_SEGS] = c
    for j in range(LRU_SEGS - 2, -1, -1):
        c = hb[j + 1:j + 2] + pb[j + 1:j + 2] * c
        cb_scr[j:j + 1] = c

    F = LRU_FINISH_ROWS
    finish_per_seg = seg_len // F

    def finish(ci, carry):
        c0 = pl.multiple_of(ci * F, F)
        seg = ci // finish_per_seg
        src = (ci % finish_per_seg) * (F * LRU_SEGS) + seg
        h = (htf[pl.ds(src, F, stride=LRU_SEGS), :] + ptf[pl.ds(src, F, stride=LRU_SEGS), :] * cf_scr[pl.ds(seg, 1), :]
             + htb[pl.ds(src, F, stride=LRU_SEGS), :] + ptb[pl.ds(src, F, stride=LRU_SEGS), :] * cb_scr[pl.ds(seg, 1), :])
        y = yb_ref[0, pl.ds(c0, F), :].astype(f32)
        gelu = y * (0.5 * (1.0 + jnp.tanh(math.sqrt(2.0 / math.pi) * (y + 0.044715 * (y * y * y)))))
        o_ref[0, pl.ds(c0, F), :] = (h * gelu).astype(o_ref.dtype)
        return carry

    lax.fori_loop(0, seq // F, finish, 0)


def _lru(proj3, conv_w, conv_b, w_gates, b_gates, lam):
    B, S, _ = proj3.shape
    xb0 = 0
    yb0 = REST_YB // LRU_BW
    return pl.pallas_call(
        functools.partial(_lru_kernel, seq=S),
        grid=(B, LRU_BLOCKS),
        in_specs=[
            pl.BlockSpec((1, S, LRU_BW), lambda b, n: (b, 0, xb0 + n)),
            pl.BlockSpec((1, S, LRU_BW), lambda b, n: (b, 0, yb0 + n)),
            pl.BlockSpec((4, LRU_BW), lambda b, n: (0, n)),
            pl.BlockSpec((1, LRU_BW), lambda b, n: (0, n)),
            pl.BlockSpec((1, LRU_BW, 4 * LRU_BW), lambda b, n: (n, 0, 0)),
            pl.BlockSpec((1, 1, 4 * LRU_BW), lambda b, n: (n, 0, 0)),
            pl.BlockSpec((2, LRU_BW), lambda b, n: (0, n)),
        ],
        out_specs=pl.BlockSpec((1, S, LRU_BW), lambda b, n: (b, 0, n)),
        out_shape=jax.ShapeDtypeStruct((B, S, LRU_WIDTH), bf16),
        scratch_shapes=([pltpu.VMEM((S + 2 * LRU_PAD, LRU_BW), f32)]
                        + [pltpu.VMEM((S + LRU_SEGS * SEG_GAP, LRU_BW), f32)] * 4
                        + [pltpu.VMEM((S, LRU_BW), f32)] * 4
                        + [pltpu.VMEM((LRU_SEGS, LRU_BW), f32)] * 2),
        compiler_params=_cparams("parallel", "parallel"),
        name="lru",
    )(proj3, proj3, conv_w, conv_b, w_gates, b_gates, lam)


def _mem_kv_kernel(m_ref, g_ref, w_ref, o_ref, h_scr):
    @pl.when(pl.program_id(0) == 0)
    def _():
        h_scr[...] = _rms(m_ref[...], g_ref[...]).astype(bf16)

    o_ref[...] = jnp.dot(h_scr[...], w_ref[...], preferred_element_type=f32).astype(o_ref.dtype)


def _mem_kv(mem2, gain, w, tn=512):
    M = mem2.shape[0]
    N = w.shape[1]
    return pl.pallas_call(
        _mem_kv_kernel,
        grid=(N // tn,),
        in_specs=[pl.BlockSpec((M, D_MODEL), lambda j: (0, 0)),
                  pl.BlockSpec((1, D_MODEL), lambda j: (0, 0)),
                  pl.BlockSpec((D_MODEL, tn), lambda j: (0, j))],
        out_specs=pl.BlockSpec((M, tn), lambda j: (0, j)),
        out_shape=jax.ShapeDtypeStruct((M, N), bf16),
        scratch_shapes=[pltpu.VMEM((M, D_MODEL), bf16)],
        compiler_params=_cparams("arbitrary"),
        name="mem_kv",
    )(mem2, gain, w)


def _xattn_kernel(q0_ref, q1_ref, q2_ref, q3_ref, kv_ref, o_ref):
    scale = 1.0 / math.sqrt(MEM_HEAD_DIM)
    for h, q_ref in enumerate((q0_ref, q1_ref, q2_ref, q3_ref)):
        c0 = h * MEM_HEAD_DIM
        k = kv_ref[0, :, c0:c0 + MEM_HEAD_DIM]
        v = kv_ref[0, :, MEM_WIDTH + c0:MEM_WIDTH + c0 + MEM_HEAD_DIM]
        logits = lax.dot_general(q_ref[0], k, (((1,), (1,)), ((), ())), preferred_element_type=f32) * scale
        m = jnp.max(logits, axis=-1, keepdims=True)
        p = jnp.exp(logits - m)
        ssum = jnp.sum(p, axis=-1, keepdims=True)
        o = jnp.dot(p.astype(bf16), v, preferred_element_type=f32) * (1.0 / ssum)
        o_ref[0, :, c0:c0 + MEM_HEAD_DIM] = o.astype(o_ref.dtype)


def _xattn(proj3, kv3, tq=1024):
    B, S, _ = proj3.shape
    qb0 = REST_QC // MEM_HEAD_DIM

    def qspec(h):
        return pl.BlockSpec((1, tq, MEM_HEAD_DIM), lambda b, t: (b, t, qb0 + h))

    return pl.pallas_call(
        _xattn_kernel,
        grid=(B, S // tq),
        in_specs=[qspec(0), qspec(1), qspec(2), qspec(3),
                  pl.BlockSpec((1, N_MEM, 2 * MEM_WIDTH), lambda b, t: (b, 0, 0))],
        out_specs=pl.BlockSpec((1, tq, MEM_WIDTH), lambda b, t: (b, t, 0)),
        out_shape=jax.ShapeDtypeStruct((B, S, MEM_WIDTH), bf16),
        compiler_params=_cparams("parallel", "parallel"),
        name="xattn",
    )(proj3, proj3, proj3, proj3, kv3)


def _combine_kernel(o0_ref, o1_ref, o2_ref, l0_ref, l1_ref, l2_ref, ya_ref, o1_scr, o2_scr, l1_scr, l2_scr, *, tm):
    for g, o_ref, l_ref, o_scr, l_scr in ((1, o1_ref, l1_ref, o1_scr, l1_scr),
                                          (2, o2_ref, l2_ref, o2_scr, l2_scr)):
        d = ATTN_GROUPS[g][1]
        for r in range(d):
            l_scr[pl.ds(r, tm // d, stride=d), :] = l_ref[0, r]
            for h in range(HEADS_PER_GROUP):
                o_scr[h, pl.ds(r, tm // d, stride=d), :] = (
                    o_ref[0, r, :, h * HEAD_DIM_A:(h + 1) * HEAD_DIM_A].astype(f32))
    l0, l1, l2 = l0_ref[...], l1_scr[...], l2_scr[...]
    m = jnp.maximum(jnp.maximum(l0, l1), l2)
    e0, e1, e2 = jnp.exp(l0 - m), jnp.exp(l1 - m), jnp.exp(l2 - m)
    inv = 1.0 / (e0 + e1 + e2)
    for h in range(HEADS_PER_GROUP):
        c0 = h * HEAD_DIM_A
        lane = slice(h * LSE_REP, h * LSE_REP + 1)
        y = ((e0 * inv)[:, lane] * o0_ref[:, c0:c0 + HEAD_DIM_A].astype(f32)
             + (e1 * inv)[:, lane] * o1_scr[h] + (e2 * inv)[:, lane] * o2_scr[h])
        ya_ref[:, c0:c0 + HEAD_DIM_A] = y.astype(bf16)


def _combine(o_groups, lse_groups, seq, tm=512):
    T = o_groups[0].shape[0]
    nt = seq // tm
    d1, d2 = ATTN_GROUPS[1][1], ATTN_GROUPS[2][1]

    def rows(width):
        return pl.BlockSpec((tm, width), lambda i: (i, 0))

    def strided_rows(d, width):
        return pl.BlockSpec((1, d, tm // d, width), lambda i: (i // nt, 0, i % nt, 0))

    return pl.pallas_call(
        functools.partial(_combine_kernel, tm=tm),
        grid=(T // tm,),
        in_specs=[rows(GROUP_WIDTH), strided_rows(d1, GROUP_WIDTH), strided_rows(d2, GROUP_WIDTH),
                  rows(LSE_LANES), strided_rows(d1, LSE_LANES), strided_rows(d2, LSE_LANES)],
        out_specs=rows(GROUP_WIDTH),
        out_shape=jax.ShapeDtypeStruct((T, GROUP_WIDTH), bf16),
        scratch_shapes=[pltpu.VMEM((HEADS_PER_GROUP, tm, HEAD_DIM_A), f32),
                        pltpu.VMEM((HEADS_PER_GROUP, tm, HEAD_DIM_A), f32),
                        pltpu.VMEM((tm, LSE_LANES), f32), pltpu.VMEM((tm, LSE_LANES), f32)],
        compiler_params=_cparams("parallel"),
        name="combine",
    )(*o_groups, *lse_groups)


def _gate_mix_kernel(h_ref, ya_ref, yl_ref, yc_ref, wga_ref, wgb_ref, wgc_ref, bga_ref, bgb_ref, bgc_ref,
                     woa_ref, wol_ref, wom_ref, mix_ref):
    h = h_ref[...]

    def gate(w_ref, b_ref):
        return jax.nn.sigmoid(jnp.dot(h, w_ref[...], preferred_element_type=f32) + b_ref[...])

    mixed = (gate(wga_ref, bga_ref) * jnp.dot(ya_ref[...], woa_ref[...], preferred_element_type=f32)
             + gate(wgb_ref, bgb_ref) * jnp.dot(yl_ref[...], wol_ref[...], preferred_element_type=f32)
             + gate(wgc_ref, bgc_ref) * jnp.dot(yc_ref[...], wom_ref[...], preferred_element_type=f32))
    mix_ref[...] = mixed.astype(mix_ref.dtype)


def _gate_mix(h, y_a, y_lru, y_c, w_gate, b_gate, w_o_attn, w_o_lru, w_o_mem, tm=512, tn=512):
    T = h.shape[0]
    nj = D_MODEL // tn

    def rows(width):
        return pl.BlockSpec((tm, width), lambda j, i: (i, 0))

    def gate_w(k):
        return pl.BlockSpec((D_MODEL, tn), lambda j, i: (0, k * nj + j))

    def gate_b(k):
        return pl.BlockSpec((1, tn), lambda j, i: (0, k * nj + j))

    def cols(width):
        return pl.BlockSpec((width, tn), lambda j, i: (0, j))

    return pl.pallas_call(
        _gate_mix_kernel,
        grid=(nj, T // tm),
        in_specs=[rows(D_MODEL), rows(GROUP_WIDTH), rows(LRU_WIDTH), rows(MEM_WIDTH),
                  gate_w(0), gate_w(1), gate_w(2), gate_b(0), gate_b(1), gate_b(2),
                  cols(GROUP_WIDTH), cols(LRU_WIDTH), cols(MEM_WIDTH)],
        out_specs=pl.BlockSpec((tm, tn), lambda j, i: (i, j)),
        out_shape=jax.ShapeDtypeStruct((T, D_MODEL), bf16),
        compiler_params=_cparams("arbitrary", "arbitrary"),
        name="gate_mix",
    )(h, y_a, y_lru, y_c, w_gate, w_gate, w_gate, b_gate, b_gate, b_gate, w_o_attn, w_o_lru, w_o_mem)


def _mlp_kernel(x_ref, mix_ref, wo_ref, g_ref, gf_ref, wu_ref, wd_ref, out_ref, h_scr):
    j = pl.program_id(1)

    @pl.when(j == 0)
    def _():
        x = x_ref[...] + jnp.dot(mix_ref[...], wo_ref[...], preferred_element_type=f32)
        h_scr[...] = _rms(x, g_ref[...]).astype(bf16)
        out_ref[...] = x

    u = jnp.maximum(jnp.dot(h_scr[...], wu_ref[...], preferred_element_type=f32), 0.0)
    out_ref[...] += jnp.dot((u * u).astype(bf16), wd_ref[...], preferred_element_type=f32)

    @pl.when(j == pl.num_programs(1) - 1)
    def _():
        out_ref[...] = _rms(out_ref[...], gf_ref[...])


def _mlp(x2, mixed, w_out, gain, gain_final, w_up, w_down, tm=512, tf=1024):
    T = x2.shape[0]
    return pl.pallas_call(
        _mlp_kernel,
        grid=(T // tm, D_FF // tf),
        in_specs=[pl.BlockSpec((tm, D_MODEL), lambda i, j: (i, 0)),
                  pl.BlockSpec((tm, D_MODEL), lambda i, j: (i, 0)),
                  pl.BlockSpec((D_MODEL, D_MODEL), lambda i, j: (0, 0)),
                  pl.BlockSpec((1, D_MODEL), lambda i, j: (0, 0)),
                  pl.BlockSpec((1, D_MODEL), lambda i, j: (0, 0)),
                  pl.BlockSpec((D_MODEL, tf), lambda i, j: (0, j)),
                  pl.BlockSpec((tf, D_MODEL), lambda i, j: (j, 0))],
        out_specs=pl.BlockSpec((tm, D_MODEL), lambda i, j: (i, 0)),
        out_shape=jax.ShapeDtypeStruct((T, D_MODEL), f32),
        scratch_shapes=[pltpu.VMEM((tm, D_MODEL), bf16)],
        compiler_params=_cparams("parallel", "arbitrary"),
        name="mlp",
    )(x2, mixed, w_out, gain, gain_final, w_up, w_down)


def kernel(x, mem, rel_bias, norm_mix, norm_mem, norm_mlp, norm_final, w_in, w_gate, b_gate, conv_w, conv_b,
           lru_wa, lru_ba, lru_wi, lru_bi, lru_lambda, w_mem_kv, w_o_attn, w_o_lru, w_o_mem, w_out, w_up, w_down):
    B, S, D = x.shape
    T = B * S
    depth = w_in.shape[0]
    assert depth == 1, "the final RMSNorm is fused into the (single) layer's MLP kernel"
    x2 = x.reshape(T, D)
    mem2 = mem.reshape(B * N_MEM, D)
    for l in range(depth):
        h = _norm(x2, norm_mix[l].reshape(1, D))
        w_qkv = w_in[l][:, :COL_XB].astype(bf16)
        rest = _rest_proj(h, w_in[l][:, COL_XB:].astype(bf16))
        proj3 = rest.reshape(B, S, REST_W)

        attn = []
        for g in range(len(ATTN_GROUPS)):
            d = ATTN_GROUPS[g][1]
            qkv = _qkv_proj(h, w_qkv, g, B, S)
            o, lse = _attn_group(qkv.reshape(B * d, S // d, QKV_W), rel_bias, g)
            if g == 0:
                attn.append((o.reshape(T, GROUP_WIDTH), lse.reshape(T, LSE_LANES)))
            else:
                attn.append((o.reshape(B, d, S // d, GROUP_WIDTH), lse.reshape(B, d, S // d, LSE_LANES)))

        w_gates = (0.5 * jnp.concatenate([lru_wa[l, 0], lru_wi[l, 0], lru_wa[l, 1], lru_wi[l, 1]], axis=-1)
                   ).astype(bf16)
        b_gates = 0.5 * jnp.concatenate([lru_ba[l, 0], lru_bi[l, 0], lru_ba[l, 1], lru_bi[l, 1]], axis=-1)
        y_lru = _lru(proj3, conv_w[l], conv_b[l].reshape(1, LRU_WIDTH), w_gates,
                     b_gates.reshape(LRU_BLOCKS, 1, 4 * LRU_BW), lru_lambda[l])

        kv = _mem_kv(mem2, norm_mem[l].reshape(1, D), w_mem_kv[l].astype(bf16))
        y_c = _xattn(proj3, kv.reshape(B, N_MEM, 2 * MEM_WIDTH))

        y_a = _combine([a[0] for a in attn], [a[1] for a in attn], S)
        mixed = _gate_mix(h, y_a, y_lru.reshape(T, LRU_WIDTH), y_c.reshape(T, MEM_WIDTH),
                          w_gate[l].astype(bf16), b_gate[l].reshape(1, 3 * D),
                          w_o_attn[l].astype(bf16), w_o_lru[l].astype(bf16), w_o_mem[l].astype(bf16))
        x2 = _mlp(x2, mixed, w_out[l].astype(bf16), norm_mlp[l].reshape(1, D), norm_final.reshape(1, D),
                  w_up[l].astype(bf16), w_down[l].astype(bf16))
    return x2.reshape(B, S, D)
```

```python
import functools
import math

import jax
import jax.numpy as jnp
import numpy as np
from jax import lax
from jax.experimental import pallas as pl
from jax.experimental.pallas import tpu as pltpu

D_MODEL = 2048
HEAD_DIM_A = 128
ATTN_GROUPS = ((128, 1), (512, 4), (2048, 16))
HEADS_PER_GROUP = 4
GROUP_WIDTH = HEADS_PER_GROUP * HEAD_DIM_A
WIDTH_A = len(ATTN_GROUPS) * GROUP_WIDTH
ATTN_RADIUS = 64
N_BUCKETS = 32
MAX_DISTANCE = 1024
LRU_WIDTH = 1536
LRU_BLOCKS = 12
LRU_BW = 128
LRU_C = 8.0
N_MEM = 256
MEM_HEADS = 4
MEM_HEAD_DIM = 256
MEM_WIDTH = MEM_HEADS * MEM_HEAD_DIM
D_FF = 4 * D_MODEL
EPS = 1e-6
N_IN = 3 * WIDTH_A + 2 * LRU_WIDTH + MEM_WIDTH
COL_K = WIDTH_A
COL_V = 2 * WIDTH_A
COL_XB = 3 * WIDTH_A
COL_YB = 3 * WIDTH_A + LRU_WIDTH
COL_QC = 3 * WIDTH_A + 2 * LRU_WIDTH
NEG_INF = -1e30

SUB_Q = 128
SUB_K = SUB_Q + 2 * ATTN_RADIUS
LSE_LANES = 128
LSE_REP = LSE_LANES // HEADS_PER_GROUP

VMEM_LIMIT = 56 * 1024 * 1024

f32 = jnp.float32
bf16 = jnp.bfloat16


def _cparams(*sem):
    return pltpu.CompilerParams(dimension_semantics=sem, vmem_limit_bytes=VMEM_LIMIT)


def _rms(x, gain):
    return x * lax.rsqrt(jnp.mean(x * x, axis=-1, keepdims=True) + EPS) * gain


QKV_W = 3 * GROUP_WIDTH
REST_W = 2 * LRU_WIDTH + MEM_WIDTH
REST_YB = LRU_WIDTH
REST_QC = 2 * LRU_WIDTH
PROJ_TN = GROUP_WIDTH
LANES = 128
SLABS = PROJ_TN // LANES
PROJ_ROW_BLOCKS = 2
DEINTERLEAVE_STEP = 4
N_STAGE = 2


def _qkv_kernel(h_ref, wq_ref, wk_ref, wv_ref, o_ref, res_scr, tmp_scr, *, d, tm):
    mb = tm // PROJ_ROW_BLOCKS
    n = 0
    for k in range(PROJ_ROW_BLOCKS):
        hk = h_ref[k * mb:(k + 1) * mb, :]
        for t, w_ref in enumerate((wq_ref, wk_ref, wv_ref)):
            res = jnp.dot(hk, w_ref[...], preferred_element_type=f32)
            col = t * PROJ_TN
            if d == 1:
                o_ref[0, k * mb:(k + 1) * mb, col:col + PROJ_TN] = res.astype(bf16)
                continue
            buf = n % N_STAGE
            n += 1
            for c in range(SLABS):
                res_scr[buf, c] = res[:, c * LANES:(c + 1) * LANES]
            src, step = res_scr, d
            if d == DEINTERLEAVE_STEP ** 2:
                step = DEINTERLEAVE_STEP
                for r in range(step):
                    for c in range(SLABS):
                        tmp_scr[buf, c, r * (mb // step):(r + 1) * (mb // step), :] = (
                            res_scr[buf, c, pl.ds(r, mb // step, stride=step), :])
                src = tmp_scr
            for r in range(d):
                start = r if src is res_scr else (r % step) * (mb // step) + r // step
                for c in range(SLABS):
                    o_ref[0, r, k * (mb // d):(k + 1) * (mb // d), col + c * LANES:col + (c + 1) * LANES] = (
                        src[buf, c, pl.ds(start, mb // d, stride=step), :].astype(bf16))


def _qkv_proj(h, w_qkv, g, batch, seq, tm=1024):
    T = h.shape[0]
    d = ATTN_GROUPS[g][1]
    nt = seq // tm
    n_groups = len(ATTN_GROUPS)
    mb = tm // PROJ_ROW_BLOCKS

    def w_spec(which):
        return pl.BlockSpec((D_MODEL, PROJ_TN), lambda i: (0, which * n_groups + g))

    if d == 1:
        out_spec = pl.BlockSpec((1, tm, QKV_W), lambda i: (i // nt, i % nt, 0))
        out_shape = jax.ShapeDtypeStruct((batch, seq, QKV_W), bf16)
    else:
        out_spec = pl.BlockSpec((1, d, tm // d, QKV_W), lambda i: (i // nt, 0, i % nt, 0))
        out_shape = jax.ShapeDtypeStruct((batch, d, seq // d, QKV_W), bf16)
    return pl.pallas_call(
        functools.partial(_qkv_kernel, d=d, tm=tm),
        grid=(T // tm,),
        in_specs=[pl.BlockSpec((tm, D_MODEL), lambda i: (i, 0)), w_spec(0), w_spec(1), w_spec(2)],
        out_specs=out_spec,
        out_shape=out_shape,
        scratch_shapes=[pltpu.VMEM((N_STAGE, SLABS, mb, LANES), f32)] * 2,
        compiler_params=_cparams("parallel"),
        name=f"qkv_g{g}",
    )(h, w_qkv, w_qkv, w_qkv)


REST_TILES = REST_W // PROJ_TN
YB_TILES = range(REST_YB // PROJ_TN, REST_QC // PROJ_TN)


def _gelu_tanh(y):
    return y * (0.5 * (1.0 + jnp.tanh(math.sqrt(2.0 / math.pi) * (y + 0.044715 * (y * y * y)))))


def _rest_kernel(x_ref, g_ref, *refs):
    w_refs = refs[:REST_TILES]
    wu_ref, wd_ref, h_ref, o_ref, wu_o_ref, wd_o_ref = refs[REST_TILES:]
    h_ref[...] = _rms(x_ref[...], g_ref[...]).astype(bf16)
    h = h_ref[...]
    for c, w_ref in enumerate(w_refs):
        res = jnp.dot(h, w_ref[...], preferred_element_type=f32)
        if c in YB_TILES:
            res = _gelu_tanh(res)
        o_ref[:, c * PROJ_TN:(c + 1) * PROJ_TN] = res.astype(bf16)
    wu_o_ref[...] = wu_ref[...].astype(bf16)
    wd_o_ref[...] = wd_ref[...].astype(bf16)


def _rest_proj(x2, gain, w_in, w_up, w_down, tm=512):
    T = x2.shape[0]
    steps = T // tm
    first = COL_XB // PROJ_TN
    up_rows, down_rows = D_MODEL // steps, D_FF // steps

    def w_spec(c):
        return pl.BlockSpec((D_MODEL, PROJ_TN), lambda i: (0, first + c), pipeline_mode=pl.Buffered(1))

    return pl.pallas_call(
        _rest_kernel,
        grid=(steps,),
        in_specs=[pl.BlockSpec((tm, D_MODEL), lambda i: (i, 0)),
                  pl.BlockSpec((1, D_MODEL), lambda i: (0, 0))]
        + [w_spec(c) for c in range(REST_TILES)]
        + [pl.BlockSpec((up_rows, D_FF), lambda i: (i, 0)),
           pl.BlockSpec((down_rows, D_MODEL), lambda i: (i, 0))],
        out_specs=[pl.BlockSpec((tm, D_MODEL), lambda i: (i, 0)),
                   pl.BlockSpec((tm, REST_W), lambda i: (i, 0)),
                   pl.BlockSpec((up_rows, D_FF), lambda i: (i, 0)),
                   pl.BlockSpec((down_rows, D_MODEL), lambda i: (i, 0))],
        out_shape=[jax.ShapeDtypeStruct((T, D_MODEL), bf16),
                   jax.ShapeDtypeStruct((T, REST_W), bf16),
                   jax.ShapeDtypeStruct((D_MODEL, D_FF), bf16),
                   jax.ShapeDtypeStruct((D_FF, D_MODEL), bf16)],
        compiler_params=_cparams("arbitrary"),
        name="rest_proj",
    )(x2, gain, *([w_in] * REST_TILES), w_up, w_down)


def _t5_bucket(rel):
    nb = N_BUCKETS // 2
    max_exact = nb // 2
    sign = (rel > 0).astype(np.int32) * nb
    n = np.abs(rel)
    large = max_exact + (np.log(np.maximum(n, 1) / max_exact)
                         / np.log(MAX_DISTANCE / max_exact) * (nb - max_exact)).astype(np.int32)
    large = np.minimum(large, nb - 1)
    return (sign + np.where(n < max_exact, n, large)).astype(np.int32)


def _band_bias(rel_bias_g, dilation):
    qq = np.arange(SUB_Q)[:, None]
    kk = np.arange(SUB_K)[None, :]
    rel = kk - ATTN_RADIUS - qq
    onehot = (_t5_bucket(rel * dilation)[None] == np.arange(N_BUCKETS)[:, None, None]).astype(np.float32)
    bias = jnp.einsum('nh,nqk->hqk', rel_bias_g.astype(f32), onehot, precision=lax.Precision.HIGHEST)
    return bias + np.where(np.abs(rel) <= ATTN_RADIUS, 0.0, NEG_INF).astype(np.float32)[None]


def _attn_kernel(q_ref, kp_ref, km_ref, kn_ref, vp_ref, vm_ref, vn_ref, bias_ref,
                 o_ref, lse_ref, kbuf, vbuf, *, tq, seq):
    R = ATTN_RADIUS
    kbuf[0:R] = kp_ref[0]
    kbuf[R:R + tq] = km_ref[0]
    kbuf[R + tq:] = kn_ref[0]
    vbuf[0:R] = vp_ref[0]
    vbuf[R:R + tq] = vm_ref[0]
    vbuf[R + tq:] = vn_ref[0]
    q0 = pl.program_id(1) * tq
    scale = 1.0 / math.sqrt(HEAD_DIM_A)
    lane = lax.broadcasted_iota(jnp.int32, (SUB_Q, LSE_LANES), 1)
    for s in range(tq // SUB_Q):
        r0 = s * SUB_Q
        pos = q0 + (r0 - R) + lax.broadcasted_iota(jnp.int32, (1, SUB_K), 1)
        edge = jnp.where(pos >= 0, jnp.where(pos < seq, 0.0, NEG_INF), NEG_INF)
        lse_tile = None
        for h in range(HEADS_PER_GROUP):
            c0 = h * HEAD_DIM_A
            q = q_ref[0, r0:r0 + SUB_Q, c0:c0 + HEAD_DIM_A]
            k = kbuf[r0:r0 + SUB_K, c0:c0 + HEAD_DIM_A]
            v = vbuf[r0:r0 + SUB_K, c0:c0 + HEAD_DIM_A]
            logits = lax.dot_general(q, k, (((1,), (1,)), ((), ())), preferred_element_type=f32)
            logits = logits * scale + bias_ref[h] + edge
            m = jnp.max(logits, axis=-1, keepdims=True)
            p = jnp.exp(logits - m)
            ssum = jnp.sum(p, axis=-1, keepdims=True)
            o = jnp.dot(p.astype(bf16), v, preferred_element_type=f32) * (1.0 / ssum)
            o_ref[0, r0:r0 + SUB_Q, c0:c0 + HEAD_DIM_A] = o.astype(o_ref.dtype)
            lse = m + jnp.log(ssum)
            lse_tile = lse if lse_tile is None else jnp.where(lane >= h * LSE_REP, lse, lse_tile)
        lse_ref[0, r0:r0 + SUB_Q, :] = jnp.broadcast_to(lse_tile, (SUB_Q, LSE_LANES))


def _attn_group(qkv, rel_bias, g):
    _, d = ATTN_GROUPS[g]
    n, L, _ = qkv.shape
    tq = min(512, L)
    R = ATTN_RADIUS
    bias = _band_bias(rel_bias[:, g * HEADS_PER_GROUP:(g + 1) * HEADS_PER_GROUP], d)
    rb = tq // R
    last_rb = L // R - 1

    def main(col):
        return pl.BlockSpec((1, tq, GROUP_WIDTH), lambda b, t: (b, t, col))

    def prev(col):
        return pl.BlockSpec((1, R, GROUP_WIDTH), lambda b, t: (b, jnp.maximum(t * rb - 1, 0), col))

    def nxt(col):
        return pl.BlockSpec((1, R, GROUP_WIDTH), lambda b, t: (b, jnp.minimum((t + 1) * rb, last_rb), col))

    return pl.pallas_call(
        functools.partial(_attn_kernel, tq=tq, seq=L),
        grid=(n, L // tq),
        in_specs=[main(0), prev(1), main(1), nxt(1), prev(2), main(2), nxt(2),
                  pl.BlockSpec((HEADS_PER_GROUP, SUB_Q, SUB_K), lambda b, t: (0, 0, 0))],
        out_specs=[pl.BlockSpec((1, tq, GROUP_WIDTH), lambda b, t: (b, t, 0)),
                   pl.BlockSpec((1, tq, LSE_LANES), lambda b, t: (b, t, 0))],
        out_shape=[jax.ShapeDtypeStruct((n, L, GROUP_WIDTH), bf16),
                   jax.ShapeDtypeStruct((n, L, LSE_LANES), f32)],
        scratch_shapes=[pltpu.VMEM((tq + 2 * R, GROUP_WIDTH), bf16),
                        pltpu.VMEM((tq + 2 * R, GROUP_WIDTH), bf16)],
        compiler_params=_cparams("parallel", "arbitrary"),
        name=f"attn_g{g}",
    )(qkv, qkv, qkv, qkv, qkv, qkv, qkv, bias)


LRU_CHUNK = 256
LRU_PAD = 8
LRU_FINISH_ROWS = 512
LRU_SEGS = 8
SEG_GAP = 8


def _lru_kernel(xb_ref, yb_ref, cw_ref, cb_ref, w_ref, gb_ref, lam_ref, o_ref,
                xpad, af, bf, ab, bb, htf, ptf, htb, ptb, cf_scr, cb_scr, *, seq):
    R = LRU_CHUNK
    P = LRU_PAD
    seg_len = seq // LRU_SEGS
    pitch = seg_len + SEG_GAP
    chunks_per_seg = seg_len // R
    n_chunks = seq // R
    xpad[0:P] = jnp.zeros((P, LRU_BW), f32)
    xpad[P + seq:] = jnp.zeros((P, LRU_BW), f32)
    xpad[P:P + seq] = xb_ref[0].astype(f32)
    lam = lam_ref[...]
    log_a_unit = -LRU_C * (jnp.maximum(-lam, 0.0) + jnp.log1p(jnp.exp(-jnp.abs(lam))))
    cw = cw_ref[...]
    cb = cb_ref[...]
    gate_bias = gb_ref[0]
    row = lax.broadcasted_iota(jnp.int32, (R, LRU_BW), 0)

    def chunk(ci, first=False, last=False):
        c0 = ci * R if isinstance(ci, int) else pl.multiple_of(ci * R, R)
        dst = (ci // chunks_per_seg) * pitch + (ci % chunks_per_seg) * R
        dst = dst if isinstance(ci, int) else pl.multiple_of(dst, 8)
        xc = (cw[0:1] * xpad[pl.ds(c0 + (P - 1), R), :] + cw[1:2] * xpad[pl.ds(c0 + P, R), :]
              + cw[2:3] * xpad[pl.ds(c0 + (P + 1), R), :] + cw[3:4] * xpad[pl.ds(c0 + (P + 2), R), :]) + cb
        th = jnp.tanh(jnp.dot(xc.astype(bf16), w_ref[0], preferred_element_type=f32) + gate_bias)
        half_xc = 0.5 * xc
        for direction, (a_scr, b_scr) in enumerate(((af, bf), (ab, bb))):
            base = direction * 2 * LRU_BW
            half_log_a = 0.5 * log_a_unit[direction:direction + 1]
            a = jnp.exp(half_log_a * th[:, base:base + LRU_BW] + half_log_a)
            gated_x = half_xc * th[:, base + LRU_BW:base + 2 * LRU_BW] + half_xc
            y = 1.0 - a * a
            mult = y * lax.rsqrt(jnp.maximum(y, 1e-30))
            if direction == 0 and first:
                mult = jnp.where(row == 0, 1.0, mult)
            if direction == 1 and last:
                mult = jnp.where(row == R - 1, 1.0, mult)
            a_scr[pl.ds(dst, R), :] = a
            b_scr[pl.ds(dst, R), :] = mult * gated_x

    chunk(0, first=True)
    lax.fori_loop(1, n_chunks - 1, lambda ci, c: (chunk(ci), c)[1], 0, unroll=2)
    chunk(n_chunks - 1, last=True)

    def scan(i, carry):
        hf, pf, hb, pb = carry
        rows = pl.ds(i, LRU_SEGS, stride=pitch)
        a = af[rows, :]
        hf = a * hf + bf[rows, :]
        pf = a * pf
        htf[rows, :] = hf
        ptf[rows, :] = pf
        rows = pl.ds(seg_len - 1 - i, LRU_SEGS, stride=pitch)
        a = ab[rows, :]
        hb = a * hb + bb[rows, :]
        pb = a * pb
        htb[rows, :] = hb
        ptb[rows, :] = pb
        return hf, pf, hb, pb

    zero = jnp.zeros((LRU_SEGS, LRU_BW), f32)
    one = jnp.ones((LRU_SEGS, LRU_BW), f32)
    hf, pf, hb, pb = lax.fori_loop(0, seg_len, scan, (zero, one, zero, one), unroll=8)

    c = jnp.zeros((1, LRU_BW), f32)
    cf_scr[0:1] = c
    for j in range(1, LRU_SEGS):
        c = hf[j - 1:j] + pf[j - 1:j] * c
        cf_scr[j:j + 1] = c
    c = jnp.zeros((1, LRU_BW), f32)
    cb_scr[LRU_SEGS - 1:LRU_SEGS] = c
    for j in range(LRU_SEGS - 2, -1, -1):
        c = hb[j + 1:j + 2] + pb[j + 1:j + 2] * c
        cb_scr[j:j + 1] = c

    F = LRU_FINISH_ROWS
    finish_per_seg = seg_len // F

    def finish(ci, carry):
        c0 = pl.multiple_of(ci * F, F)
        seg = ci // finish_per_seg
        rows = pl.ds(pl.multiple_of(seg * pitch + (ci % finish_per_seg) * F, 8), F)
        h = (htf[rows, :] + ptf[rows, :] * cf_scr[pl.ds(seg, 1), :]
             + htb[rows, :] + ptb[rows, :] * cb_scr[pl.ds(seg, 1), :])
        o_ref[0, pl.ds(c0, F), :] = (h * yb_ref[0, pl.ds(c0, F), :].astype(f32)).astype(o_ref.dtype)
        return carry

    lax.fori_loop(0, seq // F, finish, 0)


def _lru(proj3, conv_w, conv_b, w_gates, b_gates, lam):
    B, S, _ = proj3.shape
    xb0 = 0
    yb0 = REST_YB // LRU_BW
    return pl.pallas_call(
        functools.partial(_lru_kernel, seq=S),
        grid=(B, LRU_BLOCKS),
        in_specs=[
            pl.BlockSpec((1, S, LRU_BW), lambda b, n: (b, 0, xb0 + n)),
            pl.BlockSpec((1, S, LRU_BW), lambda b, n: (b, 0, yb0 + n)),
            pl.BlockSpec((4, LRU_BW), lambda b, n: (0, n)),
            pl.BlockSpec((1, LRU_BW), lambda b, n: (0, n)),
            pl.BlockSpec((1, LRU_BW, 4 * LRU_BW), lambda b, n: (n, 0, 0)),
            pl.BlockSpec((1, 1, 4 * LRU_BW), lambda b, n: (n, 0, 0)),
            pl.BlockSpec((2, LRU_BW), lambda b, n: (0, n)),
        ],
        out_specs=pl.BlockSpec((1, S, LRU_BW), lambda b, n: (b, 0, n)),
        out_shape=jax.ShapeDtypeStruct((B, S, LRU_WIDTH), bf16),
        scratch_shapes=([pltpu.VMEM((S + 2 * LRU_PAD, LRU_BW), f32)]
                        + [pltpu.VMEM((S + LRU_SEGS * SEG_GAP, LRU_BW), f32)] * 8
                        + [pltpu.VMEM((LRU_SEGS, LRU_BW), f32)] * 2),
        compiler_params=_cparams("parallel", "parallel"),
        name="lru",
    )(proj3, proj3, conv_w, conv_b, w_gates, b_gates, lam)


def _mem_kv_kernel(m_ref, g_ref, w_ref, o_ref, h_scr):
    @pl.when(pl.program_id(0) == 0)
    def _():
        h_scr[...] = _rms(m_ref[...], g_ref[...]).astype(bf16)

    o_ref[...] = jnp.dot(h_scr[...], w_ref[...], preferred_element_type=f32).astype(o_ref.dtype)


def _mem_kv(mem2, gain, w, tn=512):
    M = mem2.shape[0]
    N = w.shape[1]
    return pl.pallas_call(
        _mem_kv_kernel,
        grid=(N // tn,),
        in_specs=[pl.BlockSpec((M, D_MODEL), lambda j: (0, 0)),
                  pl.BlockSpec((1, D_MODEL), lambda j: (0, 0)),
                  pl.BlockSpec((D_MODEL, tn), lambda j: (0, j))],
        out_specs=pl.BlockSpec((M, tn), lambda j: (0, j)),
        out_shape=jax.ShapeDtypeStruct((M, N), bf16),
        scratch_shapes=[pltpu.VMEM((M, D_MODEL), bf16)],
        compiler_params=_cparams("arbitrary"),
        name="mem_kv",
    )(mem2, gain, w)


def _xattn_kernel(q0_ref, q1_ref, q2_ref, q3_ref, kv_ref, o_ref):
    scale = 1.0 / math.sqrt(MEM_HEAD_DIM)
    for h, q_ref in enumerate((q0_ref, q1_ref, q2_ref, q3_ref)):
        c0 = h * MEM_HEAD_DIM
        k = kv_ref[0, :, c0:c0 + MEM_HEAD_DIM]
        v = kv_ref[0, :, MEM_WIDTH + c0:MEM_WIDTH + c0 + MEM_HEAD_DIM]
        logits = lax.dot_general(q_ref[0], k, (((1,), (1,)), ((), ())), preferred_element_type=f32) * scale
        m = jnp.max(logits, axis=-1, keepdims=True)
        p = jnp.exp(logits - m)
        ssum = jnp.sum(p, axis=-1, keepdims=True)
        o = jnp.dot(p.astype(bf16), v, preferred_element_type=f32) * (1.0 / ssum)
        o_ref[0, :, c0:c0 + MEM_HEAD_DIM] = o.astype(o_ref.dtype)


def _xattn(proj3, kv3, tq=1024):
    B, S, _ = proj3.shape
    qb0 = REST_QC // MEM_HEAD_DIM

    def qspec(h):
        return pl.BlockSpec((1, tq, MEM_HEAD_DIM), lambda b, t: (b, t, qb0 + h))

    return pl.pallas_call(
        _xattn_kernel,
        grid=(B, S // tq),
        in_specs=[qspec(0), qspec(1), qspec(2), qspec(3),
                  pl.BlockSpec((1, N_MEM, 2 * MEM_WIDTH), lambda b, t: (b, 0, 0))],
        out_specs=pl.BlockSpec((1, tq, MEM_WIDTH), lambda b, t: (b, t, 0)),
        out_shape=jax.ShapeDtypeStruct((B, S, MEM_WIDTH), bf16),
        compiler_params=_cparams("parallel", "parallel"),
        name="xattn",
    )(proj3, proj3, proj3, proj3, kv3)


def _combine_kernel(o0_ref, o1_ref, o2_ref, l0_ref, l1_ref, l2_ref, ya_ref, o1_scr, o2_scr, l1_scr, l2_scr, *, tm):
    for g, o_ref, l_ref, o_scr, l_scr in ((1, o1_ref, l1_ref, o1_scr, l1_scr),
                                          (2, o2_ref, l2_ref, o2_scr, l2_scr)):
        d = ATTN_GROUPS[g][1]
        for r in range(d):
            l_scr[pl.ds(r, tm // d, stride=d), :] = l_ref[0, r]
            for h in range(HEADS_PER_GROUP):
                o_scr[h, pl.ds(r, tm // d, stride=d), :] = (
                    o_ref[0, r, :, h * HEAD_DIM_A:(h + 1) * HEAD_DIM_A].astype(f32))
    l0, l1, l2 = l0_ref[...], l1_scr[...], l2_scr[...]
    m = jnp.maximum(jnp.maximum(l0, l1), l2)
    e0, e1, e2 = jnp.exp(l0 - m), jnp.exp(l1 - m), jnp.exp(l2 - m)
    inv = 1.0 / (e0 + e1 + e2)
    for h in range(HEADS_PER_GROUP):
        c0 = h * HEAD_DIM_A
        lane = slice(h * LSE_REP, h * LSE_REP + 1)
        y = ((e0 * inv)[:, lane] * o0_ref[:, c0:c0 + HEAD_DIM_A].astype(f32)
             + (e1 * inv)[:, lane] * o1_scr[h] + (e2 * inv)[:, lane] * o2_scr[h])
        ya_ref[:, c0:c0 + HEAD_DIM_A] = y.astype(bf16)


def _combine(o_groups, lse_groups, seq, tm=512):
    T = o_groups[0].shape[0]
    nt = seq // tm
    d1, d2 = ATTN_GROUPS[1][1], ATTN_GROUPS[2][1]

    def rows(width):
        return pl.BlockSpec((tm, width), lambda i: (i, 0))

    def strided_rows(d, width):
        return pl.BlockSpec((1, d, tm // d, width), lambda i: (i // nt, 0, i % nt, 0))

    return pl.pallas_call(
        functools.partial(_combine_kernel, tm=tm),
        grid=(T // tm,),
        in_specs=[rows(GROUP_WIDTH), strided_rows(d1, GROUP_WIDTH), strided_rows(d2, GROUP_WIDTH),
                  rows(LSE_LANES), strided_rows(d1, LSE_LANES), strided_rows(d2, LSE_LANES)],
        out_specs=rows(GROUP_WIDTH),
        out_shape=jax.ShapeDtypeStruct((T, GROUP_WIDTH), bf16),
        scratch_shapes=[pltpu.VMEM((HEADS_PER_GROUP, tm, HEAD_DIM_A), f32),
                        pltpu.VMEM((HEADS_PER_GROUP, tm, HEAD_DIM_A), f32),
                        pltpu.VMEM((tm, LSE_LANES), f32), pltpu.VMEM((tm, LSE_LANES), f32)],
        compiler_params=_cparams("parallel"),
        name="combine",
    )(*o_groups, *lse_groups)


def _gate_mix_kernel(h_ref, ya_ref, yl_ref, yc_ref, wga_ref, wgb_ref, wgc_ref, bga_ref, bgb_ref, bgc_ref,
                     woa_ref, wol_ref, wom_ref, mix_ref):
    h = h_ref[...]

    def gate(w_ref, b_ref):
        return jax.nn.sigmoid(jnp.dot(h, w_ref[...], preferred_element_type=f32) + b_ref[...])

    mixed = (gate(wga_ref, bga_ref) * jnp.dot(ya_ref[...], woa_ref[...], preferred_element_type=f32)
             + gate(wgb_ref, bgb_ref) * jnp.dot(yl_ref[...], wol_ref[...], preferred_element_type=f32)
             + gate(wgc_ref, bgc_ref) * jnp.dot(yc_ref[...], wom_ref[...], preferred_element_type=f32))
    mix_ref[...] = mixed.astype(mix_ref.dtype)


def _gate_mix(h, y_a, y_lru, y_c, w_gate, b_gate, w_o_attn, w_o_lru, w_o_mem, tm=512, tn=512):
    T = h.shape[0]
    nj = D_MODEL // tn

    def rows(width):
        return pl.BlockSpec((tm, width), lambda j, i: (i, 0))

    def gate_w(k):
        return pl.BlockSpec((D_MODEL, tn), lambda j, i: (0, k * nj + j))

    def gate_b(k):
        return pl.BlockSpec((1, tn), lambda j, i: (0, k * nj + j))

    def cols(width):
        return pl.BlockSpec((width, tn), lambda j, i: (0, j))

    return pl.pallas_call(
        _gate_mix_kernel,
        grid=(nj, T // tm),
        in_specs=[rows(D_MODEL), rows(GROUP_WIDTH), rows(LRU_WIDTH), rows(MEM_WIDTH),
                  gate_w(0), gate_w(1), gate_w(2), gate_b(0), gate_b(1), gate_b(2),
                  cols(GROUP_WIDTH), cols(LRU_WIDTH), cols(MEM_WIDTH)],
        out_specs=pl.BlockSpec((tm, tn), lambda j, i: (i, j)),
        out_shape=jax.ShapeDtypeStruct((T, D_MODEL), bf16),
        compiler_params=_cparams("arbitrary", "arbitrary"),
        name="gate_mix",
    )(h, y_a, y_lru, y_c, w_gate, w_gate, w_gate, b_gate, b_gate, b_gate, w_o_attn, w_o_lru, w_o_mem)


def _mlp_kernel(x_ref, mix_ref, wo_ref, g_ref, gf_ref, wu_ref, wd_ref, out_ref, h_scr):
    j = pl.program_id(1)

    @pl.when(j == 0)
    def _():
        x = x_ref[...] + jnp.dot(mix_ref[...], wo_ref[...], preferred_element_type=f32)
        h_scr[...] = _rms(x, g_ref[...]).astype(bf16)
        out_ref[...] = x

    u = jnp.maximum(jnp.dot(h_scr[...], wu_ref[...], preferred_element_type=f32), 0.0)
    out_ref[...] += jnp.dot((u * u).astype(bf16), wd_ref[...], preferred_element_type=f32)

    @pl.when(j == pl.num_programs(1) - 1)
    def _():
        out_ref[...] = _rms(out_ref[...], gf_ref[...])


def _mlp(x2, mixed, w_out, gain, gain_final, w_up, w_down, tm=512, tf=1024):
    T = x2.shape[0]
    return pl.pallas_call(
        _mlp_kernel,
        grid=(T // tm, D_FF // tf),
        in_specs=[pl.BlockSpec((tm, D_MODEL), lambda i, j: (i, 0)),
                  pl.BlockSpec((tm, D_MODEL), lambda i, j: (i, 0)),
                  pl.BlockSpec((D_MODEL, D_MODEL), lambda i, j: (0, 0)),
                  pl.BlockSpec((1, D_MODEL), lambda i, j: (0, 0)),
                  pl.BlockSpec((1, D_MODEL), lambda i, j: (0, 0)),
                  pl.BlockSpec((D_MODEL, tf), lambda i, j: (0, j)),
                  pl.BlockSpec((tf, D_MODEL), lambda i, j: (j, 0))],
        out_specs=pl.BlockSpec((tm, D_MODEL), lambda i, j: (i, 0)),
        out_shape=jax.ShapeDtypeStruct((T, D_MODEL), f32),
        scratch_shapes=[pltpu.VMEM((tm, D_MODEL), bf16)],
        compiler_params=_cparams("parallel", "arbitrary"),
        name="mlp",
    )(x2, mixed, w_out, gain, gain_final, w_up, w_down)


def kernel(x, mem, rel_bias, norm_mix, norm_mem, norm_mlp, norm_final, w_in, w_gate, b_gate, conv_w, conv_b,
           lru_wa, lru_ba, lru_wi, lru_bi, lru_lambda, w_mem_kv, w_o_attn, w_o_lru, w_o_mem, w_out, w_up, w_down):
    B, S, D = x.shape
    T = B * S
    depth = w_in.shape[0]
    assert depth == 1, "the final RMSNorm is fused into the (single) layer's MLP kernel"
    x2 = x.reshape(T, D)
    mem2 = mem.reshape(B * N_MEM, D)
    for l in range(depth):
        w_qkv = w_in[l].astype(bf16)
        h, rest, w_up_bf, w_down_bf = _rest_proj(x2, norm_mix[l].reshape(1, D), w_qkv, w_up[l], w_down[l])
        proj3 = rest.reshape(B, S, REST_W)

        attn = []
        for g in range(len(ATTN_GROUPS)):
            d = ATTN_GROUPS[g][1]
            qkv = _qkv_proj(h, w_qkv, g, B, S)
            o, lse = _attn_group(qkv.reshape(B * d, S // d, QKV_W), rel_bias, g)
            if g == 0:
                attn.append((o.reshape(T, GROUP_WIDTH), lse.reshape(T, LSE_LANES)))
            else:
                attn.append((o.reshape(B, d, S // d, GROUP_WIDTH), lse.reshape(B, d, S // d, LSE_LANES)))

        w_gates = (0.5 * jnp.concatenate([lru_wa[l, 0], lru_wi[l, 0], lru_wa[l, 1], lru_wi[l, 1]], axis=-1)
                   ).astype(bf16)
        b_gates = 0.5 * jnp.concatenate([lru_ba[l, 0], lru_bi[l, 0], lru_ba[l, 1], lru_bi[l, 1]], axis=-1)
        y_lru = _lru(proj3, conv_w[l], conv_b[l].reshape(1, LRU_WIDTH), w_gates,
                     b_gates.reshape(LRU_BLOCKS, 1, 4 * LRU_BW), lru_lambda[l])

        kv = _mem_kv(mem2, norm_mem[l].reshape(1, D), w_mem_kv[l].astype(bf16))
        y_c = _xattn(proj3, kv.reshape(B, N_MEM, 2 * MEM_WIDTH))

        y_a = _combine([a[0] for a in attn], [a[1] for a in attn], S)
        mixed = _gate_mix(h, y_a, y_lru.reshape(T, LRU_WIDTH), y_c.reshape(T, MEM_WIDTH),
                          w_gate[l].astype(bf16), b_gate[l].reshape(1, 3 * D),
                          w_o_attn[l].astype(bf16), w_o_lru[l].astype(bf16), w_o_mem[l].astype(bf16))
        x2 = _mlp(x2, mixed, w_out[l].astype(bf16), norm_mlp[l].reshape(1, D), norm_final.reshape(1, D),
                  w_up_bf, w_down_bf)
    return x2.reshape(B, S, D)
```

```python
import functools
import math

import jax
import jax.numpy as jnp
import numpy as np
from jax import lax
from jax.experimental import pallas as pl
from jax.experimental.pallas import tpu as pltpu

D_MODEL = 2048
HEAD_DIM_A = 128
ATTN_GROUPS = ((128, 1), (512, 4), (2048, 16))
HEADS_PER_GROUP = 4
GROUP_WIDTH = HEADS_PER_GROUP * HEAD_DIM_A
WIDTH_A = len(ATTN_GROUPS) * GROUP_WIDTH
ATTN_RADIUS = 64
N_BUCKETS = 32
MAX_DISTANCE = 1024
LRU_WIDTH = 1536
LRU_BLOCKS = 12
LRU_BW = 128
LRU_C = 8.0
N_MEM = 256
MEM_HEADS = 4
MEM_HEAD_DIM = 256
MEM_WIDTH = MEM_HEADS * MEM_HEAD_DIM
D_FF = 4 * D_MODEL
EPS = 1e-6
N_IN = 3 * WIDTH_A + 2 * LRU_WIDTH + MEM_WIDTH
COL_K = WIDTH_A
COL_V = 2 * WIDTH_A
COL_XB = 3 * WIDTH_A
COL_YB = 3 * WIDTH_A + LRU_WIDTH
COL_QC = 3 * WIDTH_A + 2 * LRU_WIDTH
NEG_INF = -1e30

SUB_Q = 128
SUB_K = SUB_Q + 2 * ATTN_RADIUS
LSE_LANES = 128
LSE_REP = LSE_LANES // HEADS_PER_GROUP

VMEM_LIMIT = 56 * 1024 * 1024

f32 = jnp.float32
bf16 = jnp.bfloat16


def _cparams(*sem):
    return pltpu.CompilerParams(dimension_semantics=sem, vmem_limit_bytes=VMEM_LIMIT)


def _rms(x, gain):
    return x * lax.rsqrt(jnp.mean(x * x, axis=-1, keepdims=True) + EPS) * gain


QKV_W = 3 * GROUP_WIDTH
REST_W = 2 * LRU_WIDTH + MEM_WIDTH
REST_YB = LRU_WIDTH
REST_QC = 2 * LRU_WIDTH
PROJ_TN = GROUP_WIDTH
LANES = 128
SLABS = PROJ_TN // LANES
PROJ_ROW_BLOCKS = 2
DEINTERLEAVE_STEP = 4
N_STAGE = 2


def _qkv_kernel(h_ref, wq_ref, wk_ref, wv_ref, *refs, d, tm, n_cast):
    cast_in, o_ref = refs[:n_cast], refs[n_cast]
    cast_out = refs[n_cast + 1:2 * n_cast + 1]
    res_scr, tmp_scr = refs[2 * n_cast + 1:]
    for src_ref, dst_ref in zip(cast_in, cast_out):
        dst_ref[...] = src_ref[...].astype(bf16)
    mb = tm // PROJ_ROW_BLOCKS
    n = 0
    for k in range(PROJ_ROW_BLOCKS):
        hk = h_ref[k * mb:(k + 1) * mb, :]
        for t, w_ref in enumerate((wq_ref, wk_ref, wv_ref)):
            res = jnp.dot(hk, w_ref[...], preferred_element_type=f32)
            col = t * PROJ_TN
            if d == 1:
                o_ref[0, k * mb:(k + 1) * mb, col:col + PROJ_TN] = res.astype(bf16)
                continue
            buf = n % N_STAGE
            n += 1
            for c in range(SLABS):
                res_scr[buf, c] = res[:, c * LANES:(c + 1) * LANES]
            src, step = res_scr, d
            if d == DEINTERLEAVE_STEP ** 2:
                step = DEINTERLEAVE_STEP
                for r in range(step):
                    for c in range(SLABS):
                        tmp_scr[buf, c, r * (mb // step):(r + 1) * (mb // step), :] = (
                            res_scr[buf, c, pl.ds(r, mb // step, stride=step), :])
                src = tmp_scr
            for r in range(d):
                start = r if src is res_scr else (r % step) * (mb // step) + r // step
                for c in range(SLABS):
                    o_ref[0, r, k * (mb // d):(k + 1) * (mb // d), col + c * LANES:col + (c + 1) * LANES] = (
                        src[buf, c, pl.ds(start, mb // d, stride=step), :].astype(bf16))


def _qkv_proj(h, w_qkv, g, batch, seq, cast=(), tm=1024):
    T = h.shape[0]
    d = ATTN_GROUPS[g][1]
    nt = seq // tm
    n_groups = len(ATTN_GROUPS)
    mb = tm // PROJ_ROW_BLOCKS
    steps = T // tm

    def w_spec(which):
        return pl.BlockSpec((D_MODEL, PROJ_TN), lambda i: (0, which * n_groups + g))

    cast_specs = [pl.BlockSpec((w.shape[0] // steps, w.shape[1]), lambda i: (i, 0)) for w in cast]

    if d == 1:
        out_spec = pl.BlockSpec((1, tm, QKV_W), lambda i: (i // nt, i % nt, 0))
        out_shape = jax.ShapeDtypeStruct((batch, seq, QKV_W), bf16)
    else:
        out_spec = pl.BlockSpec((1, d, tm // d, QKV_W), lambda i: (i // nt, 0, i % nt, 0))
        out_shape = jax.ShapeDtypeStruct((batch, d, seq // d, QKV_W), bf16)
    return pl.pallas_call(
        functools.partial(_qkv_kernel, d=d, tm=tm, n_cast=len(cast)),
        grid=(steps,),
        in_specs=[pl.BlockSpec((tm, D_MODEL), lambda i: (i, 0)), w_spec(0), w_spec(1), w_spec(2)] + cast_specs,
        out_specs=[out_spec] + cast_specs,
        out_shape=[out_shape] + [jax.ShapeDtypeStruct(w.shape, bf16) for w in cast],
        scratch_shapes=[pltpu.VMEM((N_STAGE, SLABS, mb, LANES), f32)] * 2,
        compiler_params=_cparams("parallel"),
        name=f"qkv_g{g}",
    )(h, w_qkv, w_qkv, w_qkv, *cast)


REST_TILES = REST_W // PROJ_TN
YB_TILES = range(REST_YB // PROJ_TN, REST_QC // PROJ_TN)


def _gelu_tanh(y):
    return y * (0.5 * (1.0 + jnp.tanh(math.sqrt(2.0 / math.pi) * (y + 0.044715 * (y * y * y)))))


def _rest_kernel(x_ref, g_ref, *refs):
    w_refs = refs[:REST_TILES]
    wu_ref, wd_ref, h_ref, o_ref, wu_o_ref, wd_o_ref = refs[REST_TILES:]
    h_ref[...] = _rms(x_ref[...], g_ref[...]).astype(bf16)
    h = h_ref[...]
    for c, w_ref in enumerate(w_refs):
        res = jnp.dot(h, w_ref[...], preferred_element_type=f32)
        if c in YB_TILES:
            res = _gelu_tanh(res)
        o_ref[:, c * PROJ_TN:(c + 1) * PROJ_TN] = res.astype(bf16)
    wu_o_ref[...] = wu_ref[...].astype(bf16)
    wd_o_ref[...] = wd_ref[...].astype(bf16)


def _rest_proj(x2, gain, w_in, w_up, w_down, tm=512):
    T = x2.shape[0]
    steps = T // tm
    first = COL_XB // PROJ_TN
    up_rows, down_rows = D_MODEL // steps, D_FF // steps

    def w_spec(c):
        return pl.BlockSpec((D_MODEL, PROJ_TN), lambda i: (0, first + c), pipeline_mode=pl.Buffered(1))

    return pl.pallas_call(
        _rest_kernel,
        grid=(steps,),
        in_specs=[pl.BlockSpec((tm, D_MODEL), lambda i: (i, 0)),
                  pl.BlockSpec((1, D_MODEL), lambda i: (0, 0))]
        + [w_spec(c) for c in range(REST_TILES)]
        + [pl.BlockSpec((up_rows, D_FF), lambda i: (i, 0)),
           pl.BlockSpec((down_rows, D_MODEL), lambda i: (i, 0))],
        out_specs=[pl.BlockSpec((tm, D_MODEL), lambda i: (i, 0)),
                   pl.BlockSpec((tm, REST_W), lambda i: (i, 0)),
                   pl.BlockSpec((up_rows, D_FF), lambda i: (i, 0)),
                   pl.BlockSpec((down_rows, D_MODEL), lambda i: (i, 0))],
        out_shape=[jax.ShapeDtypeStruct((T, D_MODEL), bf16),
                   jax.ShapeDtypeStruct((T, REST_W), bf16),
                   jax.ShapeDtypeStruct((D_MODEL, D_FF), bf16),
                   jax.ShapeDtypeStruct((D_FF, D_MODEL), bf16)],
        compiler_params=_cparams("arbitrary"),
        name="rest_proj",
    )(x2, gain, *([w_in] * REST_TILES), w_up, w_down)


def _t5_bucket(rel):
    nb = N_BUCKETS // 2
    max_exact = nb // 2
    sign = (rel > 0).astype(np.int32) * nb
    n = np.abs(rel)
    large = max_exact + (np.log(np.maximum(n, 1) / max_exact)
                         / np.log(MAX_DISTANCE / max_exact) * (nb - max_exact)).astype(np.int32)
    large = np.minimum(large, nb - 1)
    return (sign + np.where(n < max_exact, n, large)).astype(np.int32)


def _band_bias(rel_bias_g, dilation):
    qq = np.arange(SUB_Q)[:, None]
    kk = np.arange(SUB_K)[None, :]
    rel = kk - ATTN_RADIUS - qq
    onehot = (_t5_bucket(rel * dilation)[None] == np.arange(N_BUCKETS)[:, None, None]).astype(np.float32)
    bias = jnp.einsum('nh,nqk->hqk', rel_bias_g.astype(f32), onehot, precision=lax.Precision.HIGHEST)
    return bias + np.where(np.abs(rel) <= ATTN_RADIUS, 0.0, NEG_INF).astype(np.float32)[None]


def _attn_kernel(q_ref, kp_ref, km_ref, kn_ref, vp_ref, vm_ref, vn_ref, bias_ref,
                 o_ref, lse_ref, kbuf, vbuf, *, tq, seq):
    R = ATTN_RADIUS
    kbuf[0:R] = kp_ref[0]
    kbuf[R:R + tq] = km_ref[0]
    kbuf[R + tq:] = kn_ref[0]
    vbuf[0:R] = vp_ref[0]
    vbuf[R:R + tq] = vm_ref[0]
    vbuf[R + tq:] = vn_ref[0]
    q0 = pl.program_id(1) * tq
    lane = lax.broadcasted_iota(jnp.int32, (SUB_Q, LSE_LANES), 1)
    n_sub = tq // SUB_Q
    for s in range(n_sub):
        r0 = s * SUB_Q
        edge = None
        if s == 0 or s == n_sub - 1:
            pos = q0 + (r0 - R) + lax.broadcasted_iota(jnp.int32, (1, SUB_K), 1)
            edge = jnp.where(pos >= 0, jnp.where(pos < seq, 0.0, NEG_INF), NEG_INF)
        lse_tile = None
        for h in range(HEADS_PER_GROUP):
            c0 = h * HEAD_DIM_A
            q = q_ref[0, r0:r0 + SUB_Q, c0:c0 + HEAD_DIM_A]
            k = kbuf[r0:r0 + SUB_K, c0:c0 + HEAD_DIM_A]
            v = vbuf[r0:r0 + SUB_K, c0:c0 + HEAD_DIM_A]
            logits = lax.dot_general(q, k, (((1,), (1,)), ((), ())), preferred_element_type=f32) + bias_ref[h]
            if edge is not None:
                logits = logits + edge
            m = jnp.max(logits, axis=-1, keepdims=True)
            p = jnp.exp(logits - m)
            ssum = jnp.sum(p, axis=-1, keepdims=True)
            o = jnp.dot(p.astype(bf16), v, preferred_element_type=f32) * (1.0 / ssum)
            o_ref[0, r0:r0 + SUB_Q, c0:c0 + HEAD_DIM_A] = o.astype(o_ref.dtype)
            lse = m + jnp.log(ssum)
            lse_tile = lse if lse_tile is None else jnp.where(lane >= h * LSE_REP, lse, lse_tile)
        lse_ref[0, r0:r0 + SUB_Q, :] = jnp.broadcast_to(lse_tile, (SUB_Q, LSE_LANES))


def _attn_group(qkv, rel_bias, g):
    _, d = ATTN_GROUPS[g]
    n, L, _ = qkv.shape
    tq = min(512, L)
    R = ATTN_RADIUS
    bias = _band_bias(rel_bias[:, g * HEADS_PER_GROUP:(g + 1) * HEADS_PER_GROUP], d)
    rb = tq // R
    last_rb = L // R - 1

    def main(col):
        return pl.BlockSpec((1, tq, GROUP_WIDTH), lambda b, t: (b, t, col))

    def prev(col):
        return pl.BlockSpec((1, R, GROUP_WIDTH), lambda b, t: (b, jnp.maximum(t * rb - 1, 0), col))

    def nxt(col):
        return pl.BlockSpec((1, R, GROUP_WIDTH), lambda b, t: (b, jnp.minimum((t + 1) * rb, last_rb), col))

    return pl.pallas_call(
        functools.partial(_attn_kernel, tq=tq, seq=L),
        grid=(n, L // tq),
        in_specs=[main(0), prev(1), main(1), nxt(1), prev(2), main(2), nxt(2),
                  pl.BlockSpec((HEADS_PER_GROUP, SUB_Q, SUB_K), lambda b, t: (0, 0, 0))],
        out_specs=[pl.BlockSpec((1, tq, GROUP_WIDTH), lambda b, t: (b, t, 0)),
                   pl.BlockSpec((1, tq, LSE_LANES), lambda b, t: (b, t, 0))],
        out_shape=[jax.ShapeDtypeStruct((n, L, GROUP_WIDTH), bf16),
                   jax.ShapeDtypeStruct((n, L, LSE_LANES), f32)],
        scratch_shapes=[pltpu.VMEM((tq + 2 * R, GROUP_WIDTH), bf16),
                        pltpu.VMEM((tq + 2 * R, GROUP_WIDTH), bf16)],
        compiler_params=_cparams("parallel", "arbitrary"),
        name=f"attn_g{g}",
    )(qkv, qkv, qkv, qkv, qkv, qkv, qkv, bias)


LRU_CHUNK = 256
LRU_PAD = 8
LRU_FINISH_ROWS = 512
LRU_SEGS = 8
SEG_GAP = 8


def _lru_kernel(xb_ref, yb_ref, cw_ref, cb_ref, w_ref, gb_ref, lam_ref, o_ref,
                xpad, af, bf, ab, bb, htf, ptf, htb, ptb, cf_scr, cb_scr, *, seq):
    R = LRU_CHUNK
    P = LRU_PAD
    seg_len = seq // LRU_SEGS
    pitch = seg_len + SEG_GAP
    chunks_per_seg = seg_len // R
    n_chunks = seq // R
    xpad[0:P] = jnp.zeros((P, LRU_BW), f32)
    xpad[P + seq:] = jnp.zeros((P, LRU_BW), f32)
    xpad[P:P + seq] = xb_ref[0].astype(f32)
    lam = lam_ref[...]
    log_a_unit = -LRU_C * (jnp.maximum(-lam, 0.0) + jnp.log1p(jnp.exp(-jnp.abs(lam))))
    cw = cw_ref[...]
    cb = cb_ref[...]
    gate_bias = gb_ref[0]
    row = lax.broadcasted_iota(jnp.int32, (R, LRU_BW), 0)

    def chunk(ci, first=False, last=False):
        c0 = ci * R if isinstance(ci, int) else pl.multiple_of(ci * R, R)
        dst = (ci // chunks_per_seg) * pitch + (ci % chunks_per_seg) * R
        dst = dst if isinstance(ci, int) else pl.multiple_of(dst, 8)
        xc = (cw[0:1] * xpad[pl.ds(c0 + (P - 1), R), :] + cw[1:2] * xpad[pl.ds(c0 + P, R), :]
              + cw[2:3] * xpad[pl.ds(c0 + (P + 1), R), :] + cw[3:4] * xpad[pl.ds(c0 + (P + 2), R), :]) + cb
        th = jnp.tanh(jnp.dot(xc.astype(bf16), w_ref[0], preferred_element_type=f32) + gate_bias)
        half_xc = 0.5 * xc
        for direction, (a_scr, b_scr) in enumerate(((af, bf), (ab, bb))):
            base = direction * 2 * LRU_BW
            half_log_a = 0.5 * log_a_unit[direction:direction + 1]
            a = jnp.exp(half_log_a * th[:, base:base + LRU_BW] + half_log_a)
            gated_x = half_xc * th[:, base + LRU_BW:base + 2 * LRU_BW] + half_xc
            y = 1.0 - a * a
            mult = y * lax.rsqrt(jnp.maximum(y, 1e-30))
            if direction == 0 and first:
                mult = jnp.where(row == 0, 1.0, mult)
            if direction == 1 and last:
                mult = jnp.where(row == R - 1, 1.0, mult)
            a_scr[pl.ds(dst, R), :] = a
            b_scr[pl.ds(dst, R), :] = mult * gated_x

    chunk(0, first=True)
    lax.fori_loop(1, n_chunks - 1, lambda ci, c: (chunk(ci), c)[1], 0, unroll=2)
    chunk(n_chunks - 1, last=True)

    def scan(i, carry):
        hf, pf, hb, pb = carry
        rows = pl.ds(i, LRU_SEGS, stride=pitch)
        a = af[rows, :]
        hf = a * hf + bf[rows, :]
        pf = a * pf
        htf[rows, :] = hf
        ptf[rows, :] = pf
        rows = pl.ds(seg_len - 1 - i, LRU_SEGS, stride=pitch)
        a = ab[rows, :]
        hb = a * hb + bb[rows, :]
        pb = a * pb
        htb[rows, :] = hb
        ptb[rows, :] = pb
        return hf, pf, hb, pb

    zero = jnp.zeros((LRU_SEGS, LRU_BW), f32)
    one = jnp.ones((LRU_SEGS, LRU_BW), f32)
    hf, pf, hb, pb = lax.fori_loop(0, seg_len, scan, (zero, one, zero, one), unroll=8)

    c = jnp.zeros((1, LRU_BW), f32)
    cf_scr[0:1] = c
    for j in range(1, LRU_SEGS):
        c = hf[j - 1:j] + pf[j - 1:j] * c
        cf_scr[j:j + 1] = c
    c = jnp.zeros((1, LRU_BW), f32)
    cb_scr[LRU_SEGS - 1:LRU_SEGS] = c
    for j in range(LRU_SEGS - 2, -1, -1):
        c = hb[j + 1:j + 2] + pb[j + 1:j + 2] * c
        cb_scr[j:j + 1] = c

    F = LRU_FINISH_ROWS
    finish_per_seg = seg_len // F

    def finish(ci, carry):
        c0 = pl.multiple_of(ci * F, F)
        seg = ci // finish_per_seg
        rows = pl.ds(pl.multiple_of(seg * pitch + (ci % finish_per_seg) * F, 8), F)
        h = (htf[rows, :] + ptf[rows, :] * cf_scr[pl.ds(seg, 1), :]
             + htb[rows, :] + ptb[rows, :] * cb_scr[pl.ds(seg, 1), :])
        o_ref[0, pl.ds(c0, F), :] = (h * yb_ref[0, pl.ds(c0, F), :].astype(f32)).astype(o_ref.dtype)
        return carry

    lax.fori_loop(0, seq // F, finish, 0)


def _lru(proj3, conv_w, conv_b, w_gates, b_gates, lam):
    B, S, _ = proj3.shape
    xb0 = 0
    yb0 = REST_YB // LRU_BW
    return pl.pallas_call(
        functools.partial(_lru_kernel, seq=S),
        grid=(B, LRU_BLOCKS),
        in_specs=[
            pl.BlockSpec((1, S, LRU_BW), lambda b, n: (b, 0, xb0 + n)),
            pl.BlockSpec((1, S, LRU_BW), lambda b, n: (b, 0, yb0 + n)),
            pl.BlockSpec((4, LRU_BW), lambda b, n: (0, n)),
            pl.BlockSpec((1, LRU_BW), lambda b, n: (0, n)),
            pl.BlockSpec((1, LRU_BW, 4 * LRU_BW), lambda b, n: (n, 0, 0)),
            pl.BlockSpec((1, 1, 4 * LRU_BW), lambda b, n: (n, 0, 0)),
            pl.BlockSpec((2, LRU_BW), lambda b, n: (0, n)),
        ],
        out_specs=pl.BlockSpec((1, S, LRU_BW), lambda b, n: (b, 0, n)),
        out_shape=jax.ShapeDtypeStruct((B, S, LRU_WIDTH), bf16),
        scratch_shapes=([pltpu.VMEM((S + 2 * LRU_PAD, LRU_BW), f32)]
                        + [pltpu.VMEM((S + LRU_SEGS * SEG_GAP, LRU_BW), f32)] * 8
                        + [pltpu.VMEM((LRU_SEGS, LRU_BW), f32)] * 2),
        compiler_params=_cparams("parallel", "parallel"),
        name="lru",
    )(proj3, proj3, conv_w, conv_b, w_gates, b_gates, lam)


def _mem_kv_kernel(m_ref, g_ref, w_ref, o_ref, h_scr):
    @pl.when(pl.program_id(0) == 0)
    def _():
        h_scr[...] = _rms(m_ref[...], g_ref[...]).astype(bf16)

    o_ref[...] = jnp.dot(h_scr[...], w_ref[...], preferred_element_type=f32).astype(o_ref.dtype)


def _mem_kv(mem2, gain, w, tn=512):
    M = mem2.shape[0]
    N = w.shape[1]
    return pl.pallas_call(
        _mem_kv_kernel,
        grid=(N // tn,),
        in_specs=[pl.BlockSpec((M, D_MODEL), lambda j: (0, 0)),
                  pl.BlockSpec((1, D_MODEL), lambda j: (0, 0)),
                  pl.BlockSpec((D_MODEL, tn), lambda j: (0, j))],
        out_specs=pl.BlockSpec((M, tn), lambda j: (0, j)),
        out_shape=jax.ShapeDtypeStruct((M, N), bf16),
        scratch_shapes=[pltpu.VMEM((M, D_MODEL), bf16)],
        compiler_params=_cparams("arbitrary"),
        name="mem_kv",
    )(mem2, gain, w)


def _xattn_kernel(q0_ref, q1_ref, q2_ref, q3_ref, kv_ref, o_ref):
    for h, q_ref in enumerate((q0_ref, q1_ref, q2_ref, q3_ref)):
        c0 = h * MEM_HEAD_DIM
        k = kv_ref[0, :, c0:c0 + MEM_HEAD_DIM]
        v = kv_ref[0, :, MEM_WIDTH + c0:MEM_WIDTH + c0 + MEM_HEAD_DIM]
        logits = lax.dot_general(q_ref[0], k, (((1,), (1,)), ((), ())), preferred_element_type=f32)
        m = jnp.max(logits, axis=-1, keepdims=True)
        p = jnp.exp(logits - m)
        ssum = jnp.sum(p, axis=-1, keepdims=True)
        o = jnp.dot(p.astype(bf16), v, preferred_element_type=f32) * (1.0 / ssum)
        o_ref[0, :, c0:c0 + MEM_HEAD_DIM] = o.astype(o_ref.dtype)


def _xattn(proj3, kv3, tq=1024):
    B, S, _ = proj3.shape
    qb0 = REST_QC // MEM_HEAD_DIM

    def qspec(h):
        return pl.BlockSpec((1, tq, MEM_HEAD_DIM), lambda b, t: (b, t, qb0 + h))

    return pl.pallas_call(
        _xattn_kernel,
        grid=(B, S // tq),
        in_specs=[qspec(0), qspec(1), qspec(2), qspec(3),
                  pl.BlockSpec((1, N_MEM, 2 * MEM_WIDTH), lambda b, t: (b, 0, 0))],
        out_specs=pl.BlockSpec((1, tq, MEM_WIDTH), lambda b, t: (b, t, 0)),
        out_shape=jax.ShapeDtypeStruct((B, S, MEM_WIDTH), bf16),
        compiler_params=_cparams("parallel", "parallel"),
        name="xattn",
    )(proj3, proj3, proj3, proj3, kv3)


def _combine_kernel(o0_ref, o1_ref, o2_ref, l0_ref, l1_ref, l2_ref, ya_ref, o1_scr, o2_scr, l1_scr, l2_scr, *, tm):
    for g, o_ref, l_ref, o_scr, l_scr in ((1, o1_ref, l1_ref, o1_scr, l1_scr),
                                          (2, o2_ref, l2_ref, o2_scr, l2_scr)):
        d = ATTN_GROUPS[g][1]
        for r in range(d):
            l_scr[pl.ds(r, tm // d, stride=d), :] = l_ref[0, r]
            for h in range(HEADS_PER_GROUP):
                o_scr[h, pl.ds(r, tm // d, stride=d), :] = (
                    o_ref[0, r, :, h * HEAD_DIM_A:(h + 1) * HEAD_DIM_A].astype(f32))
    l0, l1, l2 = l0_ref[...], l1_scr[...], l2_scr[...]
    m = jnp.maximum(jnp.maximum(l0, l1), l2)
    e0, e1, e2 = jnp.exp(l0 - m), jnp.exp(l1 - m), jnp.exp(l2 - m)
    inv = 1.0 / (e0 + e1 + e2)
    for h in range(HEADS_PER_GROUP):
        c0 = h * HEAD_DIM_A
        lane = slice(h * LSE_REP, h * LSE_REP + 1)
        y = ((e0 * inv)[:, lane] * o0_ref[:, c0:c0 + HEAD_DIM_A].astype(f32)
             + (e1 * inv)[:, lane] * o1_scr[h] + (e2 * inv)[:, lane] * o2_scr[h])
        ya_ref[:, c0:c0 + HEAD_DIM_A] = y.astype(bf16)


def _combine(o_groups, lse_groups, seq, tm=512):
    T = o_groups[0].shape[0]
    nt = seq // tm
    d1, d2 = ATTN_GROUPS[1][1], ATTN_GROUPS[2][1]

    def rows(width):
        return pl.BlockSpec((tm, width), lambda i: (i, 0))

    def strided_rows(d, width):
        return pl.BlockSpec((1, d, tm // d, width), lambda i: (i // nt, 0, i % nt, 0))

    return pl.pallas_call(
        functools.partial(_combine_kernel, tm=tm),
        grid=(T // tm,),
        in_specs=[rows(GROUP_WIDTH), strided_rows(d1, GROUP_WIDTH), strided_rows(d2, GROUP_WIDTH),
                  rows(LSE_LANES), strided_rows(d1, LSE_LANES), strided_rows(d2, LSE_LANES)],
        out_specs=rows(GROUP_WIDTH),
        out_shape=jax.ShapeDtypeStruct((T, GROUP_WIDTH), bf16),
        scratch_shapes=[pltpu.VMEM((HEADS_PER_GROUP, tm, HEAD_DIM_A), f32),
                        pltpu.VMEM((HEADS_PER_GROUP, tm, HEAD_DIM_A), f32),
                        pltpu.VMEM((tm, LSE_LANES), f32), pltpu.VMEM((tm, LSE_LANES), f32)],
        compiler_params=_cparams("parallel"),
        name="combine",
    )(*o_groups, *lse_groups)


def _gate_mix_kernel(h_ref, ya_ref, yl_ref, yc_ref, wga_ref, wgb_ref, wgc_ref, bga_ref, bgb_ref, bgc_ref,
                     woa_ref, wol_ref, wom_ref, mix_ref):
    h = h_ref[...]

    def gate(w_ref, b_ref):
        return jax.nn.sigmoid(jnp.dot(h, w_ref[...], preferred_element_type=f32) + b_ref[...])

    mixed = (gate(wga_ref, bga_ref) * jnp.dot(ya_ref[...], woa_ref[...], preferred_element_type=f32)
             + gate(wgb_ref, bgb_ref) * jnp.dot(yl_ref[...], wol_ref[...], preferred_element_type=f32)
             + gate(wgc_ref, bgc_ref) * jnp.dot(yc_ref[...], wom_ref[...], preferred_element_type=f32))
    mix_ref[...] = mixed.astype(mix_ref.dtype)


def _gate_mix(h, y_a, y_lru, y_c, w_gate, b_gate, w_o_attn, w_o_lru, w_o_mem, tm=512, tn=512):
    T = h.shape[0]
    nj = D_MODEL // tn

    def rows(width):
        return pl.BlockSpec((tm, width), lambda j, i: (i, 0))

    def gate_w(k):
        return pl.BlockSpec((D_MODEL, tn), lambda j, i: (0, k * nj + j))

    def gate_b(k):
        return pl.BlockSpec((1, tn), lambda j, i: (0, k * nj + j))

    def cols(width):
        return pl.BlockSpec((width, tn), lambda j, i: (0, j))

    return pl.pallas_call(
        _gate_mix_kernel,
        grid=(nj, T // tm),
        in_specs=[rows(D_MODEL), rows(GROUP_WIDTH), rows(LRU_WIDTH), rows(MEM_WIDTH),
                  gate_w(0), gate_w(1), gate_w(2), gate_b(0), gate_b(1), gate_b(2),
                  cols(GROUP_WIDTH), cols(LRU_WIDTH), cols(MEM_WIDTH)],
        out_specs=pl.BlockSpec((tm, tn), lambda j, i: (i, j)),
        out_shape=jax.ShapeDtypeStruct((T, D_MODEL), bf16),
        compiler_params=_cparams("arbitrary", "arbitrary"),
        name="gate_mix",
    )(h, y_a, y_lru, y_c, w_gate, w_gate, w_gate, b_gate, b_gate, b_gate, w_o_attn, w_o_lru, w_o_mem)


def _mlp_kernel(x_ref, mix_ref, wo_ref, g_ref, gf_ref, wu_ref, wd_ref, out_ref, h_scr):
    j = pl.program_id(1)

    @pl.when(j == 0)
    def _():
        x = x_ref[...] + jnp.dot(mix_ref[...], wo_ref[...], preferred_element_type=f32)
        h_scr[...] = _rms(x, g_ref[...]).astype(bf16)
        out_ref[...] = x

    u = jnp.maximum(jnp.dot(h_scr[...], wu_ref[...], preferred_element_type=f32), 0.0)
    out_ref[...] += jnp.dot((u * u).astype(bf16), wd_ref[...], preferred_element_type=f32)

    @pl.when(j == pl.num_programs(1) - 1)
    def _():
        out_ref[...] = _rms(out_ref[...], gf_ref[...])


def _mlp(x2, mixed, w_out, gain, gain_final, w_up, w_down, tm=512, tf=1024):
    T = x2.shape[0]
    return pl.pallas_call(
        _mlp_kernel,
        grid=(T // tm, D_FF // tf),
        in_specs=[pl.BlockSpec((tm, D_MODEL), lambda i, j: (i, 0)),
                  pl.BlockSpec((tm, D_MODEL), lambda i, j: (i, 0)),
                  pl.BlockSpec((D_MODEL, D_MODEL), lambda i, j: (0, 0)),
                  pl.BlockSpec((1, D_MODEL), lambda i, j: (0, 0)),
                  pl.BlockSpec((1, D_MODEL), lambda i, j: (0, 0)),
                  pl.BlockSpec((D_MODEL, tf), lambda i, j: (0, j)),
                  pl.BlockSpec((tf, D_MODEL), lambda i, j: (j, 0))],
        out_specs=pl.BlockSpec((tm, D_MODEL), lambda i, j: (i, 0)),
        out_shape=jax.ShapeDtypeStruct((T, D_MODEL), f32),
        scratch_shapes=[pltpu.VMEM((tm, D_MODEL), bf16)],
        compiler_params=_cparams("parallel", "arbitrary"),
        name="mlp",
    )(x2, mixed, w_out, gain, gain_final, w_up, w_down)


def _query_scale():
    scale = np.ones((1, N_IN), np.float32)
    scale[:, :WIDTH_A] = 1.0 / math.sqrt(HEAD_DIM_A)
    scale[:, COL_QC:] = 1.0 / math.sqrt(MEM_HEAD_DIM)
    return scale


def kernel(x, mem, rel_bias, norm_mix, norm_mem, norm_mlp, norm_final, w_in, w_gate, b_gate, conv_w, conv_b,
           lru_wa, lru_ba, lru_wi, lru_bi, lru_lambda, w_mem_kv, w_o_attn, w_o_lru, w_o_mem, w_out, w_up, w_down):
    B, S, D = x.shape
    T = B * S
    depth = w_in.shape[0]
    assert depth == 1, "the final RMSNorm is fused into the (single) layer's MLP kernel"
    x2 = x.reshape(T, D)
    mem2 = mem.reshape(B * N_MEM, D)
    for l in range(depth):
        w_qkv = (w_in[l] * _query_scale()).astype(bf16)
        h, rest, w_up_bf, w_down_bf = _rest_proj(x2, norm_mix[l].reshape(1, D), w_qkv, w_up[l], w_down[l])
        proj3 = rest.reshape(B, S, REST_W)

        side_casts = ((w_gate[l],), (w_out[l], w_o_lru[l]), (w_o_attn[l], w_o_mem[l], w_mem_kv[l]))
        casted = []
        attn = []
        for g in range(len(ATTN_GROUPS)):
            d = ATTN_GROUPS[g][1]
            qkv, *bf_copies = _qkv_proj(h, w_qkv, g, B, S, cast=side_casts[g])
            casted.append(bf_copies)
            o, lse = _attn_group(qkv.reshape(B * d, S // d, QKV_W), rel_bias, g)
            if g == 0:
                attn.append((o.reshape(T, GROUP_WIDTH), lse.reshape(T, LSE_LANES)))
            else:
                attn.append((o.reshape(B, d, S // d, GROUP_WIDTH), lse.reshape(B, d, S // d, LSE_LANES)))

        w_gates = (0.5 * jnp.concatenate([lru_wa[l, 0], lru_wi[l, 0], lru_wa[l, 1], lru_wi[l, 1]], axis=-1)
                   ).astype(bf16)
        b_gates = 0.5 * jnp.concatenate([lru_ba[l, 0], lru_bi[l, 0], lru_ba[l, 1], lru_bi[l, 1]], axis=-1)
        y_lru = _lru(proj3, conv_w[l], conv_b[l].reshape(1, LRU_WIDTH), w_gates,
                     b_gates.reshape(LRU_BLOCKS, 1, 4 * LRU_BW), lru_lambda[l])

        (w_gate_bf,), (w_out_bf, w_o_lru_bf), (w_o_attn_bf, w_o_mem_bf, w_mem_kv_bf) = casted
        kv = _mem_kv(mem2, norm_mem[l].reshape(1, D), w_mem_kv_bf)
        y_c = _xattn(proj3, kv.reshape(B, N_MEM, 2 * MEM_WIDTH))

        y_a = _combine([a[0] for a in attn], [a[1] for a in attn], S)
        mixed = _gate_mix(h, y_a, y_lru.reshape(T, LRU_WIDTH), y_c.reshape(T, MEM_WIDTH),
                          w_gate_bf, b_gate[l].reshape(1, 3 * D), w_o_attn_bf, w_o_lru_bf, w_o_mem_bf)
        x2 = _mlp(x2, mixed, w_out_bf, norm_mlp[l].reshape(1, D), norm_final.reshape(1, D),
                  w_up_bf, w_down_bf)
    return x2.reshape(B, S, D)
```

```python
import functools
import math

import jax
import jax.numpy as jnp
import numpy as np
from jax import lax
from jax.experimental import pallas as pl
from jax.experimental.pallas import tpu as pltpu

D_MODEL = 2048
HEAD_DIM_A = 128
ATTN_GROUPS = ((128, 1), (512, 4), (2048, 16))
HEADS_PER_GROUP = 4
GROUP_WIDTH = HEADS_PER_GROUP * HEAD_DIM_A
WIDTH_A = len(ATTN_GROUPS) * GROUP_WIDTH
ATTN_RADIUS = 64
N_BUCKETS = 32
MAX_DISTANCE = 1024
LRU_WIDTH = 1536
LRU_BLOCKS = 12
LRU_BW = 128
LRU_C = 8.0
N_MEM = 256
MEM_HEADS = 4
MEM_HEAD_DIM = 256
MEM_WIDTH = MEM_HEADS * MEM_HEAD_DIM
D_FF = 4 * D_MODEL
EPS = 1e-6
N_IN = 3 * WIDTH_A + 2 * LRU_WIDTH + MEM_WIDTH
COL_K = WIDTH_A
COL_V = 2 * WIDTH_A
COL_XB = 3 * WIDTH_A
COL_YB = 3 * WIDTH_A + LRU_WIDTH
COL_QC = 3 * WIDTH_A + 2 * LRU_WIDTH
NEG_INF = -1e30

ATTN_ROWS_PER_STEP = 1024
SUB_Q = 128
SUB_K = SUB_Q + 2 * ATTN_RADIUS
LSE_LANES = 128
LSE_REP = LSE_LANES // HEADS_PER_GROUP

VMEM_LIMIT = 56 * 1024 * 1024

f32 = jnp.float32
bf16 = jnp.bfloat16


def _cparams(*sem):
    return pltpu.CompilerParams(dimension_semantics=sem, vmem_limit_bytes=VMEM_LIMIT)


def _rms(x, gain):
    return x * lax.rsqrt(jnp.mean(x * x, axis=-1, keepdims=True) + EPS) * gain


QKV_W = 3 * GROUP_WIDTH
REST_W = 2 * LRU_WIDTH + MEM_WIDTH
REST_YB = LRU_WIDTH
REST_QC = 2 * LRU_WIDTH
PROJ_TN = GROUP_WIDTH
LANES = 128
SLABS = PROJ_TN // LANES
PROJ_ROW_BLOCKS = 2
DEINTERLEAVE_STEP = 4
N_STAGE = 2


def _qkv_kernel(h_ref, wq_ref, wk_ref, wv_ref, *refs, d, tm, n_cast):
    cast_in, o_ref = refs[:n_cast], refs[n_cast]
    cast_out = refs[n_cast + 1:2 * n_cast + 1]
    res_scr, tmp_scr = refs[2 * n_cast + 1:]
    for src_ref, dst_ref in zip(cast_in, cast_out):
        dst_ref[...] = src_ref[...].astype(bf16)
    mb = tm // PROJ_ROW_BLOCKS
    n = 0
    for k in range(PROJ_ROW_BLOCKS):
        hk = h_ref[k * mb:(k + 1) * mb, :]
        for t, w_ref in enumerate((wq_ref, wk_ref, wv_ref)):
            res = jnp.dot(hk, w_ref[...], preferred_element_type=f32)
            col = t * PROJ_TN
            if d == 1:
                o_ref[0, k * mb:(k + 1) * mb, col:col + PROJ_TN] = res.astype(bf16)
                continue
            buf = n % N_STAGE
            n += 1
            for c in range(SLABS):
                res_scr[buf, c] = res[:, c * LANES:(c + 1) * LANES]
            src, step = res_scr, d
            if d == DEINTERLEAVE_STEP ** 2:
                step = DEINTERLEAVE_STEP
                for r in range(step):
                    for c in range(SLABS):
                        tmp_scr[buf, c, r * (mb // step):(r + 1) * (mb // step), :] = (
                            res_scr[buf, c, pl.ds(r, mb // step, stride=step), :])
                src = tmp_scr
            for r in range(d):
                start = r if src is res_scr else (r % step) * (mb // step) + r // step
                for c in range(SLABS):
                    o_ref[0, r, k * (mb // d):(k + 1) * (mb // d), col + c * LANES:col + (c + 1) * LANES] = (
                        src[buf, c, pl.ds(start, mb // d, stride=step), :].astype(bf16))


def _qkv_proj(h, w_qkv, g, batch, seq, cast=(), tm=1024):
    T = h.shape[0]
    d = ATTN_GROUPS[g][1]
    nt = seq // tm
    n_groups = len(ATTN_GROUPS)
    mb = tm // PROJ_ROW_BLOCKS
    steps = T // tm

    def w_spec(which):
        return pl.BlockSpec((D_MODEL, PROJ_TN), lambda i: (0, which * n_groups + g))

    cast_specs = [pl.BlockSpec((w.shape[0] // steps, w.shape[1]), lambda i: (i, 0)) for w in cast]

    if d == 1:
        out_spec = pl.BlockSpec((1, tm, QKV_W), lambda i: (i // nt, i % nt, 0))
        out_shape = jax.ShapeDtypeStruct((batch, seq, QKV_W), bf16)
    else:
        out_spec = pl.BlockSpec((1, d, tm // d, QKV_W), lambda i: (i // nt, 0, i % nt, 0))
        out_shape = jax.ShapeDtypeStruct((batch, d, seq // d, QKV_W), bf16)
    return pl.pallas_call(
        functools.partial(_qkv_kernel, d=d, tm=tm, n_cast=len(cast)),
        grid=(steps,),
        in_specs=[pl.BlockSpec((tm, D_MODEL), lambda i: (i, 0)), w_spec(0), w_spec(1), w_spec(2)] + cast_specs,
        out_specs=[out_spec] + cast_specs,
        out_shape=[out_shape] + [jax.ShapeDtypeStruct(w.shape, bf16) for w in cast],
        scratch_shapes=[pltpu.VMEM((N_STAGE, SLABS, mb, LANES), f32)] * 2,
        compiler_params=_cparams("parallel"),
        name=f"qkv_g{g}",
    )(h, w_qkv, w_qkv, w_qkv, *cast)


REST_TILES = REST_W // PROJ_TN
YB_TILES = range(REST_YB // PROJ_TN, REST_QC // PROJ_TN)


def _gelu_tanh(y):
    return y * (0.5 * (1.0 + jnp.tanh(math.sqrt(2.0 / math.pi) * (y + 0.044715 * (y * y * y)))))


def _rest_kernel(x_ref, g_ref, *refs):
    w_refs = refs[:REST_TILES]
    wu_ref, wd_ref, h_ref, o_ref, wu_o_ref, wd_o_ref = refs[REST_TILES:]
    h_ref[...] = _rms(x_ref[...], g_ref[...]).astype(bf16)
    h = h_ref[...]
    for c, w_ref in enumerate(w_refs):
        res = jnp.dot(h, w_ref[...], preferred_element_type=f32)
        if c in YB_TILES:
            res = _gelu_tanh(res)
        o_ref[:, c * PROJ_TN:(c + 1) * PROJ_TN] = res.astype(bf16)
    wu_o_ref[...] = wu_ref[...].astype(bf16)
    wd_o_ref[...] = wd_ref[...].astype(bf16)


def _rest_proj(x2, gain, w_in, w_up, w_down, tm=512):
    T = x2.shape[0]
    steps = T // tm
    first = COL_XB // PROJ_TN
    up_rows, down_rows = D_MODEL // steps, D_FF // steps

    def w_spec(c):
        return pl.BlockSpec((D_MODEL, PROJ_TN), lambda i: (0, first + c), pipeline_mode=pl.Buffered(1))

    return pl.pallas_call(
        _rest_kernel,
        grid=(steps,),
        in_specs=[pl.BlockSpec((tm, D_MODEL), lambda i: (i, 0)),
                  pl.BlockSpec((1, D_MODEL), lambda i: (0, 0))]
        + [w_spec(c) for c in range(REST_TILES)]
        + [pl.BlockSpec((up_rows, D_FF), lambda i: (i, 0)),
           pl.BlockSpec((down_rows, D_MODEL), lambda i: (i, 0))],
        out_specs=[pl.BlockSpec((tm, D_MODEL), lambda i: (i, 0)),
                   pl.BlockSpec((tm, REST_W), lambda i: (i, 0)),
                   pl.BlockSpec((up_rows, D_FF), lambda i: (i, 0)),
                   pl.BlockSpec((down_rows, D_MODEL), lambda i: (i, 0))],
        out_shape=[jax.ShapeDtypeStruct((T, D_MODEL), bf16),
                   jax.ShapeDtypeStruct((T, REST_W), bf16),
                   jax.ShapeDtypeStruct((D_MODEL, D_FF), bf16),
                   jax.ShapeDtypeStruct((D_FF, D_MODEL), bf16)],
        compiler_params=_cparams("arbitrary"),
        name="rest_proj",
    )(x2, gain, *([w_in] * REST_TILES), w_up, w_down)


def _t5_bucket(rel):
    nb = N_BUCKETS // 2
    max_exact = nb // 2
    sign = (rel > 0).astype(np.int32) * nb
    n = np.abs(rel)
    large = max_exact + (np.log(np.maximum(n, 1) / max_exact)
                         / np.log(MAX_DISTANCE / max_exact) * (nb - max_exact)).astype(np.int32)
    large = np.minimum(large, nb - 1)
    return (sign + np.where(n < max_exact, n, large)).astype(np.int32)


def _band_bias(rel_bias_g, dilation):
    qq = np.arange(SUB_Q)[:, None]
    kk = np.arange(SUB_K)[None, :]
    rel = kk - ATTN_RADIUS - qq
    onehot = (_t5_bucket(rel * dilation)[None] == np.arange(N_BUCKETS)[:, None, None]).astype(np.float32)
    bias = jnp.einsum('nh,nqk->hqk', rel_bias_g.astype(f32), onehot, precision=lax.Precision.HIGHEST)
    return bias + np.where(np.abs(rel) <= ATTN_RADIUS, 0.0, NEG_INF).astype(np.float32)[None]


def _attn_kernel(q_ref, kp_ref, km_ref, kn_ref, vp_ref, vm_ref, vn_ref, bias_ref,
                 o_ref, lse_ref, kbuf, vbuf, *, tq, seq, n_seq):
    R = ATTN_RADIUS
    q0 = pl.program_id(1) * tq
    lane = lax.broadcasted_iota(jnp.int32, (SUB_Q, LSE_LANES), 1)
    n_sub = tq // SUB_Q
    for i in range(n_seq):
        kbuf[i, 0:R] = kp_ref[i]
        kbuf[i, R:R + tq] = km_ref[i]
        kbuf[i, R + tq:] = kn_ref[i]
        vbuf[i, 0:R] = vp_ref[i]
        vbuf[i, R:R + tq] = vm_ref[i]
        vbuf[i, R + tq:] = vn_ref[i]
        for s in range(n_sub):
            r0 = s * SUB_Q
            edge = None
            if s == 0 or s == n_sub - 1:
                pos = q0 + (r0 - R) + lax.broadcasted_iota(jnp.int32, (1, SUB_K), 1)
                edge = jnp.where(pos >= 0, jnp.where(pos < seq, 0.0, NEG_INF), NEG_INF)
            lse_tile = None
            for h in range(HEADS_PER_GROUP):
                c0 = h * HEAD_DIM_A
                q = q_ref[i, r0:r0 + SUB_Q, c0:c0 + HEAD_DIM_A]
                k = kbuf[i, r0:r0 + SUB_K, c0:c0 + HEAD_DIM_A]
                v = vbuf[i, r0:r0 + SUB_K, c0:c0 + HEAD_DIM_A]
                logits = lax.dot_general(q, k, (((1,), (1,)), ((), ())), preferred_element_type=f32) + bias_ref[h]
                if edge is not None:
                    logits = logits + edge
                m = jnp.max(logits, axis=-1, keepdims=True)
                p = jnp.exp(logits - m)
                ssum = jnp.sum(p, axis=-1, keepdims=True)
                o = jnp.dot(p.astype(bf16), v, preferred_element_type=f32) * (1.0 / ssum)
                o_ref[i, r0:r0 + SUB_Q, c0:c0 + HEAD_DIM_A] = o.astype(o_ref.dtype)
                lse = m + jnp.log(ssum)
                lse_tile = lse if lse_tile is None else jnp.where(lane >= h * LSE_REP, lse, lse_tile)
            lse_ref[i, r0:r0 + SUB_Q, :] = jnp.broadcast_to(lse_tile, (SUB_Q, LSE_LANES))


def _attn_group(qkv, rel_bias, g):
    _, d = ATTN_GROUPS[g]
    n, L, _ = qkv.shape
    tq = min(ATTN_ROWS_PER_STEP, L)
    ns = ATTN_ROWS_PER_STEP // tq
    R = ATTN_RADIUS
    bias = _band_bias(rel_bias[:, g * HEADS_PER_GROUP:(g + 1) * HEADS_PER_GROUP], d)
    rb = tq // R
    last_rb = L // R - 1

    def main(col, width=GROUP_WIDTH):
        return pl.BlockSpec((ns, tq, width), lambda b, t: (b, t, col))

    def prev(col):
        return pl.BlockSpec((ns, R, GROUP_WIDTH), lambda b, t: (b, jnp.maximum(t * rb - 1, 0), col))

    def nxt(col):
        return pl.BlockSpec((ns, R, GROUP_WIDTH), lambda b, t: (b, jnp.minimum((t + 1) * rb, last_rb), col))

    return pl.pallas_call(
        functools.partial(_attn_kernel, tq=tq, seq=L, n_seq=ns),
        grid=(n // ns, L // tq),
        in_specs=[main(0), prev(1), main(1), nxt(1), prev(2), main(2), nxt(2),
                  pl.BlockSpec((HEADS_PER_GROUP, SUB_Q, SUB_K), lambda b, t: (0, 0, 0))],
        out_specs=[main(0), main(0, LSE_LANES)],
        out_shape=[jax.ShapeDtypeStruct((n, L, GROUP_WIDTH), bf16),
                   jax.ShapeDtypeStruct((n, L, LSE_LANES), f32)],
        scratch_shapes=[pltpu.VMEM((ns, tq + 2 * R, GROUP_WIDTH), bf16),
                        pltpu.VMEM((ns, tq + 2 * R, GROUP_WIDTH), bf16)],
        compiler_params=_cparams("parallel", "arbitrary"),
        name=f"attn_g{g}",
    )(qkv, qkv, qkv, qkv, qkv, qkv, qkv, bias)


LRU_CHUNK = 256
LRU_PAD = 8
LRU_FINISH_ROWS = 512
LRU_SEGS = 8
SEG_GAP = 4


def _lru_kernel(xb_ref, yb_ref, cw_ref, cb_ref, w_ref, gb_ref, lam_ref, o_ref,
                xpad, af, bf, ab, bb, htf, ptf, htb, ptb, cf_scr, cb_scr, *, seq):
    R = LRU_CHUNK
    P = LRU_PAD
    seg_len = seq // LRU_SEGS
    pitch = seg_len + SEG_GAP
    chunks_per_seg = seg_len // R
    n_chunks = seq // R
    xpad[0:P] = jnp.zeros((P, LRU_BW), f32)
    xpad[P + seq:] = jnp.zeros((P, LRU_BW), f32)
    xpad[P:P + seq] = xb_ref[0].astype(f32)
    lam = lam_ref[...]
    log_a_unit = -LRU_C * (jnp.maximum(-lam, 0.0) + jnp.log1p(jnp.exp(-jnp.abs(lam))))
    cw = cw_ref[...]
    cb = cb_ref[...]
    gate_bias = gb_ref[0]
    row = lax.broadcasted_iota(jnp.int32, (R, LRU_BW), 0)

    def chunk(ci, first=False, last=False):
        c0 = ci * R
        dst = (ci // chunks_per_seg) * pitch + (ci % chunks_per_seg) * R
        xc = (cw[0:1] * xpad[pl.ds(c0 + (P - 1), R), :] + cw[1:2] * xpad[pl.ds(c0 + P, R), :]
              + cw[2:3] * xpad[pl.ds(c0 + (P + 1), R), :] + cw[3:4] * xpad[pl.ds(c0 + (P + 2), R), :]) + cb
        th = jnp.tanh(jnp.dot(xc.astype(bf16), w_ref[0], preferred_element_type=f32) + gate_bias)
        half_xc = 0.5 * xc
        for direction, (a_scr, b_scr) in enumerate(((af, bf), (ab, bb))):
            base = direction * 2 * LRU_BW
            half_log_a = 0.5 * log_a_unit[direction:direction + 1]
            a = jnp.exp(half_log_a * th[:, base:base + LRU_BW] + half_log_a)
            gated_x = half_xc * th[:, base + LRU_BW:base + 2 * LRU_BW] + half_xc
            y = 1.0 - a * a
            mult = y * lax.rsqrt(jnp.maximum(y, 1e-30))
            if direction == 0 and first:
                mult = jnp.where(row == 0, 1.0, mult)
            if direction == 1 and last:
                mult = jnp.where(row == R - 1, 1.0, mult)
            a_scr[pl.ds(dst, R), :] = a
            b_scr[pl.ds(dst, R), :] = mult * gated_x

    for ci in range(n_chunks):
        chunk(ci, first=ci == 0, last=ci == n_chunks - 1)

    def scan(i, carry):
        hf, pf, hb, pb = carry
        rows = pl.ds(i, LRU_SEGS, stride=pitch)
        a = af[rows, :]
        hf = a * hf + bf[rows, :]
        pf = a * pf
        htf[rows, :] = hf
        ptf[rows, :] = pf
        rows = pl.ds(seg_len - 1 - i, LRU_SEGS, stride=pitch)
        a = ab[rows, :]
        hb = a * hb + bb[rows, :]
        pb = a * pb
        htb[rows, :] = hb
        ptb[rows, :] = pb
        return hf, pf, hb, pb

    zero = jnp.zeros((LRU_SEGS, LRU_BW), f32)
    one = jnp.ones((LRU_SEGS, LRU_BW), f32)
    hf, pf, hb, pb = lax.fori_loop(0, seg_len, scan, (zero, one, zero, one), unroll=8)

    c = jnp.zeros((1, LRU_BW), f32)
    cf_scr[0:1] = c
    for j in range(1, LRU_SEGS):
        c = hf[j - 1:j] + pf[j - 1:j] * c
        cf_scr[j:j + 1] = c
    c = jnp.zeros((1, LRU_BW), f32)
    cb_scr[LRU_SEGS - 1:LRU_SEGS] = c
    for j in range(LRU_SEGS - 2, -1, -1):
        c = hb[j + 1:j + 2] + pb[j + 1:j + 2] * c
        cb_scr[j:j + 1] = c

    F = LRU_FINISH_ROWS
    finish_per_seg = seg_len // F

    for ci in range(seq // F):
        c0 = ci * F
        seg = ci // finish_per_seg
        rows = pl.ds(seg * pitch + (ci % finish_per_seg) * F, F)
        h = (htf[rows, :] + ptf[rows, :] * cf_scr[seg:seg + 1, :]
             + htb[rows, :] + ptb[rows, :] * cb_scr[seg:seg + 1, :])
        o_ref[0, c0:c0 + F, :] = (h * yb_ref[0, c0:c0 + F, :].astype(f32)).astype(o_ref.dtype)


def _lru(proj3, conv_w, conv_b, w_gates, b_gates, lam):
    B, S, _ = proj3.shape
    xb0 = 0
    yb0 = REST_YB // LRU_BW
    return pl.pallas_call(
        functools.partial(_lru_kernel, seq=S),
        grid=(B, LRU_BLOCKS),
        in_specs=[
            pl.BlockSpec((1, S, LRU_BW), lambda b, n: (b, 0, xb0 + n)),
            pl.BlockSpec((1, S, LRU_BW), lambda b, n: (b, 0, yb0 + n)),
            pl.BlockSpec((4, LRU_BW), lambda b, n: (0, n)),
            pl.BlockSpec((1, LRU_BW), lambda b, n: (0, n)),
            pl.BlockSpec((1, LRU_BW, 4 * LRU_BW), lambda b, n: (n, 0, 0)),
            pl.BlockSpec((1, 1, 4 * LRU_BW), lambda b, n: (n, 0, 0)),
            pl.BlockSpec((2, LRU_BW), lambda b, n: (0, n)),
        ],
        out_specs=pl.BlockSpec((1, S, LRU_BW), lambda b, n: (b, 0, n)),
        out_shape=jax.ShapeDtypeStruct((B, S, LRU_WIDTH), bf16),
        scratch_shapes=([pltpu.VMEM((S + 2 * LRU_PAD, LRU_BW), f32)]
                        + [pltpu.VMEM((S + LRU_SEGS * SEG_GAP, LRU_BW), f32)] * 8
                        + [pltpu.VMEM((LRU_SEGS, LRU_BW), f32)] * 2),
        compiler_params=_cparams("parallel", "parallel"),
        name="lru",
    )(proj3, proj3, conv_w, conv_b, w_gates, b_gates, lam)


def _mem_kv_kernel(m_ref, g_ref, w_ref, o_ref, h_scr):
    @pl.when(pl.program_id(0) == 0)
    def _():
        h_scr[...] = _rms(m_ref[...], g_ref[...]).astype(bf16)

    o_ref[...] = jnp.dot(h_scr[...], w_ref[...], preferred_element_type=f32).astype(o_ref.dtype)


def _mem_kv(mem2, gain, w, tn=512):
    M = mem2.shape[0]
    N = w.shape[1]
    return pl.pallas_call(
        _mem_kv_kernel,
        grid=(N // tn,),
        in_specs=[pl.BlockSpec((M, D_MODEL), lambda j: (0, 0)),
                  pl.BlockSpec((1, D_MODEL), lambda j: (0, 0)),
                  pl.BlockSpec((D_MODEL, tn), lambda j: (0, j))],
        out_specs=pl.BlockSpec((M, tn), lambda j: (0, j)),
        out_shape=jax.ShapeDtypeStruct((M, N), bf16),
        scratch_shapes=[pltpu.VMEM((M, D_MODEL), bf16)],
        compiler_params=_cparams("arbitrary"),
        name="mem_kv",
    )(mem2, gain, w)


def _xattn_kernel(q0_ref, q1_ref, q2_ref, q3_ref, kv_ref, o_ref):
    for h, q_ref in enumerate((q0_ref, q1_ref, q2_ref, q3_ref)):
        c0 = h * MEM_HEAD_DIM
        k = kv_ref[0, :, c0:c0 + MEM_HEAD_DIM]
        v = kv_ref[0, :, MEM_WIDTH + c0:MEM_WIDTH + c0 + MEM_HEAD_DIM]
        logits = lax.dot_general(q_ref[0], k, (((1,), (1,)), ((), ())), preferred_element_type=f32)
        m = jnp.max(logits, axis=-1, keepdims=True)
        p = jnp.exp(logits - m)
        ssum = jnp.sum(p, axis=-1, keepdims=True)
        o = jnp.dot(p.astype(bf16), v, preferred_element_type=f32) * (1.0 / ssum)
        o_ref[0, :, c0:c0 + MEM_HEAD_DIM] = o.astype(o_ref.dtype)


def _xattn(proj3, kv3, tq=1024):
    B, S, _ = proj3.shape
    qb0 = REST_QC // MEM_HEAD_DIM

    def qspec(h):
        return pl.BlockSpec((1, tq, MEM_HEAD_DIM), lambda b, t: (b, t, qb0 + h))

    return pl.pallas_call(
        _xattn_kernel,
        grid=(B, S // tq),
        in_specs=[qspec(0), qspec(1), qspec(2), qspec(3),
                  pl.BlockSpec((1, N_MEM, 2 * MEM_WIDTH), lambda b, t: (b, 0, 0))],
        out_specs=pl.BlockSpec((1, tq, MEM_WIDTH), lambda b, t: (b, t, 0)),
        out_shape=jax.ShapeDtypeStruct((B, S, MEM_WIDTH), bf16),
        compiler_params=_cparams("parallel", "parallel"),
        name="xattn",
    )(proj3, proj3, proj3, proj3, kv3)


def _combine_kernel(o0_ref, o1_ref, o2_ref, l0_ref, l1_ref, l2_ref, ya_ref,
                    o1_scr, o2_scr, l1_scr, l2_scr, tmp_scr, *, tm):
    step = DEINTERLEAVE_STEP
    for g, o_ref, l_ref, o_scr, l_scr in ((1, o1_ref, l1_ref, o1_scr, l1_scr),
                                          (2, o2_ref, l2_ref, o2_scr, l2_scr)):
        d = ATTN_GROUPS[g][1]
        slabs = [(l_scr, lambda r: l_ref[0, r])]
        slabs += [(o_scr.at[h], lambda r, h=h: o_ref[0, r, :, h * HEAD_DIM_A:(h + 1) * HEAD_DIM_A].astype(f32))
                  for h in range(HEADS_PER_GROUP)]
        for k, (dst, rows_of) in enumerate(slabs):
            if d == step:
                for r in range(d):
                    dst[pl.ds(r, tm // d, stride=d), :] = rows_of(r)
                continue
            tmp = tmp_scr.at[k]
            for r in range(d):
                tmp[pl.ds((r % step) * (tm // step) + r // step, tm // d, stride=step), :] = rows_of(r)
            for lo in range(step):
                dst[pl.ds(lo, tm // step, stride=step), :] = tmp[lo * (tm // step):(lo + 1) * (tm // step), :]
    l0, l1, l2 = l0_ref[...], l1_scr[...], l2_scr[...]
    m = jnp.maximum(jnp.maximum(l0, l1), l2)
    e0, e1, e2 = jnp.exp(l0 - m), jnp.exp(l1 - m), jnp.exp(l2 - m)
    inv = 1.0 / (e0 + e1 + e2)
    for h in range(HEADS_PER_GROUP):
        c0 = h * HEAD_DIM_A
        lane = slice(h * LSE_REP, h * LSE_REP + 1)
        y = ((e0 * inv)[:, lane] * o0_ref[:, c0:c0 + HEAD_DIM_A].astype(f32)
             + (e1 * inv)[:, lane] * o1_scr[h] + (e2 * inv)[:, lane] * o2_scr[h])
        ya_ref[:, c0:c0 + HEAD_DIM_A] = y.astype(bf16)


def _combine(o_groups, lse_groups, seq, tm=512):
    T = o_groups[0].shape[0]
    nt = seq // tm
    d1, d2 = ATTN_GROUPS[1][1], ATTN_GROUPS[2][1]

    def rows(width):
        return pl.BlockSpec((tm, width), lambda i: (i, 0))

    def strided_rows(d, width):
        return pl.BlockSpec((1, d, tm // d, width), lambda i: (i // nt, 0, i % nt, 0))

    return pl.pallas_call(
        functools.partial(_combine_kernel, tm=tm),
        grid=(T // tm,),
        in_specs=[rows(GROUP_WIDTH), strided_rows(d1, GROUP_WIDTH), strided_rows(d2, GROUP_WIDTH),
                  rows(LSE_LANES), strided_rows(d1, LSE_LANES), strided_rows(d2, LSE_LANES)],
        out_specs=rows(GROUP_WIDTH),
        out_shape=jax.ShapeDtypeStruct((T, GROUP_WIDTH), bf16),
        scratch_shapes=[pltpu.VMEM((HEADS_PER_GROUP, tm, HEAD_DIM_A), f32),
                        pltpu.VMEM((HEADS_PER_GROUP, tm, HEAD_DIM_A), f32),
                        pltpu.VMEM((tm, LSE_LANES), f32), pltpu.VMEM((tm, LSE_LANES), f32),
                        pltpu.VMEM((1 + HEADS_PER_GROUP, tm, LANES), f32)],
        compiler_params=_cparams("parallel"),
        name="combine",
    )(*o_groups, *lse_groups)


def _gate_mix_kernel(h_ref, ya_ref, yl_ref, yc_ref, wga_ref, wgb_ref, wgc_ref, bga_ref, bgb_ref, bgc_ref,
                     woa_ref, wol_ref, wom_ref, mix_ref):
    h = h_ref[...]

    def gate(w_ref, b_ref):
        return jax.nn.sigmoid(jnp.dot(h, w_ref[...], preferred_element_type=f32) + b_ref[...])

    mixed = (gate(wga_ref, bga_ref) * jnp.dot(ya_ref[...], woa_ref[...], preferred_element_type=f32)
             + gate(wgb_ref, bgb_ref) * jnp.dot(yl_ref[...], wol_ref[...], preferred_element_type=f32)
             + gate(wgc_ref, bgc_ref) * jnp.dot(yc_ref[...], wom_ref[...], preferred_element_type=f32))
    mix_ref[...] = mixed.astype(mix_ref.dtype)


def _gate_mix(h, y_a, y_lru, y_c, w_gate, b_gate, w_o_attn, w_o_lru, w_o_mem, tm=512, tn=512):
    T = h.shape[0]
    nj = D_MODEL // tn

    def rows(width):
        return pl.BlockSpec((tm, width), lambda j, i: (i, 0))

    def gate_w(k):
        return pl.BlockSpec((D_MODEL, tn), lambda j, i: (0, k * nj + j))

    def gate_b(k):
        return pl.BlockSpec((1, tn), lambda j, i: (0, k * nj + j))

    def cols(width):
        return pl.BlockSpec((width, tn), lambda j, i: (0, j))

    return pl.pallas_call(
        _gate_mix_kernel,
        grid=(nj, T // tm),
        in_specs=[rows(D_MODEL), rows(GROUP_WIDTH), rows(LRU_WIDTH), rows(MEM_WIDTH),
                  gate_w(0), gate_w(1), gate_w(2), gate_b(0), gate_b(1), gate_b(2),
                  cols(GROUP_WIDTH), cols(LRU_WIDTH), cols(MEM_WIDTH)],
        out_specs=pl.BlockSpec((tm, tn), lambda j, i: (i, j)),
        out_shape=jax.ShapeDtypeStruct((T, D_MODEL), bf16),
        compiler_params=_cparams("arbitrary", "arbitrary"),
        name="gate_mix",
    )(h, y_a, y_lru, y_c, w_gate, w_gate, w_gate, b_gate, b_gate, b_gate, w_o_attn, w_o_lru, w_o_mem)


def _mlp_kernel(x_ref, mix_ref, wo_ref, g_ref, gf_ref, wu_ref, wd_ref, out_ref, h_scr):
    j = pl.program_id(1)

    @pl.when(j == 0)
    def _():
        x = x_ref[...] + jnp.dot(mix_ref[...], wo_ref[...], preferred_element_type=f32)
        h_scr[...] = _rms(x, g_ref[...]).astype(bf16)
        out_ref[...] = x

    u = jnp.maximum(jnp.dot(h_scr[...], wu_ref[...], preferred_element_type=f32), 0.0)
    out_ref[...] += jnp.dot((u * u).astype(bf16), wd_ref[...], preferred_element_type=f32)

    @pl.when(j == pl.num_programs(1) - 1)
    def _():
        out_ref[...] = _rms(out_ref[...], gf_ref[...])


def _mlp(x2, mixed, w_out, gain, gain_final, w_up, w_down, tm=512, tf=1024):
    T = x2.shape[0]
    return pl.pallas_call(
        _mlp_kernel,
        grid=(T // tm, D_FF // tf),
        in_specs=[pl.BlockSpec((tm, D_MODEL), lambda i, j: (i, 0)),
                  pl.BlockSpec((tm, D_MODEL), lambda i, j: (i, 0)),
                  pl.BlockSpec((D_MODEL, D_MODEL), lambda i, j: (0, 0)),
                  pl.BlockSpec((1, D_MODEL), lambda i, j: (0, 0)),
                  pl.BlockSpec((1, D_MODEL), lambda i, j: (0, 0)),
                  pl.BlockSpec((D_MODEL, tf), lambda i, j: (0, j)),
                  pl.BlockSpec((tf, D_MODEL), lambda i, j: (j, 0))],
        out_specs=pl.BlockSpec((tm, D_MODEL), lambda i, j: (i, 0)),
        out_shape=jax.ShapeDtypeStruct((T, D_MODEL), f32),
        scratch_shapes=[pltpu.VMEM((tm, D_MODEL), bf16)],
        compiler_params=_cparams("parallel", "arbitrary"),
        name="mlp",
    )(x2, mixed, w_out, gain, gain_final, w_up, w_down)


def _query_scale():
    scale = np.ones((1, N_IN), np.float32)
    scale[:, :WIDTH_A] = 1.0 / math.sqrt(HEAD_DIM_A)
    scale[:, COL_QC:] = 1.0 / math.sqrt(MEM_HEAD_DIM)
    return scale


def kernel(x, mem, rel_bias, norm_mix, norm_mem, norm_mlp, norm_final, w_in, w_gate, b_gate, conv_w, conv_b,
           lru_wa, lru_ba, lru_wi, lru_bi, lru_lambda, w_mem_kv, w_o_attn, w_o_lru, w_o_mem, w_out, w_up, w_down):
    B, S, D = x.shape
    T = B * S
    depth = w_in.shape[0]
    assert depth == 1, "the final RMSNorm is fused into the (single) layer's MLP kernel"
    x2 = x.reshape(T, D)
    mem2 = mem.reshape(B * N_MEM, D)
    for l in range(depth):
        w_qkv = (w_in[l] * _query_scale()).astype(bf16)
        h, rest, w_up_bf, w_down_bf = _rest_proj(x2, norm_mix[l].reshape(1, D), w_qkv, w_up[l], w_down[l])
        proj3 = rest.reshape(B, S, REST_W)

        side_casts = ((w_gate[l],), (w_out[l], w_o_lru[l]), (w_o_attn[l], w_o_mem[l], w_mem_kv[l]))
        casted = []
        attn = []
        for g in range(len(ATTN_GROUPS)):
            d = ATTN_GROUPS[g][1]
            qkv, *bf_copies = _qkv_proj(h, w_qkv, g, B, S, cast=side_casts[g])
            casted.append(bf_copies)
            o, lse = _attn_group(qkv.reshape(B * d, S // d, QKV_W), rel_bias, g)
            if g == 0:
                attn.append((o.reshape(T, GROUP_WIDTH), lse.reshape(T, LSE_LANES)))
            else:
                attn.append((o.reshape(B, d, S // d, GROUP_WIDTH), lse.reshape(B, d, S // d, LSE_LANES)))

        w_gates = (0.5 * jnp.concatenate([lru_wa[l, 0], lru_wi[l, 0], lru_wa[l, 1], lru_wi[l, 1]], axis=-1)
                   ).astype(bf16)
        b_gates = 0.5 * jnp.concatenate([lru_ba[l, 0], lru_bi[l, 0], lru_ba[l, 1], lru_bi[l, 1]], axis=-1)
        y_lru = _lru(proj3, conv_w[l], conv_b[l].reshape(1, LRU_WIDTH), w_gates,
                     b_gates.reshape(LRU_BLOCKS, 1, 4 * LRU_BW), lru_lambda[l])

        (w_gate_bf,), (w_out_bf, w_o_lru_bf), (w_o_attn_bf, w_o_mem_bf, w_mem_kv_bf) = casted
        kv = _mem_kv(mem2, norm_mem[l].reshape(1, D), w_mem_kv_bf)
        y_c = _xattn(proj3, kv.reshape(B, N_MEM, 2 * MEM_WIDTH))

        y_a = _combine([a[0] for a in attn], [a[1] for a in attn], S)
        mixed = _gate_mix(h, y_a, y_lru.reshape(T, LRU_WIDTH), y_c.reshape(T, MEM_WIDTH),
                          w_gate_bf, b_gate[l].reshape(1, 3 * D), w_o_attn_bf, w_o_lru_bf, w_o_mem_bf)
        x2 = _mlp(x2, mixed, w_out_bf, norm_mlp[l].reshape(1, D), norm_final.reshape(1, D),
                  w_up_bf, w_down_bf)
    return x2.reshape(B, S, D)
```

```python
import functools
import math

import jax
import jax.numpy as jnp
import numpy as np
from jax import lax
from jax.experimental import pallas as pl
from jax.experimental.pallas import tpu as pltpu

D_MODEL = 2048
HEAD_DIM_A = 128
ATTN_GROUPS = ((128, 1), (512, 4), (2048, 16))
HEADS_PER_GROUP = 4
GROUP_WIDTH = HEADS_PER_GROUP * HEAD_DIM_A
WIDTH_A = len(ATTN_GROUPS) * GROUP_WIDTH
ATTN_RADIUS = 64
N_BUCKETS = 32
MAX_DISTANCE = 1024
LRU_WIDTH = 1536
LRU_BLOCKS = 12
LRU_BW = 128
LRU_C = 8.0
N_MEM = 256
MEM_HEADS = 4
MEM_HEAD_DIM = 256
MEM_WIDTH = MEM_HEADS * MEM_HEAD_DIM
D_FF = 4 * D_MODEL
EPS = 1e-6
N_IN = 3 * WIDTH_A + 2 * LRU_WIDTH + MEM_WIDTH
COL_K = WIDTH_A
COL_V = 2 * WIDTH_A
COL_XB = 3 * WIDTH_A
COL_YB = 3 * WIDTH_A + LRU_WIDTH
COL_QC = 3 * WIDTH_A + 2 * LRU_WIDTH
NEG_INF = -1e30

ATTN_ROWS_PER_STEP = 1024
SUB_Q = 128
SUB_K = SUB_Q + 2 * ATTN_RADIUS
LSE_LANES = 128
LSE_REP = LSE_LANES // HEADS_PER_GROUP

VMEM_LIMIT = 56 * 1024 * 1024

f32 = jnp.float32
bf16 = jnp.bfloat16


def _cparams(*sem):
    return pltpu.CompilerParams(dimension_semantics=sem, vmem_limit_bytes=VMEM_LIMIT)


def _rms(x, gain):
    return x * lax.rsqrt(jnp.mean(x * x, axis=-1, keepdims=True) + EPS) * gain


QKV_W = 3 * GROUP_WIDTH
REST_W = 2 * LRU_WIDTH + MEM_WIDTH
REST_YB = LRU_WIDTH
REST_QC = 2 * LRU_WIDTH
PROJ_TN = GROUP_WIDTH
LANES = 128
SLABS = PROJ_TN // LANES
PROJ_ROW_BLOCKS = 2
DEINTERLEAVE_STEP = 4
N_STAGE = 2


def _qkv_kernel(h_ref, wq_ref, wk_ref, wv_ref, *refs, d, tm, n_cast):
    cast_in, o_ref = refs[:n_cast], refs[n_cast]
    cast_out = refs[n_cast + 1:2 * n_cast + 1]
    res_scr, tmp_scr = refs[2 * n_cast + 1:]
    for src_ref, dst_ref in zip(cast_in, cast_out):
        dst_ref[...] = src_ref[...].astype(bf16)
    mb = tm // PROJ_ROW_BLOCKS
    n = 0
    for k in range(PROJ_ROW_BLOCKS):
        hk = h_ref[k * mb:(k + 1) * mb, :]
        for t, w_ref in enumerate((wq_ref, wk_ref, wv_ref)):
            res = jnp.dot(hk, w_ref[...], preferred_element_type=f32)
            col = t * PROJ_TN
            if d == 1:
                o_ref[0, k * mb:(k + 1) * mb, col:col + PROJ_TN] = res.astype(bf16)
                continue
            buf = n % N_STAGE
            n += 1
            for c in range(SLABS):
                res_scr[buf, c] = res[:, c * LANES:(c + 1) * LANES]
            src, step = res_scr, d
            if d == DEINTERLEAVE_STEP ** 2:
                step = DEINTERLEAVE_STEP
                for r in range(step):
                    for c in range(SLABS):
                        tmp_scr[buf, c, r * (mb // step):(r + 1) * (mb // step), :] = (
                            res_scr[buf, c, pl.ds(r, mb // step, stride=step), :])
                src = tmp_scr
            for r in range(d):
                start = r if src is res_scr else (r % step) * (mb // step) + r // step
                for c in range(SLABS):
                    o_ref[0, r, k * (mb // d):(k + 1) * (mb // d), col + c * LANES:col + (c + 1) * LANES] = (
                        src[buf, c, pl.ds(start, mb // d, stride=step), :].astype(bf16))


def _qkv_proj(h, w_qkv, g, batch, seq, cast=(), tm=1024):
    T = h.shape[0]
    d = ATTN_GROUPS[g][1]
    nt = seq // tm
    n_groups = len(ATTN_GROUPS)
    mb = tm // PROJ_ROW_BLOCKS
    steps = T // tm

    def w_spec(which):
        return pl.BlockSpec((D_MODEL, PROJ_TN), lambda i: (0, which * n_groups + g))

    cast_specs = [pl.BlockSpec((w.shape[0] // steps, w.shape[1]), lambda i: (i, 0)) for w in cast]

    if d == 1:
        out_spec = pl.BlockSpec((1, tm, QKV_W), lambda i: (i // nt, i % nt, 0))
        out_shape = jax.ShapeDtypeStruct((batch, seq, QKV_W), bf16)
    else:
        out_spec = pl.BlockSpec((1, d, tm // d, QKV_W), lambda i: (i // nt, 0, i % nt, 0))
        out_shape = jax.ShapeDtypeStruct((batch, d, seq // d, QKV_W), bf16)
    return pl.pallas_call(
        functools.partial(_qkv_kernel, d=d, tm=tm, n_cast=len(cast)),
        grid=(steps,),
        in_specs=[pl.BlockSpec((tm, D_MODEL), lambda i: (i, 0)), w_spec(0), w_spec(1), w_spec(2)] + cast_specs,
        out_specs=[out_spec] + cast_specs,
        out_shape=[out_shape] + [jax.ShapeDtypeStruct(w.shape, bf16) for w in cast],
        scratch_shapes=[pltpu.VMEM((N_STAGE, SLABS, mb, LANES), f32)] * 2,
        compiler_params=_cparams("parallel"),
        name=f"qkv_g{g}",
    )(h, w_qkv, w_qkv, w_qkv, *cast)


REST_TILES = REST_W // PROJ_TN
YB_TILES = range(REST_YB // PROJ_TN, REST_QC // PROJ_TN)


def _gelu_tanh(y):
    return y * (0.5 * (1.0 + jnp.tanh(math.sqrt(2.0 / math.pi) * (y + 0.044715 * (y * y * y)))))


def _rest_kernel(x_ref, g_ref, *refs):
    w_refs = refs[:REST_TILES]
    wu_ref, wd_ref, h_ref, o_ref, wu_o_ref, wd_o_ref = refs[REST_TILES:]
    wu_o_ref[...] = wu_ref[...].astype(bf16)
    wd_o_ref[...] = wd_ref[...].astype(bf16)
    mb = x_ref.shape[0] // PROJ_ROW_BLOCKS
    for k in range(PROJ_ROW_BLOCKS):
        rows = slice(k * mb, (k + 1) * mb)
        h = _rms(x_ref[rows, :], g_ref[...]).astype(bf16)
        h_ref[rows, :] = h
        for c, w_ref in enumerate(w_refs):
            res = jnp.dot(h, w_ref[...], preferred_element_type=f32)
            if c in YB_TILES:
                res = _gelu_tanh(res)
            o_ref[rows, c * PROJ_TN:(c + 1) * PROJ_TN] = res.astype(bf16)


def _rest_proj(x2, gain, w_in, w_up, w_down, tm=512):
    T = x2.shape[0]
    steps = T // tm
    first = COL_XB // PROJ_TN
    up_rows, down_rows = D_MODEL // steps, D_FF // steps

    def w_spec(c):
        return pl.BlockSpec((D_MODEL, PROJ_TN), lambda i: (0, first + c), pipeline_mode=pl.Buffered(1))

    return pl.pallas_call(
        _rest_kernel,
        grid=(steps,),
        in_specs=[pl.BlockSpec((tm, D_MODEL), lambda i: (i, 0)),
                  pl.BlockSpec((1, D_MODEL), lambda i: (0, 0))]
        + [w_spec(c) for c in range(REST_TILES)]
        + [pl.BlockSpec((up_rows, D_FF), lambda i: (i, 0)),
           pl.BlockSpec((down_rows, D_MODEL), lambda i: (i, 0))],
        out_specs=[pl.BlockSpec((tm, D_MODEL), lambda i: (i, 0)),
                   pl.BlockSpec((tm, REST_W), lambda i: (i, 0)),
                   pl.BlockSpec((up_rows, D_FF), lambda i: (i, 0)),
                   pl.BlockSpec((down_rows, D_MODEL), lambda i: (i, 0))],
        out_shape=[jax.ShapeDtypeStruct((T, D_MODEL), bf16),
                   jax.ShapeDtypeStruct((T, REST_W), bf16),
                   jax.ShapeDtypeStruct((D_MODEL, D_FF), bf16),
                   jax.ShapeDtypeStruct((D_FF, D_MODEL), bf16)],
        compiler_params=_cparams("arbitrary"),
        name="rest_proj",
    )(x2, gain, *([w_in] * REST_TILES), w_up, w_down)


def _t5_bucket(rel):
    nb = N_BUCKETS // 2
    max_exact = nb // 2
    sign = (rel > 0).astype(np.int32) * nb
    n = np.abs(rel)
    large = max_exact + (np.log(np.maximum(n, 1) / max_exact)
                         / np.log(MAX_DISTANCE / max_exact) * (nb - max_exact)).astype(np.int32)
    large = np.minimum(large, nb - 1)
    return (sign + np.where(n < max_exact, n, large)).astype(np.int32)


def _band_bias(rel_bias_g, dilation):
    qq = np.arange(SUB_Q)[:, None]
    kk = np.arange(SUB_K)[None, :]
    rel = kk - ATTN_RADIUS - qq
    onehot = (_t5_bucket(rel * dilation)[None] == np.arange(N_BUCKETS)[:, None, None]).astype(np.float32)
    bias = jnp.einsum('nh,nqk->hqk', rel_bias_g.astype(f32), onehot, precision=lax.Precision.HIGHEST)
    return bias + np.where(np.abs(rel) <= ATTN_RADIUS, 0.0, NEG_INF).astype(np.float32)[None]


def _attn_kernel(q_ref, kp_ref, km_ref, kn_ref, vp_ref, vm_ref, vn_ref, bias_ref,
                 o_ref, lse_ref, kbuf, vbuf, *, tq, seq, n_seq):
    R = ATTN_RADIUS
    q0 = pl.program_id(1) * tq
    lane = lax.broadcasted_iota(jnp.int32, (SUB_Q, LSE_LANES), 1)
    n_sub = tq // SUB_Q
    for i in range(n_seq):
        kbuf[i, 0:R] = kp_ref[i]
        kbuf[i, R:R + tq] = km_ref[i]
        kbuf[i, R + tq:] = kn_ref[i]
        vbuf[i, 0:R] = vp_ref[i]
        vbuf[i, R:R + tq] = vm_ref[i]
        vbuf[i, R + tq:] = vn_ref[i]
        for s in range(n_sub):
            r0 = s * SUB_Q
            edge = None
            if s == 0 or s == n_sub - 1:
                pos = q0 + (r0 - R) + lax.broadcasted_iota(jnp.int32, (1, SUB_K), 1)
                edge = jnp.where(pos >= 0, jnp.where(pos < seq, 0.0, NEG_INF), NEG_INF)
            lse_tile = None
            for h in range(HEADS_PER_GROUP):
                c0 = h * HEAD_DIM_A
                q = q_ref[i, r0:r0 + SUB_Q, c0:c0 + HEAD_DIM_A]
                k = kbuf[i, r0:r0 + SUB_K, c0:c0 + HEAD_DIM_A]
                v = vbuf[i, r0:r0 + SUB_K, c0:c0 + HEAD_DIM_A]
                logits = lax.dot_general(q, k, (((1,), (1,)), ((), ())), preferred_element_type=f32) + bias_ref[h]
                if edge is not None:
                    logits = logits + edge
                m = jnp.max(logits, axis=-1, keepdims=True)
                p = jnp.exp(logits - m)
                ssum = jnp.sum(p, axis=-1, keepdims=True)
                o = jnp.dot(p.astype(bf16), v, preferred_element_type=f32) * (1.0 / ssum)
                o_ref[i, r0:r0 + SUB_Q, c0:c0 + HEAD_DIM_A] = o.astype(o_ref.dtype)
                lse = m + jnp.log(ssum)
                lse_tile = lse if lse_tile is None else jnp.where(lane >= h * LSE_REP, lse, lse_tile)
            lse_ref[i, r0:r0 + SUB_Q, :] = jnp.broadcast_to(lse_tile, (SUB_Q, LSE_LANES))


def _attn_group(qkv, rel_bias, g):
    _, d = ATTN_GROUPS[g]
    n, L, _ = qkv.shape
    tq = min(ATTN_ROWS_PER_STEP, L)
    ns = ATTN_ROWS_PER_STEP // tq
    R = ATTN_RADIUS
    bias = _band_bias(rel_bias[:, g * HEADS_PER_GROUP:(g + 1) * HEADS_PER_GROUP], d)
    rb = tq // R
    last_rb = L // R - 1

    def main(col, width=GROUP_WIDTH):
        return pl.BlockSpec((ns, tq, width), lambda b, t: (b, t, col))

    def prev(col):
        return pl.BlockSpec((ns, R, GROUP_WIDTH), lambda b, t: (b, jnp.maximum(t * rb - 1, 0), col))

    def nxt(col):
        return pl.BlockSpec((ns, R, GROUP_WIDTH), lambda b, t: (b, jnp.minimum((t + 1) * rb, last_rb), col))

    return pl.pallas_call(
        functools.partial(_attn_kernel, tq=tq, seq=L, n_seq=ns),
        grid=(n // ns, L // tq),
        in_specs=[main(0), prev(1), main(1), nxt(1), prev(2), main(2), nxt(2),
                  pl.BlockSpec((HEADS_PER_GROUP, SUB_Q, SUB_K), lambda b, t: (0, 0, 0))],
        out_specs=[main(0), main(0, LSE_LANES)],
        out_shape=[jax.ShapeDtypeStruct((n, L, GROUP_WIDTH), bf16),
                   jax.ShapeDtypeStruct((n, L, LSE_LANES), f32)],
        scratch_shapes=[pltpu.VMEM((ns, tq + 2 * R, GROUP_WIDTH), bf16),
                        pltpu.VMEM((ns, tq + 2 * R, GROUP_WIDTH), bf16)],
        compiler_params=_cparams("parallel", "arbitrary"),
        name=f"attn_g{g}",
    )(qkv, qkv, qkv, qkv, qkv, qkv, qkv, bias)


LRU_CHUNK = 256
LRU_PAD = 8
LRU_FINISH_ROWS = 512
GATE_BIAS_ROWS = 3
LRU_SEGS = 8
SEG_GAP = 4


def _lru_kernel(xb_ref, yb_ref, cw_ref, cb_ref, w_ref, lam_ref, o_ref,
                xpad, af, bf, ab, bb, htf, ptf, htb, ptb, cf_scr, cb_scr, *, seq):
    R = LRU_CHUNK
    P = LRU_PAD
    seg_len = seq // LRU_SEGS
    pitch = seg_len + SEG_GAP
    chunks_per_seg = seg_len // R
    n_chunks = seq // R
    xpad[0:P] = jnp.zeros((P, LRU_BW), f32)
    xpad[P + seq:] = jnp.zeros((P, LRU_BW), f32)
    xpad[P:P + seq] = xb_ref[0].astype(f32)
    lam = lam_ref[...]
    log_a_unit = -LRU_C * (jnp.maximum(-lam, 0.0) + jnp.log1p(jnp.exp(-jnp.abs(lam))))
    cw = cw_ref[...]
    cb = cb_ref[...]
    row = lax.broadcasted_iota(jnp.int32, (R, LRU_BW), 0)
    lane = lax.broadcasted_iota(jnp.int32, (R, LRU_BW), 1)
    bias_cols = jnp.where(lane < GATE_BIAS_ROWS, 1.0, 0.0).astype(bf16)

    def chunk(ci, first=False, last=False):
        c0 = ci * R
        dst = (ci // chunks_per_seg) * pitch + (ci % chunks_per_seg) * R
        xc = (cw[0:1] * xpad[pl.ds(c0 + (P - 1), R), :] + cw[1:2] * xpad[pl.ds(c0 + P, R), :]
              + cw[2:3] * xpad[pl.ds(c0 + (P + 1), R), :] + cw[3:4] * xpad[pl.ds(c0 + (P + 2), R), :]) + cb
        lhs = jnp.concatenate([xc.astype(bf16), bias_cols], axis=1)
        th = jnp.tanh(jnp.dot(lhs, w_ref[0], preferred_element_type=f32))
        half_xc = 0.5 * xc
        for direction, (a_scr, b_scr) in enumerate(((af, bf), (ab, bb))):
            base = direction * 2 * LRU_BW
            half_log2_a = (0.5 * math.log2(math.e)) * log_a_unit[direction:direction + 1]
            a = jnp.exp2(half_log2_a * th[:, base:base + LRU_BW] + half_log2_a)
            gated_x = half_xc * th[:, base + LRU_BW:base + 2 * LRU_BW] + half_xc
            y = 1.0 - a * a
            mult = y * lax.rsqrt(jnp.maximum(y, 1e-30))
            if direction == 0 and first:
                mult = jnp.where(row == 0, 1.0, mult)
            if direction == 1 and last:
                mult = jnp.where(row == R - 1, 1.0, mult)
            a_scr[pl.ds(dst, R), :] = a
            b_scr[pl.ds(dst, R), :] = mult * gated_x

    for ci in range(n_chunks):
        chunk(ci, first=ci == 0, last=ci == n_chunks - 1)

    def scan(i, carry):
        hf, pf, hb, pb = carry
        rows = pl.ds(i, LRU_SEGS, stride=pitch)
        a = af[rows, :]
        hf = a * hf + bf[rows, :]
        pf = a * pf
        htf[rows, :] = hf
        ptf[rows, :] = pf
        rows = pl.ds(seg_len - 1 - i, LRU_SEGS, stride=pitch)
        a = ab[rows, :]
        hb = a * hb + bb[rows, :]
        pb = a * pb
        htb[rows, :] = hb
        ptb[rows, :] = pb
        return hf, pf, hb, pb

    zero = jnp.zeros((LRU_SEGS, LRU_BW), f32)
    one = jnp.ones((LRU_SEGS, LRU_BW), f32)
    hf, pf, hb, pb = lax.fori_loop(0, seg_len, scan, (zero, one, zero, one), unroll=8)

    c = jnp.zeros((1, LRU_BW), f32)
    cf_scr[0:1] = c
    for j in range(1, LRU_SEGS):
        c = hf[j - 1:j] + pf[j - 1:j] * c
        cf_scr[j:j + 1] = c
    c = jnp.zeros((1, LRU_BW), f32)
    cb_scr[LRU_SEGS - 1:LRU_SEGS] = c
    for j in range(LRU_SEGS - 2, -1, -1):
        c = hb[j + 1:j + 2] + pb[j + 1:j + 2] * c
        cb_scr[j:j + 1] = c

    F = LRU_FINISH_ROWS
    finish_per_seg = seg_len // F

    for ci in range(seq // F):
        c0 = ci * F
        seg = ci // finish_per_seg
        rows = pl.ds(seg * pitch + (ci % finish_per_seg) * F, F)
        h = (htf[rows, :] + ptf[rows, :] * cf_scr[seg:seg + 1, :]
             + htb[rows, :] + ptb[rows, :] * cb_scr[seg:seg + 1, :])
        o_ref[0, c0:c0 + F, :] = (h * yb_ref[0, c0:c0 + F, :].astype(f32)).astype(o_ref.dtype)


def _pack_lru_gates(w, b):
    rows, rest = [], b
    for _ in range(GATE_BIAS_ROWS):
        piece = rest.astype(bf16)
        rows.append(piece[:, None, :])
        rest = rest - piece.astype(f32)
    pad = jnp.zeros((LRU_BLOCKS, LRU_BW - GATE_BIAS_ROWS, 4 * LRU_BW), bf16)
    return jnp.concatenate([w.astype(bf16)] + rows + [pad], axis=1)


def _lru(proj3, conv_w, conv_b, w_gates, lam):
    B, S, _ = proj3.shape
    xb0 = 0
    yb0 = REST_YB // LRU_BW
    return pl.pallas_call(
        functools.partial(_lru_kernel, seq=S),
        grid=(B, LRU_BLOCKS),
        in_specs=[
            pl.BlockSpec((1, S, LRU_BW), lambda b, n: (b, 0, xb0 + n)),
            pl.BlockSpec((1, S, LRU_BW), lambda b, n: (b, 0, yb0 + n)),
            pl.BlockSpec((4, LRU_BW), lambda b, n: (0, n)),
            pl.BlockSpec((1, LRU_BW), lambda b, n: (0, n)),
            pl.BlockSpec((1, 2 * LRU_BW, 4 * LRU_BW), lambda b, n: (n, 0, 0)),
            pl.BlockSpec((2, LRU_BW), lambda b, n: (0, n)),
        ],
        out_specs=pl.BlockSpec((1, S, LRU_BW), lambda b, n: (b, 0, n)),
        out_shape=jax.ShapeDtypeStruct((B, S, LRU_WIDTH), bf16),
        scratch_shapes=([pltpu.VMEM((S + 2 * LRU_PAD, LRU_BW), f32)]
                        + [pltpu.VMEM((S + LRU_SEGS * SEG_GAP, LRU_BW), f32)] * 8
                        + [pltpu.VMEM((LRU_SEGS, LRU_BW), f32)] * 2),
        compiler_params=_cparams("parallel", "parallel"),
        name="lru",
    )(proj3, proj3, conv_w, conv_b, w_gates, lam)


def _mem_kv_kernel(m_ref, g_ref, w_ref, o_ref, h_scr):
    @pl.when(pl.program_id(0) == 0)
    def _():
        h_scr[...] = _rms(m_ref[...], g_ref[...]).astype(bf16)

    o_ref[...] = jnp.dot(h_scr[...], w_ref[...], preferred_element_type=f32).astype(o_ref.dtype)


def _mem_kv(mem2, gain, w, tn=512):
    M = mem2.shape[0]
    N = w.shape[1]
    return pl.pallas_call(
        _mem_kv_kernel,
        grid=(N // tn,),
        in_specs=[pl.BlockSpec((M, D_MODEL), lambda j: (0, 0)),
                  pl.BlockSpec((1, D_MODEL), lambda j: (0, 0)),
                  pl.BlockSpec((D_MODEL, tn), lambda j: (0, j))],
        out_specs=pl.BlockSpec((M, tn), lambda j: (0, j)),
        out_shape=jax.ShapeDtypeStruct((M, N), bf16),
        scratch_shapes=[pltpu.VMEM((M, D_MODEL), bf16)],
        compiler_params=_cparams("arbitrary"),
        name="mem_kv",
    )(mem2, gain, w)


def _xattn_kernel(q0_ref, q1_ref, q2_ref, q3_ref, kv_ref, o_ref):
    for h, q_ref in enumerate((q0_ref, q1_ref, q2_ref, q3_ref)):
        c0 = h * MEM_HEAD_DIM
        k = kv_ref[0, :, c0:c0 + MEM_HEAD_DIM]
        v = kv_ref[0, :, MEM_WIDTH + c0:MEM_WIDTH + c0 + MEM_HEAD_DIM]
        logits = lax.dot_general(q_ref[0], k, (((1,), (1,)), ((), ())), preferred_element_type=f32)
        m = jnp.max(logits, axis=-1, keepdims=True)
        p = jnp.exp(logits - m)
        ssum = jnp.sum(p, axis=-1, keepdims=True)
        o = jnp.dot(p.astype(bf16), v, preferred_element_type=f32) * (1.0 / ssum)
        o_ref[0, :, c0:c0 + MEM_HEAD_DIM] = o.astype(o_ref.dtype)


def _xattn(proj3, kv3, tq=1024):
    B, S, _ = proj3.shape
    qb0 = REST_QC // MEM_HEAD_DIM

    def qspec(h):
        return pl.BlockSpec((1, tq, MEM_HEAD_DIM), lambda b, t: (b, t, qb0 + h))

    return pl.pallas_call(
        _xattn_kernel,
        grid=(B, S // tq),
        in_specs=[qspec(0), qspec(1), qspec(2), qspec(3),
                  pl.BlockSpec((1, N_MEM, 2 * MEM_WIDTH), lambda b, t: (b, 0, 0))],
        out_specs=pl.BlockSpec((1, tq, MEM_WIDTH), lambda b, t: (b, t, 0)),
        out_shape=jax.ShapeDtypeStruct((B, S, MEM_WIDTH), bf16),
        compiler_params=_cparams("parallel", "parallel"),
        name="xattn",
    )(proj3, proj3, proj3, proj3, kv3)


def _combine_kernel(o0_ref, o1_ref, o2_ref, l0_ref, l1_ref, l2_ref, ya_ref,
                    o1_scr, o2_scr, l1_scr, l2_scr, tmp_scr, *, tm):
    step = DEINTERLEAVE_STEP
    for g, o_ref, l_ref, o_scr, l_scr in ((1, o1_ref, l1_ref, o1_scr, l1_scr),
                                          (2, o2_ref, l2_ref, o2_scr, l2_scr)):
        d = ATTN_GROUPS[g][1]
        slabs = [(l_scr, lambda r: l_ref[0, r])]
        slabs += [(o_scr.at[h], lambda r, h=h: o_ref[0, r, :, h * HEAD_DIM_A:(h + 1) * HEAD_DIM_A].astype(f32))
                  for h in range(HEADS_PER_GROUP)]
        for k, (dst, rows_of) in enumerate(slabs):
            if d == step:
                for r in range(d):
                    dst[pl.ds(r, tm // d, stride=d), :] = rows_of(r)
                continue
            tmp = tmp_scr.at[k]
            for r in range(d):
                tmp[pl.ds((r % step) * (tm // step) + r // step, tm // d, stride=step), :] = rows_of(r)
            for lo in range(step):
                dst[pl.ds(lo, tm // step, stride=step), :] = tmp[lo * (tm // step):(lo + 1) * (tm // step), :]
    l0, l1, l2 = l0_ref[...], l1_scr[...], l2_scr[...]
    m = jnp.maximum(jnp.maximum(l0, l1), l2)
    e0, e1, e2 = jnp.exp(l0 - m), jnp.exp(l1 - m), jnp.exp(l2 - m)
    inv = 1.0 / (e0 + e1 + e2)
    for h in range(HEADS_PER_GROUP):
        c0 = h * HEAD_DIM_A
        lane = slice(h * LSE_REP, h * LSE_REP + 1)
        y = ((e0 * inv)[:, lane] * o0_ref[:, c0:c0 + HEAD_DIM_A].astype(f32)
             + (e1 * inv)[:, lane] * o1_scr[h] + (e2 * inv)[:, lane] * o2_scr[h])
        ya_ref[:, c0:c0 + HEAD_DIM_A] = y.astype(bf16)


def _combine(o_groups, lse_groups, seq, tm=512):
    T = o_groups[0].shape[0]
    nt = seq // tm
    d1, d2 = ATTN_GROUPS[1][1], ATTN_GROUPS[2][1]

    def rows(width):
        return pl.BlockSpec((tm, width), lambda i: (i, 0))

    def strided_rows(d, width):
        return pl.BlockSpec((1, d, tm // d, width), lambda i: (i // nt, 0, i % nt, 0))

    return pl.pallas_call(
        functools.partial(_combine_kernel, tm=tm),
        grid=(T // tm,),
        in_specs=[rows(GROUP_WIDTH), strided_rows(d1, GROUP_WIDTH), strided_rows(d2, GROUP_WIDTH),
                  rows(LSE_LANES), strided_rows(d1, LSE_LANES), strided_rows(d2, LSE_LANES)],
        out_specs=rows(GROUP_WIDTH),
        out_shape=jax.ShapeDtypeStruct((T, GROUP_WIDTH), bf16),
        scratch_shapes=[pltpu.VMEM((HEADS_PER_GROUP, tm, HEAD_DIM_A), f32),
                        pltpu.VMEM((HEADS_PER_GROUP, tm, HEAD_DIM_A), f32),
                        pltpu.VMEM((tm, LSE_LANES), f32), pltpu.VMEM((tm, LSE_LANES), f32),
                        pltpu.VMEM((1 + HEADS_PER_GROUP, tm, LANES), f32)],
        compiler_params=_cparams("parallel"),
        name="combine",
    )(*o_groups, *lse_groups)


def _gate_mix_kernel(h_ref, ya_ref, yl_ref, yc_ref, wga_ref, wgb_ref, wgc_ref, bga_ref, bgb_ref, bgc_ref,
                     woa_ref, wol_ref, wom_ref, mix_ref):
    h = h_ref[...]

    def gate(w_ref, b_ref):
        return jax.nn.sigmoid(jnp.dot(h, w_ref[...], preferred_element_type=f32) + b_ref[...])

    mixed = (gate(wga_ref, bga_ref) * jnp.dot(ya_ref[...], woa_ref[...], preferred_element_type=f32)
             + gate(wgb_ref, bgb_ref) * jnp.dot(yl_ref[...], wol_ref[...], preferred_element_type=f32)
             + gate(wgc_ref, bgc_ref) * jnp.dot(yc_ref[...], wom_ref[...], preferred_element_type=f32))
    mix_ref[...] = mixed.astype(mix_ref.dtype)


def _gate_mix(h, y_a, y_lru, y_c, w_gate, b_gate, w_o_attn, w_o_lru, w_o_mem, tm=512, tn=512):
    T = h.shape[0]
    nj = D_MODEL // tn

    def rows(width):
        return pl.BlockSpec((tm, width), lambda j, i: (i, 0))

    def gate_w(k):
        return pl.BlockSpec((D_MODEL, tn), lambda j, i: (0, k * nj + j))

    def gate_b(k):
        return pl.BlockSpec((1, tn), lambda j, i: (0, k * nj + j))

    def cols(width):
        return pl.BlockSpec((width, tn), lambda j, i: (0, j))

    return pl.pallas_call(
        _gate_mix_kernel,
        grid=(nj, T // tm),
        in_specs=[rows(D_MODEL), rows(GROUP_WIDTH), rows(LRU_WIDTH), rows(MEM_WIDTH),
                  gate_w(0), gate_w(1), gate_w(2), gate_b(0), gate_b(1), gate_b(2),
                  cols(GROUP_WIDTH), cols(LRU_WIDTH), cols(MEM_WIDTH)],
        out_specs=pl.BlockSpec((tm, tn), lambda j, i: (i, j)),
        out_shape=jax.ShapeDtypeStruct((T, D_MODEL), bf16),
        compiler_params=_cparams("arbitrary", "arbitrary"),
        name="gate_mix",
    )(h, y_a, y_lru, y_c, w_gate, w_gate, w_gate, b_gate, b_gate, b_gate, w_o_attn, w_o_lru, w_o_mem)


MLP_ROW_BLOCKS = 2


def _mlp_kernel(x_ref, mix_ref, wo_ref, g_ref, gf_ref, wu_ref, wd_ref, out_ref, h_scr, *, tm):
    j = pl.program_id(1)
    last = pl.num_programs(1) - 1
    mb = tm // MLP_ROW_BLOCKS

    def step(first, final):
        for k in range(MLP_ROW_BLOCKS):
            rows = slice(k * mb, (k + 1) * mb)
            if first:
                x = x_ref[rows, :] + jnp.dot(mix_ref[rows, :], wo_ref[...], preferred_element_type=f32)
                h_scr[rows, :] = _rms(x, g_ref[...]).astype(bf16)
                out_ref[rows, :] = x
            u = jnp.maximum(jnp.dot(h_scr[rows, :], wu_ref[...], preferred_element_type=f32), 0.0)
            acc = out_ref[rows, :] + jnp.dot((u * u).astype(bf16), wd_ref[...], preferred_element_type=f32)
            out_ref[rows, :] = _rms(acc, gf_ref[...]) if final else acc

    pl.when(j == 0)(lambda: step(True, False))
    pl.when((j > 0) & (j < last))(lambda: step(False, False))
    pl.when(j == last)(lambda: step(False, True))


def _mlp(x2, mixed, w_out, gain, gain_final, w_up, w_down, tm=512, tf=1024):
    T = x2.shape[0]
    assert D_FF // tf >= 2, "first and last hidden-chunk steps are compiled as separate variants"
    return pl.pallas_call(
        functools.partial(_mlp_kernel, tm=tm),
        grid=(T // tm, D_FF // tf),
        in_specs=[pl.BlockSpec((tm, D_MODEL), lambda i, j: (i, 0)),
                  pl.BlockSpec((tm, D_MODEL), lambda i, j: (i, 0)),
                  pl.BlockSpec((D_MODEL, D_MODEL), lambda i, j: (0, 0)),
                  pl.BlockSpec((1, D_MODEL), lambda i, j: (0, 0)),
                  pl.BlockSpec((1, D_MODEL), lambda i, j: (0, 0)),
                  pl.BlockSpec((D_MODEL, tf), lambda i, j: (0, j)),
                  pl.BlockSpec((tf, D_MODEL), lambda i, j: (j, 0))],
        out_specs=pl.BlockSpec((tm, D_MODEL), lambda i, j: (i, 0)),
        out_shape=jax.ShapeDtypeStruct((T, D_MODEL), f32),
        scratch_shapes=[pltpu.VMEM((tm, D_MODEL), bf16)],
        compiler_params=_cparams("parallel", "arbitrary"),
        name="mlp",
    )(x2, mixed, w_out, gain, gain_final, w_up, w_down)


def _query_scale():
    scale = np.ones((1, N_IN), np.float32)
    scale[:, :WIDTH_A] = 1.0 / math.sqrt(HEAD_DIM_A)
    scale[:, COL_QC:] = 1.0 / math.sqrt(MEM_HEAD_DIM)
    return scale


def kernel(x, mem, rel_bias, norm_mix, norm_mem, norm_mlp, norm_final, w_in, w_gate, b_gate, conv_w, conv_b,
           lru_wa, lru_ba, lru_wi, lru_bi, lru_lambda, w_mem_kv, w_o_attn, w_o_lru, w_o_mem, w_out, w_up, w_down):
    B, S, D = x.shape
    T = B * S
    depth = w_in.shape[0]
    assert depth == 1, "the final RMSNorm is fused into the (single) layer's MLP kernel"
    x2 = x.reshape(T, D)
    mem2 = mem.reshape(B * N_MEM, D)
    for l in range(depth):
        w_qkv = (w_in[l] * _query_scale()).astype(bf16)
        h, rest, w_up_bf, w_down_bf = _rest_proj(x2, norm_mix[l].reshape(1, D), w_qkv, w_up[l], w_down[l])
        proj3 = rest.reshape(B, S, REST_W)

        side_casts = ((w_gate[l],), (w_out[l], w_o_lru[l]), (w_o_attn[l], w_o_mem[l], w_mem_kv[l]))
        casted = []
        attn = []
        for g in range(len(ATTN_GROUPS)):
            d = ATTN_GROUPS[g][1]
            qkv, *bf_copies = _qkv_proj(h, w_qkv, g, B, S, cast=side_casts[g])
            casted.append(bf_copies)
            o, lse = _attn_group(qkv.reshape(B * d, S // d, QKV_W), rel_bias, g)
            if g == 0:
                attn.append((o.reshape(T, GROUP_WIDTH), lse.reshape(T, LSE_LANES)))
            else:
                attn.append((o.reshape(B, d, S // d, GROUP_WIDTH), lse.reshape(B, d, S // d, LSE_LANES)))

        w_gates = 0.5 * jnp.concatenate([lru_wa[l, 0], lru_wi[l, 0], lru_wa[l, 1], lru_wi[l, 1]], axis=-1)
        b_gates = 0.5 * jnp.concatenate([lru_ba[l, 0], lru_bi[l, 0], lru_ba[l, 1], lru_bi[l, 1]], axis=-1)
        y_lru = _lru(proj3, conv_w[l], conv_b[l].reshape(1, LRU_WIDTH), _pack_lru_gates(w_gates, b_gates),
                     lru_lambda[l])

        (w_gate_bf,), (w_out_bf, w_o_lru_bf), (w_o_attn_bf, w_o_mem_bf, w_mem_kv_bf) = casted
        kv = _mem_kv(mem2, norm_mem[l].reshape(1, D), w_mem_kv_bf)
        y_c = _xattn(proj3, kv.reshape(B, N_MEM, 2 * MEM_WIDTH))

        y_a = _combine([a[0] for a in attn], [a[1] for a in attn], S)
        mixed = _gate_mix(h, y_a, y_lru.reshape(T, LRU_WIDTH), y_c.reshape(T, MEM_WIDTH),
                          w_gate_bf, b_gate[l].reshape(1, 3 * D), w_o_attn_bf, w_o_lru_bf, w_o_mem_bf)
        x2 = _mlp(x2, mixed, w_out_bf, norm_mlp[l].reshape(1, D), norm_final.reshape(1, D),
                  w_up_bf, w_down_bf)
    return x2.reshape(B, S, D)
```

```python
import functools
import math

import jax
import jax.numpy as jnp
import numpy as np
from jax import lax
from jax.experimental import pallas as pl
from jax.experimental.pallas import tpu as pltpu

D_MODEL = 2048
HEAD_DIM_A = 128
ATTN_GROUPS = ((128, 1), (512, 4), (2048, 16))
HEADS_PER_GROUP = 4
GROUP_WIDTH = HEADS_PER_GROUP * HEAD_DIM_A
WIDTH_A = len(ATTN_GROUPS) * GROUP_WIDTH
ATTN_RADIUS = 64
N_BUCKETS = 32
MAX_DISTANCE = 1024
LRU_WIDTH = 1536
LRU_BLOCKS = 12
LRU_BW = 128
LRU_C = 8.0
N_MEM = 256
MEM_HEADS = 4
MEM_HEAD_DIM = 256
MEM_WIDTH = MEM_HEADS * MEM_HEAD_DIM
D_FF = 4 * D_MODEL
EPS = 1e-6
N_IN = 3 * WIDTH_A + 2 * LRU_WIDTH + MEM_WIDTH
COL_K = WIDTH_A
COL_V = 2 * WIDTH_A
COL_XB = 3 * WIDTH_A
COL_YB = 3 * WIDTH_A + LRU_WIDTH
COL_QC = 3 * WIDTH_A + 2 * LRU_WIDTH
NEG_INF = -1e30

ATTN_ROWS_PER_STEP = 1024
SUB_Q = 128
SUB_K = SUB_Q + 2 * ATTN_RADIUS
LSE_LANES = 128
LSE_REP = LSE_LANES // HEADS_PER_GROUP

VMEM_LIMIT = 56 * 1024 * 1024

f32 = jnp.float32
bf16 = jnp.bfloat16


def _cparams(*sem):
    return pltpu.CompilerParams(dimension_semantics=sem, vmem_limit_bytes=VMEM_LIMIT)


def _rms(x, gain):
    return x * lax.rsqrt(jnp.mean(x * x, axis=-1, keepdims=True) + EPS) * gain


QKV_W = 3 * GROUP_WIDTH
REST_W = 2 * LRU_WIDTH + MEM_WIDTH
REST_YB = LRU_WIDTH
REST_QC = 2 * LRU_WIDTH
PROJ_TN = GROUP_WIDTH
LANES = 128
SLABS = PROJ_TN // LANES
PROJ_ROW_BLOCKS = 2
DEINTERLEAVE_STEP = 4
N_STAGE = 2


def _qkv_kernel(h_ref, wq_ref, wk_ref, wv_ref, *refs, d, tm, n_cast):
    cast_in, o_ref = refs[:n_cast], refs[n_cast]
    cast_out = refs[n_cast + 1:2 * n_cast + 1]
    res_scr, tmp_scr = refs[2 * n_cast + 1:]
    for src_ref, dst_ref in zip(cast_in, cast_out):
        dst_ref[...] = src_ref[...].astype(bf16)
    mb = tm // PROJ_ROW_BLOCKS
    n = 0
    for k in range(PROJ_ROW_BLOCKS):
        hk = h_ref[k * mb:(k + 1) * mb, :]
        for t, w_ref in enumerate((wq_ref, wk_ref, wv_ref)):
            res = jnp.dot(hk, w_ref[...], preferred_element_type=f32)
            col = t * PROJ_TN
            if d == 1:
                o_ref[0, k * mb:(k + 1) * mb, col:col + PROJ_TN] = res.astype(bf16)
                continue
            buf = n % N_STAGE
            n += 1
            for c in range(SLABS):
                res_scr[buf, c] = res[:, c * LANES:(c + 1) * LANES]
            src, step = res_scr, d
            if d == DEINTERLEAVE_STEP ** 2:
                step = DEINTERLEAVE_STEP
                for r in range(step):
                    for c in range(SLABS):
                        tmp_scr[buf, c, r * (mb // step):(r + 1) * (mb // step), :] = (
                            res_scr[buf, c, pl.ds(r, mb // step, stride=step), :])
                src = tmp_scr
            for r in range(d):
                start = r if src is res_scr else (r % step) * (mb // step) + r // step
                for c in range(SLABS):
                    o_ref[0, r, k * (mb // d):(k + 1) * (mb // d), col + c * LANES:col + (c + 1) * LANES] = (
                        src[buf, c, pl.ds(start, mb // d, stride=step), :].astype(bf16))


def _qkv_proj(h, w_qkv, g, batch, seq, cast=(), tm=1024):
    T = h.shape[0]
    d = ATTN_GROUPS[g][1]
    nt = seq // tm
    n_groups = len(ATTN_GROUPS)
    mb = tm // PROJ_ROW_BLOCKS
    steps = T // tm

    def w_spec(which):
        return pl.BlockSpec((D_MODEL, PROJ_TN), lambda i: (0, which * n_groups + g))

    cast_specs = [pl.BlockSpec((w.shape[0] // steps, w.shape[1]), lambda i: (i, 0)) for w in cast]

    if d == 1:
        out_spec = pl.BlockSpec((1, tm, QKV_W), lambda i: (i // nt, i % nt, 0))
        out_shape = jax.ShapeDtypeStruct((batch, seq, QKV_W), bf16)
    else:
        out_spec = pl.BlockSpec((1, d, tm // d, QKV_W), lambda i: (i // nt, 0, i % nt, 0))
        out_shape = jax.ShapeDtypeStruct((batch, d, seq // d, QKV_W), bf16)
    return pl.pallas_call(
        functools.partial(_qkv_kernel, d=d, tm=tm, n_cast=len(cast)),
        grid=(steps,),
        in_specs=[pl.BlockSpec((tm, D_MODEL), lambda i: (i, 0)), w_spec(0), w_spec(1), w_spec(2)] + cast_specs,
        out_specs=[out_spec] + cast_specs,
        out_shape=[out_shape] + [jax.ShapeDtypeStruct(w.shape, bf16) for w in cast],
        scratch_shapes=[pltpu.VMEM((N_STAGE, SLABS, mb, LANES), f32)] * 2,
        compiler_params=_cparams("parallel"),
        name=f"qkv_g{g}",
    )(h, w_qkv, w_qkv, w_qkv, *cast)


REST_TILES = REST_W // PROJ_TN
YB_TILES = range(REST_YB // PROJ_TN, REST_QC // PROJ_TN)


def _gelu_tanh(y):
    return y * (0.5 * (1.0 + jnp.tanh(math.sqrt(2.0 / math.pi) * (y + 0.044715 * (y * y * y)))))


def _rest_kernel(x_ref, g_ref, *refs):
    w_refs = refs[:REST_TILES]
    wu_ref, wd_ref, h_ref, o_ref, wu_o_ref, wd_o_ref = refs[REST_TILES:]
    wu_o_ref[...] = wu_ref[...].astype(bf16)
    wd_o_ref[...] = wd_ref[...].astype(bf16)
    mb = x_ref.shape[0] // PROJ_ROW_BLOCKS
    for k in range(PROJ_ROW_BLOCKS):
        rows = slice(k * mb, (k + 1) * mb)
        h = _rms(x_ref[rows, :], g_ref[...]).astype(bf16)
        h_ref[rows, :] = h
        for c, w_ref in enumerate(w_refs):
            res = jnp.dot(h, w_ref[...], preferred_element_type=f32)
            if c in YB_TILES:
                res = _gelu_tanh(res)
            o_ref[rows, c * PROJ_TN:(c + 1) * PROJ_TN] = res.astype(bf16)


def _rest_proj(x2, gain, w_in, w_up, w_down, tm=512):
    T = x2.shape[0]
    steps = T // tm
    first = COL_XB // PROJ_TN
    up_rows, down_rows = D_MODEL // steps, D_FF // steps

    def w_spec(c):
        return pl.BlockSpec((D_MODEL, PROJ_TN), lambda i: (0, first + c), pipeline_mode=pl.Buffered(1))

    return pl.pallas_call(
        _rest_kernel,
        grid=(steps,),
        in_specs=[pl.BlockSpec((tm, D_MODEL), lambda i: (i, 0)),
                  pl.BlockSpec((1, D_MODEL), lambda i: (0, 0))]
        + [w_spec(c) for c in range(REST_TILES)]
        + [pl.BlockSpec((up_rows, D_FF), lambda i: (i, 0)),
           pl.BlockSpec((down_rows, D_MODEL), lambda i: (i, 0))],
        out_specs=[pl.BlockSpec((tm, D_MODEL), lambda i: (i, 0)),
                   pl.BlockSpec((tm, REST_W), lambda i: (i, 0)),
                   pl.BlockSpec((up_rows, D_FF), lambda i: (i, 0)),
                   pl.BlockSpec((down_rows, D_MODEL), lambda i: (i, 0))],
        out_shape=[jax.ShapeDtypeStruct((T, D_MODEL), bf16),
                   jax.ShapeDtypeStruct((T, REST_W), bf16),
                   jax.ShapeDtypeStruct((D_MODEL, D_FF), bf16),
                   jax.ShapeDtypeStruct((D_FF, D_MODEL), bf16)],
        compiler_params=_cparams("arbitrary"),
        name="rest_proj",
    )(x2, gain, *([w_in] * REST_TILES), w_up, w_down)


def _t5_bucket(rel):
    nb = N_BUCKETS // 2
    max_exact = nb // 2
    sign = (rel > 0).astype(np.int32) * nb
    n = np.abs(rel)
    large = max_exact + (np.log(np.maximum(n, 1) / max_exact)
                         / np.log(MAX_DISTANCE / max_exact) * (nb - max_exact)).astype(np.int32)
    large = np.minimum(large, nb - 1)
    return (sign + np.where(n < max_exact, n, large)).astype(np.int32)


def _band_bias(rel_bias_g, dilation):
    qq = np.arange(SUB_Q)[:, None]
    kk = np.arange(SUB_K)[None, :]
    rel = kk - ATTN_RADIUS - qq
    onehot = (_t5_bucket(rel * dilation)[None] == np.arange(N_BUCKETS)[:, None, None]).astype(np.float32)
    bias = jnp.einsum('nh,nqk->hqk', rel_bias_g.astype(f32), onehot, precision=lax.Precision.HIGHEST)
    return bias + np.where(np.abs(rel) <= ATTN_RADIUS, 0.0, NEG_INF).astype(np.float32)[None]


def _attn_kernel(q_ref, kp_ref, km_ref, kn_ref, vp_ref, vm_ref, vn_ref, bias_ref,
                 o_ref, lse_ref, kbuf, vbuf, *, tq, seq, n_seq):
    R = ATTN_RADIUS
    q0 = pl.program_id(1) * tq
    lane = lax.broadcasted_iota(jnp.int32, (SUB_Q, LSE_LANES), 1)
    n_sub = tq // SUB_Q
    for i in range(n_seq):
        kbuf[i, 0:R] = kp_ref[i]
        kbuf[i, R:R + tq] = km_ref[i]
        kbuf[i, R + tq:] = kn_ref[i]
        vbuf[i, 0:R] = vp_ref[i]
        vbuf[i, R:R + tq] = vm_ref[i]
        vbuf[i, R + tq:] = vn_ref[i]
        for s in range(n_sub):
            r0 = s * SUB_Q
            edge = None
            if s == 0 or s == n_sub - 1:
                pos = q0 + (r0 - R) + lax.broadcasted_iota(jnp.int32, (1, SUB_K), 1)
                edge = jnp.where(pos >= 0, jnp.where(pos < seq, 0.0, NEG_INF), NEG_INF)
            lse_tile = None
            for h in range(HEADS_PER_GROUP):
                c0 = h * HEAD_DIM_A
                q = q_ref[i, r0:r0 + SUB_Q, c0:c0 + HEAD_DIM_A]
                k = kbuf[i, r0:r0 + SUB_K, c0:c0 + HEAD_DIM_A]
                v = vbuf[i, r0:r0 + SUB_K, c0:c0 + HEAD_DIM_A]
                logits = lax.dot_general(q, k, (((1,), (1,)), ((), ())), preferred_element_type=f32) + bias_ref[h]
                if edge is not None:
                    logits = logits + edge
                m = jnp.max(logits, axis=-1, keepdims=True)
                p = jnp.exp(logits - m)
                ssum = jnp.sum(p, axis=-1, keepdims=True)
                o = jnp.dot(p.astype(bf16), v, preferred_element_type=f32) * (1.0 / ssum)
                o_ref[i, r0:r0 + SUB_Q, c0:c0 + HEAD_DIM_A] = o.astype(o_ref.dtype)
                lse = m + jnp.log(ssum)
                lse_tile = lse if lse_tile is None else jnp.where(lane >= h * LSE_REP, lse, lse_tile)
            lse_ref[i, r0:r0 + SUB_Q, :] = jnp.broadcast_to(lse_tile, (SUB_Q, LSE_LANES))


def _attn_group(qkv, rel_bias, g):
    _, d = ATTN_GROUPS[g]
    n, L, _ = qkv.shape
    tq = min(ATTN_ROWS_PER_STEP, L)
    ns = ATTN_ROWS_PER_STEP // tq
    R = ATTN_RADIUS
    bias = _band_bias(rel_bias[:, g * HEADS_PER_GROUP:(g + 1) * HEADS_PER_GROUP], d)
    rb = tq // R
    last_rb = L // R - 1

    def main(col, width=GROUP_WIDTH):
        return pl.BlockSpec((ns, tq, width), lambda b, t: (b, t, col))

    def prev(col):
        return pl.BlockSpec((ns, R, GROUP_WIDTH), lambda b, t: (b, jnp.maximum(t * rb - 1, 0), col))

    def nxt(col):
        return pl.BlockSpec((ns, R, GROUP_WIDTH), lambda b, t: (b, jnp.minimum((t + 1) * rb, last_rb), col))

    return pl.pallas_call(
        functools.partial(_attn_kernel, tq=tq, seq=L, n_seq=ns),
        grid=(n // ns, L // tq),
        in_specs=[main(0), prev(1), main(1), nxt(1), prev(2), main(2), nxt(2),
                  pl.BlockSpec((HEADS_PER_GROUP, SUB_Q, SUB_K), lambda b, t: (0, 0, 0))],
        out_specs=[main(0), main(0, LSE_LANES)],
        out_shape=[jax.ShapeDtypeStruct((n, L, GROUP_WIDTH), bf16),
                   jax.ShapeDtypeStruct((n, L, LSE_LANES), f32)],
        scratch_shapes=[pltpu.VMEM((ns, tq + 2 * R, GROUP_WIDTH), bf16),
                        pltpu.VMEM((ns, tq + 2 * R, GROUP_WIDTH), bf16)],
        compiler_params=_cparams("parallel", "arbitrary"),
        name=f"attn_g{g}",
    )(qkv, qkv, qkv, qkv, qkv, qkv, qkv, bias)


LRU_CHUNK = 256
LRU_PAD = 8
LRU_FINISH_ROWS = 512
GATE_BIAS_ROWS = 3
LRU_SEGS = 8
SEG_GAP = 4


def _lru_kernel(xb_ref, yb_ref, cw_ref, cb_ref, w_ref, lam_ref, o_ref,
                xpad, af, bf, ab, bb, htf, ptf, htb, ptb, cf_scr, cb_scr, *, seq):
    R = LRU_CHUNK
    P = LRU_PAD
    seg_len = seq // LRU_SEGS
    pitch = seg_len + SEG_GAP
    chunks_per_seg = seg_len // R
    n_chunks = seq // R
    xpad[0:P] = jnp.zeros((P, LRU_BW), f32)
    xpad[P + seq:] = jnp.zeros((P, LRU_BW), f32)
    xpad[P:P + seq] = xb_ref[0].astype(f32)
    lam = lam_ref[...]
    log_a_unit = -LRU_C * (jnp.maximum(-lam, 0.0) + jnp.log1p(jnp.exp(-jnp.abs(lam))))
    cw = cw_ref[...]
    cb = cb_ref[...]
    row = lax.broadcasted_iota(jnp.int32, (R, LRU_BW), 0)
    lane = lax.broadcasted_iota(jnp.int32, (R, LRU_BW), 1)
    bias_cols = jnp.where(lane < GATE_BIAS_ROWS, 1.0, 0.0).astype(bf16)

    def chunk(ci, first=False, last=False):
        c0 = ci * R
        dst = (ci // chunks_per_seg) * pitch + (ci % chunks_per_seg) * R
        xc = (cw[0:1] * xpad[pl.ds(c0 + (P - 1), R), :] + cw[1:2] * xpad[pl.ds(c0 + P, R), :]
              + cw[2:3] * xpad[pl.ds(c0 + (P + 1), R), :] + cw[3:4] * xpad[pl.ds(c0 + (P + 2), R), :]) + cb
        lhs = jnp.concatenate([xc.astype(bf16), bias_cols], axis=1)
        th = jnp.tanh(jnp.dot(lhs, w_ref[0], preferred_element_type=f32))
        half_xc = 0.5 * xc
        for direction, (a_scr, b_scr) in enumerate(((af, bf), (ab, bb))):
            base = direction * 2 * LRU_BW
            half_log2_a = (0.5 * math.log2(math.e)) * log_a_unit[direction:direction + 1]
            a = jnp.exp2(half_log2_a * th[:, base:base + LRU_BW] + half_log2_a)
            gated_x = half_xc * th[:, base + LRU_BW:base + 2 * LRU_BW] + half_xc
            y = 1.0 - a * a
            mult = y * lax.rsqrt(jnp.maximum(y, 1e-30))
            if direction == 0 and first:
                mult = jnp.where(row == 0, 1.0, mult)
            if direction == 1 and last:
                mult = jnp.where(row == R - 1, 1.0, mult)
            a_scr[pl.ds(dst, R), :] = a
            b_scr[pl.ds(dst, R), :] = mult * gated_x

    for ci in range(n_chunks):
        chunk(ci, first=ci == 0, last=ci == n_chunks - 1)

    def scan(i, carry):
        hf, pf, hb, pb = carry
        rows = pl.ds(i, LRU_SEGS, stride=pitch)
        a = af[rows, :]
        hf = a * hf + bf[rows, :]
        pf = a * pf
        htf[rows, :] = hf
        ptf[rows, :] = pf
        rows = pl.ds(seg_len - 1 - i, LRU_SEGS, stride=pitch)
        a = ab[rows, :]
        hb = a * hb + bb[rows, :]
        pb = a * pb
        htb[rows, :] = hb
        ptb[rows, :] = pb
        return hf, pf, hb, pb

    zero = jnp.zeros((LRU_SEGS, LRU_BW), f32)
    one = jnp.ones((LRU_SEGS, LRU_BW), f32)
    hf, pf, hb, pb = lax.fori_loop(0, seg_len, scan, (zero, one, zero, one), unroll=8)

    c = jnp.zeros((1, LRU_BW), f32)
    cf_scr[0:1] = c
    for j in range(1, LRU_SEGS):
        c = hf[j - 1:j] + pf[j - 1:j] * c
        cf_scr[j:j + 1] = c
    c = jnp.zeros((1, LRU_BW), f32)
    cb_scr[LRU_SEGS - 1:LRU_SEGS] = c
    for j in range(LRU_SEGS - 2, -1, -1):
        c = hb[j + 1:j + 2] + pb[j + 1:j + 2] * c
        cb_scr[j:j + 1] = c

    F = LRU_FINISH_ROWS
    finish_per_seg = seg_len // F

    for ci in range(seq // F):
        c0 = ci * F
        seg = ci // finish_per_seg
        rows = pl.ds(seg * pitch + (ci % finish_per_seg) * F, F)
        h = (htf[rows, :] + ptf[rows, :] * cf_scr[seg:seg + 1, :]
             + htb[rows, :] + ptb[rows, :] * cb_scr[seg:seg + 1, :])
        o_ref[0, c0:c0 + F, :] = (h * yb_ref[0, c0:c0 + F, :].astype(f32)).astype(o_ref.dtype)


def _pack_lru_gates(w, b):
    rows, rest = [], b
    for _ in range(GATE_BIAS_ROWS):
        piece = rest.astype(bf16)
        rows.append(piece)
        rest = rest - piece.astype(f32)
    bias_rows = jnp.pad(jnp.stack(rows, axis=1), ((0, 0), (0, LRU_BW - GATE_BIAS_ROWS), (0, 0)))
    return jnp.concatenate([w.astype(bf16), bias_rows], axis=1)


def _lru(proj3, conv_w, conv_b, w_gates, lam):
    B, S, _ = proj3.shape
    xb0 = 0
    yb0 = REST_YB // LRU_BW
    return pl.pallas_call(
        functools.partial(_lru_kernel, seq=S),
        grid=(B, LRU_BLOCKS),
        in_specs=[
            pl.BlockSpec((1, S, LRU_BW), lambda b, n: (b, 0, xb0 + n)),
            pl.BlockSpec((1, S, LRU_BW), lambda b, n: (b, 0, yb0 + n)),
            pl.BlockSpec((4, LRU_BW), lambda b, n: (0, n)),
            pl.BlockSpec((1, LRU_BW), lambda b, n: (0, n)),
            pl.BlockSpec((1, 2 * LRU_BW, 4 * LRU_BW), lambda b, n: (n, 0, 0)),
            pl.BlockSpec((2, LRU_BW), lambda b, n: (0, n)),
        ],
        out_specs=pl.BlockSpec((1, S, LRU_BW), lambda b, n: (b, 0, n)),
        out_shape=jax.ShapeDtypeStruct((B, S, LRU_WIDTH), bf16),
        scratch_shapes=([pltpu.VMEM((S + 2 * LRU_PAD, LRU_BW), f32)]
                        + [pltpu.VMEM((S + LRU_SEGS * SEG_GAP, LRU_BW), f32)] * 8
                        + [pltpu.VMEM((LRU_SEGS, LRU_BW), f32)] * 2),
        compiler_params=_cparams("parallel", "parallel"),
        name="lru",
    )(proj3, proj3, conv_w, conv_b, w_gates, lam)


def _mem_kv_kernel(m_ref, g_ref, w_ref, o_ref, h_scr):
    @pl.when(pl.program_id(0) == 0)
    def _():
        h_scr[...] = _rms(m_ref[...], g_ref[...]).astype(bf16)

    o_ref[...] = jnp.dot(h_scr[...], w_ref[...], preferred_element_type=f32).astype(o_ref.dtype)


def _mem_kv(mem2, gain, w, tn=512):
    M = mem2.shape[0]
    N = w.shape[1]
    return pl.pallas_call(
        _mem_kv_kernel,
        grid=(N // tn,),
        in_specs=[pl.BlockSpec((M, D_MODEL), lambda j: (0, 0)),
                  pl.BlockSpec((1, D_MODEL), lambda j: (0, 0)),
                  pl.BlockSpec((D_MODEL, tn), lambda j: (0, j))],
        out_specs=pl.BlockSpec((M, tn), lambda j: (0, j)),
        out_shape=jax.ShapeDtypeStruct((M, N), bf16),
        scratch_shapes=[pltpu.VMEM((M, D_MODEL), bf16)],
        compiler_params=_cparams("arbitrary"),
        name="mem_kv",
    )(mem2, gain, w)


def _xattn_kernel(q0_ref, q1_ref, q2_ref, q3_ref, kv_ref, o_ref):
    for h, q_ref in enumerate((q0_ref, q1_ref, q2_ref, q3_ref)):
        c0 = h * MEM_HEAD_DIM
        k = kv_ref[0, :, c0:c0 + MEM_HEAD_DIM]
        v = kv_ref[0, :, MEM_WIDTH + c0:MEM_WIDTH + c0 + MEM_HEAD_DIM]
        logits = lax.dot_general(q_ref[0], k, (((1,), (1,)), ((), ())), preferred_element_type=f32)
        m = jnp.max(logits, axis=-1, keepdims=True)
        p = jnp.exp(logits - m)
        ssum = jnp.sum(p, axis=-1, keepdims=True)
        o = jnp.dot(p.astype(bf16), v, preferred_element_type=f32) * (1.0 / ssum)
        o_ref[0, :, c0:c0 + MEM_HEAD_DIM] = o.astype(o_ref.dtype)


def _xattn(proj3, kv3, tq=1024):
    B, S, _ = proj3.shape
    qb0 = REST_QC // MEM_HEAD_DIM

    def qspec(h):
        return pl.BlockSpec((1, tq, MEM_HEAD_DIM), lambda b, t: (b, t, qb0 + h))

    return pl.pallas_call(
        _xattn_kernel,
        grid=(B, S // tq),
        in_specs=[qspec(0), qspec(1), qspec(2), qspec(3),
                  pl.BlockSpec((1, N_MEM, 2 * MEM_WIDTH), lambda b, t: (b, 0, 0))],
        out_specs=pl.BlockSpec((1, tq, MEM_WIDTH), lambda b, t: (b, t, 0)),
        out_shape=jax.ShapeDtypeStruct((B, S, MEM_WIDTH), bf16),
        compiler_params=_cparams("parallel", "parallel"),
        name="xattn",
    )(proj3, proj3, proj3, proj3, kv3)


def _combine_kernel(o0_ref, o1_ref, o2_ref, l0_ref, l1_ref, l2_ref, ya_ref,
                    o1_scr, o2_scr, l1_scr, l2_scr, tmp_scr, *, tm):
    step = DEINTERLEAVE_STEP
    for g, o_ref, l_ref, o_scr, l_scr in ((1, o1_ref, l1_ref, o1_scr, l1_scr),
                                          (2, o2_ref, l2_ref, o2_scr, l2_scr)):
        d = ATTN_GROUPS[g][1]
        slabs = [(l_scr, lambda r: l_ref[0, r])]
        slabs += [(o_scr.at[h], lambda r, h=h: o_ref[0, r, :, h * HEAD_DIM_A:(h + 1) * HEAD_DIM_A].astype(f32))
                  for h in range(HEADS_PER_GROUP)]
        for k, (dst, rows_of) in enumerate(slabs):
            if d == step:
                for r in range(d):
                    dst[pl.ds(r, tm // d, stride=d), :] = rows_of(r)
                continue
            tmp = tmp_scr.at[k]
            for r in range(d):
                tmp[pl.ds((r % step) * (tm // step) + r // step, tm // d, stride=step), :] = rows_of(r)
            for lo in range(step):
                dst[pl.ds(lo, tm // step, stride=step), :] = tmp[lo * (tm // step):(lo + 1) * (tm // step), :]
    l0, l1, l2 = l0_ref[...], l1_scr[...], l2_scr[...]
    m = jnp.maximum(jnp.maximum(l0, l1), l2)
    e0, e1, e2 = jnp.exp(l0 - m), jnp.exp(l1 - m), jnp.exp(l2 - m)
    inv = 1.0 / (e0 + e1 + e2)
    for h in range(HEADS_PER_GROUP):
        c0 = h * HEAD_DIM_A
        lane = slice(h * LSE_REP, h * LSE_REP + 1)
        y = ((e0 * inv)[:, lane] * o0_ref[:, c0:c0 + HEAD_DIM_A].astype(f32)
             + (e1 * inv)[:, lane] * o1_scr[h] + (e2 * inv)[:, lane] * o2_scr[h])
        ya_ref[:, c0:c0 + HEAD_DIM_A] = y.astype(bf16)


def _combine(o_groups, lse_groups, seq, tm=512):
    T = o_groups[0].shape[0]
    nt = seq // tm
    d1, d2 = ATTN_GROUPS[1][1], ATTN_GROUPS[2][1]

    def rows(width):
        return pl.BlockSpec((tm, width), lambda i: (i, 0))

    def strided_rows(d, width):
        return pl.BlockSpec((1, d, tm // d, width), lambda i: (i // nt, 0, i % nt, 0))

    return pl.pallas_call(
        functools.partial(_combine_kernel, tm=tm),
        grid=(T // tm,),
        in_specs=[rows(GROUP_WIDTH), strided_rows(d1, GROUP_WIDTH), strided_rows(d2, GROUP_WIDTH),
                  rows(LSE_LANES), strided_rows(d1, LSE_LANES), strided_rows(d2, LSE_LANES)],
        out_specs=rows(GROUP_WIDTH),
        out_shape=jax.ShapeDtypeStruct((T, GROUP_WIDTH), bf16),
        scratch_shapes=[pltpu.VMEM((HEADS_PER_GROUP, tm, HEAD_DIM_A), f32),
                        pltpu.VMEM((HEADS_PER_GROUP, tm, HEAD_DIM_A), f32),
                        pltpu.VMEM((tm, LSE_LANES), f32), pltpu.VMEM((tm, LSE_LANES), f32),
                        pltpu.VMEM((1 + HEADS_PER_GROUP, tm, LANES), f32)],
        compiler_params=_cparams("parallel"),
        name="combine",
    )(*o_groups, *lse_groups)


def _gate_mix_kernel(h_ref, ya_ref, yl_ref, yc_ref, wga_ref, wgb_ref, wgc_ref, bga_ref, bgb_ref, bgc_ref,
                     woa_ref, wol_ref, wom_ref, mix_ref):
    h = h_ref[...]

    def gate(w_ref, b_ref):
        return jax.nn.sigmoid(jnp.dot(h, w_ref[...], preferred_element_type=f32) + b_ref[...])

    mixed = (gate(wga_ref, bga_ref) * jnp.dot(ya_ref[...], woa_ref[...], preferred_element_type=f32)
             + gate(wgb_ref, bgb_ref) * jnp.dot(yl_ref[...], wol_ref[...], preferred_element_type=f32)
             + gate(wgc_ref, bgc_ref) * jnp.dot(yc_ref[...], wom_ref[...], preferred_element_type=f32))
    mix_ref[...] = mixed.astype(mix_ref.dtype)


def _gate_mix(h, y_a, y_lru, y_c, w_gate, b_gate, w_o_attn, w_o_lru, w_o_mem, tm=512, tn=512):
    T = h.shape[0]
    nj = D_MODEL // tn

    def rows(width):
        return pl.BlockSpec((tm, width), lambda j, i: (i, 0))

    def gate_w(k):
        return pl.BlockSpec((D_MODEL, tn), lambda j, i: (0, k * nj + j))

    def gate_b(k):
        return pl.BlockSpec((1, tn), lambda j, i: (0, k * nj + j))

    def cols(width):
        return pl.BlockSpec((width, tn), lambda j, i: (0, j))

    return pl.pallas_call(
        _gate_mix_kernel,
        grid=(nj, T // tm),
        in_specs=[rows(D_MODEL), rows(GROUP_WIDTH), rows(LRU_WIDTH), rows(MEM_WIDTH),
                  gate_w(0), gate_w(1), gate_w(2), gate_b(0), gate_b(1), gate_b(2),
                  cols(GROUP_WIDTH), cols(LRU_WIDTH), cols(MEM_WIDTH)],
        out_specs=pl.BlockSpec((tm, tn), lambda j, i: (i, j)),
        out_shape=jax.ShapeDtypeStruct((T, D_MODEL), bf16),
        compiler_params=_cparams("arbitrary", "arbitrary"),
        name="gate_mix",
    )(h, y_a, y_lru, y_c, w_gate, w_gate, w_gate, b_gate, b_gate, b_gate, w_o_attn, w_o_lru, w_o_mem)


def _mlp_kernel(x_ref, mix_ref, wo_ref, g_ref, gf_ref, wu_ref, wd_ref, out_ref, h_scr):
    j = pl.program_id(1)

    @pl.when(j == 0)
    def _():
        x = x_ref[...] + jnp.dot(mix_ref[...], wo_ref[...], preferred_element_type=f32)
        h_scr[...] = _rms(x, g_ref[...]).astype(bf16)
        out_ref[...] = x

    u = jnp.maximum(jnp.dot(h_scr[...], wu_ref[...], preferred_element_type=f32), 0.0)
    out_ref[...] += jnp.dot((u * u).astype(bf16), wd_ref[...], preferred_element_type=f32)

    @pl.when(j == pl.num_programs(1) - 1)
    def _():
        out_ref[...] = _rms(out_ref[...], gf_ref[...])


def _mlp(x2, mixed, w_out, gain, gain_final, w_up, w_down, tm=512, tf=1024):
    T = x2.shape[0]
    return pl.pallas_call(
        _mlp_kernel,
        grid=(T // tm, D_FF // tf),
        in_specs=[pl.BlockSpec((tm, D_MODEL), lambda i, j: (i, 0)),
                  pl.BlockSpec((tm, D_MODEL), lambda i, j: (i, 0)),
                  pl.BlockSpec((D_MODEL, D_MODEL), lambda i, j: (0, 0)),
                  pl.BlockSpec((1, D_MODEL), lambda i, j: (0, 0)),
                  pl.BlockSpec((1, D_MODEL), lambda i, j: (0, 0)),
                  pl.BlockSpec((D_MODEL, tf), lambda i, j: (0, j)),
                  pl.BlockSpec((tf, D_MODEL), lambda i, j: (j, 0))],
        out_specs=pl.BlockSpec((tm, D_MODEL), lambda i, j: (i, 0)),
        out_shape=jax.ShapeDtypeStruct((T, D_MODEL), f32),
        scratch_shapes=[pltpu.VMEM((tm, D_MODEL), bf16)],
        compiler_params=_cparams("parallel", "arbitrary"),
        name="mlp",
    )(x2, mixed, w_out, gain, gain_final, w_up, w_down)


def _query_scale():
    scale = np.ones((1, N_IN), np.float32)
    scale[:, :WIDTH_A] = 1.0 / math.sqrt(HEAD_DIM_A)
    scale[:, COL_QC:] = 1.0 / math.sqrt(MEM_HEAD_DIM)
    return scale


def kernel(x, mem, rel_bias, norm_mix, norm_mem, norm_mlp, norm_final, w_in, w_gate, b_gate, conv_w, conv_b,
           lru_wa, lru_ba, lru_wi, lru_bi, lru_lambda, w_mem_kv, w_o_attn, w_o_lru, w_o_mem, w_out, w_up, w_down):
    B, S, D = x.shape
    T = B * S
    depth = w_in.shape[0]
    assert depth == 1, "the final RMSNorm is fused into the (single) layer's MLP kernel"
    x2 = x.reshape(T, D)
    mem2 = mem.reshape(B * N_MEM, D)
    for l in range(depth):
        w_qkv = (w_in[l] * _query_scale()).astype(bf16)
        h, rest, w_up_bf, w_down_bf = _rest_proj(x2, norm_mix[l].reshape(1, D), w_qkv, w_up[l], w_down[l])
        proj3 = rest.reshape(B, S, REST_W)

        side_casts = ((w_gate[l],), (w_out[l], w_o_lru[l]), (w_o_attn[l], w_o_mem[l], w_mem_kv[l]))
        casted = []
        attn = []
        for g in range(len(ATTN_GROUPS)):
            d = ATTN_GROUPS[g][1]
            qkv, *bf_copies = _qkv_proj(h, w_qkv, g, B, S, cast=side_casts[g])
            casted.append(bf_copies)
            o, lse = _attn_group(qkv.reshape(B * d, S // d, QKV_W), rel_bias, g)
            if g == 0:
                attn.append((o.reshape(T, GROUP_WIDTH), lse.reshape(T, LSE_LANES)))
            else:
                attn.append((o.reshape(B, d, S // d, GROUP_WIDTH), lse.reshape(B, d, S // d, LSE_LANES)))

        w_gates = 0.5 * jnp.concatenate([lru_wa[l, 0], lru_wi[l, 0], lru_wa[l, 1], lru_wi[l, 1]], axis=-1)
        b_gates = 0.5 * jnp.concatenate([lru_ba[l, 0], lru_bi[l, 0], lru_ba[l, 1], lru_bi[l, 1]], axis=-1)
        y_lru = _lru(proj3, conv_w[l], conv_b[l].reshape(1, LRU_WIDTH), _pack_lru_gates(w_gates, b_gates),
                     lru_lambda[l])

        (w_gate_bf,), (w_out_bf, w_o_lru_bf), (w_o_attn_bf, w_o_mem_bf, w_mem_kv_bf) = casted
        kv = _mem_kv(mem2, norm_mem[l].reshape(1, D), w_mem_kv_bf)
        y_c = _xattn(proj3, kv.reshape(B, N_MEM, 2 * MEM_WIDTH))

        y_a = _combine([a[0] for a in attn], [a[1] for a in attn], S)
        mixed = _gate_mix(h, y_a, y_lru.reshape(T, LRU_WIDTH), y_c.reshape(T, MEM_WIDTH),
                          w_gate_bf, b_gate[l].reshape(1, 3 * D), w_o_attn_bf, w_o_lru_bf, w_o_mem_bf)
        x2 = _mlp(x2, mixed, w_out_bf, norm_mlp[l].reshape(1, D), norm_final.reshape(1, D),
                  w_up_bf, w_down_bf)
    return x2.reshape(B, S, D)
```

```python
import functools
import math

import jax
import jax.numpy as jnp
import numpy as np
from jax import lax
from jax.experimental import pallas as pl
from jax.experimental.pallas import tpu as pltpu

D_MODEL = 2048
HEAD_DIM_A = 128
ATTN_GROUPS = ((128, 1), (512, 4), (2048, 16))
HEADS_PER_GROUP = 4
GROUP_WIDTH = HEADS_PER_GROUP * HEAD_DIM_A
WIDTH_A = len(ATTN_GROUPS) * GROUP_WIDTH
ATTN_RADIUS = 64
N_BUCKETS = 32
MAX_DISTANCE = 1024
LRU_WIDTH = 1536
LRU_BLOCKS = 12
LRU_BW = 128
LRU_C = 8.0
N_MEM = 256
MEM_HEADS = 4
MEM_HEAD_DIM = 256
MEM_WIDTH = MEM_HEADS * MEM_HEAD_DIM
D_FF = 4 * D_MODEL
EPS = 1e-6
N_IN = 3 * WIDTH_A + 2 * LRU_WIDTH + MEM_WIDTH
COL_K = WIDTH_A
COL_V = 2 * WIDTH_A
COL_XB = 3 * WIDTH_A
COL_YB = 3 * WIDTH_A + LRU_WIDTH
COL_QC = 3 * WIDTH_A + 2 * LRU_WIDTH
NEG_INF = -1e30

ATTN_ROWS_PER_STEP = 1024
SUB_Q = 128
SUB_K = SUB_Q + 2 * ATTN_RADIUS
LSE_LANES = 128
LSE_REP = LSE_LANES // HEADS_PER_GROUP

VMEM_LIMIT = 56 * 1024 * 1024
QKV_VMEM_LIMIT = 60 * 1024 * 1024

f32 = jnp.float32
bf16 = jnp.bfloat16


def _cparams(*sem):
    return pltpu.CompilerParams(dimension_semantics=sem, vmem_limit_bytes=VMEM_LIMIT)


def _rms(x, gain):
    return x * lax.rsqrt(jnp.mean(x * x, axis=-1, keepdims=True) + EPS) * gain


QKV_W = 3 * GROUP_WIDTH
REST_W = 2 * LRU_WIDTH + MEM_WIDTH
REST_YB = LRU_WIDTH
REST_QC = 2 * LRU_WIDTH
PROJ_TN = GROUP_WIDTH
LANES = 128
SLABS = PROJ_TN // LANES
PROJ_ROW_BLOCKS = 2
DEINTERLEAVE_STEP = 4
N_STAGE = 2


N_LRU_IN = 6


def _qkv_kernel(h_ref, wq_ref, wk_ref, wv_ref, *refs, d, tm, n_cast, seq):
    lru_in, refs = refs[:N_LRU_IN], refs[N_LRU_IN:]
    cast_in, o_ref = refs[:n_cast], refs[n_cast]
    cast_out = refs[n_cast + 1:2 * n_cast + 1]
    ylru_ref = refs[2 * n_cast + 1]
    res_scr, tmp_scr = refs[2 * n_cast + 2:2 * n_cast + 4]
    lru_init, lru_chunk, n_chunks, lru_scan, lru_finish, n_finish = _lru_stages(
        *lru_in, ylru_ref, *refs[2 * n_cast + 4:], seq=seq)
    for src_ref, dst_ref in zip(cast_in, cast_out):
        dst_ref[...] = src_ref[...].astype(bf16)
    mb = tm // PROJ_ROW_BLOCKS
    w_refs = (wq_ref, wk_ref, wv_ref)

    def dot(n):
        k, t = divmod(n, len(w_refs))
        res = jnp.dot(h_ref[k * mb:(k + 1) * mb, :], w_refs[t][...], preferred_element_type=f32)
        col = t * PROJ_TN
        if d == 1:
            o_ref[0, k * mb:(k + 1) * mb, col:col + PROJ_TN] = res.astype(bf16)
            return
        buf = n % N_STAGE
        for c in range(SLABS):
            res_scr[buf, c] = res[:, c * LANES:(c + 1) * LANES]
        src, step = res_scr, d
        if d == DEINTERLEAVE_STEP ** 2:
            step = DEINTERLEAVE_STEP
            for r in range(step):
                for c in range(SLABS):
                    tmp_scr[buf, c, r * (mb // step):(r + 1) * (mb // step), :] = (
                        res_scr[buf, c, pl.ds(r, mb // step, stride=step), :])
            src = tmp_scr
        for r in range(d):
            start = r if src is res_scr else (r % step) * (mb // step) + r // step
            for c in range(SLABS):
                o_ref[0, r, k * (mb // d):(k + 1) * (mb // d), col + c * LANES:col + (c + 1) * LANES] = (
                    src[buf, c, pl.ds(start, mb // d, stride=step), :].astype(bf16))

    n_dots = PROJ_ROW_BLOCKS * len(w_refs)
    per_dot = -(-n_chunks // (n_dots - 1))
    lru_init()
    for n in range(n_dots - 1):
        dot(n)
        for ci in range(n * per_dot, min((n + 1) * per_dot, n_chunks)):
            lru_chunk(ci)
    lru_scan()
    dot(n_dots - 1)
    for ci in range(n_finish):
        lru_finish(ci)


def _qkv_proj(h, w_qkv, g, batch, seq, rest3, conv_w, conv_b, w_gates, lam, cast=(), tm=1024):
    T = h.shape[0]
    d = ATTN_GROUPS[g][1]
    nt = seq // tm
    n_groups = len(ATTN_GROUPS)
    mb = tm // PROJ_ROW_BLOCKS
    steps = T // tm
    upb = LRU_BLOCKS // n_groups
    assert steps == batch * upb

    def w_spec(which):
        return pl.BlockSpec((D_MODEL, PROJ_TN), lambda i: (0, which * n_groups + g), pipeline_mode=pl.Buffered(1))

    def blk(i):
        return g * upb + i % upb

    yb0 = REST_YB // LRU_BW
    lru_specs = [pl.BlockSpec((1, seq, LRU_BW), lambda i: (i // upb, 0, blk(i))),
                 pl.BlockSpec((1, seq, LRU_BW), lambda i: (i // upb, 0, yb0 + blk(i))),
                 pl.BlockSpec((4, LRU_BW), lambda i: (0, blk(i))),
                 pl.BlockSpec((1, LRU_BW), lambda i: (0, blk(i))),
                 pl.BlockSpec((1, 2 * LRU_BW, 4 * LRU_BW), lambda i: (blk(i), 0, 0)),
                 pl.BlockSpec((2, LRU_BW), lambda i: (0, blk(i)))]
    ylru_spec = pl.BlockSpec((1, seq, LRU_BW), lambda i: (i // upb, 0, i % upb))
    cast_specs = [pl.BlockSpec((w.shape[0] // steps, w.shape[1]), lambda i: (i, 0)) for w in cast]

    if d == 1:
        out_spec = pl.BlockSpec((1, tm, QKV_W), lambda i: (i // nt, i % nt, 0))
        out_shape = jax.ShapeDtypeStruct((batch, seq, QKV_W), bf16)
    else:
        out_spec = pl.BlockSpec((1, d, tm // d, QKV_W), lambda i: (i // nt, 0, i % nt, 0))
        out_shape = jax.ShapeDtypeStruct((batch, d, seq // d, QKV_W), bf16)
    return pl.pallas_call(
        functools.partial(_qkv_kernel, d=d, tm=tm, n_cast=len(cast), seq=seq),
        grid=(steps,),
        in_specs=([pl.BlockSpec((tm, D_MODEL), lambda i: (i, 0)), w_spec(0), w_spec(1), w_spec(2)]
                  + lru_specs + cast_specs),
        out_specs=[out_spec] + cast_specs + [ylru_spec],
        out_shape=([out_shape] + [jax.ShapeDtypeStruct(w.shape, bf16) for w in cast]
                   + [jax.ShapeDtypeStruct((batch, seq, upb * LRU_BW), bf16)]),
        scratch_shapes=[pltpu.VMEM((N_STAGE, SLABS, mb, LANES), f32)] * 2 + _lru_scratch(seq),
        compiler_params=pltpu.CompilerParams(dimension_semantics=("arbitrary",),
                                             vmem_limit_bytes=QKV_VMEM_LIMIT),
        name=f"qkv_g{g}",
    )(h, w_qkv, w_qkv, w_qkv, rest3, rest3, conv_w, conv_b, w_gates, lam, *cast)


REST_TILES = REST_W // PROJ_TN
YB_TILES = range(REST_YB // PROJ_TN, REST_QC // PROJ_TN)


def _gelu_tanh(y):
    return y * (0.5 * (1.0 + jnp.tanh(math.sqrt(2.0 / math.pi) * (y + 0.044715 * (y * y * y)))))


def _rest_kernel(x_ref, g_ref, *refs):
    w_refs = refs[:REST_TILES]
    wu_ref, wd_ref, h_ref, o_ref, wu_o_ref, wd_o_ref = refs[REST_TILES:]
    wu_o_ref[...] = wu_ref[...].astype(bf16)
    wd_o_ref[...] = wd_ref[...].astype(bf16)
    mb = x_ref.shape[0] // PROJ_ROW_BLOCKS
    for k in range(PROJ_ROW_BLOCKS):
        rows = slice(k * mb, (k + 1) * mb)
        h = _rms(x_ref[rows, :], g_ref[...]).astype(bf16)
        h_ref[rows, :] = h
        for c, w_ref in enumerate(w_refs):
            res = jnp.dot(h, w_ref[...], preferred_element_type=f32)
            if c in YB_TILES:
                res = _gelu_tanh(res)
            o_ref[rows, c * PROJ_TN:(c + 1) * PROJ_TN] = res.astype(bf16)


def _rest_proj(x2, gain, w_in, w_up, w_down, tm=512):
    T = x2.shape[0]
    steps = T // tm
    first = COL_XB // PROJ_TN
    up_rows, down_rows = D_MODEL // steps, D_FF // steps

    def w_spec(c):
        return pl.BlockSpec((D_MODEL, PROJ_TN), lambda i: (0, first + c), pipeline_mode=pl.Buffered(1))

    return pl.pallas_call(
        _rest_kernel,
        grid=(steps,),
        in_specs=[pl.BlockSpec((tm, D_MODEL), lambda i: (i, 0)),
                  pl.BlockSpec((1, D_MODEL), lambda i: (0, 0))]
        + [w_spec(c) for c in range(REST_TILES)]
        + [pl.BlockSpec((up_rows, D_FF), lambda i: (i, 0)),
           pl.BlockSpec((down_rows, D_MODEL), lambda i: (i, 0))],
        out_specs=[pl.BlockSpec((tm, D_MODEL), lambda i: (i, 0)),
                   pl.BlockSpec((tm, REST_W), lambda i: (i, 0)),
                   pl.BlockSpec((up_rows, D_FF), lambda i: (i, 0)),
                   pl.BlockSpec((down_rows, D_MODEL), lambda i: (i, 0))],
        out_shape=[jax.ShapeDtypeStruct((T, D_MODEL), bf16),
                   jax.ShapeDtypeStruct((T, REST_W), bf16),
                   jax.ShapeDtypeStruct((D_MODEL, D_FF), bf16),
                   jax.ShapeDtypeStruct((D_FF, D_MODEL), bf16)],
        compiler_params=_cparams("arbitrary"),
        name="rest_proj",
    )(x2, gain, *([w_in] * REST_TILES), w_up, w_down)


def _t5_bucket(rel):
    nb = N_BUCKETS // 2
    max_exact = nb // 2
    sign = (rel > 0).astype(np.int32) * nb
    n = np.abs(rel)
    large = max_exact + (np.log(np.maximum(n, 1) / max_exact)
                         / np.log(MAX_DISTANCE / max_exact) * (nb - max_exact)).astype(np.int32)
    large = np.minimum(large, nb - 1)
    return (sign + np.where(n < max_exact, n, large)).astype(np.int32)


def _band_bias(rel_bias_g, dilation):
    qq = np.arange(SUB_Q)[:, None]
    kk = np.arange(SUB_K)[None, :]
    rel = kk - ATTN_RADIUS - qq
    onehot = (_t5_bucket(rel * dilation)[None] == np.arange(N_BUCKETS)[:, None, None]).astype(np.float32)
    bias = jnp.einsum('nh,nqk->hqk', rel_bias_g.astype(f32), onehot, precision=lax.Precision.HIGHEST)
    return bias + np.where(np.abs(rel) <= ATTN_RADIUS, 0.0, NEG_INF).astype(np.float32)[None]


def _attn_kernel(q_ref, kp_ref, km_ref, kn_ref, vp_ref, vm_ref, vn_ref, bias_ref,
                 o_ref, lse_ref, kbuf, vbuf, *, tq, seq, n_seq):
    R = ATTN_RADIUS
    q0 = pl.program_id(1) * tq
    lane = lax.broadcasted_iota(jnp.int32, (SUB_Q, LSE_LANES), 1)
    n_sub = tq // SUB_Q
    for i in range(n_seq):
        kbuf[i, 0:R] = kp_ref[i]
        kbuf[i, R:R + tq] = km_ref[i]
        kbuf[i, R + tq:] = kn_ref[i]
        vbuf[i, 0:R] = vp_ref[i]
        vbuf[i, R:R + tq] = vm_ref[i]
        vbuf[i, R + tq:] = vn_ref[i]
        for s in range(n_sub):
            r0 = s * SUB_Q
            edge = None
            if s == 0 or s == n_sub - 1:
                pos = q0 + (r0 - R) + lax.broadcasted_iota(jnp.int32, (1, SUB_K), 1)
                edge = jnp.where(pos >= 0, jnp.where(pos < seq, 0.0, NEG_INF), NEG_INF)
            lse_tile = None
            for h in range(HEADS_PER_GROUP):
                c0 = h * HEAD_DIM_A
                q = q_ref[i, r0:r0 + SUB_Q, c0:c0 + HEAD_DIM_A]
                k = kbuf[i, r0:r0 + SUB_K, c0:c0 + HEAD_DIM_A]
                v = vbuf[i, r0:r0 + SUB_K, c0:c0 + HEAD_DIM_A]
                logits = lax.dot_general(q, k, (((1,), (1,)), ((), ())), preferred_element_type=f32) + bias_ref[h]
                if edge is not None:
                    logits = logits + edge
                m = jnp.max(logits, axis=-1, keepdims=True)
                p = jnp.exp(logits - m)
                ssum = jnp.sum(p, axis=-1, keepdims=True)
                o = jnp.dot(p.astype(bf16), v, preferred_element_type=f32) * (1.0 / ssum)
                o_ref[i, r0:r0 + SUB_Q, c0:c0 + HEAD_DIM_A] = o.astype(o_ref.dtype)
                lse = m + jnp.log(ssum)
                lse_tile = lse if lse_tile is None else jnp.where(lane >= h * LSE_REP, lse, lse_tile)
            lse_ref[i, r0:r0 + SUB_Q, :] = jnp.broadcast_to(lse_tile, (SUB_Q, LSE_LANES))


def _attn_group(qkv, rel_bias, g):
    _, d = ATTN_GROUPS[g]
    n, L, _ = qkv.shape
    tq = min(ATTN_ROWS_PER_STEP, L)
    ns = ATTN_ROWS_PER_STEP // tq
    R = ATTN_RADIUS
    bias = _band_bias(rel_bias[:, g * HEADS_PER_GROUP:(g + 1) * HEADS_PER_GROUP], d)
    rb = tq // R
    last_rb = L // R - 1

    def main(col, width=GROUP_WIDTH):
        return pl.BlockSpec((ns, tq, width), lambda b, t: (b, t, col))

    def prev(col):
        return pl.BlockSpec((ns, R, GROUP_WIDTH), lambda b, t: (b, jnp.maximum(t * rb - 1, 0), col))

    def nxt(col):
        return pl.BlockSpec((ns, R, GROUP_WIDTH), lambda b, t: (b, jnp.minimum((t + 1) * rb, last_rb), col))

    return pl.pallas_call(
        functools.partial(_attn_kernel, tq=tq, seq=L, n_seq=ns),
        grid=(n // ns, L // tq),
        in_specs=[main(0), prev(1), main(1), nxt(1), prev(2), main(2), nxt(2),
                  pl.BlockSpec((HEADS_PER_GROUP, SUB_Q, SUB_K), lambda b, t: (0, 0, 0))],
        out_specs=[main(0), main(0, LSE_LANES)],
        out_shape=[jax.ShapeDtypeStruct((n, L, GROUP_WIDTH), bf16),
                   jax.ShapeDtypeStruct((n, L, LSE_LANES), f32)],
        scratch_shapes=[pltpu.VMEM((ns, tq + 2 * R, GROUP_WIDTH), bf16),
                        pltpu.VMEM((ns, tq + 2 * R, GROUP_WIDTH), bf16)],
        compiler_params=_cparams("parallel", "arbitrary"),
        name=f"attn_g{g}",
    )(qkv, qkv, qkv, qkv, qkv, qkv, qkv, bias)


LRU_CHUNK = 256
LRU_PAD = 8
LRU_FINISH_ROWS = 512
GATE_BIAS_ROWS = 3
LRU_SEGS = 8
SEG_GAP = 4


def _lru_scratch(seq):
    return ([pltpu.VMEM((seq + 2 * LRU_PAD, LRU_BW), f32)]
            + [pltpu.VMEM((seq + LRU_SEGS * SEG_GAP, LRU_BW), f32)] * 8
            + [pltpu.VMEM((LRU_SEGS, LRU_BW), f32)] * 2)


def _lru_stages(xb_ref, yb_ref, cw_ref, cb_ref, w_ref, lam_ref, o_ref,
                xpad, af, bf, ab, bb, htf, ptf, htb, ptb, cf_scr, cb_scr, *, seq):
    R = LRU_CHUNK
    P = LRU_PAD
    seg_len = seq // LRU_SEGS
    pitch = seg_len + SEG_GAP
    chunks_per_seg = seg_len // R
    n_chunks = seq // R

    def init():
        xpad[0:P] = jnp.zeros((P, LRU_BW), f32)
        xpad[P + seq:] = jnp.zeros((P, LRU_BW), f32)
        xpad[P:P + seq] = xb_ref[0].astype(f32)

    lam = lam_ref[...]
    log_a_unit = -LRU_C * (jnp.maximum(-lam, 0.0) + jnp.log1p(jnp.exp(-jnp.abs(lam))))
    cw = cw_ref[...]
    cb = cb_ref[...]
    row = lax.broadcasted_iota(jnp.int32, (R, LRU_BW), 0)
    lane = lax.broadcasted_iota(jnp.int32, (R, LRU_BW), 1)
    bias_cols = jnp.where(lane < GATE_BIAS_ROWS, 1.0, 0.0).astype(bf16)

    def chunk(ci):
        first, last = ci == 0, ci == n_chunks - 1
        c0 = ci * R
        dst = (ci // chunks_per_seg) * pitch + (ci % chunks_per_seg) * R
        xc = (cw[0:1] * xpad[pl.ds(c0 + (P - 1), R), :] + cw[1:2] * xpad[pl.ds(c0 + P, R), :]
              + cw[2:3] * xpad[pl.ds(c0 + (P + 1), R), :] + cw[3:4] * xpad[pl.ds(c0 + (P + 2), R), :]) + cb
        lhs = jnp.concatenate([xc.astype(bf16), bias_cols], axis=1)
        th = jnp.tanh(jnp.dot(lhs, w_ref[0], preferred_element_type=f32))
        half_xc = 0.5 * xc
        for direction, (a_scr, b_scr) in enumerate(((af, bf), (ab, bb))):
            base = direction * 2 * LRU_BW
            half_log2_a = (0.5 * math.log2(math.e)) * log_a_unit[direction:direction + 1]
            a = jnp.exp2(half_log2_a * th[:, base:base + LRU_BW] + half_log2_a)
            gated_x = half_xc * th[:, base + LRU_BW:base + 2 * LRU_BW] + half_xc
            y = 1.0 - a * a
            mult = y * lax.rsqrt(jnp.maximum(y, 1e-30))
            if direction == 0 and first:
                mult = jnp.where(row == 0, 1.0, mult)
            if direction == 1 and last:
                mult = jnp.where(row == R - 1, 1.0, mult)
            a_scr[pl.ds(dst, R), :] = a
            b_scr[pl.ds(dst, R), :] = mult * gated_x

    def scan_step(i, carry):
        hf, pf, hb, pb = carry
        rows = pl.ds(i, LRU_SEGS, stride=pitch)
        a = af[rows, :]
        hf = a * hf + bf[rows, :]
        pf = a * pf
        htf[rows, :] = hf
        ptf[rows, :] = pf
        rows = pl.ds(seg_len - 1 - i, LRU_SEGS, stride=pitch)
        a = ab[rows, :]
        hb = a * hb + bb[rows, :]
        pb = a * pb
        htb[rows, :] = hb
        ptb[rows, :] = pb
        return hf, pf, hb, pb

    def scan():
        zero = jnp.zeros((LRU_SEGS, LRU_BW), f32)
        one = jnp.ones((LRU_SEGS, LRU_BW), f32)
        hf, pf, hb, pb = lax.fori_loop(0, seg_len, scan_step, (zero, one, zero, one), unroll=8)
        c = jnp.zeros((1, LRU_BW), f32)
        cf_scr[0:1] = c
        for j in range(1, LRU_SEGS):
            c = hf[j - 1:j] + pf[j - 1:j] * c
            cf_scr[j:j + 1] = c
        c = jnp.zeros((1, LRU_BW), f32)
        cb_scr[LRU_SEGS - 1:LRU_SEGS] = c
        for j in range(LRU_SEGS - 2, -1, -1):
            c = hb[j + 1:j + 2] + pb[j + 1:j + 2] * c
            cb_scr[j:j + 1] = c

    F = LRU_FINISH_ROWS
    finish_per_seg = seg_len // F

    def finish(ci):
        c0 = ci * F
        seg = ci // finish_per_seg
        rows = pl.ds(seg * pitch + (ci % finish_per_seg) * F, F)
        h = (htf[rows, :] + ptf[rows, :] * cf_scr[seg:seg + 1, :]
             + htb[rows, :] + ptb[rows, :] * cb_scr[seg:seg + 1, :])
        o_ref[0, c0:c0 + F, :] = (h * yb_ref[0, c0:c0 + F, :].astype(f32)).astype(o_ref.dtype)

    return init, chunk, n_chunks, scan, finish, seq // F


def _pack_lru_gates(w, b):
    rows, rest = [], b
    for _ in range(GATE_BIAS_ROWS):
        piece = rest.astype(bf16)
        rows.append(piece)
        rest = rest - piece.astype(f32)
    bias_rows = jnp.pad(jnp.stack(rows, axis=1), ((0, 0), (0, LRU_BW - GATE_BIAS_ROWS), (0, 0)))
    return jnp.concatenate([w.astype(bf16), bias_rows], axis=1)


def _mem_kv_kernel(m_ref, g_ref, w_ref, o_ref, h_scr):
    @pl.when(pl.program_id(0) == 0)
    def _():
        h_scr[...] = _rms(m_ref[...], g_ref[...]).astype(bf16)

    o_ref[...] = jnp.dot(h_scr[...], w_ref[...], preferred_element_type=f32).astype(o_ref.dtype)


def _mem_kv(mem2, gain, w, tn=512):
    M = mem2.shape[0]
    N = w.shape[1]
    return pl.pallas_call(
        _mem_kv_kernel,
        grid=(N // tn,),
        in_specs=[pl.BlockSpec((M, D_MODEL), lambda j: (0, 0)),
                  pl.BlockSpec((1, D_MODEL), lambda j: (0, 0)),
                  pl.BlockSpec((D_MODEL, tn), lambda j: (0, j))],
        out_specs=pl.BlockSpec((M, tn), lambda j: (0, j)),
        out_shape=jax.ShapeDtypeStruct((M, N), bf16),
        scratch_shapes=[pltpu.VMEM((M, D_MODEL), bf16)],
        compiler_params=_cparams("arbitrary"),
        name="mem_kv",
    )(mem2, gain, w)


def _xattn_kernel(q0_ref, q1_ref, q2_ref, q3_ref, kv_ref, o_ref):
    for h, q_ref in enumerate((q0_ref, q1_ref, q2_ref, q3_ref)):
        c0 = h * MEM_HEAD_DIM
        k = kv_ref[0, :, c0:c0 + MEM_HEAD_DIM]
        v = kv_ref[0, :, MEM_WIDTH + c0:MEM_WIDTH + c0 + MEM_HEAD_DIM]
        logits = lax.dot_general(q_ref[0], k, (((1,), (1,)), ((), ())), preferred_element_type=f32)
        m = jnp.max(logits, axis=-1, keepdims=True)
        p = jnp.exp(logits - m)
        ssum = jnp.sum(p, axis=-1, keepdims=True)
        o = jnp.dot(p.astype(bf16), v, preferred_element_type=f32) * (1.0 / ssum)
        o_ref[0, :, c0:c0 + MEM_HEAD_DIM] = o.astype(o_ref.dtype)


def _xattn(proj3, kv3, tq=1024):
    B, S, _ = proj3.shape
    qb0 = REST_QC // MEM_HEAD_DIM

    def qspec(h):
        return pl.BlockSpec((1, tq, MEM_HEAD_DIM), lambda b, t: (b, t, qb0 + h))

    return pl.pallas_call(
        _xattn_kernel,
        grid=(B, S // tq),
        in_specs=[qspec(0), qspec(1), qspec(2), qspec(3),
                  pl.BlockSpec((1, N_MEM, 2 * MEM_WIDTH), lambda b, t: (b, 0, 0))],
        out_specs=pl.BlockSpec((1, tq, MEM_WIDTH), lambda b, t: (b, t, 0)),
        out_shape=jax.ShapeDtypeStruct((B, S, MEM_WIDTH), bf16),
        compiler_params=_cparams("parallel", "parallel"),
        name="xattn",
    )(proj3, proj3, proj3, proj3, kv3)


def _combine_kernel(o0_ref, o1_ref, o2_ref, l0_ref, l1_ref, l2_ref, ya_ref,
                    o1_scr, o2_scr, l1_scr, l2_scr, tmp_scr, *, tm):
    step = DEINTERLEAVE_STEP
    for g, o_ref, l_ref, o_scr, l_scr in ((1, o1_ref, l1_ref, o1_scr, l1_scr),
                                          (2, o2_ref, l2_ref, o2_scr, l2_scr)):
        d = ATTN_GROUPS[g][1]
        slabs = [(l_scr, lambda r: l_ref[0, r])]
        slabs += [(o_scr.at[h], lambda r, h=h: o_ref[0, r, :, h * HEAD_DIM_A:(h + 1) * HEAD_DIM_A].astype(f32))
                  for h in range(HEADS_PER_GROUP)]
        for k, (dst, rows_of) in enumerate(slabs):
            if d == step:
                for r in range(d):
                    dst[pl.ds(r, tm // d, stride=d), :] = rows_of(r)
                continue
            tmp = tmp_scr.at[k]
            for r in range(d):
                tmp[pl.ds((r % step) * (tm // step) + r // step, tm // d, stride=step), :] = rows_of(r)
            for lo in range(step):
                dst[pl.ds(lo, tm // step, stride=step), :] = tmp[lo * (tm // step):(lo + 1) * (tm // step), :]
    l0, l1, l2 = l0_ref[...], l1_scr[...], l2_scr[...]
    m = jnp.maximum(jnp.maximum(l0, l1), l2)
    e0, e1, e2 = jnp.exp(l0 - m), jnp.exp(l1 - m), jnp.exp(l2 - m)
    inv = 1.0 / (e0 + e1 + e2)
    for h in range(HEADS_PER_GROUP):
        c0 = h * HEAD_DIM_A
        lane = slice(h * LSE_REP, h * LSE_REP + 1)
        y = ((e0 * inv)[:, lane] * o0_ref[:, c0:c0 + HEAD_DIM_A].astype(f32)
             + (e1 * inv)[:, lane] * o1_scr[h] + (e2 * inv)[:, lane] * o2_scr[h])
        ya_ref[:, c0:c0 + HEAD_DIM_A] = y.astype(bf16)


def _combine(o_groups, lse_groups, seq, tm=512):
    T = o_groups[0].shape[0]
    nt = seq // tm
    d1, d2 = ATTN_GROUPS[1][1], ATTN_GROUPS[2][1]

    def rows(width):
        return pl.BlockSpec((tm, width), lambda i: (i, 0))

    def strided_rows(d, width):
        return pl.BlockSpec((1, d, tm // d, width), lambda i: (i // nt, 0, i % nt, 0))

    return pl.pallas_call(
        functools.partial(_combine_kernel, tm=tm),
        grid=(T // tm,),
        in_specs=[rows(GROUP_WIDTH), strided_rows(d1, GROUP_WIDTH), strided_rows(d2, GROUP_WIDTH),
                  rows(LSE_LANES), strided_rows(d1, LSE_LANES), strided_rows(d2, LSE_LANES)],
        out_specs=rows(GROUP_WIDTH),
        out_shape=jax.ShapeDtypeStruct((T, GROUP_WIDTH), bf16),
        scratch_shapes=[pltpu.VMEM((HEADS_PER_GROUP, tm, HEAD_DIM_A), f32),
                        pltpu.VMEM((HEADS_PER_GROUP, tm, HEAD_DIM_A), f32),
                        pltpu.VMEM((tm, LSE_LANES), f32), pltpu.VMEM((tm, LSE_LANES), f32),
                        pltpu.VMEM((1 + HEADS_PER_GROUP, tm, LANES), f32)],
        compiler_params=_cparams("parallel"),
        name="combine",
    )(*o_groups, *lse_groups)


def _gate_mix_kernel(h_ref, ya_ref, yl0_ref, yl1_ref, yl2_ref, yc_ref,
                     wga_ref, wgb_ref, wgc_ref, bga_ref, bgb_ref, bgc_ref,
                     woa_ref, wol_ref, wom_ref, mix_ref):
    h = h_ref[...]

    def gate(w_ref, b_ref):
        return jax.nn.sigmoid(jnp.dot(h, w_ref[...], preferred_element_type=f32) + b_ref[...])

    yl = jnp.concatenate([yl0_ref[...], yl1_ref[...], yl2_ref[...]], axis=1)
    mixed = (gate(wga_ref, bga_ref) * jnp.dot(ya_ref[...], woa_ref[...], preferred_element_type=f32)
             + gate(wgb_ref, bgb_ref) * jnp.dot(yl, wol_ref[...], preferred_element_type=f32)
             + gate(wgc_ref, bgc_ref) * jnp.dot(yc_ref[...], wom_ref[...], preferred_element_type=f32))
    mix_ref[...] = mixed.astype(mix_ref.dtype)


def _gate_mix(h, y_a, y_lru, y_c, w_gate, b_gate, w_o_attn, w_o_lru, w_o_mem, tm=512, tn=512):
    T = h.shape[0]
    nj = D_MODEL // tn

    def rows(width):
        return pl.BlockSpec((tm, width), lambda j, i: (i, 0))

    def gate_w(k):
        return pl.BlockSpec((D_MODEL, tn), lambda j, i: (0, k * nj + j))

    def gate_b(k):
        return pl.BlockSpec((1, tn), lambda j, i: (0, k * nj + j))

    def cols(width):
        return pl.BlockSpec((width, tn), lambda j, i: (0, j))

    return pl.pallas_call(
        _gate_mix_kernel,
        grid=(nj, T // tm),
        in_specs=[rows(D_MODEL), rows(GROUP_WIDTH)] + [rows(y.shape[1]) for y in y_lru] + [rows(MEM_WIDTH),
                  gate_w(0), gate_w(1), gate_w(2), gate_b(0), gate_b(1), gate_b(2),
                  cols(GROUP_WIDTH), cols(LRU_WIDTH), cols(MEM_WIDTH)],
        out_specs=pl.BlockSpec((tm, tn), lambda j, i: (i, j)),
        out_shape=jax.ShapeDtypeStruct((T, D_MODEL), bf16),
        compiler_params=_cparams("arbitrary", "arbitrary"),
        name="gate_mix",
    )(h, y_a, *y_lru, y_c, w_gate, w_gate, w_gate, b_gate, b_gate, b_gate, w_o_attn, w_o_lru, w_o_mem)


def _mlp_kernel(x_ref, mix_ref, wo_ref, g_ref, gf_ref, wu_ref, wd_ref, out_ref, h_scr):
    j = pl.program_id(1)

    @pl.when(j == 0)
    def _():
        x = x_ref[...] + jnp.dot(mix_ref[...], wo_ref[...], preferred_element_type=f32)
        h_scr[...] = _rms(x, g_ref[...]).astype(bf16)
        out_ref[...] = x

    u = jnp.maximum(jnp.dot(h_scr[...], wu_ref[...], preferred_element_type=f32), 0.0)
    out_ref[...] += jnp.dot((u * u).astype(bf16), wd_ref[...], preferred_element_type=f32)

    @pl.when(j == pl.num_programs(1) - 1)
    def _():
        out_ref[...] = _rms(out_ref[...], gf_ref[...])


def _mlp(x2, mixed, w_out, gain, gain_final, w_up, w_down, tm=512, tf=1024):
    T = x2.shape[0]
    return pl.pallas_call(
        _mlp_kernel,
        grid=(T // tm, D_FF // tf),
        in_specs=[pl.BlockSpec((tm, D_MODEL), lambda i, j: (i, 0)),
                  pl.BlockSpec((tm, D_MODEL), lambda i, j: (i, 0)),
                  pl.BlockSpec((D_MODEL, D_MODEL), lambda i, j: (0, 0)),
                  pl.BlockSpec((1, D_MODEL), lambda i, j: (0, 0)),
                  pl.BlockSpec((1, D_MODEL), lambda i, j: (0, 0)),
                  pl.BlockSpec((D_MODEL, tf), lambda i, j: (0, j)),
                  pl.BlockSpec((tf, D_MODEL), lambda i, j: (j, 0))],
        out_specs=pl.BlockSpec((tm, D_MODEL), lambda i, j: (i, 0)),
        out_shape=jax.ShapeDtypeStruct((T, D_MODEL), f32),
        scratch_shapes=[pltpu.VMEM((tm, D_MODEL), bf16)],
        compiler_params=_cparams("parallel", "arbitrary"),
        name="mlp",
    )(x2, mixed, w_out, gain, gain_final, w_up, w_down)


def _query_scale():
    scale = np.ones((1, N_IN), np.float32)
    scale[:, :WIDTH_A] = 1.0 / math.sqrt(HEAD_DIM_A)
    scale[:, COL_QC:] = 1.0 / math.sqrt(MEM_HEAD_DIM)
    return scale


def kernel(x, mem, rel_bias, norm_mix, norm_mem, norm_mlp, norm_final, w_in, w_gate, b_gate, conv_w, conv_b,
           lru_wa, lru_ba, lru_wi, lru_bi, lru_lambda, w_mem_kv, w_o_attn, w_o_lru, w_o_mem, w_out, w_up, w_down):
    B, S, D = x.shape
    T = B * S
    depth = w_in.shape[0]
    assert depth == 1, "the final RMSNorm is fused into the (single) layer's MLP kernel"
    x2 = x.reshape(T, D)
    mem2 = mem.reshape(B * N_MEM, D)
    for l in range(depth):
        w_qkv = (w_in[l] * _query_scale()).astype(bf16)
        h, rest, w_up_bf, w_down_bf = _rest_proj(x2, norm_mix[l].reshape(1, D), w_qkv, w_up[l], w_down[l])
        proj3 = rest.reshape(B, S, REST_W)

        w_gates = 0.5 * jnp.concatenate([lru_wa[l, 0], lru_wi[l, 0], lru_wa[l, 1], lru_wi[l, 1]], axis=-1)
        b_gates = 0.5 * jnp.concatenate([lru_ba[l, 0], lru_bi[l, 0], lru_ba[l, 1], lru_bi[l, 1]], axis=-1)
        lru_params = (conv_w[l], conv_b[l].reshape(1, LRU_WIDTH), _pack_lru_gates(w_gates, b_gates), lru_lambda[l])

        side_casts = ((w_gate[l],), (w_out[l], w_o_lru[l]), (w_o_attn[l], w_o_mem[l], w_mem_kv[l]))
        casted, attn, y_lru = [], [], []
        for g in range(len(ATTN_GROUPS)):
            d = ATTN_GROUPS[g][1]
            qkv, *bf_copies, y_lru_g = _qkv_proj(h, w_qkv, g, B, S, proj3, *lru_params, cast=side_casts[g])
            casted.append(bf_copies)
            y_lru.append(y_lru_g.reshape(T, -1))
            o, lse = _attn_group(qkv.reshape(B * d, S // d, QKV_W), rel_bias, g)
            if g == 0:
                attn.append((o.reshape(T, GROUP_WIDTH), lse.reshape(T, LSE_LANES)))
            else:
                attn.append((o.reshape(B, d, S // d, GROUP_WIDTH), lse.reshape(B, d, S // d, LSE_LANES)))

        (w_gate_bf,), (w_out_bf, w_o_lru_bf), (w_o_attn_bf, w_o_mem_bf, w_mem_kv_bf) = casted
        kv = _mem_kv(mem2, norm_mem[l].reshape(1, D), w_mem_kv_bf)
        y_c = _xattn(proj3, kv.reshape(B, N_MEM, 2 * MEM_WIDTH))

        y_a = _combine([a[0] for a in attn], [a[1] for a in attn], S)
        mixed = _gate_mix(h, y_a, y_lru, y_c.reshape(T, MEM_WIDTH),
                          w_gate_bf, b_gate[l].reshape(1, 3 * D), w_o_attn_bf, w_o_lru_bf, w_o_mem_bf)
        x2 = _mlp(x2, mixed, w_out_bf, norm_mlp[l].reshape(1, D), norm_final.reshape(1, D),
                  w_up_bf, w_down_bf)
    return x2.reshape(B, S, D)
```

```python
import functools
import math

import jax
import jax.numpy as jnp
import numpy as np
from jax import lax
from jax.experimental import pallas as pl
from jax.experimental.pallas import tpu as pltpu

D_MODEL = 2048
HEAD_DIM_A = 128
ATTN_GROUPS = ((128, 1), (512, 4), (2048, 16))
HEADS_PER_GROUP = 4
GROUP_WIDTH = HEADS_PER_GROUP * HEAD_DIM_A
WIDTH_A = len(ATTN_GROUPS) * GROUP_WIDTH
ATTN_RADIUS = 64
N_BUCKETS = 32
MAX_DISTANCE = 1024
LRU_WIDTH = 1536
LRU_BLOCKS = 12
LRU_BW = 128
LRU_C = 8.0
N_MEM = 256
MEM_HEADS = 4
MEM_HEAD_DIM = 256
MEM_WIDTH = MEM_HEADS * MEM_HEAD_DIM
D_FF = 4 * D_MODEL
EPS = 1e-6
N_IN = 3 * WIDTH_A + 2 * LRU_WIDTH + MEM_WIDTH
COL_K = WIDTH_A
COL_V = 2 * WIDTH_A
COL_XB = 3 * WIDTH_A
COL_YB = 3 * WIDTH_A + LRU_WIDTH
COL_QC = 3 * WIDTH_A + 2 * LRU_WIDTH
NEG_INF = -1e30

ATTN_ROWS_PER_STEP = 1024
SUB_Q = 128
SUB_K = SUB_Q + 2 * ATTN_RADIUS
LSE_LANES = 128
LSE_REP = LSE_LANES // HEADS_PER_GROUP

VMEM_LIMIT = 56 * 1024 * 1024

f32 = jnp.float32
bf16 = jnp.bfloat16


def _cparams(*sem):
    return pltpu.CompilerParams(dimension_semantics=sem, vmem_limit_bytes=VMEM_LIMIT)


def _rms(x, gain):
    return x * lax.rsqrt(jnp.mean(x * x, axis=-1, keepdims=True) + EPS) * gain


QKV_W = 3 * GROUP_WIDTH
REST_W = 2 * LRU_WIDTH + MEM_WIDTH
REST_YB = LRU_WIDTH
REST_QC = 2 * LRU_WIDTH
PROJ_TN = GROUP_WIDTH
LANES = 128
SLABS = PROJ_TN // LANES
PROJ_ROW_BLOCKS = 2
DEINTERLEAVE_STEP = 4
N_STAGE = 2


def _qkv_kernel(h_ref, wq_ref, wk_ref, wv_ref, *refs, d, tm, n_cast):
    cast_in, o_ref = refs[:n_cast], refs[n_cast]
    cast_out = refs[n_cast + 1:2 * n_cast + 1]
    res_scr, tmp_scr = refs[2 * n_cast + 1:]
    for src_ref, dst_ref in zip(cast_in, cast_out):
        dst_ref[...] = src_ref[...].astype(bf16)
    mb = tm // PROJ_ROW_BLOCKS
    n = 0
    for k in range(PROJ_ROW_BLOCKS):
        hk = h_ref[k * mb:(k + 1) * mb, :]
        for t, w_ref in enumerate((wq_ref, wk_ref, wv_ref)):
            res = jnp.dot(hk, w_ref[...], preferred_element_type=f32)
            col = t * PROJ_TN
            if d == 1:
                o_ref[0, k * mb:(k + 1) * mb, col:col + PROJ_TN] = res.astype(bf16)
                continue
            buf = n % N_STAGE
            n += 1
            for c in range(SLABS):
                res_scr[buf, c] = res[:, c * LANES:(c + 1) * LANES]
            src, step = res_scr, d
            if d == DEINTERLEAVE_STEP ** 2:
                step = DEINTERLEAVE_STEP
                for r in range(step):
                    for c in range(SLABS):
                        tmp_scr[buf, c, r * (mb // step):(r + 1) * (mb // step), :] = (
                            res_scr[buf, c, pl.ds(r, mb // step, stride=step), :])
                src = tmp_scr
            for r in range(d):
                start = r if src is res_scr else (r % step) * (mb // step) + r // step
                for c in range(SLABS):
                    o_ref[0, r, k * (mb // d):(k + 1) * (mb // d), col + c * LANES:col + (c + 1) * LANES] = (
                        src[buf, c, pl.ds(start, mb // d, stride=step), :].astype(bf16))


def _qkv_proj(h, w_qkv, g, batch, seq, cast=(), tm=1024):
    T = h.shape[0]
    d = ATTN_GROUPS[g][1]
    nt = seq // tm
    n_groups = len(ATTN_GROUPS)
    mb = tm // PROJ_ROW_BLOCKS
    steps = T // tm

    def w_spec(which):
        return pl.BlockSpec((D_MODEL, PROJ_TN), lambda i: (0, which * n_groups + g))

    cast_specs = [pl.BlockSpec((w.shape[0] // steps, w.shape[1]), lambda i: (i, 0)) for w in cast]

    if d == 1:
        out_spec = pl.BlockSpec((1, tm, QKV_W), lambda i: (i // nt, i % nt, 0))
        out_shape = jax.ShapeDtypeStruct((batch, seq, QKV_W), bf16)
    else:
        out_spec = pl.BlockSpec((1, d, tm // d, QKV_W), lambda i: (i // nt, 0, i % nt, 0))
        out_shape = jax.ShapeDtypeStruct((batch, d, seq // d, QKV_W), bf16)
    return pl.pallas_call(
        functools.partial(_qkv_kernel, d=d, tm=tm, n_cast=len(cast)),
        grid=(steps,),
        in_specs=[pl.BlockSpec((tm, D_MODEL), lambda i: (i, 0)), w_spec(0), w_spec(1), w_spec(2)] + cast_specs,
        out_specs=[out_spec] + cast_specs,
        out_shape=[out_shape] + [jax.ShapeDtypeStruct(w.shape, bf16) for w in cast],
        scratch_shapes=[pltpu.VMEM((N_STAGE, SLABS, mb, LANES), f32)] * 2,
        compiler_params=_cparams("parallel"),
        name=f"qkv_g{g}",
    )(h, w_qkv, w_qkv, w_qkv, *cast)


REST_TILES = REST_W // PROJ_TN
YB_TILES = range(REST_YB // PROJ_TN, REST_QC // PROJ_TN)


def _gelu_tanh(y):
    return y * (0.5 * (1.0 + jnp.tanh(math.sqrt(2.0 / math.pi) * (y + 0.044715 * (y * y * y)))))


def _rest_kernel(x_ref, g_ref, *refs):
    w_refs = refs[:REST_TILES]
    wu_ref, wd_ref, h_ref, o_ref, wu_o_ref, wd_o_ref = refs[REST_TILES:]
    wu_o_ref[...] = wu_ref[...].astype(bf16)
    wd_o_ref[...] = wd_ref[...].astype(bf16)
    mb = x_ref.shape[0] // PROJ_ROW_BLOCKS
    for k in range(PROJ_ROW_BLOCKS):
        rows = slice(k * mb, (k + 1) * mb)
        h = _rms(x_ref[rows, :], g_ref[...]).astype(bf16)
        h_ref[rows, :] = h
        for c, w_ref in enumerate(w_refs):
            res = jnp.dot(h, w_ref[...], preferred_element_type=f32)
            if c in YB_TILES:
                res = _gelu_tanh(res)
            o_ref[rows, c * PROJ_TN:(c + 1) * PROJ_TN] = res.astype(bf16)


def _rest_proj(x2, gain, w_in, w_up, w_down, tm=512):
    T = x2.shape[0]
    steps = T // tm
    first = COL_XB // PROJ_TN
    up_rows, down_rows = D_MODEL // steps, D_FF // steps

    def w_spec(c):
        return pl.BlockSpec((D_MODEL, PROJ_TN), lambda i: (0, first + c), pipeline_mode=pl.Buffered(1))

    return pl.pallas_call(
        _rest_kernel,
        grid=(steps,),
        in_specs=[pl.BlockSpec((tm, D_MODEL), lambda i: (i, 0)),
                  pl.BlockSpec((1, D_MODEL), lambda i: (0, 0))]
        + [w_spec(c) for c in range(REST_TILES)]
        + [pl.BlockSpec((up_rows, D_FF), lambda i: (i, 0)),
           pl.BlockSpec((down_rows, D_MODEL), lambda i: (i, 0))],
        out_specs=[pl.BlockSpec((tm, D_MODEL), lambda i: (i, 0)),
                   pl.BlockSpec((tm, REST_W), lambda i: (i, 0)),
                   pl.BlockSpec((up_rows, D_FF), lambda i: (i, 0)),
                   pl.BlockSpec((down_rows, D_MODEL), lambda i: (i, 0))],
        out_shape=[jax.ShapeDtypeStruct((T, D_MODEL), bf16),
                   jax.ShapeDtypeStruct((T, REST_W), bf16),
                   jax.ShapeDtypeStruct((D_MODEL, D_FF), bf16),
                   jax.ShapeDtypeStruct((D_FF, D_MODEL), bf16)],
        compiler_params=_cparams("arbitrary"),
        name="rest_proj",
    )(x2, gain, *([w_in] * REST_TILES), w_up, w_down)


def _t5_bucket(rel):
    nb = N_BUCKETS // 2
    max_exact = nb // 2
    sign = (rel > 0).astype(np.int32) * nb
    n = np.abs(rel)
    large = max_exact + (np.log(np.maximum(n, 1) / max_exact)
                         / np.log(MAX_DISTANCE / max_exact) * (nb - max_exact)).astype(np.int32)
    large = np.minimum(large, nb - 1)
    return (sign + np.where(n < max_exact, n, large)).astype(np.int32)


def _band_bias(rel_bias_g, dilation):
    qq = np.arange(SUB_Q)[:, None]
    kk = np.arange(SUB_K)[None, :]
    rel = kk - ATTN_RADIUS - qq
    onehot = (_t5_bucket(rel * dilation)[None] == np.arange(N_BUCKETS)[:, None, None]).astype(np.float32)
    bias = jnp.einsum('nh,nqk->hqk', rel_bias_g.astype(f32), onehot, precision=lax.Precision.HIGHEST)
    return bias + np.where(np.abs(rel) <= ATTN_RADIUS, 0.0, NEG_INF).astype(np.float32)[None]


def _attn_kernel(q_ref, kp_ref, km_ref, kn_ref, vp_ref, vm_ref, vn_ref, bias_ref,
                 o_ref, lse_ref, kbuf, vbuf, *, tq, seq, n_seq):
    R = ATTN_RADIUS
    q0 = pl.program_id(1) * tq
    lane = lax.broadcasted_iota(jnp.int32, (SUB_Q, LSE_LANES), 1)
    n_sub = tq // SUB_Q
    for i in range(n_seq):
        kbuf[i, 0:R] = kp_ref[i]
        kbuf[i, R:R + tq] = km_ref[i]
        kbuf[i, R + tq:] = kn_ref[i]
        vbuf[i, 0:R] = vp_ref[i]
        vbuf[i, R:R + tq] = vm_ref[i]
        vbuf[i, R + tq:] = vn_ref[i]
        for s in range(n_sub):
            r0 = s * SUB_Q
            edge = None
            if s == 0 or s == n_sub - 1:
                pos = q0 + (r0 - R) + lax.broadcasted_iota(jnp.int32, (1, SUB_K), 1)
                edge = jnp.where(pos >= 0, jnp.where(pos < seq, 0.0, NEG_INF), NEG_INF)
            m_tile = s_tile = None
            for h in range(HEADS_PER_GROUP):
                c0 = h * HEAD_DIM_A
                q = q_ref[i, r0:r0 + SUB_Q, c0:c0 + HEAD_DIM_A]
                k = kbuf[i, r0:r0 + SUB_K, c0:c0 + HEAD_DIM_A]
                v = vbuf[i, r0:r0 + SUB_K, c0:c0 + HEAD_DIM_A]
                logits = lax.dot_general(q, k, (((1,), (1,)), ((), ())), preferred_element_type=f32) + bias_ref[h]
                if edge is not None:
                    logits = logits + edge
                m = jnp.max(logits, axis=-1, keepdims=True)
                p = jnp.exp(logits - m)
                ssum = jnp.sum(p, axis=-1, keepdims=True)
                o = jnp.dot(p.astype(bf16), v, preferred_element_type=f32) * (1.0 / ssum)
                o_ref[i, r0:r0 + SUB_Q, c0:c0 + HEAD_DIM_A] = o.astype(o_ref.dtype)
                m_tile = m if m_tile is None else jnp.where(lane >= h * LSE_REP, m, m_tile)
                s_tile = ssum if s_tile is None else jnp.where(lane >= h * LSE_REP, ssum, s_tile)
            lse_ref[i, r0:r0 + SUB_Q, :] = m_tile + jnp.log(s_tile)


def _attn_group(qkv, rel_bias, g):
    _, d = ATTN_GROUPS[g]
    n, L, _ = qkv.shape
    tq = min(ATTN_ROWS_PER_STEP, L)
    ns = ATTN_ROWS_PER_STEP // tq
    R = ATTN_RADIUS
    bias = _band_bias(rel_bias[:, g * HEADS_PER_GROUP:(g + 1) * HEADS_PER_GROUP], d)
    rb = tq // R
    last_rb = L // R - 1

    def main(col, width=GROUP_WIDTH):
        return pl.BlockSpec((ns, tq, width), lambda b, t: (b, t, col))

    def prev(col):
        return pl.BlockSpec((ns, R, GROUP_WIDTH), lambda b, t: (b, jnp.maximum(t * rb - 1, 0), col))

    def nxt(col):
        return pl.BlockSpec((ns, R, GROUP_WIDTH), lambda b, t: (b, jnp.minimum((t + 1) * rb, last_rb), col))

    return pl.pallas_call(
        functools.partial(_attn_kernel, tq=tq, seq=L, n_seq=ns),
        grid=(n // ns, L // tq),
        in_specs=[main(0), prev(1), main(1), nxt(1), prev(2), main(2), nxt(2),
                  pl.BlockSpec((HEADS_PER_GROUP, SUB_Q, SUB_K), lambda b, t: (0, 0, 0))],
        out_specs=[main(0), main(0, LSE_LANES)],
        out_shape=[jax.ShapeDtypeStruct((n, L, GROUP_WIDTH), bf16),
                   jax.ShapeDtypeStruct((n, L, LSE_LANES), f32)],
        scratch_shapes=[pltpu.VMEM((ns, tq + 2 * R, GROUP_WIDTH), bf16),
                        pltpu.VMEM((ns, tq + 2 * R, GROUP_WIDTH), bf16)],
        compiler_params=_cparams("parallel", "arbitrary"),
        name=f"attn_g{g}",
    )(qkv, qkv, qkv, qkv, qkv, qkv, qkv, bias)


LRU_CHUNK = 256
LRU_PAD = 8
LRU_FINISH_ROWS = 512
GATE_BIAS_ROWS = 3
LRU_SEGS = 8
SEG_GAP = 4


def _lru_kernel(xb_ref, yb_ref, cw_ref, cb_ref, w_ref, lam_ref, o_ref,
                xpad, af, bf, ab, bb, htf, ptf, htb, ptb, cf_scr, cb_scr, *, seq):
    R = LRU_CHUNK
    P = LRU_PAD
    seg_len = seq // LRU_SEGS
    pitch = seg_len + SEG_GAP
    chunks_per_seg = seg_len // R
    n_chunks = seq // R
    xpad[0:P] = jnp.zeros((P, LRU_BW), f32)
    xpad[P + seq:] = jnp.zeros((P, LRU_BW), f32)
    xpad[P:P + seq] = xb_ref[0].astype(f32)
    lam = lam_ref[...]
    log_a_unit = -LRU_C * (jnp.maximum(-lam, 0.0) + jnp.log1p(jnp.exp(-jnp.abs(lam))))
    cw = cw_ref[...]
    cb = cb_ref[...]
    row = lax.broadcasted_iota(jnp.int32, (R, LRU_BW), 0)
    lane = lax.broadcasted_iota(jnp.int32, (R, LRU_BW), 1)
    bias_cols = jnp.where(lane < GATE_BIAS_ROWS, 1.0, 0.0).astype(bf16)

    def chunk(ci, first=False, last=False):
        c0 = ci * R
        dst = (ci // chunks_per_seg) * pitch + (ci % chunks_per_seg) * R
        xc = (cw[0:1] * xpad[pl.ds(c0 + (P - 1), R), :] + cw[1:2] * xpad[pl.ds(c0 + P, R), :]
              + cw[2:3] * xpad[pl.ds(c0 + (P + 1), R), :] + cw[3:4] * xpad[pl.ds(c0 + (P + 2), R), :]) + cb
        lhs = jnp.concatenate([xc.astype(bf16), bias_cols], axis=1)
        th = jnp.tanh(jnp.dot(lhs, w_ref[0], preferred_element_type=f32))
        half_xc = 0.5 * xc
        for direction, (a_scr, b_scr) in enumerate(((af, bf), (ab, bb))):
            base = direction * 2 * LRU_BW
            half_log2_a = (0.5 * math.log2(math.e)) * log_a_unit[direction:direction + 1]
            a = jnp.exp2(half_log2_a * th[:, base:base + LRU_BW] + half_log2_a)
            gated_x = half_xc * th[:, base + LRU_BW:base + 2 * LRU_BW] + half_xc
            y = 1.0 - a * a
            mult = y * lax.rsqrt(jnp.maximum(y, 1e-30))
            if direction == 0 and first:
                mult = jnp.where(row == 0, 1.0, mult)
            if direction == 1 and last:
                mult = jnp.where(row == R - 1, 1.0, mult)
            a_scr[pl.ds(dst, R), :] = a
            b_scr[pl.ds(dst, R), :] = mult * gated_x

    for ci in range(n_chunks):
        chunk(ci, first=ci == 0, last=ci == n_chunks - 1)

    def scan(i, carry):
        hf, pf, hb, pb = carry
        rows = pl.ds(i, LRU_SEGS, stride=pitch)
        a = af[rows, :]
        hf = a * hf + bf[rows, :]
        pf = a * pf
        htf[rows, :] = hf
        ptf[rows, :] = pf
        rows = pl.ds(seg_len - 1 - i, LRU_SEGS, stride=pitch)
        a = ab[rows, :]
        hb = a * hb + bb[rows, :]
        pb = a * pb
        htb[rows, :] = hb
        ptb[rows, :] = pb
        return hf, pf, hb, pb

    zero = jnp.zeros((LRU_SEGS, LRU_BW), f32)
    one = jnp.ones((LRU_SEGS, LRU_BW), f32)
    hf, pf, hb, pb = lax.fori_loop(0, seg_len, scan, (zero, one, zero, one), unroll=8)

    c = jnp.zeros((1, LRU_BW), f32)
    cf_scr[0:1] = c
    for j in range(1, LRU_SEGS):
        c = hf[j - 1:j] + pf[j - 1:j] * c
        cf_scr[j:j + 1] = c
    c = jnp.zeros((1, LRU_BW), f32)
    cb_scr[LRU_SEGS - 1:LRU_SEGS] = c
    for j in range(LRU_SEGS - 2, -1, -1):
        c = hb[j + 1:j + 2] + pb[j + 1:j + 2] * c
        cb_scr[j:j + 1] = c

    F = LRU_FINISH_ROWS
    finish_per_seg = seg_len // F

    for ci in range(seq // F):
        c0 = ci * F
        seg = ci // finish_per_seg
        rows = pl.ds(seg * pitch + (ci % finish_per_seg) * F, F)
        h = (htf[rows, :] + ptf[rows, :] * cf_scr[seg:seg + 1, :]
             + htb[rows, :] + ptb[rows, :] * cb_scr[seg:seg + 1, :])
        o_ref[0, c0:c0 + F, :] = (h * yb_ref[0, c0:c0 + F, :].astype(f32)).astype(o_ref.dtype)


def _pack_lru_gates(w, b):
    rows, rest = [], b
    for _ in range(GATE_BIAS_ROWS):
        piece = rest.astype(bf16)
        rows.append(piece)
        rest = rest - piece.astype(f32)
    bias_rows = jnp.pad(jnp.stack(rows, axis=1), ((0, 0), (0, LRU_BW - GATE_BIAS_ROWS), (0, 0)))
    return jnp.concatenate([w.astype(bf16), bias_rows], axis=1)


def _lru(proj3, conv_w, conv_b, w_gates, lam):
    B, S, _ = proj3.shape
    xb0 = 0
    yb0 = REST_YB // LRU_BW
    return pl.pallas_call(
        functools.partial(_lru_kernel, seq=S),
        grid=(B, LRU_BLOCKS),
        in_specs=[
            pl.BlockSpec((1, S, LRU_BW), lambda b, n: (b, 0, xb0 + n)),
            pl.BlockSpec((1, S, LRU_BW), lambda b, n: (b, 0, yb0 + n)),
            pl.BlockSpec((4, LRU_BW), lambda b, n: (0, n)),
            pl.BlockSpec((1, LRU_BW), lambda b, n: (0, n)),
            pl.BlockSpec((1, 2 * LRU_BW, 4 * LRU_BW), lambda b, n: (n, 0, 0)),
            pl.BlockSpec((2, LRU_BW), lambda b, n: (0, n)),
        ],
        out_specs=pl.BlockSpec((1, S, LRU_BW), lambda b, n: (b, 0, n)),
        out_shape=jax.ShapeDtypeStruct((B, S, LRU_WIDTH), bf16),
        scratch_shapes=([pltpu.VMEM((S + 2 * LRU_PAD, LRU_BW), f32)]
                        + [pltpu.VMEM((S + LRU_SEGS * SEG_GAP, LRU_BW), f32)] * 8
                        + [pltpu.VMEM((LRU_SEGS, LRU_BW), f32)] * 2),
        compiler_params=_cparams("parallel", "parallel"),
        name="lru",
    )(proj3, proj3, conv_w, conv_b, w_gates, lam)


def _mem_kv_kernel(m_ref, g_ref, w_ref, o_ref, h_scr):
    @pl.when(pl.program_id(0) == 0)
    def _():
        h_scr[...] = _rms(m_ref[...], g_ref[...]).astype(bf16)

    o_ref[...] = jnp.dot(h_scr[...], w_ref[...], preferred_element_type=f32).astype(o_ref.dtype)


def _mem_kv(mem2, gain, w, tn=512):
    M = mem2.shape[0]
    N = w.shape[1]
    return pl.pallas_call(
        _mem_kv_kernel,
        grid=(N // tn,),
        in_specs=[pl.BlockSpec((M, D_MODEL), lambda j: (0, 0)),
                  pl.BlockSpec((1, D_MODEL), lambda j: (0, 0)),
                  pl.BlockSpec((D_MODEL, tn), lambda j: (0, j))],
        out_specs=pl.BlockSpec((M, tn), lambda j: (0, j)),
        out_shape=jax.ShapeDtypeStruct((M, N), bf16),
        scratch_shapes=[pltpu.VMEM((M, D_MODEL), bf16)],
        compiler_params=_cparams("arbitrary"),
        name="mem_kv",
    )(mem2, gain, w)


def _xattn_kernel(q0_ref, q1_ref, q2_ref, q3_ref, kv_ref, o_ref):
    for h, q_ref in enumerate((q0_ref, q1_ref, q2_ref, q3_ref)):
        c0 = h * MEM_HEAD_DIM
        k = kv_ref[0, :, c0:c0 + MEM_HEAD_DIM]
        v = kv_ref[0, :, MEM_WIDTH + c0:MEM_WIDTH + c0 + MEM_HEAD_DIM]
        logits = lax.dot_general(q_ref[0], k, (((1,), (1,)), ((), ())), preferred_element_type=f32)
        m = jnp.max(logits, axis=-1, keepdims=True)
        p = jnp.exp(logits - m)
        ssum = jnp.sum(p, axis=-1, keepdims=True)
        o = jnp.dot(p.astype(bf16), v, preferred_element_type=f32) * (1.0 / ssum)
        o_ref[0, :, c0:c0 + MEM_HEAD_DIM] = o.astype(o_ref.dtype)


def _xattn(proj3, kv3, tq=1024):
    B, S, _ = proj3.shape
    qb0 = REST_QC // MEM_HEAD_DIM

    def qspec(h):
        return pl.BlockSpec((1, tq, MEM_HEAD_DIM), lambda b, t: (b, t, qb0 + h))

    return pl.pallas_call(
        _xattn_kernel,
        grid=(B, S // tq),
        in_specs=[qspec(0), qspec(1), qspec(2), qspec(3),
                  pl.BlockSpec((1, N_MEM, 2 * MEM_WIDTH), lambda b, t: (b, 0, 0))],
        out_specs=pl.BlockSpec((1, tq, MEM_WIDTH), lambda b, t: (b, t, 0)),
        out_shape=jax.ShapeDtypeStruct((B, S, MEM_WIDTH), bf16),
        compiler_params=_cparams("parallel", "parallel"),
        name="xattn",
    )(proj3, proj3, proj3, proj3, kv3)


def _combine_kernel(o0_ref, o1_ref, o2_ref, l0_ref, l1_ref, l2_ref, ya_ref,
                    o1_scr, o2_scr, l1_scr, l2_scr, tmp_scr, *, tm):
    step = DEINTERLEAVE_STEP
    for g, o_ref, l_ref, o_scr, l_scr in ((1, o1_ref, l1_ref, o1_scr, l1_scr),
                                          (2, o2_ref, l2_ref, o2_scr, l2_scr)):
        d = ATTN_GROUPS[g][1]
        slabs = [(l_scr, lambda r: l_ref[0, r])]
        slabs += [(o_scr.at[h], lambda r, h=h: o_ref[0, r, :, h * HEAD_DIM_A:(h + 1) * HEAD_DIM_A].astype(f32))
                  for h in range(HEADS_PER_GROUP)]
        for k, (dst, rows_of) in enumerate(slabs):
            if d == step:
                for r in range(d):
                    dst[pl.ds(r, tm // d, stride=d), :] = rows_of(r)
                continue
            tmp = tmp_scr.at[k]
            for r in range(d):
                tmp[pl.ds((r % step) * (tm // step) + r // step, tm // d, stride=step), :] = rows_of(r)
            for lo in range(step):
                dst[pl.ds(lo, tm // step, stride=step), :] = tmp[lo * (tm // step):(lo + 1) * (tm // step), :]
    l0, l1, l2 = l0_ref[...], l1_scr[...], l2_scr[...]
    m = jnp.maximum(jnp.maximum(l0, l1), l2)
    e0, e1, e2 = jnp.exp(l0 - m), jnp.exp(l1 - m), jnp.exp(l2 - m)
    inv = 1.0 / (e0 + e1 + e2)
    for h in range(HEADS_PER_GROUP):
        c0 = h * HEAD_DIM_A
        lane = slice(h * LSE_REP, h * LSE_REP + 1)
        y = ((e0 * inv)[:, lane] * o0_ref[:, c0:c0 + HEAD_DIM_A].astype(f32)
             + (e1 * inv)[:, lane] * o1_scr[h] + (e2 * inv)[:, lane] * o2_scr[h])
        ya_ref[:, c0:c0 + HEAD_DIM_A] = y.astype(bf16)


def _combine(o_groups, lse_groups, seq, tm=512):
    T = o_groups[0].shape[0]
    nt = seq // tm
    d1, d2 = ATTN_GROUPS[1][1], ATTN_GROUPS[2][1]

    def rows(width):
        return pl.BlockSpec((tm, width), lambda i: (i, 0))

    def strided_rows(d, width):
        return pl.BlockSpec((1, d, tm // d, width), lambda i: (i // nt, 0, i % nt, 0))

    return pl.pallas_call(
        functools.partial(_combine_kernel, tm=tm),
        grid=(T // tm,),
        in_specs=[rows(GROUP_WIDTH), strided_rows(d1, GROUP_WIDTH), strided_rows(d2, GROUP_WIDTH),
                  rows(LSE_LANES), strided_rows(d1, LSE_LANES), strided_rows(d2, LSE_LANES)],
        out_specs=rows(GROUP_WIDTH),
        out_shape=jax.ShapeDtypeStruct((T, GROUP_WIDTH), bf16),
        scratch_shapes=[pltpu.VMEM((HEADS_PER_GROUP, tm, HEAD_DIM_A), f32),
                        pltpu.VMEM((HEADS_PER_GROUP, tm, HEAD_DIM_A), f32),
                        pltpu.VMEM((tm, LSE_LANES), f32), pltpu.VMEM((tm, LSE_LANES), f32),
                        pltpu.VMEM((1 + HEADS_PER_GROUP, tm, LANES), f32)],
        compiler_params=_cparams("parallel"),
        name="combine",
    )(*o_groups, *lse_groups)


def _gate_mix_kernel(h_ref, ya_ref, yl_ref, yc_ref, wga_ref, wgb_ref, wgc_ref, bga_ref, bgb_ref, bgc_ref,
                     woa_ref, wol_ref, wom_ref, mix_ref):
    h = h_ref[...]

    def gate(w_ref, b_ref):
        return jax.nn.sigmoid(jnp.dot(h, w_ref[...], preferred_element_type=f32) + b_ref[...])

    mixed = (gate(wga_ref, bga_ref) * jnp.dot(ya_ref[...], woa_ref[...], preferred_element_type=f32)
             + gate(wgb_ref, bgb_ref) * jnp.dot(yl_ref[...], wol_ref[...], preferred_element_type=f32)
             + gate(wgc_ref, bgc_ref) * jnp.dot(yc_ref[...], wom_ref[...], preferred_element_type=f32))
    mix_ref[...] = mixed.astype(mix_ref.dtype)


def _gate_mix(h, y_a, y_lru, y_c, w_gate, b_gate, w_o_attn, w_o_lru, w_o_mem, tm=512, tn=512):
    T = h.shape[0]
    nj = D_MODEL // tn

    def rows(width):
        return pl.BlockSpec((tm, width), lambda j, i: (i, 0))

    def gate_w(k):
        return pl.BlockSpec((D_MODEL, tn), lambda j, i: (0, k * nj + j))

    def gate_b(k):
        return pl.BlockSpec((1, tn), lambda j, i: (0, k * nj + j))

    def cols(width):
        return pl.BlockSpec((width, tn), lambda j, i: (0, j))

    return pl.pallas_call(
        _gate_mix_kernel,
        grid=(nj, T // tm),
        in_specs=[rows(D_MODEL), rows(GROUP_WIDTH), rows(LRU_WIDTH), rows(MEM_WIDTH),
                  gate_w(0), gate_w(1), gate_w(2), gate_b(0), gate_b(1), gate_b(2),
                  cols(GROUP_WIDTH), cols(LRU_WIDTH), cols(MEM_WIDTH)],
        out_specs=pl.BlockSpec((tm, tn), lambda j, i: (i, j)),
        out_shape=jax.ShapeDtypeStruct((T, D_MODEL), bf16),
        compiler_params=_cparams("arbitrary", "arbitrary"),
        name="gate_mix",
    )(h, y_a, y_lru, y_c, w_gate, w_gate, w_gate, b_gate, b_gate, b_gate, w_o_attn, w_o_lru, w_o_mem)


def _mlp_kernel(x_ref, mix_ref, wo_ref, g_ref, gf_ref, wu_ref, wd_ref, out_ref, h_scr):
    j = pl.program_id(1)

    @pl.when(j == 0)
    def _():
        x = x_ref[...] + jnp.dot(mix_ref[...], wo_ref[...], preferred_element_type=f32)
        h_scr[...] = _rms(x, g_ref[...]).astype(bf16)
        out_ref[...] = x

    u = jnp.maximum(jnp.dot(h_scr[...], wu_ref[...], preferred_element_type=f32), 0.0)
    out_ref[...] += jnp.dot((u * u).astype(bf16), wd_ref[...], preferred_element_type=f32)

    @pl.when(j == pl.num_programs(1) - 1)
    def _():
        out_ref[...] = _rms(out_ref[...], gf_ref[...])


def _mlp(x2, mixed, w_out, gain, gain_final, w_up, w_down, tm=512, tf=1024):
    T = x2.shape[0]
    return pl.pallas_call(
        _mlp_kernel,
        grid=(T // tm, D_FF // tf),
        in_specs=[pl.BlockSpec((tm, D_MODEL), lambda i, j: (i, 0)),
                  pl.BlockSpec((tm, D_MODEL), lambda i, j: (i, 0)),
                  pl.BlockSpec((D_MODEL, D_MODEL), lambda i, j: (0, 0)),
                  pl.BlockSpec((1, D_MODEL), lambda i, j: (0, 0)),
                  pl.BlockSpec((1, D_MODEL), lambda i, j: (0, 0)),
                  pl.BlockSpec((D_MODEL, tf), lambda i, j: (0, j)),
                  pl.BlockSpec((tf, D_MODEL), lambda i, j: (j, 0))],
        out_specs=pl.BlockSpec((tm, D_MODEL), lambda i, j: (i, 0)),
        out_shape=jax.ShapeDtypeStruct((T, D_MODEL), f32),
        scratch_shapes=[pltpu.VMEM((tm, D_MODEL), bf16)],
        compiler_params=_cparams("parallel", "arbitrary"),
        name="mlp",
    )(x2, mixed, w_out, gain, gain_final, w_up, w_down)


def _query_scale():
    scale = np.ones((1, N_IN), np.float32)
    scale[:, :WIDTH_A] = 1.0 / math.sqrt(HEAD_DIM_A)
    scale[:, COL_QC:] = 1.0 / math.sqrt(MEM_HEAD_DIM)
    return scale


def kernel(x, mem, rel_bias, norm_mix, norm_mem, norm_mlp, norm_final, w_in, w_gate, b_gate, conv_w, conv_b,
           lru_wa, lru_ba, lru_wi, lru_bi, lru_lambda, w_mem_kv, w_o_attn, w_o_lru, w_o_mem, w_out, w_up, w_down):
    B, S, D = x.shape
    T = B * S
    depth = w_in.shape[0]
    assert depth == 1, "the final RMSNorm is fused into the (single) layer's MLP kernel"
    x2 = x.reshape(T, D)
    mem2 = mem.reshape(B * N_MEM, D)
    for l in range(depth):
        w_qkv = (w_in[l] * _query_scale()).astype(bf16)
        h, rest, w_up_bf, w_down_bf = _rest_proj(x2, norm_mix[l].reshape(1, D), w_qkv, w_up[l], w_down[l])
        proj3 = rest.reshape(B, S, REST_W)

        side_casts = ((w_gate[l],), (w_out[l], w_o_lru[l]), (w_o_attn[l], w_o_mem[l], w_mem_kv[l]))
        casted = []
        attn = []
        for g in range(len(ATTN_GROUPS)):
            d = ATTN_GROUPS[g][1]
            qkv, *bf_copies = _qkv_proj(h, w_qkv, g, B, S, cast=side_casts[g])
            casted.append(bf_copies)
            o, lse = _attn_group(qkv.reshape(B * d, S // d, QKV_W), rel_bias, g)
            if g == 0:
                attn.append((o.reshape(T, GROUP_WIDTH), lse.reshape(T, LSE_LANES)))
            else:
                attn.append((o.reshape(B, d, S // d, GROUP_WIDTH), lse.reshape(B, d, S // d, LSE_LANES)))

        w_gates = 0.5 * jnp.concatenate([lru_wa[l, 0], lru_wi[l, 0], lru_wa[l, 1], lru_wi[l, 1]], axis=-1)
        b_gates = 0.5 * jnp.concatenate([lru_ba[l, 0], lru_bi[l, 0], lru_ba[l, 1], lru_bi[l, 1]], axis=-1)
        y_lru = _lru(proj3, conv_w[l], conv_b[l].reshape(1, LRU_WIDTH), _pack_lru_gates(w_gates, b_gates),
                     lru_lambda[l])

        (w_gate_bf,), (w_out_bf, w_o_lru_bf), (w_o_attn_bf, w_o_mem_bf, w_mem_kv_bf) = casted
        kv = _mem_kv(mem2, norm_mem[l].reshape(1, D), w_mem_kv_bf)
        y_c = _xattn(proj3, kv.reshape(B, N_MEM, 2 * MEM_WIDTH))

        y_a = _combine([a[0] for a in attn], [a[1] for a in attn], S)
        mixed = _gate_mix(h, y_a, y_lru.reshape(T, LRU_WIDTH), y_c.reshape(T, MEM_WIDTH),
                          w_gate_bf, b_gate[l].reshape(1, 3 * D), w_o_attn_bf, w_o_lru_bf, w_o_mem_bf)
        x2 = _mlp(x2, mixed, w_out_bf, norm_mlp[l].reshape(1, D), norm_final.reshape(1, D),
                  w_up_bf, w_down_bf)
    return x2.reshape(B, S, D)
```

```python
import functools
import math

import jax
import jax.numpy as jnp
import numpy as np
from jax import lax
from jax.experimental import pallas as pl
from jax.experimental.pallas import tpu as pltpu

D_MODEL = 2048
HEAD_DIM_A = 128
ATTN_GROUPS = ((128, 1), (512, 4), (2048, 16))
HEADS_PER_GROUP = 4
GROUP_WIDTH = HEADS_PER_GROUP * HEAD_DIM_A
WIDTH_A = len(ATTN_GROUPS) * GROUP_WIDTH
ATTN_RADIUS = 64
N_BUCKETS = 32
MAX_DISTANCE = 1024
LRU_WIDTH = 1536
LRU_BLOCKS = 12
LRU_BW = 128
LRU_C = 8.0
N_MEM = 256
MEM_HEADS = 4
MEM_HEAD_DIM = 256
MEM_WIDTH = MEM_HEADS * MEM_HEAD_DIM
D_FF = 4 * D_MODEL
EPS = 1e-6
N_IN = 3 * WIDTH_A + 2 * LRU_WIDTH + MEM_WIDTH
COL_K = WIDTH_A
COL_V = 2 * WIDTH_A
COL_XB = 3 * WIDTH_A
COL_YB = 3 * WIDTH_A + LRU_WIDTH
COL_QC = 3 * WIDTH_A + 2 * LRU_WIDTH
NEG_INF = -1e30

ATTN_ROWS_PER_STEP = 1024
SUB_Q = 128
SUB_K = SUB_Q + 2 * ATTN_RADIUS
LSE_LANES = 128
LSE_REP = LSE_LANES // HEADS_PER_GROUP

VMEM_LIMIT = 56 * 1024 * 1024

f32 = jnp.float32
bf16 = jnp.bfloat16


def _cparams(*sem):
    return pltpu.CompilerParams(dimension_semantics=sem, vmem_limit_bytes=VMEM_LIMIT)


def _rms(x, gain):
    return x * lax.rsqrt(jnp.mean(x * x, axis=-1, keepdims=True) + EPS) * gain


QKV_W = 3 * GROUP_WIDTH
REST_W = 2 * LRU_WIDTH + MEM_WIDTH
REST_YB = LRU_WIDTH
REST_QC = 2 * LRU_WIDTH
PROJ_TN = GROUP_WIDTH
LANES = 128
SLABS = PROJ_TN // LANES
PROJ_ROW_BLOCKS = 2
DEINTERLEAVE_STEP = 4
N_STAGE = 2


def _qkv_kernel(h_ref, wq_ref, wk_ref, wv_ref, *refs, d, tm, n_cast):
    cast_in, o_ref = refs[:n_cast], refs[n_cast]
    cast_out = refs[n_cast + 1:2 * n_cast + 1]
    res_scr, tmp_scr = refs[2 * n_cast + 1:]
    for src_ref, dst_ref in zip(cast_in, cast_out):
        dst_ref[...] = src_ref[...].astype(bf16)
    mb = tm // PROJ_ROW_BLOCKS
    n = 0
    for k in range(PROJ_ROW_BLOCKS):
        hk = h_ref[k * mb:(k + 1) * mb, :]
        for t, w_ref in enumerate((wq_ref, wk_ref, wv_ref)):
            res = jnp.dot(hk, w_ref[...], preferred_element_type=f32)
            col = t * PROJ_TN
            if d == 1:
                o_ref[0, k * mb:(k + 1) * mb, col:col + PROJ_TN] = res.astype(bf16)
                continue
            buf = n % N_STAGE
            n += 1
            for c in range(SLABS):
                res_scr[buf, c] = res[:, c * LANES:(c + 1) * LANES]
            src, step = res_scr, d
            if d == DEINTERLEAVE_STEP ** 2:
                step = DEINTERLEAVE_STEP
                for r in range(step):
                    for c in range(SLABS):
                        tmp_scr[buf, c, r * (mb // step):(r + 1) * (mb // step), :] = (
                            res_scr[buf, c, pl.ds(r, mb // step, stride=step), :])
                src = tmp_scr
            for r in range(d):
                start = r if src is res_scr else (r % step) * (mb // step) + r // step
                for c in range(SLABS):
                    o_ref[0, r, k * (mb // d):(k + 1) * (mb // d), col + c * LANES:col + (c + 1) * LANES] = (
                        src[buf, c, pl.ds(start, mb // d, stride=step), :].astype(bf16))


def _qkv_proj(h, w_qkv, g, batch, seq, cast=(), tm=1024):
    T = h.shape[0]
    d = ATTN_GROUPS[g][1]
    nt = seq // tm
    n_groups = len(ATTN_GROUPS)
    mb = tm // PROJ_ROW_BLOCKS
    steps = T // tm

    def w_spec(which):
        return pl.BlockSpec((D_MODEL, PROJ_TN), lambda i: (0, which * n_groups + g))

    cast_specs = [pl.BlockSpec((w.shape[0] // steps, w.shape[1]), lambda i: (i, 0)) for w in cast]

    if d == 1:
        out_spec = pl.BlockSpec((1, tm, QKV_W), lambda i: (i // nt, i % nt, 0))
        out_shape = jax.ShapeDtypeStruct((batch, seq, QKV_W), bf16)
    else:
        out_spec = pl.BlockSpec((1, d, tm // d, QKV_W), lambda i: (i // nt, 0, i % nt, 0))
        out_shape = jax.ShapeDtypeStruct((batch, d, seq // d, QKV_W), bf16)
    return pl.pallas_call(
        functools.partial(_qkv_kernel, d=d, tm=tm, n_cast=len(cast)),
        grid=(steps,),
        in_specs=[pl.BlockSpec((tm, D_MODEL), lambda i: (i, 0)), w_spec(0), w_spec(1), w_spec(2)] + cast_specs,
        out_specs=[out_spec] + cast_specs,
        out_shape=[out_shape] + [jax.ShapeDtypeStruct(w.shape, bf16) for w in cast],
        scratch_shapes=[pltpu.VMEM((N_STAGE, SLABS, mb, LANES), f32)] * 2,
        compiler_params=_cparams("parallel"),
        name=f"qkv_g{g}",
    )(h, w_qkv, w_qkv, w_qkv, *cast)


REST_TILES = REST_W // PROJ_TN
YB_TILES = range(REST_YB // PROJ_TN, REST_QC // PROJ_TN)


def _gelu_tanh(y):
    return y * (0.5 * (1.0 + jnp.tanh(math.sqrt(2.0 / math.pi) * (y + 0.044715 * (y * y * y)))))


def _rest_kernel(x_ref, g_ref, *refs):
    w_refs = refs[:REST_TILES]
    h_ref, o_ref = refs[REST_TILES:]
    mb = x_ref.shape[0] // PROJ_ROW_BLOCKS
    for k in range(PROJ_ROW_BLOCKS):
        rows = slice(k * mb, (k + 1) * mb)
        h = _rms(x_ref[rows, :], g_ref[...]).astype(bf16)
        h_ref[rows, :] = h
        for c, w_ref in enumerate(w_refs):
            res = jnp.dot(h, w_ref[...], preferred_element_type=f32)
            if c in YB_TILES:
                res = _gelu_tanh(res)
            o_ref[rows, c * PROJ_TN:(c + 1) * PROJ_TN] = res.astype(bf16)


def _rest_proj(x2, gain, w_in, tm=512):
    T = x2.shape[0]
    steps = T // tm
    first = COL_XB // PROJ_TN

    def w_spec(c):
        return pl.BlockSpec((D_MODEL, PROJ_TN), lambda i: (0, first + c), pipeline_mode=pl.Buffered(1))

    return pl.pallas_call(
        _rest_kernel,
        grid=(steps,),
        in_specs=[pl.BlockSpec((tm, D_MODEL), lambda i: (i, 0)),
                  pl.BlockSpec((1, D_MODEL), lambda i: (0, 0))]
        + [w_spec(c) for c in range(REST_TILES)],
        out_specs=[pl.BlockSpec((tm, D_MODEL), lambda i: (i, 0)),
                   pl.BlockSpec((tm, REST_W), lambda i: (i, 0))],
        out_shape=[jax.ShapeDtypeStruct((T, D_MODEL), bf16),
                   jax.ShapeDtypeStruct((T, REST_W), bf16)],
        compiler_params=_cparams("arbitrary"),
        name="rest_proj",
    )(x2, gain, *([w_in] * REST_TILES))


def _t5_bucket(rel):
    nb = N_BUCKETS // 2
    max_exact = nb // 2
    sign = (rel > 0).astype(np.int32) * nb
    n = np.abs(rel)
    large = max_exact + (np.log(np.maximum(n, 1) / max_exact)
                         / np.log(MAX_DISTANCE / max_exact) * (nb - max_exact)).astype(np.int32)
    large = np.minimum(large, nb - 1)
    return (sign + np.where(n < max_exact, n, large)).astype(np.int32)


def _band_bias(rel_bias_g, dilation):
    qq = np.arange(SUB_Q)[:, None]
    kk = np.arange(SUB_K)[None, :]
    rel = kk - ATTN_RADIUS - qq
    onehot = (_t5_bucket(rel * dilation)[None] == np.arange(N_BUCKETS)[:, None, None]).astype(np.float32)
    bias = jnp.einsum('nh,nqk->hqk', rel_bias_g.astype(f32), onehot, precision=lax.Precision.HIGHEST)
    return bias + np.where(np.abs(rel) <= ATTN_RADIUS, 0.0, NEG_INF).astype(np.float32)[None]


def _attn_kernel(q_ref, kp_ref, km_ref, kn_ref, vp_ref, vm_ref, vn_ref, bias_ref,
                 o_ref, lse_ref, kbuf, vbuf, *, tq, seq, n_seq):
    R = ATTN_RADIUS
    q0 = pl.program_id(1) * tq
    lane = lax.broadcasted_iota(jnp.int32, (SUB_Q, LSE_LANES), 1)
    n_sub = tq // SUB_Q
    for i in range(n_seq):
        kbuf[i, 0:R] = kp_ref[i]
        kbuf[i, R:R + tq] = km_ref[i]
        kbuf[i, R + tq:] = kn_ref[i]
        vbuf[i, 0:R] = vp_ref[i]
        vbuf[i, R:R + tq] = vm_ref[i]
        vbuf[i, R + tq:] = vn_ref[i]
        for s in range(n_sub):
            r0 = s * SUB_Q
            edge = None
            if s == 0 or s == n_sub - 1:
                pos = q0 + (r0 - R) + lax.broadcasted_iota(jnp.int32, (1, SUB_K), 1)
                edge = jnp.where(pos >= 0, jnp.where(pos < seq, 0.0, NEG_INF), NEG_INF)
            m_tile = s_tile = None
            for h in range(HEADS_PER_GROUP):
                c0 = h * HEAD_DIM_A
                q = q_ref[i, r0:r0 + SUB_Q, c0:c0 + HEAD_DIM_A]
                k = kbuf[i, r0:r0 + SUB_K, c0:c0 + HEAD_DIM_A]
                v = vbuf[i, r0:r0 + SUB_K, c0:c0 + HEAD_DIM_A]
                logits = lax.dot_general(q, k, (((1,), (1,)), ((), ())), preferred_element_type=f32) + bias_ref[h]
                if edge is not None:
                    logits = logits + edge
                m = jnp.max(logits, axis=-1, keepdims=True)
                p = jnp.exp(logits - m)
                ssum = jnp.sum(p, axis=-1, keepdims=True)
                o = jnp.dot(p.astype(bf16), v, preferred_element_type=f32) * (1.0 / ssum)
                o_ref[i, r0:r0 + SUB_Q, c0:c0 + HEAD_DIM_A] = o.astype(o_ref.dtype)
                m_tile = m if m_tile is None else jnp.where(lane >= h * LSE_REP, m, m_tile)
                s_tile = ssum if s_tile is None else jnp.where(lane >= h * LSE_REP, ssum, s_tile)
            lse_ref[i, r0:r0 + SUB_Q, :] = m_tile + jnp.log(s_tile)


def _attn_group(qkv, rel_bias, g):
    _, d = ATTN_GROUPS[g]
    n, L, _ = qkv.shape
    tq = min(ATTN_ROWS_PER_STEP, L)
    ns = ATTN_ROWS_PER_STEP // tq
    R = ATTN_RADIUS
    bias = _band_bias(rel_bias[:, g * HEADS_PER_GROUP:(g + 1) * HEADS_PER_GROUP], d)
    rb = tq // R
    last_rb = L // R - 1

    def main(col, width=GROUP_WIDTH):
        return pl.BlockSpec((ns, tq, width), lambda b, t: (b, t, col))

    def prev(col):
        return pl.BlockSpec((ns, R, GROUP_WIDTH), lambda b, t: (b, jnp.maximum(t * rb - 1, 0), col))

    def nxt(col):
        return pl.BlockSpec((ns, R, GROUP_WIDTH), lambda b, t: (b, jnp.minimum((t + 1) * rb, last_rb), col))

    return pl.pallas_call(
        functools.partial(_attn_kernel, tq=tq, seq=L, n_seq=ns),
        grid=(n // ns, L // tq),
        in_specs=[main(0), prev(1), main(1), nxt(1), prev(2), main(2), nxt(2),
                  pl.BlockSpec((HEADS_PER_GROUP, SUB_Q, SUB_K), lambda b, t: (0, 0, 0))],
        out_specs=[main(0), main(0, LSE_LANES)],
        out_shape=[jax.ShapeDtypeStruct((n, L, GROUP_WIDTH), bf16),
                   jax.ShapeDtypeStruct((n, L, LSE_LANES), f32)],
        scratch_shapes=[pltpu.VMEM((ns, tq + 2 * R, GROUP_WIDTH), bf16),
                        pltpu.VMEM((ns, tq + 2 * R, GROUP_WIDTH), bf16)],
        compiler_params=_cparams("parallel", "arbitrary"),
        name=f"attn_g{g}",
    )(qkv, qkv, qkv, qkv, qkv, qkv, qkv, bias)


LRU_CHUNK = 256
LRU_PAD = 8
LRU_FINISH_ROWS = 512
GATE_BIAS_ROWS = 3
LRU_SEGS = 8
SEG_GAP = 4


def _lru_kernel(xb_ref, yb_ref, cw_ref, cb_ref, w_ref, lam_ref, o_ref,
                xpad, af, bf, ab, bb, htf, ptf, htb, ptb, cf_scr, cb_scr, *, seq):
    R = LRU_CHUNK
    P = LRU_PAD
    seg_len = seq // LRU_SEGS
    pitch = seg_len + SEG_GAP
    chunks_per_seg = seg_len // R
    n_chunks = seq // R
    xpad[0:P] = jnp.zeros((P, LRU_BW), f32)
    xpad[P + seq:] = jnp.zeros((P, LRU_BW), f32)
    xpad[P:P + seq] = xb_ref[0].astype(f32)
    lam = lam_ref[...]
    log_a_unit = -LRU_C * (jnp.maximum(-lam, 0.0) + jnp.log1p(jnp.exp(-jnp.abs(lam))))
    cw = cw_ref[...]
    cb = cb_ref[...]
    row = lax.broadcasted_iota(jnp.int32, (R, LRU_BW), 0)
    lane = lax.broadcasted_iota(jnp.int32, (R, LRU_BW), 1)
    bias_cols = jnp.where(lane < GATE_BIAS_ROWS, 1.0, 0.0).astype(bf16)

    def chunk(ci, first=False, last=False):
        c0 = ci * R
        dst = (ci // chunks_per_seg) * pitch + (ci % chunks_per_seg) * R
        xc = (cw[0:1] * xpad[pl.ds(c0 + (P - 1), R), :] + cw[1:2] * xpad[pl.ds(c0 + P, R), :]
              + cw[2:3] * xpad[pl.ds(c0 + (P + 1), R), :] + cw[3:4] * xpad[pl.ds(c0 + (P + 2), R), :]) + cb
        lhs = jnp.concatenate([xc.astype(bf16), bias_cols], axis=1)
        th = jnp.tanh(jnp.dot(lhs, w_ref[0], preferred_element_type=f32))
        half_xc = 0.5 * xc
        for direction, (a_scr, b_scr) in enumerate(((af, bf), (ab, bb))):
            base = direction * 2 * LRU_BW
            half_log2_a = (0.5 * math.log2(math.e)) * log_a_unit[direction:direction + 1]
            a = jnp.exp2(half_log2_a * th[:, base:base + LRU_BW] + half_log2_a)
            gated_x = half_xc * th[:, base + LRU_BW:base + 2 * LRU_BW] + half_xc
            y = 1.0 - a * a
            mult = y * lax.rsqrt(jnp.maximum(y, 1e-30))
            if direction == 0 and first:
                mult = jnp.where(row == 0, 1.0, mult)
            if direction == 1 and last:
                mult = jnp.where(row == R - 1, 1.0, mult)
            a_scr[pl.ds(dst, R), :] = a
            b_scr[pl.ds(dst, R), :] = mult * gated_x

    for ci in range(n_chunks):
        chunk(ci, first=ci == 0, last=ci == n_chunks - 1)

    def scan(i, carry):
        hf, pf, hb, pb = carry
        rows = pl.ds(i, LRU_SEGS, stride=pitch)
        a = af[rows, :]
        hf = a * hf + bf[rows, :]
        pf = a * pf
        htf[rows, :] = hf
        ptf[rows, :] = pf
        rows = pl.ds(seg_len - 1 - i, LRU_SEGS, stride=pitch)
        a = ab[rows, :]
        hb = a * hb + bb[rows, :]
        pb = a * pb
        htb[rows, :] = hb
        ptb[rows, :] = pb
        return hf, pf, hb, pb

    zero = jnp.zeros((LRU_SEGS, LRU_BW), f32)
    one = jnp.ones((LRU_SEGS, LRU_BW), f32)
    hf, pf, hb, pb = lax.fori_loop(0, seg_len, scan, (zero, one, zero, one), unroll=8)

    c = jnp.zeros((1, LRU_BW), f32)
    cf_scr[0:1] = c
    for j in range(1, LRU_SEGS):
        c = hf[j - 1:j] + pf[j - 1:j] * c
        cf_scr[j:j + 1] = c
    c = jnp.zeros((1, LRU_BW), f32)
    cb_scr[LRU_SEGS - 1:LRU_SEGS] = c
    for j in range(LRU_SEGS - 2, -1, -1):
        c = hb[j + 1:j + 2] + pb[j + 1:j + 2] * c
        cb_scr[j:j + 1] = c

    F = LRU_FINISH_ROWS
    finish_per_seg = seg_len // F

    for ci in range(seq // F):
        c0 = ci * F
        seg = ci // finish_per_seg
        rows = pl.ds(seg * pitch + (ci % finish_per_seg) * F, F)
        h = (htf[rows, :] + ptf[rows, :] * cf_scr[seg:seg + 1, :]
             + htb[rows, :] + ptb[rows, :] * cb_scr[seg:seg + 1, :])
        o_ref[0, c0:c0 + F, :] = (h * yb_ref[0, c0:c0 + F, :].astype(f32)).astype(o_ref.dtype)


def _pack_lru_gates(w, b):
    rows, rest = [], b
    for _ in range(GATE_BIAS_ROWS):
        piece = rest.astype(bf16)
        rows.append(piece)
        rest = rest - piece.astype(f32)
    bias_rows = jnp.pad(jnp.stack(rows, axis=1), ((0, 0), (0, LRU_BW - GATE_BIAS_ROWS), (0, 0)))
    return jnp.concatenate([w.astype(bf16), bias_rows], axis=1)


def _lru(proj3, conv_w, conv_b, w_gates, lam):
    B, S, _ = proj3.shape
    xb0 = 0
    yb0 = REST_YB // LRU_BW
    return pl.pallas_call(
        functools.partial(_lru_kernel, seq=S),
        grid=(B, LRU_BLOCKS),
        in_specs=[
            pl.BlockSpec((1, S, LRU_BW), lambda b, n: (b, 0, xb0 + n)),
            pl.BlockSpec((1, S, LRU_BW), lambda b, n: (b, 0, yb0 + n)),
            pl.BlockSpec((4, LRU_BW), lambda b, n: (0, n)),
            pl.BlockSpec((1, LRU_BW), lambda b, n: (0, n)),
            pl.BlockSpec((1, 2 * LRU_BW, 4 * LRU_BW), lambda b, n: (n, 0, 0)),
            pl.BlockSpec((2, LRU_BW), lambda b, n: (0, n)),
        ],
        out_specs=pl.BlockSpec((1, S, LRU_BW), lambda b, n: (b, 0, n)),
        out_shape=jax.ShapeDtypeStruct((B, S, LRU_WIDTH), bf16),
        scratch_shapes=([pltpu.VMEM((S + 2 * LRU_PAD, LRU_BW), f32)]
                        + [pltpu.VMEM((S + LRU_SEGS * SEG_GAP, LRU_BW), f32)] * 8
                        + [pltpu.VMEM((LRU_SEGS, LRU_BW), f32)] * 2),
        compiler_params=_cparams("parallel", "parallel"),
        name="lru",
    )(proj3, proj3, conv_w, conv_b, w_gates, lam)


def _mem_kv_kernel(m_ref, g_ref, w_ref, o_ref, h_scr):
    @pl.when(pl.program_id(0) == 0)
    def _():
        h_scr[...] = _rms(m_ref[...], g_ref[...]).astype(bf16)

    o_ref[...] = jnp.dot(h_scr[...], w_ref[...], preferred_element_type=f32).astype(o_ref.dtype)


def _mem_kv(mem2, gain, w, tn=512):
    M = mem2.shape[0]
    N = w.shape[1]
    return pl.pallas_call(
        _mem_kv_kernel,
        grid=(N // tn,),
        in_specs=[pl.BlockSpec((M, D_MODEL), lambda j: (0, 0)),
                  pl.BlockSpec((1, D_MODEL), lambda j: (0, 0)),
                  pl.BlockSpec((D_MODEL, tn), lambda j: (0, j))],
        out_specs=pl.BlockSpec((M, tn), lambda j: (0, j)),
        out_shape=jax.ShapeDtypeStruct((M, N), bf16),
        scratch_shapes=[pltpu.VMEM((M, D_MODEL), bf16)],
        compiler_params=_cparams("arbitrary"),
        name="mem_kv",
    )(mem2, gain, w)


def _xattn_kernel(q0_ref, q1_ref, q2_ref, q3_ref, kv_ref, o_ref):
    for h, q_ref in enumerate((q0_ref, q1_ref, q2_ref, q3_ref)):
        c0 = h * MEM_HEAD_DIM
        k = kv_ref[0, :, c0:c0 + MEM_HEAD_DIM]
        v = kv_ref[0, :, MEM_WIDTH + c0:MEM_WIDTH + c0 + MEM_HEAD_DIM]
        logits = lax.dot_general(q_ref[0], k, (((1,), (1,)), ((), ())), preferred_element_type=f32)
        m = jnp.max(logits, axis=-1, keepdims=True)
        p = jnp.exp(logits - m)
        ssum = jnp.sum(p, axis=-1, keepdims=True)
        o = jnp.dot(p.astype(bf16), v, preferred_element_type=f32) * (1.0 / ssum)
        o_ref[0, :, c0:c0 + MEM_HEAD_DIM] = o.astype(o_ref.dtype)


def _xattn(proj3, kv3, tq=1024):
    B, S, _ = proj3.shape
    qb0 = REST_QC // MEM_HEAD_DIM

    def qspec(h):
        return pl.BlockSpec((1, tq, MEM_HEAD_DIM), lambda b, t: (b, t, qb0 + h))

    return pl.pallas_call(
        _xattn_kernel,
        grid=(B, S // tq),
        in_specs=[qspec(0), qspec(1), qspec(2), qspec(3),
                  pl.BlockSpec((1, N_MEM, 2 * MEM_WIDTH), lambda b, t: (b, 0, 0))],
        out_specs=pl.BlockSpec((1, tq, MEM_WIDTH), lambda b, t: (b, t, 0)),
        out_shape=jax.ShapeDtypeStruct((B, S, MEM_WIDTH), bf16),
        compiler_params=_cparams("parallel", "parallel"),
        name="xattn",
    )(proj3, proj3, proj3, proj3, kv3)


def _combine_kernel(o0_ref, o1_ref, o2_ref, l0_ref, l1_ref, l2_ref, ya_ref,
                    o1_scr, o2_scr, l1_scr, l2_scr, tmp_scr, *, tm):
    step = DEINTERLEAVE_STEP
    for g, o_ref, l_ref, o_scr, l_scr in ((1, o1_ref, l1_ref, o1_scr, l1_scr),
                                          (2, o2_ref, l2_ref, o2_scr, l2_scr)):
        d = ATTN_GROUPS[g][1]
        slabs = [(l_scr, lambda r: l_ref[0, r])]
        slabs += [(o_scr.at[h], lambda r, h=h: o_ref[0, r, :, h * HEAD_DIM_A:(h + 1) * HEAD_DIM_A].astype(f32))
                  for h in range(HEADS_PER_GROUP)]
        for k, (dst, rows_of) in enumerate(slabs):
            if d == step:
                for r in range(d):
                    dst[pl.ds(r, tm // d, stride=d), :] = rows_of(r)
                continue
            tmp = tmp_scr.at[k]
            for r in range(d):
                tmp[pl.ds((r % step) * (tm // step) + r // step, tm // d, stride=step), :] = rows_of(r)
            for lo in range(step):
                dst[pl.ds(lo, tm // step, stride=step), :] = tmp[lo * (tm // step):(lo + 1) * (tm // step), :]
    l0, l1, l2 = l0_ref[...], l1_scr[...], l2_scr[...]
    m = jnp.maximum(jnp.maximum(l0, l1), l2)
    e0, e1, e2 = jnp.exp(l0 - m), jnp.exp(l1 - m), jnp.exp(l2 - m)
    inv = 1.0 / (e0 + e1 + e2)
    for h in range(HEADS_PER_GROUP):
        c0 = h * HEAD_DIM_A
        lane = slice(h * LSE_REP, h * LSE_REP + 1)
        y = ((e0 * inv)[:, lane] * o0_ref[:, c0:c0 + HEAD_DIM_A].astype(f32)
             + (e1 * inv)[:, lane] * o1_scr[h] + (e2 * inv)[:, lane] * o2_scr[h])
        ya_ref[:, c0:c0 + HEAD_DIM_A] = y.astype(bf16)


def _combine(o_groups, lse_groups, seq, tm=512):
    T = o_groups[0].shape[0]
    nt = seq // tm
    d1, d2 = ATTN_GROUPS[1][1], ATTN_GROUPS[2][1]

    def rows(width):
        return pl.BlockSpec((tm, width), lambda i: (i, 0))

    def strided_rows(d, width):
        return pl.BlockSpec((1, d, tm // d, width), lambda i: (i // nt, 0, i % nt, 0))

    return pl.pallas_call(
        functools.partial(_combine_kernel, tm=tm),
        grid=(T // tm,),
        in_specs=[rows(GROUP_WIDTH), strided_rows(d1, GROUP_WIDTH), strided_rows(d2, GROUP_WIDTH),
                  rows(LSE_LANES), strided_rows(d1, LSE_LANES), strided_rows(d2, LSE_LANES)],
        out_specs=rows(GROUP_WIDTH),
        out_shape=jax.ShapeDtypeStruct((T, GROUP_WIDTH), bf16),
        scratch_shapes=[pltpu.VMEM((HEADS_PER_GROUP, tm, HEAD_DIM_A), f32),
                        pltpu.VMEM((HEADS_PER_GROUP, tm, HEAD_DIM_A), f32),
                        pltpu.VMEM((tm, LSE_LANES), f32), pltpu.VMEM((tm, LSE_LANES), f32),
                        pltpu.VMEM((1 + HEADS_PER_GROUP, tm, LANES), f32)],
        compiler_params=_cparams("parallel"),
        name="combine",
    )(*o_groups, *lse_groups)


def _gate_mix_kernel(h_ref, ya_ref, yl_ref, yc_ref, wga_ref, wgb_ref, wgc_ref, bga_ref, bgb_ref, bgc_ref,
                     woa_ref, wol_ref, wom_ref, wu_ref, wd_ref, mix_ref, wu_o_ref, wd_o_ref):
    wu_o_ref[...] = wu_ref[...].astype(bf16)
    wd_o_ref[...] = wd_ref[...].astype(bf16)
    h = h_ref[...]

    def gate(w_ref, b_ref):
        return jax.nn.sigmoid(jnp.dot(h, w_ref[...], preferred_element_type=f32) + b_ref[...])

    mixed = (gate(wga_ref, bga_ref) * jnp.dot(ya_ref[...], woa_ref[...], preferred_element_type=f32)
             + gate(wgb_ref, bgb_ref) * jnp.dot(yl_ref[...], wol_ref[...], preferred_element_type=f32)
             + gate(wgc_ref, bgc_ref) * jnp.dot(yc_ref[...], wom_ref[...], preferred_element_type=f32))
    mix_ref[...] = mixed.astype(mix_ref.dtype)


def _gate_mix(h, y_a, y_lru, y_c, w_gate, b_gate, w_o_attn, w_o_lru, w_o_mem, w_up, w_down, tm=512, tn=512):
    T = h.shape[0]
    nj = D_MODEL // tn
    ni = T // tm
    up_rows, down_rows = D_MODEL // (nj * ni), D_FF // (nj * ni)

    def slab(rows_per_step, width):
        return pl.BlockSpec((rows_per_step, width), lambda j, i: (j * ni + i, 0))

    def rows(width):
        return pl.BlockSpec((tm, width), lambda j, i: (i, 0))

    def gate_w(k):
        return pl.BlockSpec((D_MODEL, tn), lambda j, i: (0, k * nj + j))

    def gate_b(k):
        return pl.BlockSpec((1, tn), lambda j, i: (0, k * nj + j))

    def cols(width):
        return pl.BlockSpec((width, tn), lambda j, i: (0, j))

    return pl.pallas_call(
        _gate_mix_kernel,
        grid=(nj, T // tm),
        in_specs=[rows(D_MODEL), rows(GROUP_WIDTH), rows(LRU_WIDTH), rows(MEM_WIDTH),
                  gate_w(0), gate_w(1), gate_w(2), gate_b(0), gate_b(1), gate_b(2),
                  cols(GROUP_WIDTH), cols(LRU_WIDTH), cols(MEM_WIDTH),
                  slab(up_rows, D_FF), slab(down_rows, D_MODEL)],
        out_specs=[pl.BlockSpec((tm, tn), lambda j, i: (i, j)), slab(up_rows, D_FF), slab(down_rows, D_MODEL)],
        out_shape=[jax.ShapeDtypeStruct((T, D_MODEL), bf16),
                   jax.ShapeDtypeStruct((D_MODEL, D_FF), bf16),
                   jax.ShapeDtypeStruct((D_FF, D_MODEL), bf16)],
        compiler_params=_cparams("arbitrary", "arbitrary"),
        name="gate_mix",
    )(h, y_a, y_lru, y_c, w_gate, w_gate, w_gate, b_gate, b_gate, b_gate, w_o_attn, w_o_lru, w_o_mem,
      w_up, w_down)


def _mlp_kernel(x_ref, mix_ref, wo_ref, g_ref, gf_ref, wu_ref, wd_ref, out_ref, h_scr):
    j = pl.program_id(1)

    @pl.when(j == 0)
    def _():
        x = x_ref[...] + jnp.dot(mix_ref[...], wo_ref[...], preferred_element_type=f32)
        h_scr[...] = _rms(x, g_ref[...]).astype(bf16)
        out_ref[...] = x

    u = jnp.maximum(jnp.dot(h_scr[...], wu_ref[...], preferred_element_type=f32), 0.0)
    out_ref[...] += jnp.dot((u * u).astype(bf16), wd_ref[...], preferred_element_type=f32)

    @pl.when(j == pl.num_programs(1) - 1)
    def _():
        out_ref[...] = _rms(out_ref[...], gf_ref[...])


def _mlp(x2, mixed, w_out, gain, gain_final, w_up, w_down, tm=512, tf=1024):
    T = x2.shape[0]
    return pl.pallas_call(
        _mlp_kernel,
        grid=(T // tm, D_FF // tf),
        in_specs=[pl.BlockSpec((tm, D_MODEL), lambda i, j: (i, 0)),
                  pl.BlockSpec((tm, D_MODEL), lambda i, j: (i, 0)),
                  pl.BlockSpec((D_MODEL, D_MODEL), lambda i, j: (0, 0)),
                  pl.BlockSpec((1, D_MODEL), lambda i, j: (0, 0)),
                  pl.BlockSpec((1, D_MODEL), lambda i, j: (0, 0)),
                  pl.BlockSpec((D_MODEL, tf), lambda i, j: (0, j)),
                  pl.BlockSpec((tf, D_MODEL), lambda i, j: (j, 0))],
        out_specs=pl.BlockSpec((tm, D_MODEL), lambda i, j: (i, 0)),
        out_shape=jax.ShapeDtypeStruct((T, D_MODEL), f32),
        scratch_shapes=[pltpu.VMEM((tm, D_MODEL), bf16)],
        compiler_params=_cparams("parallel", "arbitrary"),
        name="mlp",
    )(x2, mixed, w_out, gain, gain_final, w_up, w_down)


def _query_scale():
    scale = np.ones((1, N_IN), np.float32)
    scale[:, :WIDTH_A] = 1.0 / math.sqrt(HEAD_DIM_A)
    scale[:, COL_QC:] = 1.0 / math.sqrt(MEM_HEAD_DIM)
    return scale


def kernel(x, mem, rel_bias, norm_mix, norm_mem, norm_mlp, norm_final, w_in, w_gate, b_gate, conv_w, conv_b,
           lru_wa, lru_ba, lru_wi, lru_bi, lru_lambda, w_mem_kv, w_o_attn, w_o_lru, w_o_mem, w_out, w_up, w_down):
    B, S, D = x.shape
    T = B * S
    depth = w_in.shape[0]
    assert depth == 1, "the final RMSNorm is fused into the (single) layer's MLP kernel"
    x2 = x.reshape(T, D)
    mem2 = mem.reshape(B * N_MEM, D)
    for l in range(depth):
        w_qkv = (w_in[l] * _query_scale()).astype(bf16)
        h, rest = _rest_proj(x2, norm_mix[l].reshape(1, D), w_qkv)
        proj3 = rest.reshape(B, S, REST_W)

        side_casts = ((w_gate[l],), (w_out[l], w_o_lru[l]), (w_o_attn[l], w_o_mem[l], w_mem_kv[l]))
        casted = []
        attn = []
        for g in range(len(ATTN_GROUPS)):
            d = ATTN_GROUPS[g][1]
            qkv, *bf_copies = _qkv_proj(h, w_qkv, g, B, S, cast=side_casts[g])
            casted.append(bf_copies)
            o, lse = _attn_group(qkv.reshape(B * d, S // d, QKV_W), rel_bias, g)
            if g == 0:
                attn.append((o.reshape(T, GROUP_WIDTH), lse.reshape(T, LSE_LANES)))
            else:
                attn.append((o.reshape(B, d, S // d, GROUP_WIDTH), lse.reshape(B, d, S // d, LSE_LANES)))

        w_gates = 0.5 * jnp.concatenate([lru_wa[l, 0], lru_wi[l, 0], lru_wa[l, 1], lru_wi[l, 1]], axis=-1)
        b_gates = 0.5 * jnp.concatenate([lru_ba[l, 0], lru_bi[l, 0], lru_ba[l, 1], lru_bi[l, 1]], axis=-1)
        y_lru = _lru(proj3, conv_w[l], conv_b[l].reshape(1, LRU_WIDTH), _pack_lru_gates(w_gates, b_gates),
                     lru_lambda[l])

        (w_gate_bf,), (w_out_bf, w_o_lru_bf), (w_o_attn_bf, w_o_mem_bf, w_mem_kv_bf) = casted
        kv = _mem_kv(mem2, norm_mem[l].reshape(1, D), w_mem_kv_bf)
        y_c = _xattn(proj3, kv.reshape(B, N_MEM, 2 * MEM_WIDTH))

        y_a = _combine([a[0] for a in attn], [a[1] for a in attn], S)
        mixed, w_up_bf, w_down_bf = _gate_mix(
            h, y_a, y_lru.reshape(T, LRU_WIDTH), y_c.reshape(T, MEM_WIDTH), w_gate_bf, b_gate[l].reshape(1, 3 * D),
            w_o_attn_bf, w_o_lru_bf, w_o_mem_bf, w_up[l], w_down[l])
        x2 = _mlp(x2, mixed, w_out_bf, norm_mlp[l].reshape(1, D), norm_final.reshape(1, D),
                  w_up_bf, w_down_bf)
    return x2.reshape(B, S, D)
```

```python
import functools
import math

import jax
import jax.numpy as jnp
import numpy as np
from jax import lax
from jax.experimental import pallas as pl
from jax.experimental.pallas import tpu as pltpu

D_MODEL = 2048
HEAD_DIM_A = 128
ATTN_GROUPS = ((128, 1), (512, 4), (2048, 16))
HEADS_PER_GROUP = 4
GROUP_WIDTH = HEADS_PER_GROUP * HEAD_DIM_A
WIDTH_A = len(ATTN_GROUPS) * GROUP_WIDTH
ATTN_RADIUS = 64
N_BUCKETS = 32
MAX_DISTANCE = 1024
LRU_WIDTH = 1536
LRU_BLOCKS = 12
LRU_BW = 128
LRU_C = 8.0
N_MEM = 256
MEM_HEADS = 4
MEM_HEAD_DIM = 256
MEM_WIDTH = MEM_HEADS * MEM_HEAD_DIM
D_FF = 4 * D_MODEL
EPS = 1e-6
N_IN = 3 * WIDTH_A + 2 * LRU_WIDTH + MEM_WIDTH
COL_K = WIDTH_A
COL_V = 2 * WIDTH_A
COL_XB = 3 * WIDTH_A
COL_YB = 3 * WIDTH_A + LRU_WIDTH
COL_QC = 3 * WIDTH_A + 2 * LRU_WIDTH
NEG_INF = -1e30

ATTN_ROWS_PER_STEP = 1024
SUB_Q = 128
SUB_K = SUB_Q + 2 * ATTN_RADIUS
LSE_LANES = 128
LSE_REP = LSE_LANES // HEADS_PER_GROUP

VMEM_LIMIT = 56 * 1024 * 1024

f32 = jnp.float32
bf16 = jnp.bfloat16


def _cparams(*sem):
    return pltpu.CompilerParams(dimension_semantics=sem, vmem_limit_bytes=VMEM_LIMIT)


def _rms(x, gain):
    return x * lax.rsqrt(jnp.mean(x * x, axis=-1, keepdims=True) + EPS) * gain


QKV_W = 3 * GROUP_WIDTH
REST_W = 2 * LRU_WIDTH + MEM_WIDTH
REST_YB = LRU_WIDTH
REST_QC = 2 * LRU_WIDTH
PROJ_TN = GROUP_WIDTH
LANES = 128
SLABS = PROJ_TN // LANES
PROJ_ROW_BLOCKS = 2
DEINTERLEAVE_STEP = 4
N_STAGE = 2


def _qkv_kernel(h_ref, wq_ref, wk_ref, wv_ref, *refs, d, tm, n_cast):
    cast_in, o_ref = refs[:n_cast], refs[n_cast]
    cast_out = refs[n_cast + 1:2 * n_cast + 1]
    res_scr, tmp_scr = refs[2 * n_cast + 1:]
    for src_ref, dst_ref in zip(cast_in, cast_out):
        dst_ref[...] = src_ref[...].astype(bf16)
    mb = tm // PROJ_ROW_BLOCKS
    n = 0
    for k in range(PROJ_ROW_BLOCKS):
        hk = h_ref[k * mb:(k + 1) * mb, :]
        for t, w_ref in enumerate((wq_ref, wk_ref, wv_ref)):
            res = jnp.dot(hk, w_ref[...], preferred_element_type=f32)
            col = t * PROJ_TN
            if d == 1:
                o_ref[0, k * mb:(k + 1) * mb, col:col + PROJ_TN] = res.astype(bf16)
                continue
            buf = n % N_STAGE
            n += 1
            for c in range(SLABS):
                res_scr[buf, c] = res[:, c * LANES:(c + 1) * LANES]
            src, step = res_scr, d
            if d == DEINTERLEAVE_STEP ** 2:
                step = DEINTERLEAVE_STEP
                for r in range(step):
                    for c in range(SLABS):
                        tmp_scr[buf, c, r * (mb // step):(r + 1) * (mb // step), :] = (
                            res_scr[buf, c, pl.ds(r, mb // step, stride=step), :])
                src = tmp_scr
            for r in range(d):
                start = r if src is res_scr else (r % step) * (mb // step) + r // step
                for c in range(SLABS):
                    o_ref[0, r, k * (mb // d):(k + 1) * (mb // d), col + c * LANES:col + (c + 1) * LANES] = (
                        src[buf, c, pl.ds(start, mb // d, stride=step), :].astype(bf16))


def _qkv_proj(h, w_qkv, g, batch, seq, cast=(), tm=1024):
    T = h.shape[0]
    d = ATTN_GROUPS[g][1]
    nt = seq // tm
    n_groups = len(ATTN_GROUPS)
    mb = tm // PROJ_ROW_BLOCKS
    steps = T // tm

    def w_spec(which):
        return pl.BlockSpec((D_MODEL, PROJ_TN), lambda i: (0, which * n_groups + g))

    cast_specs = [pl.BlockSpec((w.shape[0] // steps, w.shape[1]), lambda i: (i, 0)) for w in cast]

    if d == 1:
        out_spec = pl.BlockSpec((1, tm, QKV_W), lambda i: (i // nt, i % nt, 0))
        out_shape = jax.ShapeDtypeStruct((batch, seq, QKV_W), bf16)
    else:
        out_spec = pl.BlockSpec((1, d, tm // d, QKV_W), lambda i: (i // nt, 0, i % nt, 0))
        out_shape = jax.ShapeDtypeStruct((batch, d, seq // d, QKV_W), bf16)
    return pl.pallas_call(
        functools.partial(_qkv_kernel, d=d, tm=tm, n_cast=len(cast)),
        grid=(steps,),
        in_specs=[pl.BlockSpec((tm, D_MODEL), lambda i: (i, 0)), w_spec(0), w_spec(1), w_spec(2)] + cast_specs,
        out_specs=[out_spec] + cast_specs,
        out_shape=[out_shape] + [jax.ShapeDtypeStruct(w.shape, bf16) for w in cast],
        scratch_shapes=[pltpu.VMEM((N_STAGE, SLABS, mb, LANES), f32)] * 2,
        compiler_params=_cparams("parallel"),
        name=f"qkv_g{g}",
    )(h, w_qkv, w_qkv, w_qkv, *cast)


REST_TILES = REST_W // PROJ_TN
YB_TILES = range(REST_YB // PROJ_TN, REST_QC // PROJ_TN)


def _gelu_tanh(y):
    return y * (0.5 * (1.0 + jnp.tanh(math.sqrt(2.0 / math.pi) * (y + 0.044715 * (y * y * y)))))


def _rest_kernel(x_ref, g_ref, *refs):
    w_refs = refs[:REST_TILES]
    h_ref, o_ref = refs[REST_TILES:]
    mb = x_ref.shape[0] // PROJ_ROW_BLOCKS
    for k in range(PROJ_ROW_BLOCKS):
        rows = slice(k * mb, (k + 1) * mb)
        h = _rms(x_ref[rows, :], g_ref[...]).astype(bf16)
        h_ref[rows, :] = h
        for c, w_ref in enumerate(w_refs):
            res = jnp.dot(h, w_ref[...], preferred_element_type=f32)
            if c in YB_TILES:
                res = _gelu_tanh(res)
            o_ref[rows, c * PROJ_TN:(c + 1) * PROJ_TN] = res.astype(bf16)


def _rest_proj(x2, gain, w_in, tm=512):
    T = x2.shape[0]
    steps = T // tm
    first = COL_XB // PROJ_TN

    def w_spec(c):
        return pl.BlockSpec((D_MODEL, PROJ_TN), lambda i: (0, first + c), pipeline_mode=pl.Buffered(1))

    return pl.pallas_call(
        _rest_kernel,
        grid=(steps,),
        in_specs=[pl.BlockSpec((tm, D_MODEL), lambda i: (i, 0)),
                  pl.BlockSpec((1, D_MODEL), lambda i: (0, 0))]
        + [w_spec(c) for c in range(REST_TILES)],
        out_specs=[pl.BlockSpec((tm, D_MODEL), lambda i: (i, 0)),
                   pl.BlockSpec((tm, REST_W), lambda i: (i, 0))],
        out_shape=[jax.ShapeDtypeStruct((T, D_MODEL), bf16),
                   jax.ShapeDtypeStruct((T, REST_W), bf16)],
        compiler_params=_cparams("arbitrary"),
        name="rest_proj",
    )(x2, gain, *([w_in] * REST_TILES))


def _t5_bucket(rel):
    nb = N_BUCKETS // 2
    max_exact = nb // 2
    sign = (rel > 0).astype(np.int32) * nb
    n = np.abs(rel)
    large = max_exact + (np.log(np.maximum(n, 1) / max_exact)
                         / np.log(MAX_DISTANCE / max_exact) * (nb - max_exact)).astype(np.int32)
    large = np.minimum(large, nb - 1)
    return (sign + np.where(n < max_exact, n, large)).astype(np.int32)


def _band_bias(rel_bias_g, dilation):
    qq = np.arange(SUB_Q)[:, None]
    kk = np.arange(SUB_K)[None, :]
    rel = kk - ATTN_RADIUS - qq
    onehot = (_t5_bucket(rel * dilation)[None] == np.arange(N_BUCKETS)[:, None, None]).astype(np.float32)
    bias = jnp.einsum('nh,nqk->hqk', rel_bias_g.astype(f32), onehot, precision=lax.Precision.HIGHEST)
    return bias + np.where(np.abs(rel) <= ATTN_RADIUS, 0.0, NEG_INF).astype(np.float32)[None]


def _attn_kernel(q_ref, kp_ref, km_ref, kn_ref, vp_ref, vm_ref, vn_ref, bias_ref,
                 o_ref, lse_ref, kbuf, vbuf, *, tq, seq, n_seq):
    R = ATTN_RADIUS
    q0 = pl.program_id(1) * tq
    lane = lax.broadcasted_iota(jnp.int32, (SUB_Q, LSE_LANES), 1)
    n_sub = tq // SUB_Q
    for i in range(n_seq):
        kbuf[i, 0:R] = kp_ref[i]
        kbuf[i, R:R + tq] = km_ref[i]
        kbuf[i, R + tq:] = kn_ref[i]
        vbuf[i, 0:R] = vp_ref[i]
        vbuf[i, R:R + tq] = vm_ref[i]
        vbuf[i, R + tq:] = vn_ref[i]
        for s in range(n_sub):
            r0 = s * SUB_Q
            edge = None
            if s == 0 or s == n_sub - 1:
                pos = q0 + (r0 - R) + lax.broadcasted_iota(jnp.int32, (1, SUB_K), 1)
                edge = jnp.where(pos >= 0, jnp.where(pos < seq, 0.0, NEG_INF), NEG_INF)
            m_tile = s_tile = None
            for h in range(HEADS_PER_GROUP):
                c0 = h * HEAD_DIM_A
                q = q_ref[i, r0:r0 + SUB_Q, c0:c0 + HEAD_DIM_A]
                k = kbuf[i, r0:r0 + SUB_K, c0:c0 + HEAD_DIM_A]
                v = vbuf[i, r0:r0 + SUB_K, c0:c0 + HEAD_DIM_A]
                logits = lax.dot_general(q, k, (((1,), (1,)), ((), ())), preferred_element_type=f32) + bias_ref[h]
                if edge is not None:
                    logits = logits + edge
                m = jnp.max(logits, axis=-1, keepdims=True)
                p = jnp.exp(logits - m)
                ssum = jnp.sum(p, axis=-1, keepdims=True)
                o = jnp.dot(p.astype(bf16), v, preferred_element_type=f32) * (1.0 / ssum)
                o_ref[i, r0:r0 + SUB_Q, c0:c0 + HEAD_DIM_A] = o.astype(o_ref.dtype)
                m_tile = m if m_tile is None else jnp.where(lane >= h * LSE_REP, m, m_tile)
                s_tile = ssum if s_tile is None else jnp.where(lane >= h * LSE_REP, ssum, s_tile)
            lse_ref[i, r0:r0 + SUB_Q, :] = m_tile + jnp.log(s_tile)


def _attn_group(qkv, rel_bias, g):
    _, d = ATTN_GROUPS[g]
    n, L, _ = qkv.shape
    tq = min(ATTN_ROWS_PER_STEP, L)
    ns = ATTN_ROWS_PER_STEP // tq
    R = ATTN_RADIUS
    bias = _band_bias(rel_bias[:, g * HEADS_PER_GROUP:(g + 1) * HEADS_PER_GROUP], d)
    rb = tq // R
    last_rb = L // R - 1

    def main(col, width=GROUP_WIDTH):
        return pl.BlockSpec((ns, tq, width), lambda b, t: (b, t, col))

    def prev(col):
        return pl.BlockSpec((ns, R, GROUP_WIDTH), lambda b, t: (b, jnp.maximum(t * rb - 1, 0), col))

    def nxt(col):
        return pl.BlockSpec((ns, R, GROUP_WIDTH), lambda b, t: (b, jnp.minimum((t + 1) * rb, last_rb), col))

    return pl.pallas_call(
        functools.partial(_attn_kernel, tq=tq, seq=L, n_seq=ns),
        grid=(n // ns, L // tq),
        in_specs=[main(0), prev(1), main(1), nxt(1), prev(2), main(2), nxt(2),
                  pl.BlockSpec((HEADS_PER_GROUP, SUB_Q, SUB_K), lambda b, t: (0, 0, 0))],
        out_specs=[main(0), main(0, LSE_LANES)],
        out_shape=[jax.ShapeDtypeStruct((n, L, GROUP_WIDTH), bf16),
                   jax.ShapeDtypeStruct((n, L, LSE_LANES), f32)],
        scratch_shapes=[pltpu.VMEM((ns, tq + 2 * R, GROUP_WIDTH), bf16),
                        pltpu.VMEM((ns, tq + 2 * R, GROUP_WIDTH), bf16)],
        compiler_params=_cparams("parallel", "arbitrary"),
        name=f"attn_g{g}",
    )(qkv, qkv, qkv, qkv, qkv, qkv, qkv, bias)


LRU_CHUNK = 256
LRU_PAD = 8
LRU_FINISH_ROWS = 512
GATE_BIAS_ROWS = 3
LRU_SEGS = 8
SEG_GAP = 4


def _lru_kernel(xb_ref, yb_ref, cw_ref, cb_ref, w_ref, lam_ref, o_ref,
                xpad, af, bf, ab, bb, htf, ptf, htb, ptb, cf_scr, cb_scr, *, seq):
    R = LRU_CHUNK
    P = LRU_PAD
    seg_len = seq // LRU_SEGS
    pitch = seg_len + SEG_GAP
    chunks_per_seg = seg_len // R
    n_chunks = seq // R
    xpad[0:P] = jnp.zeros((P, LRU_BW), f32)
    xpad[P + seq:] = jnp.zeros((P, LRU_BW), f32)
    xpad[P:P + seq] = xb_ref[0].astype(f32)
    lam = lam_ref[...]
    log_a_unit = -LRU_C * (jnp.maximum(-lam, 0.0) + jnp.log1p(jnp.exp(-jnp.abs(lam))))
    cw = cw_ref[...]
    cb = cb_ref[...]
    row = lax.broadcasted_iota(jnp.int32, (R, LRU_BW), 0)
    lane = lax.broadcasted_iota(jnp.int32, (R, LRU_BW), 1)
    bias_cols = jnp.where(lane < GATE_BIAS_ROWS, 1.0, 0.0).astype(bf16)

    def chunk(ci, first=False, last=False):
        c0 = ci * R
        dst = (ci // chunks_per_seg) * pitch + (ci % chunks_per_seg) * R
        xc = (cw[0:1] * xpad[pl.ds(c0 + (P - 1), R), :] + cw[1:2] * xpad[pl.ds(c0 + P, R), :]
              + cw[2:3] * xpad[pl.ds(c0 + (P + 1), R), :] + cw[3:4] * xpad[pl.ds(c0 + (P + 2), R), :]) + cb
        lhs = jnp.concatenate([xc.astype(bf16), bias_cols], axis=1)
        th = jnp.tanh(jnp.dot(lhs, w_ref[0], preferred_element_type=f32))
        half_xc = 0.5 * xc
        for direction, (a_scr, b_scr) in enumerate(((af, bf), (ab, bb))):
            base = direction * 2 * LRU_BW
            half_log2_a = (0.5 * math.log2(math.e)) * log_a_unit[direction:direction + 1]
            a = jnp.exp2(half_log2_a * th[:, base:base + LRU_BW] + half_log2_a)
            gated_x = half_xc * th[:, base + LRU_BW:base + 2 * LRU_BW] + half_xc
            y = 1.0 - a * a
            mult = y * lax.rsqrt(jnp.maximum(y, 1e-30))
            if direction == 0 and first:
                mult = jnp.where(row == 0, 1.0, mult)
            if direction == 1 and last:
                mult = jnp.where(row == R - 1, 1.0, mult)
            a_scr[pl.ds(dst, R), :] = a
            b_scr[pl.ds(dst, R), :] = mult * gated_x

    for ci in range(n_chunks):
        chunk(ci, first=ci == 0, last=ci == n_chunks - 1)

    def scan(i, carry):
        hf, pf, hb, pb = carry
        rows = pl.ds(i, LRU_SEGS, stride=pitch)
        a = af[rows, :]
        hf = a * hf + bf[rows, :]
        pf = a * pf
        htf[rows, :] = hf
        ptf[rows, :] = pf
        rows = pl.ds(seg_len - 1 - i, LRU_SEGS, stride=pitch)
        a = ab[rows, :]
        hb = a * hb + bb[rows, :]
        pb = a * pb
        htb[rows, :] = hb
        ptb[rows, :] = pb
        return hf, pf, hb, pb

    zero = jnp.zeros((LRU_SEGS, LRU_BW), f32)
    one = jnp.ones((LRU_SEGS, LRU_BW), f32)
    hf, pf, hb, pb = lax.fori_loop(0, seg_len, scan, (zero, one, zero, one), unroll=8)

    c = jnp.zeros((1, LRU_BW), f32)
    cf_scr[0:1] = c
    for j in range(1, LRU_SEGS):
        c = hf[j - 1:j] + pf[j - 1:j] * c
        cf_scr[j:j + 1] = c
    c = jnp.zeros((1, LRU_BW), f32)
    cb_scr[LRU_SEGS - 1:LRU_SEGS] = c
    for j in range(LRU_SEGS - 2, -1, -1):
        c = hb[j + 1:j + 2] + pb[j + 1:j + 2] * c
        cb_scr[j:j + 1] = c

    F = LRU_FINISH_ROWS
    finish_per_seg = seg_len // F

    for ci in range(seq // F):
        c0 = ci * F
        seg = ci // finish_per_seg
        rows = pl.ds(seg * pitch + (ci % finish_per_seg) * F, F)
        h = (htf[rows, :] + ptf[rows, :] * cf_scr[seg:seg + 1, :]
             + htb[rows, :] + ptb[rows, :] * cb_scr[seg:seg + 1, :])
        o_ref[0, c0:c0 + F, :] = (h * yb_ref[0, c0:c0 + F, :].astype(f32)).astype(o_ref.dtype)


def _pack_lru_gates(w, b):
    rows, rest = [], b
    for _ in range(GATE_BIAS_ROWS):
        piece = rest.astype(bf16)
        rows.append(piece)
        rest = rest - piece.astype(f32)
    bias_rows = jnp.pad(jnp.stack(rows, axis=1), ((0, 0), (0, LRU_BW - GATE_BIAS_ROWS), (0, 0)))
    return jnp.concatenate([w.astype(bf16), bias_rows], axis=1)


def _lru(proj3, conv_w, conv_b, w_gates, lam):
    B, S, _ = proj3.shape
    xb0 = 0
    yb0 = REST_YB // LRU_BW
    return pl.pallas_call(
        functools.partial(_lru_kernel, seq=S),
        grid=(B, LRU_BLOCKS),
        in_specs=[
            pl.BlockSpec((1, S, LRU_BW), lambda b, n: (b, 0, xb0 + n)),
            pl.BlockSpec((1, S, LRU_BW), lambda b, n: (b, 0, yb0 + n)),
            pl.BlockSpec((4, LRU_BW), lambda b, n: (0, n)),
            pl.BlockSpec((1, LRU_BW), lambda b, n: (0, n)),
            pl.BlockSpec((1, 2 * LRU_BW, 4 * LRU_BW), lambda b, n: (n, 0, 0)),
            pl.BlockSpec((2, LRU_BW), lambda b, n: (0, n)),
        ],
        out_specs=pl.BlockSpec((1, S, LRU_BW), lambda b, n: (b, 0, n)),
        out_shape=jax.ShapeDtypeStruct((B, S, LRU_WIDTH), bf16),
        scratch_shapes=([pltpu.VMEM((S + 2 * LRU_PAD, LRU_BW), f32)]
                        + [pltpu.VMEM((S + LRU_SEGS * SEG_GAP, LRU_BW), f32)] * 8
                        + [pltpu.VMEM((LRU_SEGS, LRU_BW), f32)] * 2),
        compiler_params=_cparams("parallel", "parallel"),
        name="lru",
    )(proj3, proj3, conv_w, conv_b, w_gates, lam)


def _mem_kv_kernel(m_ref, g_ref, w_ref, o_ref, h_scr):
    @pl.when(pl.program_id(0) == 0)
    def _():
        h_scr[...] = _rms(m_ref[...], g_ref[...]).astype(bf16)

    o_ref[...] = jnp.dot(h_scr[...], w_ref[...], preferred_element_type=f32).astype(o_ref.dtype)


def _mem_kv(mem2, gain, w, tn=512):
    M = mem2.shape[0]
    N = w.shape[1]
    return pl.pallas_call(
        _mem_kv_kernel,
        grid=(N // tn,),
        in_specs=[pl.BlockSpec((M, D_MODEL), lambda j: (0, 0)),
                  pl.BlockSpec((1, D_MODEL), lambda j: (0, 0)),
                  pl.BlockSpec((D_MODEL, tn), lambda j: (0, j))],
        out_specs=pl.BlockSpec((M, tn), lambda j: (0, j)),
        out_shape=jax.ShapeDtypeStruct((M, N), bf16),
        scratch_shapes=[pltpu.VMEM((M, D_MODEL), bf16)],
        compiler_params=_cparams("arbitrary"),
        name="mem_kv",
    )(mem2, gain, w)


def _xattn_kernel(q0_ref, q1_ref, q2_ref, q3_ref, kv_ref, o_ref):
    for h, q_ref in enumerate((q0_ref, q1_ref, q2_ref, q3_ref)):
        c0 = h * MEM_HEAD_DIM
        k = kv_ref[0, :, c0:c0 + MEM_HEAD_DIM]
        v = kv_ref[0, :, MEM_WIDTH + c0:MEM_WIDTH + c0 + MEM_HEAD_DIM]
        logits = lax.dot_general(q_ref[0], k, (((1,), (1,)), ((), ())), preferred_element_type=f32)
        m = jnp.max(logits, axis=-1, keepdims=True)
        p = jnp.exp(logits - m)
        ssum = jnp.sum(p, axis=-1, keepdims=True)
        o = jnp.dot(p.astype(bf16), v, preferred_element_type=f32) * (1.0 / ssum)
        o_ref[0, :, c0:c0 + MEM_HEAD_DIM] = o.astype(o_ref.dtype)


def _xattn(proj3, kv3, tq=1024):
    B, S, _ = proj3.shape
    qb0 = REST_QC // MEM_HEAD_DIM

    def qspec(h):
        return pl.BlockSpec((1, tq, MEM_HEAD_DIM), lambda b, t: (b, t, qb0 + h))

    return pl.pallas_call(
        _xattn_kernel,
        grid=(B, S // tq),
        in_specs=[qspec(0), qspec(1), qspec(2), qspec(3),
                  pl.BlockSpec((1, N_MEM, 2 * MEM_WIDTH), lambda b, t: (b, 0, 0))],
        out_specs=pl.BlockSpec((1, tq, MEM_WIDTH), lambda b, t: (b, t, 0)),
        out_shape=jax.ShapeDtypeStruct((B, S, MEM_WIDTH), bf16),
        compiler_params=_cparams("parallel", "parallel"),
        name="xattn",
    )(proj3, proj3, proj3, proj3, kv3)


def _combine_kernel(o0_ref, o1_ref, o2_ref, l0_ref, l1_ref, l2_ref, ya_ref,
                    o1_scr, o2_scr, l1_scr, l2_scr, tmp_scr, *, tm):
    step = DEINTERLEAVE_STEP
    for g, o_ref, l_ref, o_scr, l_scr in ((1, o1_ref, l1_ref, o1_scr, l1_scr),
                                          (2, o2_ref, l2_ref, o2_scr, l2_scr)):
        d = ATTN_GROUPS[g][1]
        slabs = [(l_scr, lambda r: l_ref[0, r])]
        slabs += [(o_scr.at[h], lambda r, h=h: o_ref[0, r, :, h * HEAD_DIM_A:(h + 1) * HEAD_DIM_A].astype(f32))
                  for h in range(HEADS_PER_GROUP)]
        for k, (dst, rows_of) in enumerate(slabs):
            if d == step:
                for r in range(d):
                    dst[pl.ds(r, tm // d, stride=d), :] = rows_of(r)
                continue
            tmp = tmp_scr.at[k]
            for r in range(d):
                tmp[pl.ds((r % step) * (tm // step) + r // step, tm // d, stride=step), :] = rows_of(r)
            for lo in range(step):
                dst[pl.ds(lo, tm // step, stride=step), :] = tmp[lo * (tm // step):(lo + 1) * (tm // step), :]
    l0, l1, l2 = l0_ref[...], l1_scr[...], l2_scr[...]
    m = jnp.maximum(jnp.maximum(l0, l1), l2)
    e0, e1, e2 = jnp.exp(l0 - m), jnp.exp(l1 - m), jnp.exp(l2 - m)
    inv = 1.0 / (e0 + e1 + e2)
    for h in range(HEADS_PER_GROUP):
        c0 = h * HEAD_DIM_A
        lane = slice(h * LSE_REP, h * LSE_REP + 1)
        y = ((e0 * inv)[:, lane] * o0_ref[:, c0:c0 + HEAD_DIM_A].astype(f32)
             + (e1 * inv)[:, lane] * o1_scr[h] + (e2 * inv)[:, lane] * o2_scr[h])
        ya_ref[:, c0:c0 + HEAD_DIM_A] = y.astype(bf16)


def _combine(o_groups, lse_groups, seq, tm=512):
    T = o_groups[0].shape[0]
    nt = seq // tm
    d1, d2 = ATTN_GROUPS[1][1], ATTN_GROUPS[2][1]

    def rows(width):
        return pl.BlockSpec((tm, width), lambda i: (i, 0))

    def strided_rows(d, width):
        return pl.BlockSpec((1, d, tm // d, width), lambda i: (i // nt, 0, i % nt, 0))

    return pl.pallas_call(
        functools.partial(_combine_kernel, tm=tm),
        grid=(T // tm,),
        in_specs=[rows(GROUP_WIDTH), strided_rows(d1, GROUP_WIDTH), strided_rows(d2, GROUP_WIDTH),
                  rows(LSE_LANES), strided_rows(d1, LSE_LANES), strided_rows(d2, LSE_LANES)],
        out_specs=rows(GROUP_WIDTH),
        out_shape=jax.ShapeDtypeStruct((T, GROUP_WIDTH), bf16),
        scratch_shapes=[pltpu.VMEM((HEADS_PER_GROUP, tm, HEAD_DIM_A), f32),
                        pltpu.VMEM((HEADS_PER_GROUP, tm, HEAD_DIM_A), f32),
                        pltpu.VMEM((tm, LSE_LANES), f32), pltpu.VMEM((tm, LSE_LANES), f32),
                        pltpu.VMEM((1 + HEADS_PER_GROUP, tm, LANES), f32)],
        compiler_params=_cparams("parallel"),
        name="combine",
    )(*o_groups, *lse_groups)


def _gate_mix_kernel(h_ref, ya_ref, yl_ref, yc_ref, wga_ref, wgb_ref, wgc_ref, bga_ref, bgb_ref, bgc_ref,
                     woa_ref, wol_ref, wom_ref, wu_ref, wd_ref, mix_ref, wu_o_ref, wd_o_ref):
    wu_o_ref[...] = wu_ref[...].astype(bf16)
    wd_o_ref[...] = wd_ref[...].astype(bf16)
    mb = h_ref.shape[0] // GATE_ROW_BLOCKS
    for k in range(GATE_ROW_BLOCKS):
        rows = slice(k * mb, (k + 1) * mb)
        h = h_ref[rows, :]

        def gate(w_ref, b_ref):
            return jax.nn.sigmoid(jnp.dot(h, w_ref[...], preferred_element_type=f32) + b_ref[...])

        mixed = (gate(wga_ref, bga_ref) * jnp.dot(ya_ref[rows, :], woa_ref[...], preferred_element_type=f32)
                 + gate(wgb_ref, bgb_ref) * jnp.dot(yl_ref[rows, :], wol_ref[...], preferred_element_type=f32)
                 + gate(wgc_ref, bgc_ref) * jnp.dot(yc_ref[rows, :], wom_ref[...], preferred_element_type=f32))
        mix_ref[rows, :] = mixed.astype(mix_ref.dtype)


GATE_ROW_BLOCKS = 2


def _gate_mix(h, y_a, y_lru, y_c, w_gate, b_gate, w_o_attn, w_o_lru, w_o_mem, w_up, w_down, tm=1024, tn=512):
    T = h.shape[0]
    nj = D_MODEL // tn
    ni = T // tm
    up_rows, down_rows = D_MODEL // (nj * ni), D_FF // (nj * ni)

    def slab(rows_per_step, width):
        return pl.BlockSpec((rows_per_step, width), lambda j, i: (j * ni + i, 0))

    def rows(width):
        return pl.BlockSpec((tm, width), lambda j, i: (i, 0))

    def gate_w(k):
        return pl.BlockSpec((D_MODEL, tn), lambda j, i: (0, k * nj + j))

    def gate_b(k):
        return pl.BlockSpec((1, tn), lambda j, i: (0, k * nj + j))

    def cols(width):
        return pl.BlockSpec((width, tn), lambda j, i: (0, j))

    return pl.pallas_call(
        _gate_mix_kernel,
        grid=(nj, T // tm),
        in_specs=[rows(D_MODEL), rows(GROUP_WIDTH), rows(LRU_WIDTH), rows(MEM_WIDTH),
                  gate_w(0), gate_w(1), gate_w(2), gate_b(0), gate_b(1), gate_b(2),
                  cols(GROUP_WIDTH), cols(LRU_WIDTH), cols(MEM_WIDTH),
                  slab(up_rows, D_FF), slab(down_rows, D_MODEL)],
        out_specs=[pl.BlockSpec((tm, tn), lambda j, i: (i, j)), slab(up_rows, D_FF), slab(down_rows, D_MODEL)],
        out_shape=[jax.ShapeDtypeStruct((T, D_MODEL), bf16),
                   jax.ShapeDtypeStruct((D_MODEL, D_FF), bf16),
                   jax.ShapeDtypeStruct((D_FF, D_MODEL), bf16)],
        compiler_params=_cparams("arbitrary", "arbitrary"),
        name="gate_mix",
    )(h, y_a, y_lru, y_c, w_gate, w_gate, w_gate, b_gate, b_gate, b_gate, w_o_attn, w_o_lru, w_o_mem,
      w_up, w_down)


def _mlp_kernel(x_ref, mix_ref, wo_ref, g_ref, gf_ref, wu_ref, wd_ref, out_ref, h_scr):
    j = pl.program_id(1)

    @pl.when(j == 0)
    def _():
        x = x_ref[...] + jnp.dot(mix_ref[...], wo_ref[...], preferred_element_type=f32)
        h_scr[...] = _rms(x, g_ref[...]).astype(bf16)
        out_ref[...] = x

    u = jnp.maximum(jnp.dot(h_scr[...], wu_ref[...], preferred_element_type=f32), 0.0)
    out_ref[...] += jnp.dot((u * u).astype(bf16), wd_ref[...], preferred_element_type=f32)

    @pl.when(j == pl.num_programs(1) - 1)
    def _():
        out_ref[...] = _rms(out_ref[...], gf_ref[...])


def _mlp(x2, mixed, w_out, gain, gain_final, w_up, w_down, tm=512, tf=1024):
    T = x2.shape[0]
    return pl.pallas_call(
        _mlp_kernel,
        grid=(T // tm, D_FF // tf),
        in_specs=[pl.BlockSpec((tm, D_MODEL), lambda i, j: (i, 0)),
                  pl.BlockSpec((tm, D_MODEL), lambda i, j: (i, 0)),
                  pl.BlockSpec((D_MODEL, D_MODEL), lambda i, j: (0, 0)),
                  pl.BlockSpec((1, D_MODEL), lambda i, j: (0, 0)),
                  pl.BlockSpec((1, D_MODEL), lambda i, j: (0, 0)),
                  pl.BlockSpec((D_MODEL, tf), lambda i, j: (0, j)),
                  pl.BlockSpec((tf, D_MODEL), lambda i, j: (j, 0))],
        out_specs=pl.BlockSpec((tm, D_MODEL), lambda i, j: (i, 0)),
        out_shape=jax.ShapeDtypeStruct((T, D_MODEL), f32),
        scratch_shapes=[pltpu.VMEM((tm, D_MODEL), bf16)],
        compiler_params=_cparams("parallel", "arbitrary"),
        name="mlp",
    )(x2, mixed, w_out, gain, gain_final, w_up, w_down)


def _query_scale():
    scale = np.ones((1, N_IN), np.float32)
    scale[:, :WIDTH_A] = 1.0 / math.sqrt(HEAD_DIM_A)
    scale[:, COL_QC:] = 1.0 / math.sqrt(MEM_HEAD_DIM)
    return scale


def kernel(x, mem, rel_bias, norm_mix, norm_mem, norm_mlp, norm_final, w_in, w_gate, b_gate, conv_w, conv_b,
           lru_wa, lru_ba, lru_wi, lru_bi, lru_lambda, w_mem_kv, w_o_attn, w_o_lru, w_o_mem, w_out, w_up, w_down):
    B, S, D = x.shape
    T = B * S
    depth = w_in.shape[0]
    assert depth == 1, "the final RMSNorm is fused into the (single) layer's MLP kernel"
    x2 = x.reshape(T, D)
    mem2 = mem.reshape(B * N_MEM, D)
    for l in range(depth):
        w_qkv = (w_in[l] * _query_scale()).astype(bf16)
        h, rest = _rest_proj(x2, norm_mix[l].reshape(1, D), w_qkv)
        proj3 = rest.reshape(B, S, REST_W)

        side_casts = ((w_gate[l],), (w_out[l], w_o_lru[l]), (w_o_attn[l], w_o_mem[l], w_mem_kv[l]))
        casted = []
        attn = []
        for g in range(len(ATTN_GROUPS)):
            d = ATTN_GROUPS[g][1]
            qkv, *bf_copies = _qkv_proj(h, w_qkv, g, B, S, cast=side_casts[g])
            casted.append(bf_copies)
            o, lse = _attn_group(qkv.reshape(B * d, S // d, QKV_W), rel_bias, g)
            if g == 0:
                attn.append((o.reshape(T, GROUP_WIDTH), lse.reshape(T, LSE_LANES)))
            else:
                attn.append((o.reshape(B, d, S // d, GROUP_WIDTH), lse.reshape(B, d, S // d, LSE_LANES)))

        w_gates = 0.5 * jnp.concatenate([lru_wa[l, 0], lru_wi[l, 0], lru_wa[l, 1], lru_wi[l, 1]], axis=-1)
        b_gates = 0.5 * jnp.concatenate([lru_ba[l, 0], lru_bi[l, 0], lru_ba[l, 1], lru_bi[l, 1]], axis=-1)
        y_lru = _lru(proj3, conv_w[l], conv_b[l].reshape(1, LRU_WIDTH), _pack_lru_gates(w_gates, b_gates),
                     lru_lambda[l])

        (w_gate_bf,), (w_out_bf, w_o_lru_bf), (w_o_attn_bf, w_o_mem_bf, w_mem_kv_bf) = casted
        kv = _mem_kv(mem2, norm_mem[l].reshape(1, D), w_mem_kv_bf)
        y_c = _xattn(proj3, kv.reshape(B, N_MEM, 2 * MEM_WIDTH))

        y_a = _combine([a[0] for a in attn], [a[1] for a in attn], S)
        mixed, w_up_bf, w_down_bf = _gate_mix(
            h, y_a, y_lru.reshape(T, LRU_WIDTH), y_c.reshape(T, MEM_WIDTH), w_gate_bf, b_gate[l].reshape(1, 3 * D),
            w_o_attn_bf, w_o_lru_bf, w_o_mem_bf, w_up[l], w_down[l])
        x2 = _mlp(x2, mixed, w_out_bf, norm_mlp[l].reshape(1, D), norm_final.reshape(1, D),
                  w_up_bf, w_down_bf)
    return x2.reshape(B, S, D)
```

```python
import functools
import math

import jax
import jax.numpy as jnp
import numpy as np
from jax import lax
from jax.experimental import pallas as pl
from jax.experimental.pallas import tpu as pltpu

D_MODEL = 2048
HEAD_DIM_A = 128
ATTN_GROUPS = ((128, 1), (512, 4), (2048, 16))
HEADS_PER_GROUP = 4
GROUP_WIDTH = HEADS_PER_GROUP * HEAD_DIM_A
WIDTH_A = len(ATTN_GROUPS) * GROUP_WIDTH
ATTN_RADIUS = 64
N_BUCKETS = 32
MAX_DISTANCE = 1024
LRU_WIDTH = 1536
LRU_BLOCKS = 12
LRU_BW = 128
LRU_C = 8.0
N_MEM = 256
MEM_HEADS = 4
MEM_HEAD_DIM = 256
MEM_WIDTH = MEM_HEADS * MEM_HEAD_DIM
D_FF = 4 * D_MODEL
EPS = 1e-6
N_IN = 3 * WIDTH_A + 2 * LRU_WIDTH + MEM_WIDTH
COL_K = WIDTH_A
COL_V = 2 * WIDTH_A
COL_XB = 3 * WIDTH_A
COL_YB = 3 * WIDTH_A + LRU_WIDTH
COL_QC = 3 * WIDTH_A + 2 * LRU_WIDTH
NEG_INF = -1e30

ATTN_ROWS_PER_STEP = 2048
SUB_Q = 128
SUB_K = SUB_Q + 2 * ATTN_RADIUS
LSE_LANES = 128
LSE_REP = LSE_LANES // HEADS_PER_GROUP

VMEM_LIMIT = 56 * 1024 * 1024

f32 = jnp.float32
bf16 = jnp.bfloat16


def _cparams(*sem):
    return pltpu.CompilerParams(dimension_semantics=sem, vmem_limit_bytes=VMEM_LIMIT)


def _rms(x, gain):
    return x * lax.rsqrt(jnp.mean(x * x, axis=-1, keepdims=True) + EPS) * gain


QKV_W = 3 * GROUP_WIDTH
REST_W = 2 * LRU_WIDTH + MEM_WIDTH
REST_YB = LRU_WIDTH
REST_QC = 2 * LRU_WIDTH
PROJ_TN = GROUP_WIDTH
LANES = 128
SLABS = PROJ_TN // LANES
PROJ_ROW_BLOCKS = 2
DEINTERLEAVE_STEP = 4
N_STAGE = 2


def _qkv_kernel(h_ref, wq_ref, wk_ref, wv_ref, *refs, d, tm, n_cast):
    cast_in, o_ref = refs[:n_cast], refs[n_cast]
    cast_out = refs[n_cast + 1:2 * n_cast + 1]
    res_scr, tmp_scr = refs[2 * n_cast + 1:]
    for src_ref, dst_ref in zip(cast_in, cast_out):
        dst_ref[...] = src_ref[...].astype(bf16)
    mb = tm // PROJ_ROW_BLOCKS
    n = 0
    for k in range(PROJ_ROW_BLOCKS):
        hk = h_ref[k * mb:(k + 1) * mb, :]
        for t, w_ref in enumerate((wq_ref, wk_ref, wv_ref)):
            res = jnp.dot(hk, w_ref[...], preferred_element_type=f32)
            col = t * PROJ_TN
            if d == 1:
                o_ref[0, k * mb:(k + 1) * mb, col:col + PROJ_TN] = res.astype(bf16)
                continue
            buf = n % N_STAGE
            n += 1
            for c in range(SLABS):
                res_scr[buf, c] = res[:, c * LANES:(c + 1) * LANES]
            src, step = res_scr, d
            if d == DEINTERLEAVE_STEP ** 2:
                step = DEINTERLEAVE_STEP
                for r in range(step):
                    for c in range(SLABS):
                        tmp_scr[buf, c, r * (mb // step):(r + 1) * (mb // step), :] = (
                            res_scr[buf, c, pl.ds(r, mb // step, stride=step), :])
                src = tmp_scr
            for r in range(d):
                start = r if src is res_scr else (r % step) * (mb // step) + r // step
                for c in range(SLABS):
                    o_ref[0, r, k * (mb // d):(k + 1) * (mb // d), col + c * LANES:col + (c + 1) * LANES] = (
                        src[buf, c, pl.ds(start, mb // d, stride=step), :].astype(bf16))


def _qkv_proj(h, w_qkv, g, batch, seq, cast=(), tm=1024):
    T = h.shape[0]
    d = ATTN_GROUPS[g][1]
    nt = seq // tm
    n_groups = len(ATTN_GROUPS)
    mb = tm // PROJ_ROW_BLOCKS
    steps = T // tm

    def w_spec(which):
        return pl.BlockSpec((D_MODEL, PROJ_TN), lambda i: (0, which * n_groups + g))

    cast_specs = [pl.BlockSpec((w.shape[0] // steps, w.shape[1]), lambda i: (i, 0)) for w in cast]

    if d == 1:
        out_spec = pl.BlockSpec((1, tm, QKV_W), lambda i: (i // nt, i % nt, 0))
        out_shape = jax.ShapeDtypeStruct((batch, seq, QKV_W), bf16)
    else:
        out_spec = pl.BlockSpec((1, d, tm // d, QKV_W), lambda i: (i // nt, 0, i % nt, 0))
        out_shape = jax.ShapeDtypeStruct((batch, d, seq // d, QKV_W), bf16)
    return pl.pallas_call(
        functools.partial(_qkv_kernel, d=d, tm=tm, n_cast=len(cast)),
        grid=(steps,),
        in_specs=[pl.BlockSpec((tm, D_MODEL), lambda i: (i, 0)), w_spec(0), w_spec(1), w_spec(2)] + cast_specs,
        out_specs=[out_spec] + cast_specs,
        out_shape=[out_shape] + [jax.ShapeDtypeStruct(w.shape, bf16) for w in cast],
        scratch_shapes=[pltpu.VMEM((N_STAGE, SLABS, mb, LANES), f32)] * 2,
        compiler_params=_cparams("parallel"),
        name=f"qkv_g{g}",
    )(h, w_qkv, w_qkv, w_qkv, *cast)


REST_TILES = REST_W // PROJ_TN
YB_TILES = range(REST_YB // PROJ_TN, REST_QC // PROJ_TN)


def _gelu_tanh(y):
    return y * (0.5 * (1.0 + jnp.tanh(math.sqrt(2.0 / math.pi) * (y + 0.044715 * (y * y * y)))))


def _rest_kernel(x_ref, g_ref, *refs):
    w_refs = refs[:REST_TILES]
    h_ref, o_ref = refs[REST_TILES:]
    mb = x_ref.shape[0] // PROJ_ROW_BLOCKS
    for k in range(PROJ_ROW_BLOCKS):
        rows = slice(k * mb, (k + 1) * mb)
        h = _rms(x_ref[rows, :], g_ref[...]).astype(bf16)
        h_ref[rows, :] = h
        for c, w_ref in enumerate(w_refs):
            res = jnp.dot(h, w_ref[...], preferred_element_type=f32)
            if c in YB_TILES:
                res = _gelu_tanh(res)
            o_ref[rows, c * PROJ_TN:(c + 1) * PROJ_TN] = res.astype(bf16)


def _rest_proj(x2, gain, w_in, tm=512):
    T = x2.shape[0]
    steps = T // tm
    first = COL_XB // PROJ_TN

    def w_spec(c):
        return pl.BlockSpec((D_MODEL, PROJ_TN), lambda i: (0, first + c), pipeline_mode=pl.Buffered(1))

    return pl.pallas_call(
        _rest_kernel,
        grid=(steps,),
        in_specs=[pl.BlockSpec((tm, D_MODEL), lambda i: (i, 0)),
                  pl.BlockSpec((1, D_MODEL), lambda i: (0, 0))]
        + [w_spec(c) for c in range(REST_TILES)],
        out_specs=[pl.BlockSpec((tm, D_MODEL), lambda i: (i, 0)),
                   pl.BlockSpec((tm, REST_W), lambda i: (i, 0))],
        out_shape=[jax.ShapeDtypeStruct((T, D_MODEL), bf16),
                   jax.ShapeDtypeStruct((T, REST_W), bf16)],
        compiler_params=_cparams("arbitrary"),
        name="rest_proj",
    )(x2, gain, *([w_in] * REST_TILES))


def _t5_bucket(rel):
    nb = N_BUCKETS // 2
    max_exact = nb // 2
    sign = (rel > 0).astype(np.int32) * nb
    n = np.abs(rel)
    large = max_exact + (np.log(np.maximum(n, 1) / max_exact)
                         / np.log(MAX_DISTANCE / max_exact) * (nb - max_exact)).astype(np.int32)
    large = np.minimum(large, nb - 1)
    return (sign + np.where(n < max_exact, n, large)).astype(np.int32)


def _band_bias(rel_bias_g, dilation):
    qq = np.arange(SUB_Q)[:, None]
    kk = np.arange(SUB_K)[None, :]
    rel = kk - ATTN_RADIUS - qq
    onehot = (_t5_bucket(rel * dilation)[None] == np.arange(N_BUCKETS)[:, None, None]).astype(np.float32)
    bias = jnp.einsum('nh,nqk->hqk', rel_bias_g.astype(f32), onehot, precision=lax.Precision.HIGHEST)
    return bias + np.where(np.abs(rel) <= ATTN_RADIUS, 0.0, NEG_INF).astype(np.float32)[None]


def _attn_kernel(q_ref, kp_ref, km_ref, kn_ref, vp_ref, vm_ref, vn_ref, bias_ref,
                 o_ref, lse_ref, kbuf, vbuf, *, tq, seq, n_seq):
    R = ATTN_RADIUS
    q0 = pl.program_id(1) * tq
    lane = lax.broadcasted_iota(jnp.int32, (SUB_Q, LSE_LANES), 1)
    n_sub = tq // SUB_Q
    for i in range(n_seq):
        kbuf[i, 0:R] = kp_ref[i]
        kbuf[i, R:R + tq] = km_ref[i]
        kbuf[i, R + tq:] = kn_ref[i]
        vbuf[i, 0:R] = vp_ref[i]
        vbuf[i, R:R + tq] = vm_ref[i]
        vbuf[i, R + tq:] = vn_ref[i]
        for s in range(n_sub):
            r0 = s * SUB_Q
            edge = None
            if s == 0 or s == n_sub - 1:
                pos = q0 + (r0 - R) + lax.broadcasted_iota(jnp.int32, (1, SUB_K), 1)
                edge = jnp.where(pos >= 0, jnp.where(pos < seq, 0.0, NEG_INF), NEG_INF)
            m_tile = s_tile = None
            for h in range(HEADS_PER_GROUP):
                c0 = h * HEAD_DIM_A
                q = q_ref[i, r0:r0 + SUB_Q, c0:c0 + HEAD_DIM_A]
                k = kbuf[i, r0:r0 + SUB_K, c0:c0 + HEAD_DIM_A]
                v = vbuf[i, r0:r0 + SUB_K, c0:c0 + HEAD_DIM_A]
                logits = lax.dot_general(q, k, (((1,), (1,)), ((), ())), preferred_element_type=f32) + bias_ref[h]
                if edge is not None:
                    logits = logits + edge
                m = jnp.max(logits, axis=-1, keepdims=True)
                p = jnp.exp(logits - m)
                ssum = jnp.sum(p, axis=-1, keepdims=True)
                o = jnp.dot(p.astype(bf16), v, preferred_element_type=f32) * (1.0 / ssum)
                o_ref[i, r0:r0 + SUB_Q, c0:c0 + HEAD_DIM_A] = o.astype(o_ref.dtype)
                m_tile = m if m_tile is None else jnp.where(lane >= h * LSE_REP, m, m_tile)
                s_tile = ssum if s_tile is None else jnp.where(lane >= h * LSE_REP, ssum, s_tile)
            lse_ref[i, r0:r0 + SUB_Q, :] = m_tile + jnp.log(s_tile)


def _attn_group(qkv, rel_bias, g):
    _, d = ATTN_GROUPS[g]
    n, L, _ = qkv.shape
    tq = min(ATTN_ROWS_PER_STEP, L)
    ns = ATTN_ROWS_PER_STEP // tq
    R = ATTN_RADIUS
    bias = _band_bias(rel_bias[:, g * HEADS_PER_GROUP:(g + 1) * HEADS_PER_GROUP], d)
    rb = tq // R
    last_rb = L // R - 1

    def main(col, width=GROUP_WIDTH):
        return pl.BlockSpec((ns, tq, width), lambda b, t: (b, t, col))

    def prev(col):
        return pl.BlockSpec((ns, R, GROUP_WIDTH), lambda b, t: (b, jnp.maximum(t * rb - 1, 0), col))

    def nxt(col):
        return pl.BlockSpec((ns, R, GROUP_WIDTH), lambda b, t: (b, jnp.minimum((t + 1) * rb, last_rb), col))

    return pl.pallas_call(
        functools.partial(_attn_kernel, tq=tq, seq=L, n_seq=ns),
        grid=(n // ns, L // tq),
        in_specs=[main(0), prev(1), main(1), nxt(1), prev(2), main(2), nxt(2),
                  pl.BlockSpec((HEADS_PER_GROUP, SUB_Q, SUB_K), lambda b, t: (0, 0, 0))],
        out_specs=[main(0), main(0, LSE_LANES)],
        out_shape=[jax.ShapeDtypeStruct((n, L, GROUP_WIDTH), bf16),
                   jax.ShapeDtypeStruct((n, L, LSE_LANES), f32)],
        scratch_shapes=[pltpu.VMEM((ns, tq + 2 * R, GROUP_WIDTH), bf16),
                        pltpu.VMEM((ns, tq + 2 * R, GROUP_WIDTH), bf16)],
        compiler_params=_cparams("parallel", "arbitrary"),
        name=f"attn_g{g}",
    )(qkv, qkv, qkv, qkv, qkv, qkv, qkv, bias)


LRU_CHUNK = 256
LRU_PAD = 8
LRU_FINISH_ROWS = 512
GATE_BIAS_ROWS = 3
LRU_SEGS = 8
SEG_GAP = 4


def _lru_kernel(xb_ref, yb_ref, cw_ref, cb_ref, w_ref, lam_ref, o_ref,
                xpad, af, bf, ab, bb, htf, ptf, htb, ptb, cf_scr, cb_scr, *, seq):
    R = LRU_CHUNK
    P = LRU_PAD
    seg_len = seq // LRU_SEGS
    pitch = seg_len + SEG_GAP
    chunks_per_seg = seg_len // R
    n_chunks = seq // R
    xpad[0:P] = jnp.zeros((P, LRU_BW), f32)
    xpad[P + seq:] = jnp.zeros((P, LRU_BW), f32)
    xpad[P:P + seq] = xb_ref[0].astype(f32)
    lam = lam_ref[...]
    log_a_unit = -LRU_C * (jnp.maximum(-lam, 0.0) + jnp.log1p(jnp.exp(-jnp.abs(lam))))
    cw = cw_ref[...]
    cb = cb_ref[...]
    row = lax.broadcasted_iota(jnp.int32, (R, LRU_BW), 0)
    lane = lax.broadcasted_iota(jnp.int32, (R, LRU_BW), 1)
    bias_cols = jnp.where(lane < GATE_BIAS_ROWS, 1.0, 0.0).astype(bf16)

    def chunk(ci, first=False, last=False):
        c0 = ci * R
        dst = (ci // chunks_per_seg) * pitch + (ci % chunks_per_seg) * R
        xc = (cw[0:1] * xpad[pl.ds(c0 + (P - 1), R), :] + cw[1:2] * xpad[pl.ds(c0 + P, R), :]
              + cw[2:3] * xpad[pl.ds(c0 + (P + 1), R), :] + cw[3:4] * xpad[pl.ds(c0 + (P + 2), R), :]) + cb
        lhs = jnp.concatenate([xc.astype(bf16), bias_cols], axis=1)
        th = jnp.tanh(jnp.dot(lhs, w_ref[0], preferred_element_type=f32))
        half_xc = 0.5 * xc
        for direction, (a_scr, b_scr) in enumerate(((af, bf), (ab, bb))):
            base = direction * 2 * LRU_BW
            half_log2_a = (0.5 * math.log2(math.e)) * log_a_unit[direction:direction + 1]
            a = jnp.exp2(half_log2_a * th[:, base:base + LRU_BW] + half_log2_a)
            gated_x = half_xc * th[:, base + LRU_BW:base + 2 * LRU_BW] + half_xc
            y = 1.0 - a * a
            mult = y * lax.rsqrt(jnp.maximum(y, 1e-30))
            if direction == 0 and first:
                mult = jnp.where(row == 0, 1.0, mult)
            if direction == 1 and last:
                mult = jnp.where(row == R - 1, 1.0, mult)
            a_scr[pl.ds(dst, R), :] = a
            b_scr[pl.ds(dst, R), :] = mult * gated_x

    for ci in range(n_chunks):
        chunk(ci, first=ci == 0, last=ci == n_chunks - 1)

    def scan(i, carry):
        hf, pf, hb, pb = carry
        rows = pl.ds(i, LRU_SEGS, stride=pitch)
        a = af[rows, :]
        hf = a * hf + bf[rows, :]
        pf = a * pf
        htf[rows, :] = hf
        ptf[rows, :] = pf
        rows = pl.ds(seg_len - 1 - i, LRU_SEGS, stride=pitch)
        a = ab[rows, :]
        hb = a * hb + bb[rows, :]
        pb = a * pb
        htb[rows, :] = hb
        ptb[rows, :] = pb
        return hf, pf, hb, pb

    zero = jnp.zeros((LRU_SEGS, LRU_BW), f32)
    one = jnp.ones((LRU_SEGS, LRU_BW), f32)
    hf, pf, hb, pb = lax.fori_loop(0, seg_len, scan, (zero, one, zero, one), unroll=8)

    c = jnp.zeros((1, LRU_BW), f32)
    cf_scr[0:1] = c
    for j in range(1, LRU_SEGS):
        c = hf[j - 1:j] + pf[j - 1:j] * c
        cf_scr[j:j + 1] = c
    c = jnp.zeros((1, LRU_BW), f32)
    cb_scr[LRU_SEGS - 1:LRU_SEGS] = c
    for j in range(LRU_SEGS - 2, -1, -1):
        c = hb[j + 1:j + 2] + pb[j + 1:j + 2] * c
        cb_scr[j:j + 1] = c

    F = LRU_FINISH_ROWS
    finish_per_seg = seg_len // F

    for ci in range(seq // F):
        c0 = ci * F
        seg = ci // finish_per_seg
        rows = pl.ds(seg * pitch + (ci % finish_per_seg) * F, F)
        h = (htf[rows, :] + ptf[rows, :] * cf_scr[seg:seg + 1, :]
             + htb[rows, :] + ptb[rows, :] * cb_scr[seg:seg + 1, :])
        o_ref[0, c0:c0 + F, :] = (h * yb_ref[0, c0:c0 + F, :].astype(f32)).astype(o_ref.dtype)


def _pack_lru_gates(w, b):
    rows, rest = [], b
    for _ in range(GATE_BIAS_ROWS):
        piece = rest.astype(bf16)
        rows.append(piece)
        rest = rest - piece.astype(f32)
    bias_rows = jnp.pad(jnp.stack(rows, axis=1), ((0, 0), (0, LRU_BW - GATE_BIAS_ROWS), (0, 0)))
    return jnp.concatenate([w.astype(bf16), bias_rows], axis=1)


def _lru(proj3, conv_w, conv_b, w_gates, lam):
    B, S, _ = proj3.shape
    xb0 = 0
    yb0 = REST_YB // LRU_BW
    return pl.pallas_call(
        functools.partial(_lru_kernel, seq=S),
        grid=(B, LRU_BLOCKS),
        in_specs=[
            pl.BlockSpec((1, S, LRU_BW), lambda b, n: (b, 0, xb0 + n)),
            pl.BlockSpec((1, S, LRU_BW), lambda b, n: (b, 0, yb0 + n)),
            pl.BlockSpec((4, LRU_BW), lambda b, n: (0, n)),
            pl.BlockSpec((1, LRU_BW), lambda b, n: (0, n)),
            pl.BlockSpec((1, 2 * LRU_BW, 4 * LRU_BW), lambda b, n: (n, 0, 0)),
            pl.BlockSpec((2, LRU_BW), lambda b, n: (0, n)),
        ],
        out_specs=pl.BlockSpec((1, S, LRU_BW), lambda b, n: (b, 0, n)),
        out_shape=jax.ShapeDtypeStruct((B, S, LRU_WIDTH), bf16),
        scratch_shapes=([pltpu.VMEM((S + 2 * LRU_PAD, LRU_BW), f32)]
                        + [pltpu.VMEM((S + LRU_SEGS * SEG_GAP, LRU_BW), f32)] * 8
                        + [pltpu.VMEM((LRU_SEGS, LRU_BW), f32)] * 2),
        compiler_params=_cparams("parallel", "parallel"),
        name="lru",
    )(proj3, proj3, conv_w, conv_b, w_gates, lam)


def _mem_kv_kernel(m_ref, g_ref, w_ref, o_ref, h_scr):
    @pl.when(pl.program_id(0) == 0)
    def _():
        h_scr[...] = _rms(m_ref[...], g_ref[...]).astype(bf16)

    o_ref[...] = jnp.dot(h_scr[...], w_ref[...], preferred_element_type=f32).astype(o_ref.dtype)


def _mem_kv(mem2, gain, w, tn=512):
    M = mem2.shape[0]
    N = w.shape[1]
    return pl.pallas_call(
        _mem_kv_kernel,
        grid=(N // tn,),
        in_specs=[pl.BlockSpec((M, D_MODEL), lambda j: (0, 0)),
                  pl.BlockSpec((1, D_MODEL), lambda j: (0, 0)),
                  pl.BlockSpec((D_MODEL, tn), lambda j: (0, j))],
        out_specs=pl.BlockSpec((M, tn), lambda j: (0, j)),
        out_shape=jax.ShapeDtypeStruct((M, N), bf16),
        scratch_shapes=[pltpu.VMEM((M, D_MODEL), bf16)],
        compiler_params=_cparams("arbitrary"),
        name="mem_kv",
    )(mem2, gain, w)


def _xattn_kernel(q0_ref, q1_ref, q2_ref, q3_ref, kv_ref, o_ref):
    for h, q_ref in enumerate((q0_ref, q1_ref, q2_ref, q3_ref)):
        c0 = h * MEM_HEAD_DIM
        k = kv_ref[0, :, c0:c0 + MEM_HEAD_DIM]
        v = kv_ref[0, :, MEM_WIDTH + c0:MEM_WIDTH + c0 + MEM_HEAD_DIM]
        logits = lax.dot_general(q_ref[0], k, (((1,), (1,)), ((), ())), preferred_element_type=f32)
        m = jnp.max(logits, axis=-1, keepdims=True)
        p = jnp.exp(logits - m)
        ssum = jnp.sum(p, axis=-1, keepdims=True)
        o = jnp.dot(p.astype(bf16), v, preferred_element_type=f32) * (1.0 / ssum)
        o_ref[0, :, c0:c0 + MEM_HEAD_DIM] = o.astype(o_ref.dtype)


def _xattn(proj3, kv3, tq=2048):
    B, S, _ = proj3.shape
    qb0 = REST_QC // MEM_HEAD_DIM

    def qspec(h):
        return pl.BlockSpec((1, tq, MEM_HEAD_DIM), lambda b, t: (b, t, qb0 + h))

    return pl.pallas_call(
        _xattn_kernel,
        grid=(B, S // tq),
        in_specs=[qspec(0), qspec(1), qspec(2), qspec(3),
                  pl.BlockSpec((1, N_MEM, 2 * MEM_WIDTH), lambda b, t: (b, 0, 0))],
        out_specs=pl.BlockSpec((1, tq, MEM_WIDTH), lambda b, t: (b, t, 0)),
        out_shape=jax.ShapeDtypeStruct((B, S, MEM_WIDTH), bf16),
        compiler_params=_cparams("parallel", "parallel"),
        name="xattn",
    )(proj3, proj3, proj3, proj3, kv3)


def _combine_kernel(o0_ref, o1_ref, o2_ref, l0_ref, l1_ref, l2_ref, ya_ref,
                    o1_scr, o2_scr, l1_scr, l2_scr, tmp_scr, *, tm):
    step = DEINTERLEAVE_STEP
    for g, o_ref, l_ref, o_scr, l_scr in ((1, o1_ref, l1_ref, o1_scr, l1_scr),
                                          (2, o2_ref, l2_ref, o2_scr, l2_scr)):
        d = ATTN_GROUPS[g][1]
        slabs = [(l_scr, lambda r: l_ref[0, r])]
        slabs += [(o_scr.at[h], lambda r, h=h: o_ref[0, r, :, h * HEAD_DIM_A:(h + 1) * HEAD_DIM_A].astype(f32))
                  for h in range(HEADS_PER_GROUP)]
        for k, (dst, rows_of) in enumerate(slabs):
            if d == step:
                for r in range(d):
                    dst[pl.ds(r, tm // d, stride=d), :] = rows_of(r)
                continue
            tmp = tmp_scr.at[k]
            for r in range(d):
                tmp[pl.ds((r % step) * (tm // step) + r // step, tm // d, stride=step), :] = rows_of(r)
            for lo in range(step):
                dst[pl.ds(lo, tm // step, stride=step), :] = tmp[lo * (tm // step):(lo + 1) * (tm // step), :]
    l0, l1, l2 = l0_ref[...], l1_scr[...], l2_scr[...]
    m = jnp.maximum(jnp.maximum(l0, l1), l2)
    e0, e1, e2 = jnp.exp(l0 - m), jnp.exp(l1 - m), jnp.exp(l2 - m)
    inv = 1.0 / (e0 + e1 + e2)
    for h in range(HEADS_PER_GROUP):
        c0 = h * HEAD_DIM_A
        lane = slice(h * LSE_REP, h * LSE_REP + 1)
        y = ((e0 * inv)[:, lane] * o0_ref[:, c0:c0 + HEAD_DIM_A].astype(f32)
             + (e1 * inv)[:, lane] * o1_scr[h] + (e2 * inv)[:, lane] * o2_scr[h])
        ya_ref[:, c0:c0 + HEAD_DIM_A] = y.astype(bf16)


def _combine(o_groups, lse_groups, seq, tm=1024):
    T = o_groups[0].shape[0]
    nt = seq // tm
    d1, d2 = ATTN_GROUPS[1][1], ATTN_GROUPS[2][1]

    def rows(width):
        return pl.BlockSpec((tm, width), lambda i: (i, 0))

    def strided_rows(d, width):
        return pl.BlockSpec((1, d, tm // d, width), lambda i: (i // nt, 0, i % nt, 0))

    return pl.pallas_call(
        functools.partial(_combine_kernel, tm=tm),
        grid=(T // tm,),
        in_specs=[rows(GROUP_WIDTH), strided_rows(d1, GROUP_WIDTH), strided_rows(d2, GROUP_WIDTH),
                  rows(LSE_LANES), strided_rows(d1, LSE_LANES), strided_rows(d2, LSE_LANES)],
        out_specs=rows(GROUP_WIDTH),
        out_shape=jax.ShapeDtypeStruct((T, GROUP_WIDTH), bf16),
        scratch_shapes=[pltpu.VMEM((HEADS_PER_GROUP, tm, HEAD_DIM_A), f32),
                        pltpu.VMEM((HEADS_PER_GROUP, tm, HEAD_DIM_A), f32),
                        pltpu.VMEM((tm, LSE_LANES), f32), pltpu.VMEM((tm, LSE_LANES), f32),
                        pltpu.VMEM((1 + HEADS_PER_GROUP, tm, LANES), f32)],
        compiler_params=_cparams("parallel"),
        name="combine",
    )(*o_groups, *lse_groups)


def _gate_mix_kernel(h_ref, ya_ref, yl_ref, yc_ref, wga_ref, wgb_ref, wgc_ref, bga_ref, bgb_ref, bgc_ref,
                     woa_ref, wol_ref, wom_ref, wu_ref, wd_ref, mix_ref, wu_o_ref, wd_o_ref):
    wu_o_ref[...] = wu_ref[...].astype(bf16)
    wd_o_ref[...] = wd_ref[...].astype(bf16)
    mb = h_ref.shape[0] // GATE_ROW_BLOCKS
    for k in range(GATE_ROW_BLOCKS):
        rows = slice(k * mb, (k + 1) * mb)
        h = h_ref[rows, :]

        def gate(w_ref, b_ref):
            return jax.nn.sigmoid(jnp.dot(h, w_ref[...], preferred_element_type=f32) + b_ref[...])

        mixed = (gate(wga_ref, bga_ref) * jnp.dot(ya_ref[rows, :], woa_ref[...], preferred_element_type=f32)
                 + gate(wgb_ref, bgb_ref) * jnp.dot(yl_ref[rows, :], wol_ref[...], preferred_element_type=f32)
                 + gate(wgc_ref, bgc_ref) * jnp.dot(yc_ref[rows, :], wom_ref[...], preferred_element_type=f32))
        mix_ref[rows, :] = mixed.astype(mix_ref.dtype)


GATE_ROW_BLOCKS = 2


def _gate_mix(h, y_a, y_lru, y_c, w_gate, b_gate, w_o_attn, w_o_lru, w_o_mem, w_up, w_down, tm=1024, tn=512):
    T = h.shape[0]
    nj = D_MODEL // tn
    ni = T // tm
    up_rows, down_rows = D_MODEL // (nj * ni), D_FF // (nj * ni)

    def slab(rows_per_step, width):
        return pl.BlockSpec((rows_per_step, width), lambda j, i: (j * ni + i, 0))

    def rows(width):
        return pl.BlockSpec((tm, width), lambda j, i: (i, 0))

    def gate_w(k):
        return pl.BlockSpec((D_MODEL, tn), lambda j, i: (0, k * nj + j))

    def gate_b(k):
        return pl.BlockSpec((1, tn), lambda j, i: (0, k * nj + j))

    def cols(width):
        return pl.BlockSpec((width, tn), lambda j, i: (0, j))

    return pl.pallas_call(
        _gate_mix_kernel,
        grid=(nj, T // tm),
        in_specs=[rows(D_MODEL), rows(GROUP_WIDTH), rows(LRU_WIDTH), rows(MEM_WIDTH),
                  gate_w(0), gate_w(1), gate_w(2), gate_b(0), gate_b(1), gate_b(2),
                  cols(GROUP_WIDTH), cols(LRU_WIDTH), cols(MEM_WIDTH),
                  slab(up_rows, D_FF), slab(down_rows, D_MODEL)],
        out_specs=[pl.BlockSpec((tm, tn), lambda j, i: (i, j)), slab(up_rows, D_FF), slab(down_rows, D_MODEL)],
        out_shape=[jax.ShapeDtypeStruct((T, D_MODEL), bf16),
                   jax.ShapeDtypeStruct((D_MODEL, D_FF), bf16),
                   jax.ShapeDtypeStruct((D_FF, D_MODEL), bf16)],
        compiler_params=_cparams("arbitrary", "arbitrary"),
        name="gate_mix",
    )(h, y_a, y_lru, y_c, w_gate, w_gate, w_gate, b_gate, b_gate, b_gate, w_o_attn, w_o_lru, w_o_mem,
      w_up, w_down)


def _mlp_kernel(x_ref, mix_ref, wo_ref, g_ref, gf_ref, wu_ref, wd_ref, out_ref, h_scr):
    j = pl.program_id(1)

    @pl.when(j == 0)
    def _():
        x = x_ref[...] + jnp.dot(mix_ref[...], wo_ref[...], preferred_element_type=f32)
        h_scr[...] = _rms(x, g_ref[...]).astype(bf16)
        out_ref[...] = x

    u = jnp.maximum(jnp.dot(h_scr[...], wu_ref[...], preferred_element_type=f32), 0.0)
    out_ref[...] += jnp.dot((u * u).astype(bf16), wd_ref[...], preferred_element_type=f32)

    @pl.when(j == pl.num_programs(1) - 1)
    def _():
        out_ref[...] = _rms(out_ref[...], gf_ref[...])


def _mlp(x2, mixed, w_out, gain, gain_final, w_up, w_down, tm=512, tf=1024):
    T = x2.shape[0]
    return pl.pallas_call(
        _mlp_kernel,
        grid=(T // tm, D_FF // tf),
        in_specs=[pl.BlockSpec((tm, D_MODEL), lambda i, j: (i, 0)),
                  pl.BlockSpec((tm, D_MODEL), lambda i, j: (i, 0)),
                  pl.BlockSpec((D_MODEL, D_MODEL), lambda i, j: (0, 0)),
                  pl.BlockSpec((1, D_MODEL), lambda i, j: (0, 0)),
                  pl.BlockSpec((1, D_MODEL), lambda i, j: (0, 0)),
                  pl.BlockSpec((D_MODEL, tf), lambda i, j: (0, j)),
                  pl.BlockSpec((tf, D_MODEL), lambda i, j: (j, 0))],
        out_specs=pl.BlockSpec((tm, D_MODEL), lambda i, j: (i, 0)),
        out_shape=jax.ShapeDtypeStruct((T, D_MODEL), f32),
        scratch_shapes=[pltpu.VMEM((tm, D_MODEL), bf16)],
        compiler_params=_cparams("parallel", "arbitrary"),
        name="mlp",
    )(x2, mixed, w_out, gain, gain_final, w_up, w_down)


def _query_scale():
    scale = np.ones((1, N_IN), np.float32)
    scale[:, :WIDTH_A] = 1.0 / math.sqrt(HEAD_DIM_A)
    scale[:, COL_QC:] = 1.0 / math.sqrt(MEM_HEAD_DIM)
    return scale


def kernel(x, mem, rel_bias, norm_mix, norm_mem, norm_mlp, norm_final, w_in, w_gate, b_gate, conv_w, conv_b,
           lru_wa, lru_ba, lru_wi, lru_bi, lru_lambda, w_mem_kv, w_o_attn, w_o_lru, w_o_mem, w_out, w_up, w_down):
    B, S, D = x.shape
    T = B * S
    depth = w_in.shape[0]
    assert depth == 1, "the final RMSNorm is fused into the (single) layer's MLP kernel"
    x2 = x.reshape(T, D)
    mem2 = mem.reshape(B * N_MEM, D)
    for l in range(depth):
        w_qkv = (w_in[l] * _query_scale()).astype(bf16)
        h, rest = _rest_proj(x2, norm_mix[l].reshape(1, D), w_qkv)
        proj3 = rest.reshape(B, S, REST_W)

        side_casts = ((w_gate[l],), (w_out[l], w_o_lru[l]), (w_o_attn[l], w_o_mem[l], w_mem_kv[l]))
        casted = []
        attn = []
        for g in range(len(ATTN_GROUPS)):
            d = ATTN_GROUPS[g][1]
            qkv, *bf_copies = _qkv_proj(h, w_qkv, g, B, S, cast=side_casts[g])
            casted.append(bf_copies)
            o, lse = _attn_group(qkv.reshape(B * d, S // d, QKV_W), rel_bias, g)
            if g == 0:
                attn.append((o.reshape(T, GROUP_WIDTH), lse.reshape(T, LSE_LANES)))
            else:
                attn.append((o.reshape(B, d, S // d, GROUP_WIDTH), lse.reshape(B, d, S // d, LSE_LANES)))

        w_gates = 0.5 * jnp.concatenate([lru_wa[l, 0], lru_wi[l, 0], lru_wa[l, 1], lru_wi[l, 1]], axis=-1)
        b_gates = 0.5 * jnp.concatenate([lru_ba[l, 0], lru_bi[l, 0], lru_ba[l, 1], lru_bi[l, 1]], axis=-1)
        y_lru = _lru(proj3, conv_w[l], conv_b[l].reshape(1, LRU_WIDTH), _pack_lru_gates(w_gates, b_gates),
                     lru_lambda[l])

        (w_gate_bf,), (w_out_bf, w_o_lru_bf), (w_o_attn_bf, w_o_mem_bf, w_mem_kv_bf) = casted
        kv = _mem_kv(mem2, norm_mem[l].reshape(1, D), w_mem_kv_bf)
        y_c = _xattn(proj3, kv.reshape(B, N_MEM, 2 * MEM_WIDTH))

        y_a = _combine([a[0] for a in attn], [a[1] for a in attn], S)
        mixed, w_up_bf, w_down_bf = _gate_mix(
            h, y_a, y_lru.reshape(T, LRU_WIDTH), y_c.reshape(T, MEM_WIDTH), w_gate_bf, b_gate[l].reshape(1, 3 * D),
            w_o_attn_bf, w_o_lru_bf, w_o_mem_bf, w_up[l], w_down[l])
        x2 = _mlp(x2, mixed, w_out_bf, norm_mlp[l].reshape(1, D), norm_final.reshape(1, D),
                  w_up_bf, w_down_bf)
    return x2.reshape(B, S, D)
```

```python
import functools
import math

import jax
import jax.numpy as jnp
import numpy as np
from jax import lax
from jax.experimental import pallas as pl
from jax.experimental.pallas import tpu as pltpu

D_MODEL = 2048
HEAD_DIM_A = 128
ATTN_GROUPS = ((128, 1), (512, 4), (2048, 16))
HEADS_PER_GROUP = 4
GROUP_WIDTH = HEADS_PER_GROUP * HEAD_DIM_A
WIDTH_A = len(ATTN_GROUPS) * GROUP_WIDTH
ATTN_RADIUS = 64
N_BUCKETS = 32
MAX_DISTANCE = 1024
LRU_WIDTH = 1536
LRU_BLOCKS = 12
LRU_BW = 128
LRU_C = 8.0
N_MEM = 256
MEM_HEADS = 4
MEM_HEAD_DIM = 256
MEM_WIDTH = MEM_HEADS * MEM_HEAD_DIM
D_FF = 4 * D_MODEL
EPS = 1e-6
N_IN = 3 * WIDTH_A + 2 * LRU_WIDTH + MEM_WIDTH
COL_K = WIDTH_A
COL_V = 2 * WIDTH_A
COL_XB = 3 * WIDTH_A
COL_YB = 3 * WIDTH_A + LRU_WIDTH
COL_QC = 3 * WIDTH_A + 2 * LRU_WIDTH
NEG_INF = -1e30

ATTN_ROWS_PER_STEP = 2048
SUB_Q = 128
SUB_K = SUB_Q + 2 * ATTN_RADIUS
LSE_LANES = 128
LSE_REP = LSE_LANES // HEADS_PER_GROUP

VMEM_LIMIT = 56 * 1024 * 1024

f32 = jnp.float32
bf16 = jnp.bfloat16


def _cparams(*sem):
    return pltpu.CompilerParams(dimension_semantics=sem, vmem_limit_bytes=VMEM_LIMIT)


def _rms(x, gain):
    return x * lax.rsqrt(jnp.mean(x * x, axis=-1, keepdims=True) + EPS) * gain


QKV_W = 3 * GROUP_WIDTH
REST_W = 2 * LRU_WIDTH + MEM_WIDTH
REST_YB = LRU_WIDTH
REST_QC = 2 * LRU_WIDTH
PROJ_TN = GROUP_WIDTH
LANES = 128
SLABS = PROJ_TN // LANES
PROJ_ROW_BLOCKS = 2
DEINTERLEAVE_STEP = 4
N_STAGE = 2


def _qkv_kernel(h_ref, wq_ref, wk_ref, wv_ref, *refs, d, tm, n_cast):
    cast_in, o_ref = refs[:n_cast], refs[n_cast]
    cast_out = refs[n_cast + 1:2 * n_cast + 1]
    res_scr, tmp_scr = refs[2 * n_cast + 1:]
    for src_ref, dst_ref in zip(cast_in, cast_out):
        dst_ref[...] = src_ref[...].astype(bf16)
    mb = tm // PROJ_ROW_BLOCKS
    n = 0
    for k in range(PROJ_ROW_BLOCKS):
        hk = h_ref[k * mb:(k + 1) * mb, :]
        for t, w_ref in enumerate((wq_ref, wk_ref, wv_ref)):
            res = jnp.dot(hk, w_ref[...], preferred_element_type=f32)
            col = t * PROJ_TN
            if d == 1:
                o_ref[0, k * mb:(k + 1) * mb, col:col + PROJ_TN] = res.astype(bf16)
                continue
            buf = n % N_STAGE
            n += 1
            for c in range(SLABS):
                res_scr[buf, c] = res[:, c * LANES:(c + 1) * LANES]
            src, step = res_scr, d
            if d == DEINTERLEAVE_STEP ** 2:
                step = DEINTERLEAVE_STEP
                for r in range(step):
                    for c in range(SLABS):
                        tmp_scr[buf, c, r * (mb // step):(r + 1) * (mb // step), :] = (
                            res_scr[buf, c, pl.ds(r, mb // step, stride=step), :])
                src = tmp_scr
            for r in range(d):
                start = r if src is res_scr else (r % step) * (mb // step) + r // step
                for c in range(SLABS):
                    o_ref[0, r, k * (mb // d):(k + 1) * (mb // d), col + c * LANES:col + (c + 1) * LANES] = (
                        src[buf, c, pl.ds(start, mb // d, stride=step), :].astype(bf16))


def _qkv_proj(h, w_qkv, g, batch, seq, cast=(), tm=1024):
    T = h.shape[0]
    d = ATTN_GROUPS[g][1]
    nt = seq // tm
    n_groups = len(ATTN_GROUPS)
    mb = tm // PROJ_ROW_BLOCKS
    steps = T // tm

    def w_spec(which):
        return pl.BlockSpec((D_MODEL, PROJ_TN), lambda i: (0, which * n_groups + g))

    cast_specs = [pl.BlockSpec((w.shape[0] // steps, w.shape[1]), lambda i: (i, 0)) for w in cast]

    if d == 1:
        out_spec = pl.BlockSpec((1, tm, QKV_W), lambda i: (i // nt, i % nt, 0))
        out_shape = jax.ShapeDtypeStruct((batch, seq, QKV_W), bf16)
    else:
        out_spec = pl.BlockSpec((1, d, tm // d, QKV_W), lambda i: (i // nt, 0, i % nt, 0))
        out_shape = jax.ShapeDtypeStruct((batch, d, seq // d, QKV_W), bf16)
    return pl.pallas_call(
        functools.partial(_qkv_kernel, d=d, tm=tm, n_cast=len(cast)),
        grid=(steps,),
        in_specs=[pl.BlockSpec((tm, D_MODEL), lambda i: (i, 0)), w_spec(0), w_spec(1), w_spec(2)] + cast_specs,
        out_specs=[out_spec] + cast_specs,
        out_shape=[out_shape] + [jax.ShapeDtypeStruct(w.shape, bf16) for w in cast],
        scratch_shapes=[pltpu.VMEM((N_STAGE, SLABS, mb, LANES), f32)] * 2,
        compiler_params=_cparams("parallel"),
        name=f"qkv_g{g}",
    )(h, w_qkv, w_qkv, w_qkv, *cast)


REST_TILES = REST_W // PROJ_TN
YB_TILES = range(REST_YB // PROJ_TN, REST_QC // PROJ_TN)


def _gelu_tanh(y):
    return y * (0.5 * (1.0 + jnp.tanh(math.sqrt(2.0 / math.pi) * (y + 0.044715 * (y * y * y)))))


def _rest_kernel(x_ref, g_ref, *refs):
    w_refs = refs[:REST_TILES]
    h_ref, o_ref = refs[REST_TILES:]
    mb = x_ref.shape[0] // PROJ_ROW_BLOCKS
    for k in range(PROJ_ROW_BLOCKS):
        rows = slice(k * mb, (k + 1) * mb)
        h = _rms(x_ref[rows, :], g_ref[...]).astype(bf16)
        h_ref[rows, :] = h
        for c, w_ref in enumerate(w_refs):
            res = jnp.dot(h, w_ref[...], preferred_element_type=f32)
            if c in YB_TILES:
                res = _gelu_tanh(res)
            o_ref[rows, c * PROJ_TN:(c + 1) * PROJ_TN] = res.astype(bf16)


def _rest_proj(x2, gain, w_in, tm=512):
    T = x2.shape[0]
    steps = T // tm
    first = COL_XB // PROJ_TN

    def w_spec(c):
        return pl.BlockSpec((D_MODEL, PROJ_TN), lambda i: (0, first + c), pipeline_mode=pl.Buffered(1))

    return pl.pallas_call(
        _rest_kernel,
        grid=(steps,),
        in_specs=[pl.BlockSpec((tm, D_MODEL), lambda i: (i, 0)),
                  pl.BlockSpec((1, D_MODEL), lambda i: (0, 0))]
        + [w_spec(c) for c in range(REST_TILES)],
        out_specs=[pl.BlockSpec((tm, D_MODEL), lambda i: (i, 0)),
                   pl.BlockSpec((tm, REST_W), lambda i: (i, 0))],
        out_shape=[jax.ShapeDtypeStruct((T, D_MODEL), bf16),
                   jax.ShapeDtypeStruct((T, REST_W), bf16)],
        compiler_params=_cparams("arbitrary"),
        name="rest_proj",
    )(x2, gain, *([w_in] * REST_TILES))


def _t5_bucket(rel):
    nb = N_BUCKETS // 2
    max_exact = nb // 2
    sign = (rel > 0).astype(np.int32) * nb
    n = np.abs(rel)
    large = max_exact + (np.log(np.maximum(n, 1) / max_exact)
                         / np.log(MAX_DISTANCE / max_exact) * (nb - max_exact)).astype(np.int32)
    large = np.minimum(large, nb - 1)
    return (sign + np.where(n < max_exact, n, large)).astype(np.int32)


def _band_bias(rel_bias_g, dilation):
    qq = np.arange(SUB_Q)[:, None]
    kk = np.arange(SUB_K)[None, :]
    rel = kk - ATTN_RADIUS - qq
    onehot = (_t5_bucket(rel * dilation)[None] == np.arange(N_BUCKETS)[:, None, None]).astype(np.float32)
    bias = jnp.einsum('nh,nqk->hqk', rel_bias_g.astype(f32), onehot, precision=lax.Precision.HIGHEST)
    return bias + np.where(np.abs(rel) <= ATTN_RADIUS, 0.0, NEG_INF).astype(np.float32)[None]


def _attn_kernel(q_ref, kp_ref, km_ref, kn_ref, vp_ref, vm_ref, vn_ref, bias_ref,
                 o_ref, lse_ref, kbuf, vbuf, *, tq, seq, n_seq):
    R = ATTN_RADIUS
    q0 = pl.program_id(1) * tq
    lane = lax.broadcasted_iota(jnp.int32, (SUB_Q, LSE_LANES), 1)
    n_sub = tq // SUB_Q
    for i in range(n_seq):
        kbuf[i, 0:R] = kp_ref[i]
        kbuf[i, R:R + tq] = km_ref[i]
        kbuf[i, R + tq:] = kn_ref[i]
        vbuf[i, 0:R] = vp_ref[i]
        vbuf[i, R:R + tq] = vm_ref[i]
        vbuf[i, R + tq:] = vn_ref[i]
        for s in range(n_sub):
            r0 = s * SUB_Q
            edge = None
            if s == 0 or s == n_sub - 1:
                pos = q0 + (r0 - R) + lax.broadcasted_iota(jnp.int32, (1, SUB_K), 1)
                edge = jnp.where(pos >= 0, jnp.where(pos < seq, 0.0, NEG_INF), NEG_INF)
            m_tile = s_tile = None
            for h in range(HEADS_PER_GROUP):
                c0 = h * HEAD_DIM_A
                q = q_ref[i, r0:r0 + SUB_Q, c0:c0 + HEAD_DIM_A]
                k = kbuf[i, r0:r0 + SUB_K, c0:c0 + HEAD_DIM_A]
                v = vbuf[i, r0:r0 + SUB_K, c0:c0 + HEAD_DIM_A]
                logits = lax.dot_general(q, k, (((1,), (1,)), ((), ())), preferred_element_type=f32) + bias_ref[h]
                if edge is not None:
                    logits = logits + edge
                m = jnp.max(logits, axis=-1, keepdims=True)
                p = jnp.exp(logits - m)
                ssum = jnp.sum(p, axis=-1, keepdims=True)
                o = jnp.dot(p.astype(bf16), v, preferred_element_type=f32) * (1.0 / ssum)
                o_ref[i, r0:r0 + SUB_Q, c0:c0 + HEAD_DIM_A] = o.astype(o_ref.dtype)
                m_tile = m if m_tile is None else jnp.where(lane >= h * LSE_REP, m, m_tile)
                s_tile = ssum if s_tile is None else jnp.where(lane >= h * LSE_REP, ssum, s_tile)
            lse_ref[i, r0:r0 + SUB_Q, :] = m_tile + jnp.log(s_tile)


def _attn_group(qkv, rel_bias, g):
    _, d = ATTN_GROUPS[g]
    n, L, _ = qkv.shape
    tq = min(ATTN_ROWS_PER_STEP, L)
    ns = ATTN_ROWS_PER_STEP // tq
    R = ATTN_RADIUS
    bias = _band_bias(rel_bias[:, g * HEADS_PER_GROUP:(g + 1) * HEADS_PER_GROUP], d)
    rb = tq // R
    last_rb = L // R - 1

    def main(col, width=GROUP_WIDTH):
        return pl.BlockSpec((ns, tq, width), lambda b, t: (b, t, col))

    def prev(col):
        return pl.BlockSpec((ns, R, GROUP_WIDTH), lambda b, t: (b, jnp.maximum(t * rb - 1, 0), col))

    def nxt(col):
        return pl.BlockSpec((ns, R, GROUP_WIDTH), lambda b, t: (b, jnp.minimum((t + 1) * rb, last_rb), col))

    return pl.pallas_call(
        functools.partial(_attn_kernel, tq=tq, seq=L, n_seq=ns),
        grid=(n // ns, L // tq),
        in_specs=[main(0), prev(1), main(1), nxt(1), prev(2), main(2), nxt(2),
                  pl.BlockSpec((HEADS_PER_GROUP, SUB_Q, SUB_K), lambda b, t: (0, 0, 0))],
        out_specs=[main(0), main(0, LSE_LANES)],
        out_shape=[jax.ShapeDtypeStruct((n, L, GROUP_WIDTH), bf16),
                   jax.ShapeDtypeStruct((n, L, LSE_LANES), f32)],
        scratch_shapes=[pltpu.VMEM((ns, tq + 2 * R, GROUP_WIDTH), bf16),
                        pltpu.VMEM((ns, tq + 2 * R, GROUP_WIDTH), bf16)],
        compiler_params=_cparams("parallel", "arbitrary"),
        name=f"attn_g{g}",
    )(qkv, qkv, qkv, qkv, qkv, qkv, qkv, bias)


LRU_CHUNK = 256
LRU_PAD = 8
LRU_FINISH_ROWS = 512
LRU_UNITS_PER_STEP = 2
GATE_BIAS_ROWS = 3
LRU_SEGS = 8
SEG_GAP = 4


def _lru_kernel(xb_ref, yb_ref, cw_ref, cb_ref, w_ref, lam_ref, o_ref, *scratch, seq):
    for u in range(LRU_UNITS_PER_STEP):
        lanes = slice(u * LRU_BW, (u + 1) * LRU_BW)
        _lru_unit(xb_ref.at[0, :, lanes], yb_ref.at[0, :, lanes], cw_ref.at[:, lanes], cb_ref.at[:, lanes],
                  w_ref.at[u], lam_ref.at[:, lanes], o_ref.at[0, :, lanes], *scratch, seq=seq)


def _lru_unit(xb_ref, yb_ref, cw_ref, cb_ref, w_ref, lam_ref, o_ref,
              xpad, af, bf, ab, bb, htf, ptf, htb, ptb, cf_scr, cb_scr, *, seq):
    R = LRU_CHUNK
    P = LRU_PAD
    seg_len = seq // LRU_SEGS
    pitch = seg_len + SEG_GAP
    chunks_per_seg = seg_len // R
    n_chunks = seq // R
    xpad[0:P] = jnp.zeros((P, LRU_BW), f32)
    xpad[P + seq:] = jnp.zeros((P, LRU_BW), f32)
    xpad[P:P + seq] = xb_ref[...].astype(f32)
    lam = lam_ref[...]
    log_a_unit = -LRU_C * (jnp.maximum(-lam, 0.0) + jnp.log1p(jnp.exp(-jnp.abs(lam))))
    cw = cw_ref[...]
    cb = cb_ref[...]
    row = lax.broadcasted_iota(jnp.int32, (R, LRU_BW), 0)
    lane = lax.broadcasted_iota(jnp.int32, (R, LRU_BW), 1)
    bias_cols = jnp.where(lane < GATE_BIAS_ROWS, 1.0, 0.0).astype(bf16)

    def chunk(ci, first=False, last=False):
        c0 = ci * R
        dst = (ci // chunks_per_seg) * pitch + (ci % chunks_per_seg) * R
        xc = (cw[0:1] * xpad[pl.ds(c0 + (P - 1), R), :] + cw[1:2] * xpad[pl.ds(c0 + P, R), :]
              + cw[2:3] * xpad[pl.ds(c0 + (P + 1), R), :] + cw[3:4] * xpad[pl.ds(c0 + (P + 2), R), :]) + cb
        lhs = jnp.concatenate([xc.astype(bf16), bias_cols], axis=1)
        th = jnp.tanh(jnp.dot(lhs, w_ref[...], preferred_element_type=f32))
        half_xc = 0.5 * xc
        for direction, (a_scr, b_scr) in enumerate(((af, bf), (ab, bb))):
            base = direction * 2 * LRU_BW
            half_log2_a = (0.5 * math.log2(math.e)) * log_a_unit[direction:direction + 1]
            a = jnp.exp2(half_log2_a * th[:, base:base + LRU_BW] + half_log2_a)
            gated_x = half_xc * th[:, base + LRU_BW:base + 2 * LRU_BW] + half_xc
            y = 1.0 - a * a
            mult = y * lax.rsqrt(jnp.maximum(y, 1e-30))
            if direction == 0 and first:
                mult = jnp.where(row == 0, 1.0, mult)
            if direction == 1 and last:
                mult = jnp.where(row == R - 1, 1.0, mult)
            a_scr[pl.ds(dst, R), :] = a
            b_scr[pl.ds(dst, R), :] = mult * gated_x

    for ci in range(n_chunks):
        chunk(ci, first=ci == 0, last=ci == n_chunks - 1)

    def scan(i, carry):
        hf, pf, hb, pb = carry
        rows = pl.ds(i, LRU_SEGS, stride=pitch)
        a = af[rows, :]
        hf = a * hf + bf[rows, :]
        pf = a * pf
        htf[rows, :] = hf
        ptf[rows, :] = pf
        rows = pl.ds(seg_len - 1 - i, LRU_SEGS, stride=pitch)
        a = ab[rows, :]
        hb = a * hb + bb[rows, :]
        pb = a * pb
        htb[rows, :] = hb
        ptb[rows, :] = pb
        return hf, pf, hb, pb

    zero = jnp.zeros((LRU_SEGS, LRU_BW), f32)
    one = jnp.ones((LRU_SEGS, LRU_BW), f32)
    hf, pf, hb, pb = lax.fori_loop(0, seg_len, scan, (zero, one, zero, one), unroll=8)

    c = jnp.zeros((1, LRU_BW), f32)
    cf_scr[0:1] = c
    for j in range(1, LRU_SEGS):
        c = hf[j - 1:j] + pf[j - 1:j] * c
        cf_scr[j:j + 1] = c
    c = jnp.zeros((1, LRU_BW), f32)
    cb_scr[LRU_SEGS - 1:LRU_SEGS] = c
    for j in range(LRU_SEGS - 2, -1, -1):
        c = hb[j + 1:j + 2] + pb[j + 1:j + 2] * c
        cb_scr[j:j + 1] = c

    F = LRU_FINISH_ROWS
    finish_per_seg = seg_len // F

    for ci in range(seq // F):
        c0 = ci * F
        seg = ci // finish_per_seg
        rows = pl.ds(seg * pitch + (ci % finish_per_seg) * F, F)
        h = (htf[rows, :] + ptf[rows, :] * cf_scr[seg:seg + 1, :]
             + htb[rows, :] + ptb[rows, :] * cb_scr[seg:seg + 1, :])
        o_ref[c0:c0 + F, :] = (h * yb_ref[c0:c0 + F, :].astype(f32)).astype(o_ref.dtype)


def _pack_lru_gates(w, b):
    rows, rest = [], b
    for _ in range(GATE_BIAS_ROWS):
        piece = rest.astype(bf16)
        rows.append(piece)
        rest = rest - piece.astype(f32)
    bias_rows = jnp.pad(jnp.stack(rows, axis=1), ((0, 0), (0, LRU_BW - GATE_BIAS_ROWS), (0, 0)))
    return jnp.concatenate([w.astype(bf16), bias_rows], axis=1)


def _lru(proj3, conv_w, conv_b, w_gates, lam):
    B, S, _ = proj3.shape
    U = LRU_UNITS_PER_STEP
    W = U * LRU_BW
    yb0 = REST_YB // W
    return pl.pallas_call(
        functools.partial(_lru_kernel, seq=S),
        grid=(B, LRU_BLOCKS // U),
        in_specs=[
            pl.BlockSpec((1, S, W), lambda b, n: (b, 0, n)),
            pl.BlockSpec((1, S, W), lambda b, n: (b, 0, yb0 + n)),
            pl.BlockSpec((4, W), lambda b, n: (0, n)),
            pl.BlockSpec((1, W), lambda b, n: (0, n)),
            pl.BlockSpec((U, 2 * LRU_BW, 4 * LRU_BW), lambda b, n: (n, 0, 0)),
            pl.BlockSpec((2, W), lambda b, n: (0, n)),
        ],
        out_specs=pl.BlockSpec((1, S, W), lambda b, n: (b, 0, n)),
        out_shape=jax.ShapeDtypeStruct((B, S, LRU_WIDTH), bf16),
        scratch_shapes=([pltpu.VMEM((S + 2 * LRU_PAD, LRU_BW), f32)]
                        + [pltpu.VMEM((S + LRU_SEGS * SEG_GAP, LRU_BW), f32)] * 8
                        + [pltpu.VMEM((LRU_SEGS, LRU_BW), f32)] * 2),
        compiler_params=_cparams("parallel", "parallel"),
        name="lru",
    )(proj3, proj3, conv_w, conv_b, w_gates, lam)


def _mem_kv_kernel(m_ref, g_ref, w_ref, o_ref, h_scr):
    @pl.when(pl.program_id(0) == 0)
    def _():
        h_scr[...] = _rms(m_ref[...], g_ref[...]).astype(bf16)

    o_ref[...] = jnp.dot(h_scr[...], w_ref[...], preferred_element_type=f32).astype(o_ref.dtype)


def _mem_kv(mem2, gain, w, tn=1024):
    M = mem2.shape[0]
    N = w.shape[1]
    return pl.pallas_call(
        _mem_kv_kernel,
        grid=(N // tn,),
        in_specs=[pl.BlockSpec((M, D_MODEL), lambda j: (0, 0)),
                  pl.BlockSpec((1, D_MODEL), lambda j: (0, 0)),
                  pl.BlockSpec((D_MODEL, tn), lambda j: (0, j))],
        out_specs=pl.BlockSpec((M, tn), lambda j: (0, j)),
        out_shape=jax.ShapeDtypeStruct((M, N), bf16),
        scratch_shapes=[pltpu.VMEM((M, D_MODEL), bf16)],
        compiler_params=_cparams("arbitrary"),
        name="mem_kv",
    )(mem2, gain, w)


def _xattn_kernel(q0_ref, q1_ref, q2_ref, q3_ref, kv_ref, o_ref):
    for h, q_ref in enumerate((q0_ref, q1_ref, q2_ref, q3_ref)):
        c0 = h * MEM_HEAD_DIM
        k = kv_ref[0, :, c0:c0 + MEM_HEAD_DIM]
        v = kv_ref[0, :, MEM_WIDTH + c0:MEM_WIDTH + c0 + MEM_HEAD_DIM]
        logits = lax.dot_general(q_ref[0], k, (((1,), (1,)), ((), ())), preferred_element_type=f32)
        m = jnp.max(logits, axis=-1, keepdims=True)
        p = jnp.exp(logits - m)
        ssum = jnp.sum(p, axis=-1, keepdims=True)
        o = jnp.dot(p.astype(bf16), v, preferred_element_type=f32) * (1.0 / ssum)
        o_ref[0, :, c0:c0 + MEM_HEAD_DIM] = o.astype(o_ref.dtype)


def _xattn(proj3, kv3, tq=2048):
    B, S, _ = proj3.shape
    qb0 = REST_QC // MEM_HEAD_DIM

    def qspec(h):
        return pl.BlockSpec((1, tq, MEM_HEAD_DIM), lambda b, t: (b, t, qb0 + h))

    return pl.pallas_call(
        _xattn_kernel,
        grid=(B, S // tq),
        in_specs=[qspec(0), qspec(1), qspec(2), qspec(3),
                  pl.BlockSpec((1, N_MEM, 2 * MEM_WIDTH), lambda b, t: (b, 0, 0))],
        out_specs=pl.BlockSpec((1, tq, MEM_WIDTH), lambda b, t: (b, t, 0)),
        out_shape=jax.ShapeDtypeStruct((B, S, MEM_WIDTH), bf16),
        compiler_params=_cparams("parallel", "parallel"),
        name="xattn",
    )(proj3, proj3, proj3, proj3, kv3)


def _combine_kernel(o0_ref, o1_ref, o2_ref, l0_ref, l1_ref, l2_ref, ya_ref,
                    o1_scr, o2_scr, l1_scr, l2_scr, tmp_scr, *, tm):
    step = DEINTERLEAVE_STEP
    for g, o_ref, l_ref, o_scr, l_scr in ((1, o1_ref, l1_ref, o1_scr, l1_scr),
                                          (2, o2_ref, l2_ref, o2_scr, l2_scr)):
        d = ATTN_GROUPS[g][1]
        slabs = [(l_scr, lambda r: l_ref[0, r])]
        slabs += [(o_scr.at[h], lambda r, h=h: o_ref[0, r, :, h * HEAD_DIM_A:(h + 1) * HEAD_DIM_A].astype(f32))
                  for h in range(HEADS_PER_GROUP)]
        for k, (dst, rows_of) in enumerate(slabs):
            if d == step:
                for r in range(d):
                    dst[pl.ds(r, tm // d, stride=d), :] = rows_of(r)
                continue
            tmp = tmp_scr.at[k]
            for r in range(d):
                tmp[pl.ds((r % step) * (tm // step) + r // step, tm // d, stride=step), :] = rows_of(r)
            for lo in range(step):
                dst[pl.ds(lo, tm // step, stride=step), :] = tmp[lo * (tm // step):(lo + 1) * (tm // step), :]
    l0, l1, l2 = l0_ref[...], l1_scr[...], l2_scr[...]
    m = jnp.maximum(jnp.maximum(l0, l1), l2)
    e0, e1, e2 = jnp.exp(l0 - m), jnp.exp(l1 - m), jnp.exp(l2 - m)
    inv = 1.0 / (e0 + e1 + e2)
    for h in range(HEADS_PER_GROUP):
        c0 = h * HEAD_DIM_A
        lane = slice(h * LSE_REP, h * LSE_REP + 1)
        y = ((e0 * inv)[:, lane] * o0_ref[:, c0:c0 + HEAD_DIM_A].astype(f32)
             + (e1 * inv)[:, lane] * o1_scr[h] + (e2 * inv)[:, lane] * o2_scr[h])
        ya_ref[:, c0:c0 + HEAD_DIM_A] = y.astype(bf16)


def _combine(o_groups, lse_groups, seq, tm=1024):
    T = o_groups[0].shape[0]
    nt = seq // tm
    d1, d2 = ATTN_GROUPS[1][1], ATTN_GROUPS[2][1]

    def rows(width):
        return pl.BlockSpec((tm, width), lambda i: (i, 0))

    def strided_rows(d, width):
        return pl.BlockSpec((1, d, tm // d, width), lambda i: (i // nt, 0, i % nt, 0))

    return pl.pallas_call(
        functools.partial(_combine_kernel, tm=tm),
        grid=(T // tm,),
        in_specs=[rows(GROUP_WIDTH), strided_rows(d1, GROUP_WIDTH), strided_rows(d2, GROUP_WIDTH),
                  rows(LSE_LANES), strided_rows(d1, LSE_LANES), strided_rows(d2, LSE_LANES)],
        out_specs=rows(GROUP_WIDTH),
        out_shape=jax.ShapeDtypeStruct((T, GROUP_WIDTH), bf16),
        scratch_shapes=[pltpu.VMEM((HEADS_PER_GROUP, tm, HEAD_DIM_A), f32),
                        pltpu.VMEM((HEADS_PER_GROUP, tm, HEAD_DIM_A), f32),
                        pltpu.VMEM((tm, LSE_LANES), f32), pltpu.VMEM((tm, LSE_LANES), f32),
                        pltpu.VMEM((1 + HEADS_PER_GROUP, tm, LANES), f32)],
        compiler_params=_cparams("parallel"),
        name="combine",
    )(*o_groups, *lse_groups)


def _gate_mix_kernel(h_ref, ya_ref, yl_ref, yc_ref, wga_ref, wgb_ref, wgc_ref, bga_ref, bgb_ref, bgc_ref,
                     woa_ref, wol_ref, wom_ref, wu_ref, wd_ref, mix_ref, wu_o_ref, wd_o_ref):
    wu_o_ref[...] = wu_ref[...].astype(bf16)
    wd_o_ref[...] = wd_ref[...].astype(bf16)
    mb = h_ref.shape[0] // GATE_ROW_BLOCKS
    for k in range(GATE_ROW_BLOCKS):
        rows = slice(k * mb, (k + 1) * mb)
        h = h_ref[rows, :]

        def gate(w_ref, b_ref):
            return jax.nn.sigmoid(jnp.dot(h, w_ref[...], preferred_element_type=f32) + b_ref[...])

        mixed = (gate(wga_ref, bga_ref) * jnp.dot(ya_ref[rows, :], woa_ref[...], preferred_element_type=f32)
                 + gate(wgb_ref, bgb_ref) * jnp.dot(yl_ref[rows, :], wol_ref[...], preferred_element_type=f32)
                 + gate(wgc_ref, bgc_ref) * jnp.dot(yc_ref[rows, :], wom_ref[...], preferred_element_type=f32))
        mix_ref[rows, :] = mixed.astype(mix_ref.dtype)


GATE_ROW_BLOCKS = 2


def _gate_mix(h, y_a, y_lru, y_c, w_gate, b_gate, w_o_attn, w_o_lru, w_o_mem, w_up, w_down, tm=1024, tn=512):
    T = h.shape[0]
    nj = D_MODEL // tn
    ni = T // tm
    up_rows, down_rows = D_MODEL // (nj * ni), D_FF // (nj * ni)

    def slab(rows_per_step, width):
        return pl.BlockSpec((rows_per_step, width), lambda j, i: (j * ni + i, 0))

    def rows(width):
        return pl.BlockSpec((tm, width), lambda j, i: (i, 0))

    def gate_w(k):
        return pl.BlockSpec((D_MODEL, tn), lambda j, i: (0, k * nj + j))

    def gate_b(k):
        return pl.BlockSpec((1, tn), lambda j, i: (0, k * nj + j))

    def cols(width):
        return pl.BlockSpec((width, tn), lambda j, i: (0, j))

    return pl.pallas_call(
        _gate_mix_kernel,
        grid=(nj, T // tm),
        in_specs=[rows(D_MODEL), rows(GROUP_WIDTH), rows(LRU_WIDTH), rows(MEM_WIDTH),
                  gate_w(0), gate_w(1), gate_w(2), gate_b(0), gate_b(1), gate_b(2),
                  cols(GROUP_WIDTH), cols(LRU_WIDTH), cols(MEM_WIDTH),
                  slab(up_rows, D_FF), slab(down_rows, D_MODEL)],
        out_specs=[pl.BlockSpec((tm, tn), lambda j, i: (i, j)), slab(up_rows, D_FF), slab(down_rows, D_MODEL)],
        out_shape=[jax.ShapeDtypeStruct((T, D_MODEL), bf16),
                   jax.ShapeDtypeStruct((D_MODEL, D_FF), bf16),
                   jax.ShapeDtypeStruct((D_FF, D_MODEL), bf16)],
        compiler_params=_cparams("arbitrary", "arbitrary"),
        name="gate_mix",
    )(h, y_a, y_lru, y_c, w_gate, w_gate, w_gate, b_gate, b_gate, b_gate, w_o_attn, w_o_lru, w_o_mem,
      w_up, w_down)


def _mlp_kernel(x_ref, mix_ref, wo_ref, g_ref, gf_ref, wu_ref, wd_ref, out_ref, h_scr):
    j = pl.program_id(1)

    @pl.when(j == 0)
    def _():
        x = x_ref[...] + jnp.dot(mix_ref[...], wo_ref[...], preferred_element_type=f32)
        h_scr[...] = _rms(x, g_ref[...]).astype(bf16)
        out_ref[...] = x

    u = jnp.maximum(jnp.dot(h_scr[...], wu_ref[...], preferred_element_type=f32), 0.0)
    out_ref[...] += jnp.dot((u * u).astype(bf16), wd_ref[...], preferred_element_type=f32)

    @pl.when(j == pl.num_programs(1) - 1)
    def _():
        out_ref[...] = _rms(out_ref[...], gf_ref[...])


def _mlp(x2, mixed, w_out, gain, gain_final, w_up, w_down, tm=512, tf=1024):
    T = x2.shape[0]
    return pl.pallas_call(
        _mlp_kernel,
        grid=(T // tm, D_FF // tf),
        in_specs=[pl.BlockSpec((tm, D_MODEL), lambda i, j: (i, 0)),
                  pl.BlockSpec((tm, D_MODEL), lambda i, j: (i, 0)),
                  pl.BlockSpec((D_MODEL, D_MODEL), lambda i, j: (0, 0)),
                  pl.BlockSpec((1, D_MODEL), lambda i, j: (0, 0)),
                  pl.BlockSpec((1, D_MODEL), lambda i, j: (0, 0)),
                  pl.BlockSpec((D_MODEL, tf), lambda i, j: (0, j)),
                  pl.BlockSpec((tf, D_MODEL), lambda i, j: (j, 0))],
        out_specs=pl.BlockSpec((tm, D_MODEL), lambda i, j: (i, 0)),
        out_shape=jax.ShapeDtypeStruct((T, D_MODEL), f32),
        scratch_shapes=[pltpu.VMEM((tm, D_MODEL), bf16)],
        compiler_params=_cparams("parallel", "arbitrary"),
        name="mlp",
    )(x2, mixed, w_out, gain, gain_final, w_up, w_down)


def _query_scale():
    scale = np.ones((1, N_IN), np.float32)
    scale[:, :WIDTH_A] = 1.0 / math.sqrt(HEAD_DIM_A)
    scale[:, COL_QC:] = 1.0 / math.sqrt(MEM_HEAD_DIM)
    return scale


def kernel(x, mem, rel_bias, norm_mix, norm_mem, norm_mlp, norm_final, w_in, w_gate, b_gate, conv_w, conv_b,
           lru_wa, lru_ba, lru_wi, lru_bi, lru_lambda, w_mem_kv, w_o_attn, w_o_lru, w_o_mem, w_out, w_up, w_down):
    B, S, D = x.shape
    T = B * S
    depth = w_in.shape[0]
    assert depth == 1, "the final RMSNorm is fused into the (single) layer's MLP kernel"
    x2 = x.reshape(T, D)
    mem2 = mem.reshape(B * N_MEM, D)
    for l in range(depth):
        w_qkv = (w_in[l] * _query_scale()).astype(bf16)
        h, rest = _rest_proj(x2, norm_mix[l].reshape(1, D), w_qkv)
        proj3 = rest.reshape(B, S, REST_W)

        side_casts = ((w_gate[l],), (w_out[l], w_o_lru[l]), (w_o_attn[l], w_o_mem[l], w_mem_kv[l]))
        casted = []
        attn = []
        for g in range(len(ATTN_GROUPS)):
            d = ATTN_GROUPS[g][1]
            qkv, *bf_copies = _qkv_proj(h, w_qkv, g, B, S, cast=side_casts[g])
            casted.append(bf_copies)
            o, lse = _attn_group(qkv.reshape(B * d, S // d, QKV_W), rel_bias, g)
            if g == 0:
                attn.append((o.reshape(T, GROUP_WIDTH), lse.reshape(T, LSE_LANES)))
            else:
                attn.append((o.reshape(B, d, S // d, GROUP_WIDTH), lse.reshape(B, d, S // d, LSE_LANES)))

        w_gates = 0.5 * jnp.concatenate([lru_wa[l, 0], lru_wi[l, 0], lru_wa[l, 1], lru_wi[l, 1]], axis=-1)
        b_gates = 0.5 * jnp.concatenate([lru_ba[l, 0], lru_bi[l, 0], lru_ba[l, 1], lru_bi[l, 1]], axis=-1)
        y_lru = _lru(proj3, conv_w[l], conv_b[l].reshape(1, LRU_WIDTH), _pack_lru_gates(w_gates, b_gates),
                     lru_lambda[l])

        (w_gate_bf,), (w_out_bf, w_o_lru_bf), (w_o_attn_bf, w_o_mem_bf, w_mem_kv_bf) = casted
        kv = _mem_kv(mem2, norm_mem[l].reshape(1, D), w_mem_kv_bf)
        y_c = _xattn(proj3, kv.reshape(B, N_MEM, 2 * MEM_WIDTH))

        y_a = _combine([a[0] for a in attn], [a[1] for a in attn], S)
        mixed, w_up_bf, w_down_bf = _gate_mix(
            h, y_a, y_lru.reshape(T, LRU_WIDTH), y_c.reshape(T, MEM_WIDTH), w_gate_bf, b_gate[l].reshape(1, 3 * D),
            w_o_attn_bf, w_o_lru_bf, w_o_mem_bf, w_up[l], w_down[l])
        x2 = _mlp(x2, mixed, w_out_bf, norm_mlp[l].reshape(1, D), norm_final.reshape(1, D),
                  w_up_bf, w_down_bf)
    return x2.reshape(B, S, D)
```

```python
import functools
import math

import jax
import jax.numpy as jnp
import numpy as np
from jax import lax
from jax.experimental import pallas as pl
from jax.experimental.pallas import tpu as pltpu

D_MODEL = 2048
HEAD_DIM_A = 128
ATTN_GROUPS = ((128, 1), (512, 4), (2048, 16))
HEADS_PER_GROUP = 4
GROUP_WIDTH = HEADS_PER_GROUP * HEAD_DIM_A
WIDTH_A = len(ATTN_GROUPS) * GROUP_WIDTH
ATTN_RADIUS = 64
N_BUCKETS = 32
MAX_DISTANCE = 1024
LRU_WIDTH = 1536
LRU_BLOCKS = 12
LRU_BW = 128
LRU_C = 8.0
N_MEM = 256
MEM_HEADS = 4
MEM_HEAD_DIM = 256
MEM_WIDTH = MEM_HEADS * MEM_HEAD_DIM
D_FF = 4 * D_MODEL
EPS = 1e-6
N_IN = 3 * WIDTH_A + 2 * LRU_WIDTH + MEM_WIDTH
COL_K = WIDTH_A
COL_V = 2 * WIDTH_A
COL_XB = 3 * WIDTH_A
COL_YB = 3 * WIDTH_A + LRU_WIDTH
COL_QC = 3 * WIDTH_A + 2 * LRU_WIDTH
NEG_INF = -1e30

ATTN_ROWS_PER_STEP = 2048
SUB_Q = 128
SUB_K = SUB_Q + 2 * ATTN_RADIUS
LSE_LANES = 128
LSE_REP = LSE_LANES // HEADS_PER_GROUP

VMEM_LIMIT = 56 * 1024 * 1024

f32 = jnp.float32
bf16 = jnp.bfloat16


def _cparams(*sem):
    return pltpu.CompilerParams(dimension_semantics=sem, vmem_limit_bytes=VMEM_LIMIT)


def _rms(x, gain):
    return x * lax.rsqrt(jnp.mean(x * x, axis=-1, keepdims=True) + EPS) * gain


QKV_W = 3 * GROUP_WIDTH
REST_W = 2 * LRU_WIDTH + MEM_WIDTH
REST_YB = LRU_WIDTH
REST_QC = 2 * LRU_WIDTH
PROJ_TN = GROUP_WIDTH
LANES = 128
SLABS = PROJ_TN // LANES
PROJ_ROW_BLOCKS = 2
DEINTERLEAVE_STEP = 4
N_STAGE = 2


def _qkv_kernel(h_ref, wq_ref, wk_ref, wv_ref, *refs, d, tm, n_cast):
    cast_in, o_ref = refs[:n_cast], refs[n_cast]
    cast_out = refs[n_cast + 1:2 * n_cast + 1]
    res_scr, tmp_scr = refs[2 * n_cast + 1:]
    for src_ref, dst_ref in zip(cast_in, cast_out):
        dst_ref[...] = src_ref[...].astype(bf16)
    mb = tm // PROJ_ROW_BLOCKS
    n = 0
    for k in range(PROJ_ROW_BLOCKS):
        hk = h_ref[k * mb:(k + 1) * mb, :]
        for t, w_ref in enumerate((wq_ref, wk_ref, wv_ref)):
            res = jnp.dot(hk, w_ref[...], preferred_element_type=f32)
            col = t * PROJ_TN
            if d == 1:
                o_ref[0, k * mb:(k + 1) * mb, col:col + PROJ_TN] = res.astype(bf16)
                continue
            buf = n % N_STAGE
            n += 1
            for c in range(SLABS):
                res_scr[buf, c] = res[:, c * LANES:(c + 1) * LANES]
            src, step = res_scr, d
            if d == DEINTERLEAVE_STEP ** 2:
                step = DEINTERLEAVE_STEP
                for r in range(step):
                    for c in range(SLABS):
                        tmp_scr[buf, c, r * (mb // step):(r + 1) * (mb // step), :] = (
                            res_scr[buf, c, pl.ds(r, mb // step, stride=step), :])
                src = tmp_scr
            for r in range(d):
                start = r if src is res_scr else (r % step) * (mb // step) + r // step
                for c in range(SLABS):
                    o_ref[0, r, k * (mb // d):(k + 1) * (mb // d), col + c * LANES:col + (c + 1) * LANES] = (
                        src[buf, c, pl.ds(start, mb // d, stride=step), :].astype(bf16))


def _qkv_proj(h, w_qkv, g, batch, seq, cast=(), tm=1024):
    T = h.shape[0]
    d = ATTN_GROUPS[g][1]
    nt = seq // tm
    n_groups = len(ATTN_GROUPS)
    mb = tm // PROJ_ROW_BLOCKS
    steps = T // tm

    def w_spec(which):
        return pl.BlockSpec((D_MODEL, PROJ_TN), lambda i: (0, which * n_groups + g))

    cast_specs = [pl.BlockSpec((w.shape[0] // steps, w.shape[1]), lambda i: (i, 0)) for w in cast]

    if d == 1:
        out_spec = pl.BlockSpec((1, tm, QKV_W), lambda i: (i // nt, i % nt, 0))
        out_shape = jax.ShapeDtypeStruct((batch, seq, QKV_W), bf16)
    else:
        out_spec = pl.BlockSpec((1, d, tm // d, QKV_W), lambda i: (i // nt, 0, i % nt, 0))
        out_shape = jax.ShapeDtypeStruct((batch, d, seq // d, QKV_W), bf16)
    return pl.pallas_call(
        functools.partial(_qkv_kernel, d=d, tm=tm, n_cast=len(cast)),
        grid=(steps,),
        in_specs=[pl.BlockSpec((tm, D_MODEL), lambda i: (i, 0)), w_spec(0), w_spec(1), w_spec(2)] + cast_specs,
        out_specs=[out_spec] + cast_specs,
        out_shape=[out_shape] + [jax.ShapeDtypeStruct(w.shape, bf16) for w in cast],
        scratch_shapes=[pltpu.VMEM((N_STAGE, SLABS, mb, LANES), f32)] * 2,
        compiler_params=_cparams("parallel"),
        name=f"qkv_g{g}",
    )(h, w_qkv, w_qkv, w_qkv, *cast)


REST_TILES = REST_W // PROJ_TN
YB_TILES = range(REST_YB // PROJ_TN, REST_QC // PROJ_TN)


def _gelu_tanh(y):
    return y * (0.5 * (1.0 + jnp.tanh(math.sqrt(2.0 / math.pi) * (y + 0.044715 * (y * y * y)))))


def _rest_kernel(x_ref, g_ref, *refs):
    w_refs = refs[:REST_TILES]
    h_ref, o_ref = refs[REST_TILES:]
    mb = x_ref.shape[0] // PROJ_ROW_BLOCKS
    for k in range(PROJ_ROW_BLOCKS):
        rows = slice(k * mb, (k + 1) * mb)
        h = _rms(x_ref[rows, :], g_ref[...]).astype(bf16)
        h_ref[rows, :] = h
        for c, w_ref in enumerate(w_refs):
            res = jnp.dot(h, w_ref[...], preferred_element_type=f32)
            if c in YB_TILES:
                res = _gelu_tanh(res)
            res = res.astype(bf16)
            for s in range(SLABS):
                o_ref[0, c * SLABS + s, rows, :] = res[:, s * LANES:(s + 1) * LANES]


def _rest_proj(x2, gain, w_in, batch, seq, tm=512):
    T = x2.shape[0]
    steps = T // tm
    nt = seq // tm
    first = COL_XB // PROJ_TN

    def w_spec(c):
        return pl.BlockSpec((D_MODEL, PROJ_TN), lambda i: (0, first + c), pipeline_mode=pl.Buffered(1))

    return pl.pallas_call(
        _rest_kernel,
        grid=(steps,),
        in_specs=[pl.BlockSpec((tm, D_MODEL), lambda i: (i, 0)),
                  pl.BlockSpec((1, D_MODEL), lambda i: (0, 0))]
        + [w_spec(c) for c in range(REST_TILES)],
        out_specs=[pl.BlockSpec((tm, D_MODEL), lambda i: (i, 0)),
                   pl.BlockSpec((1, REST_W // LANES, tm, LANES), lambda i: (i // nt, 0, i % nt, 0))],
        out_shape=[jax.ShapeDtypeStruct((T, D_MODEL), bf16),
                   jax.ShapeDtypeStruct((batch, REST_W // LANES, seq, LANES), bf16)],
        compiler_params=_cparams("arbitrary"),
        name="rest_proj",
    )(x2, gain, *([w_in] * REST_TILES))


def _t5_bucket(rel):
    nb = N_BUCKETS // 2
    max_exact = nb // 2
    sign = (rel > 0).astype(np.int32) * nb
    n = np.abs(rel)
    large = max_exact + (np.log(np.maximum(n, 1) / max_exact)
                         / np.log(MAX_DISTANCE / max_exact) * (nb - max_exact)).astype(np.int32)
    large = np.minimum(large, nb - 1)
    return (sign + np.where(n < max_exact, n, large)).astype(np.int32)


def _band_bias(rel_bias_g, dilation):
    qq = np.arange(SUB_Q)[:, None]
    kk = np.arange(SUB_K)[None, :]
    rel = kk - ATTN_RADIUS - qq
    onehot = (_t5_bucket(rel * dilation)[None] == np.arange(N_BUCKETS)[:, None, None]).astype(np.float32)
    bias = jnp.einsum('nh,nqk->hqk', rel_bias_g.astype(f32), onehot, precision=lax.Precision.HIGHEST)
    return bias + np.where(np.abs(rel) <= ATTN_RADIUS, 0.0, NEG_INF).astype(np.float32)[None]


def _attn_kernel(q_ref, kp_ref, km_ref, kn_ref, vp_ref, vm_ref, vn_ref, bias_ref,
                 o_ref, lse_ref, kbuf, vbuf, *, tq, seq, n_seq):
    R = ATTN_RADIUS
    q0 = pl.program_id(1) * tq
    lane = lax.broadcasted_iota(jnp.int32, (SUB_Q, LSE_LANES), 1)
    n_sub = tq // SUB_Q
    for i in range(n_seq):
        kbuf[i, 0:R] = kp_ref[i]
        kbuf[i, R:R + tq] = km_ref[i]
        kbuf[i, R + tq:] = kn_ref[i]
        vbuf[i, 0:R] = vp_ref[i]
        vbuf[i, R:R + tq] = vm_ref[i]
        vbuf[i, R + tq:] = vn_ref[i]
        for s in range(n_sub):
            r0 = s * SUB_Q
            edge = None
            if s == 0 or s == n_sub - 1:
                pos = q0 + (r0 - R) + lax.broadcasted_iota(jnp.int32, (1, SUB_K), 1)
                edge = jnp.where(pos >= 0, jnp.where(pos < seq, 0.0, NEG_INF), NEG_INF)
            m_tile = s_tile = None
            for h in range(HEADS_PER_GROUP):
                c0 = h * HEAD_DIM_A
                q = q_ref[i, r0:r0 + SUB_Q, c0:c0 + HEAD_DIM_A]
                k = kbuf[i, r0:r0 + SUB_K, c0:c0 + HEAD_DIM_A]
                v = vbuf[i, r0:r0 + SUB_K, c0:c0 + HEAD_DIM_A]
                logits = lax.dot_general(q, k, (((1,), (1,)), ((), ())), preferred_element_type=f32) + bias_ref[h]
                if edge is not None:
                    logits = logits + edge
                m = jnp.max(logits, axis=-1, keepdims=True)
                p = jnp.exp(logits - m)
                ssum = jnp.sum(p, axis=-1, keepdims=True)
                o = jnp.dot(p.astype(bf16), v, preferred_element_type=f32) * (1.0 / ssum)
                o_ref[i, r0:r0 + SUB_Q, c0:c0 + HEAD_DIM_A] = o.astype(o_ref.dtype)
                m_tile = m if m_tile is None else jnp.where(lane >= h * LSE_REP, m, m_tile)
                s_tile = ssum if s_tile is None else jnp.where(lane >= h * LSE_REP, ssum, s_tile)
            lse_ref[i, r0:r0 + SUB_Q, :] = m_tile + jnp.log(s_tile)


def _attn_group(qkv, rel_bias, g):
    _, d = ATTN_GROUPS[g]
    n, L, _ = qkv.shape
    tq = min(ATTN_ROWS_PER_STEP, L)
    ns = ATTN_ROWS_PER_STEP // tq
    R = ATTN_RADIUS
    bias = _band_bias(rel_bias[:, g * HEADS_PER_GROUP:(g + 1) * HEADS_PER_GROUP], d)
    rb = tq // R
    last_rb = L // R - 1

    def main(col, width=GROUP_WIDTH):
        return pl.BlockSpec((ns, tq, width), lambda b, t: (b, t, col))

    def prev(col):
        return pl.BlockSpec((ns, R, GROUP_WIDTH), lambda b, t: (b, jnp.maximum(t * rb - 1, 0), col))

    def nxt(col):
        return pl.BlockSpec((ns, R, GROUP_WIDTH), lambda b, t: (b, jnp.minimum((t + 1) * rb, last_rb), col))

    return pl.pallas_call(
        functools.partial(_attn_kernel, tq=tq, seq=L, n_seq=ns),
        grid=(n // ns, L // tq),
        in_specs=[main(0), prev(1), main(1), nxt(1), prev(2), main(2), nxt(2),
                  pl.BlockSpec((HEADS_PER_GROUP, SUB_Q, SUB_K), lambda b, t: (0, 0, 0))],
        out_specs=[main(0), main(0, LSE_LANES)],
        out_shape=[jax.ShapeDtypeStruct((n, L, GROUP_WIDTH), bf16),
                   jax.ShapeDtypeStruct((n, L, LSE_LANES), f32)],
        scratch_shapes=[pltpu.VMEM((ns, tq + 2 * R, GROUP_WIDTH), bf16),
                        pltpu.VMEM((ns, tq + 2 * R, GROUP_WIDTH), bf16)],
        compiler_params=_cparams("parallel", "arbitrary"),
        name=f"attn_g{g}",
    )(qkv, qkv, qkv, qkv, qkv, qkv, qkv, bias)


LRU_CHUNK = 256
LRU_PAD = 8
LRU_FINISH_ROWS = 512
GATE_BIAS_ROWS = 3
LRU_SEGS = 8
SEG_GAP = 4


def _lru_kernel(xb_ref, yb_ref, cw_ref, cb_ref, w_ref, lam_ref, o_ref,
                xpad, af, bf, ab, bb, htf, ptf, htb, ptb, cf_scr, cb_scr, *, seq):
    R = LRU_CHUNK
    P = LRU_PAD
    seg_len = seq // LRU_SEGS
    pitch = seg_len + SEG_GAP
    chunks_per_seg = seg_len // R
    n_chunks = seq // R
    xpad[0:P] = jnp.zeros((P, LRU_BW), f32)
    xpad[P + seq:] = jnp.zeros((P, LRU_BW), f32)
    xpad[P:P + seq] = xb_ref[0, 0].astype(f32)
    lam = lam_ref[...]
    log_a_unit = -LRU_C * (jnp.maximum(-lam, 0.0) + jnp.log1p(jnp.exp(-jnp.abs(lam))))
    cw = cw_ref[...]
    cb = cb_ref[...]
    row = lax.broadcasted_iota(jnp.int32, (R, LRU_BW), 0)
    lane = lax.broadcasted_iota(jnp.int32, (R, LRU_BW), 1)
    bias_cols = jnp.where(lane < GATE_BIAS_ROWS, 1.0, 0.0).astype(bf16)

    def chunk(ci, first=False, last=False):
        c0 = ci * R
        dst = (ci // chunks_per_seg) * pitch + (ci % chunks_per_seg) * R
        xc = (cw[0:1] * xpad[pl.ds(c0 + (P - 1), R), :] + cw[1:2] * xpad[pl.ds(c0 + P, R), :]
              + cw[2:3] * xpad[pl.ds(c0 + (P + 1), R), :] + cw[3:4] * xpad[pl.ds(c0 + (P + 2), R), :]) + cb
        lhs = jnp.concatenate([xc.astype(bf16), bias_cols], axis=1)
        th = jnp.tanh(jnp.dot(lhs, w_ref[0], preferred_element_type=f32))
        half_xc = 0.5 * xc
        for direction, (a_scr, b_scr) in enumerate(((af, bf), (ab, bb))):
            base = direction * 2 * LRU_BW
            half_log2_a = (0.5 * math.log2(math.e)) * log_a_unit[direction:direction + 1]
            a = jnp.exp2(half_log2_a * th[:, base:base + LRU_BW] + half_log2_a)
            gated_x = half_xc * th[:, base + LRU_BW:base + 2 * LRU_BW] + half_xc
            y = 1.0 - a * a
            mult = y * lax.rsqrt(jnp.maximum(y, 1e-30))
            if direction == 0 and first:
                mult = jnp.where(row == 0, 1.0, mult)
            if direction == 1 and last:
                mult = jnp.where(row == R - 1, 1.0, mult)
            a_scr[pl.ds(dst, R), :] = a
            b_scr[pl.ds(dst, R), :] = mult * gated_x

    for ci in range(n_chunks):
        chunk(ci, first=ci == 0, last=ci == n_chunks - 1)

    def scan(i, carry):
        hf, pf, hb, pb = carry
        rows = pl.ds(i, LRU_SEGS, stride=pitch)
        a = af[rows, :]
        hf = a * hf + bf[rows, :]
        pf = a * pf
        htf[rows, :] = hf
        ptf[rows, :] = pf
        rows = pl.ds(seg_len - 1 - i, LRU_SEGS, stride=pitch)
        a = ab[rows, :]
        hb = a * hb + bb[rows, :]
        pb = a * pb
        htb[rows, :] = hb
        ptb[rows, :] = pb
        return hf, pf, hb, pb

    zero = jnp.zeros((LRU_SEGS, LRU_BW), f32)
    one = jnp.ones((LRU_SEGS, LRU_BW), f32)
    hf, pf, hb, pb = lax.fori_loop(0, seg_len, scan, (zero, one, zero, one), unroll=8)

    c = jnp.zeros((1, LRU_BW), f32)
    cf_scr[0:1] = c
    for j in range(1, LRU_SEGS):
        c = hf[j - 1:j] + pf[j - 1:j] * c
        cf_scr[j:j + 1] = c
    c = jnp.zeros((1, LRU_BW), f32)
    cb_scr[LRU_SEGS - 1:LRU_SEGS] = c
    for j in range(LRU_SEGS - 2, -1, -1):
        c = hb[j + 1:j + 2] + pb[j + 1:j + 2] * c
        cb_scr[j:j + 1] = c

    F = LRU_FINISH_ROWS
    finish_per_seg = seg_len // F

    for ci in range(seq // F):
        c0 = ci * F
        seg = ci // finish_per_seg
        rows = pl.ds(seg * pitch + (ci % finish_per_seg) * F, F)
        h = (htf[rows, :] + ptf[rows, :] * cf_scr[seg:seg + 1, :]
             + htb[rows, :] + ptb[rows, :] * cb_scr[seg:seg + 1, :])
        o_ref[0, c0:c0 + F, :] = (h * yb_ref[0, 0, c0:c0 + F, :].astype(f32)).astype(o_ref.dtype)


def _pack_lru_gates(w, b):
    rows, rest = [], b
    for _ in range(GATE_BIAS_ROWS):
        piece = rest.astype(bf16)
        rows.append(piece)
        rest = rest - piece.astype(f32)
    bias_rows = jnp.pad(jnp.stack(rows, axis=1), ((0, 0), (0, LRU_BW - GATE_BIAS_ROWS), (0, 0)))
    return jnp.concatenate([w.astype(bf16), bias_rows], axis=1)


def _lru(proj3, conv_w, conv_b, w_gates, lam):
    B, _, S, _ = proj3.shape
    yb0 = REST_YB // LRU_BW
    return pl.pallas_call(
        functools.partial(_lru_kernel, seq=S),
        grid=(B, LRU_BLOCKS),
        in_specs=[
            pl.BlockSpec((1, 1, S, LRU_BW), lambda b, n: (b, n, 0, 0)),
            pl.BlockSpec((1, 1, S, LRU_BW), lambda b, n: (b, yb0 + n, 0, 0)),
            pl.BlockSpec((4, LRU_BW), lambda b, n: (0, n)),
            pl.BlockSpec((1, LRU_BW), lambda b, n: (0, n)),
            pl.BlockSpec((1, 2 * LRU_BW, 4 * LRU_BW), lambda b, n: (n, 0, 0)),
            pl.BlockSpec((2, LRU_BW), lambda b, n: (0, n)),
        ],
        out_specs=pl.BlockSpec((1, S, LRU_BW), lambda b, n: (b, 0, n)),
        out_shape=jax.ShapeDtypeStruct((B, S, LRU_WIDTH), bf16),
        scratch_shapes=([pltpu.VMEM((S + 2 * LRU_PAD, LRU_BW), f32)]
                        + [pltpu.VMEM((S + LRU_SEGS * SEG_GAP, LRU_BW), f32)] * 8
                        + [pltpu.VMEM((LRU_SEGS, LRU_BW), f32)] * 2),
        compiler_params=_cparams("parallel", "parallel"),
        name="lru",
    )(proj3, proj3, conv_w, conv_b, w_gates, lam)


def _mem_kv_kernel(m_ref, g_ref, w_ref, o_ref, h_scr):
    @pl.when(pl.program_id(0) == 0)
    def _():
        h_scr[...] = _rms(m_ref[...], g_ref[...]).astype(bf16)

    o_ref[...] = jnp.dot(h_scr[...], w_ref[...], preferred_element_type=f32).astype(o_ref.dtype)


def _mem_kv(mem2, gain, w, tn=512):
    M = mem2.shape[0]
    N = w.shape[1]
    return pl.pallas_call(
        _mem_kv_kernel,
        grid=(N // tn,),
        in_specs=[pl.BlockSpec((M, D_MODEL), lambda j: (0, 0)),
                  pl.BlockSpec((1, D_MODEL), lambda j: (0, 0)),
                  pl.BlockSpec((D_MODEL, tn), lambda j: (0, j))],
        out_specs=pl.BlockSpec((M, tn), lambda j: (0, j)),
        out_shape=jax.ShapeDtypeStruct((M, N), bf16),
        scratch_shapes=[pltpu.VMEM((M, D_MODEL), bf16)],
        compiler_params=_cparams("arbitrary"),
        name="mem_kv",
    )(mem2, gain, w)


def _xattn_kernel(q_ref, kv_ref, o_ref):
    per_head = MEM_HEAD_DIM // LANES
    for h in range(MEM_HEADS):
        c0 = h * MEM_HEAD_DIM
        k = kv_ref[0, :, c0:c0 + MEM_HEAD_DIM]
        v = kv_ref[0, :, MEM_WIDTH + c0:MEM_WIDTH + c0 + MEM_HEAD_DIM]
        q = jnp.concatenate([q_ref[0, h * per_head + s] for s in range(per_head)], axis=1)
        logits = lax.dot_general(q, k, (((1,), (1,)), ((), ())), preferred_element_type=f32)
        m = jnp.max(logits, axis=-1, keepdims=True)
        p = jnp.exp(logits - m)
        ssum = jnp.sum(p, axis=-1, keepdims=True)
        o = jnp.dot(p.astype(bf16), v, preferred_element_type=f32) * (1.0 / ssum)
        o_ref[0, :, c0:c0 + MEM_HEAD_DIM] = o.astype(o_ref.dtype)


def _xattn(proj3, kv3, tq=2048):
    B, _, S, _ = proj3.shape
    q_slabs = MEM_WIDTH // LANES
    return pl.pallas_call(
        _xattn_kernel,
        grid=(B, S // tq),
        in_specs=[pl.BlockSpec((1, q_slabs, tq, LANES), lambda b, t: (b, REST_QC // MEM_WIDTH, t, 0)),
                  pl.BlockSpec((1, N_MEM, 2 * MEM_WIDTH), lambda b, t: (b, 0, 0))],
        out_specs=pl.BlockSpec((1, tq, MEM_WIDTH), lambda b, t: (b, t, 0)),
        out_shape=jax.ShapeDtypeStruct((B, S, MEM_WIDTH), bf16),
        compiler_params=_cparams("parallel", "parallel"),
        name="xattn",
    )(proj3, kv3)


def _combine_kernel(o0_ref, o1_ref, o2_ref, l0_ref, l1_ref, l2_ref, ya_ref,
                    o1_scr, o2_scr, l1_scr, l2_scr, tmp_scr, *, tm):
    step = DEINTERLEAVE_STEP
    for g, o_ref, l_ref, o_scr, l_scr in ((1, o1_ref, l1_ref, o1_scr, l1_scr),
                                          (2, o2_ref, l2_ref, o2_scr, l2_scr)):
        d = ATTN_GROUPS[g][1]
        slabs = [(l_scr, lambda r: l_ref[0, r])]
        slabs += [(o_scr.at[h], lambda r, h=h: o_ref[0, r, :, h * HEAD_DIM_A:(h + 1) * HEAD_DIM_A].astype(f32))
                  for h in range(HEADS_PER_GROUP)]
        for k, (dst, rows_of) in enumerate(slabs):
            if d == step:
                for r in range(d):
                    dst[pl.ds(r, tm // d, stride=d), :] = rows_of(r)
                continue
            tmp = tmp_scr.at[k]
            for r in range(d):
                tmp[pl.ds((r % step) * (tm // step) + r // step, tm // d, stride=step), :] = rows_of(r)
            for lo in range(step):
                dst[pl.ds(lo, tm // step, stride=step), :] = tmp[lo * (tm // step):(lo + 1) * (tm // step), :]
    l0, l1, l2 = l0_ref[...], l1_scr[...], l2_scr[...]
    m = jnp.maximum(jnp.maximum(l0, l1), l2)
    e0, e1, e2 = jnp.exp(l0 - m), jnp.exp(l1 - m), jnp.exp(l2 - m)
    inv = 1.0 / (e0 + e1 + e2)
    for h in range(HEADS_PER_GROUP):
        c0 = h * HEAD_DIM_A
        lane = slice(h * LSE_REP, h * LSE_REP + 1)
        y = ((e0 * inv)[:, lane] * o0_ref[:, c0:c0 + HEAD_DIM_A].astype(f32)
             + (e1 * inv)[:, lane] * o1_scr[h] + (e2 * inv)[:, lane] * o2_scr[h])
        ya_ref[:, c0:c0 + HEAD_DIM_A] = y.astype(bf16)


def _combine(o_groups, lse_groups, seq, tm=1024):
    T = o_groups[0].shape[0]
    nt = seq // tm
    d1, d2 = ATTN_GROUPS[1][1], ATTN_GROUPS[2][1]

    def rows(width):
        return pl.BlockSpec((tm, width), lambda i: (i, 0))

    def strided_rows(d, width):
        return pl.BlockSpec((1, d, tm // d, width), lambda i: (i // nt, 0, i % nt, 0))

    return pl.pallas_call(
        functools.partial(_combine_kernel, tm=tm),
        grid=(T // tm,),
        in_specs=[rows(GROUP_WIDTH), strided_rows(d1, GROUP_WIDTH), strided_rows(d2, GROUP_WIDTH),
                  rows(LSE_LANES), strided_rows(d1, LSE_LANES), strided_rows(d2, LSE_LANES)],
        out_specs=rows(GROUP_WIDTH),
        out_shape=jax.ShapeDtypeStruct((T, GROUP_WIDTH), bf16),
        scratch_shapes=[pltpu.VMEM((HEADS_PER_GROUP, tm, HEAD_DIM_A), f32),
                        pltpu.VMEM((HEADS_PER_GROUP, tm, HEAD_DIM_A), f32),
                        pltpu.VMEM((tm, LSE_LANES), f32), pltpu.VMEM((tm, LSE_LANES), f32),
                        pltpu.VMEM((1 + HEADS_PER_GROUP, tm, LANES), f32)],
        compiler_params=_cparams("parallel"),
        name="combine",
    )(*o_groups, *lse_groups)


def _gate_mix_kernel(h_ref, ya_ref, yl_ref, yc_ref, wga_ref, wgb_ref, wgc_ref, bga_ref, bgb_ref, bgc_ref,
                     woa_ref, wol_ref, wom_ref, wu_ref, wd_ref, mix_ref, wu_o_ref, wd_o_ref):
    wu_o_ref[...] = wu_ref[...].astype(bf16)
    wd_o_ref[...] = wd_ref[...].astype(bf16)
    mb = h_ref.shape[0] // GATE_ROW_BLOCKS
    for k in range(GATE_ROW_BLOCKS):
        rows = slice(k * mb, (k + 1) * mb)
        h = h_ref[rows, :]

        def gate(w_ref, b_ref):
            return jax.nn.sigmoid(jnp.dot(h, w_ref[...], preferred_element_type=f32) + b_ref[...])

        mixed = (gate(wga_ref, bga_ref) * jnp.dot(ya_ref[rows, :], woa_ref[...], preferred_element_type=f32)
                 + gate(wgb_ref, bgb_ref) * jnp.dot(yl_ref[rows, :], wol_ref[...], preferred_element_type=f32)
                 + gate(wgc_ref, bgc_ref) * jnp.dot(yc_ref[rows, :], wom_ref[...], preferred_element_type=f32))
        mix_ref[rows, :] = mixed.astype(mix_ref.dtype)


GATE_ROW_BLOCKS = 2


def _gate_mix(h, y_a, y_lru, y_c, w_gate, b_gate, w_o_attn, w_o_lru, w_o_mem, w_up, w_down, tm=1024, tn=512):
    T = h.shape[0]
    nj = D_MODEL // tn
    ni = T // tm
    up_rows, down_rows = D_MODEL // (nj * ni), D_FF // (nj * ni)

    def slab(rows_per_step, width):
        return pl.BlockSpec((rows_per_step, width), lambda j, i: (j * ni + i, 0))

    def rows(width):
        return pl.BlockSpec((tm, width), lambda j, i: (i, 0))

    def gate_w(k):
        return pl.BlockSpec((D_MODEL, tn), lambda j, i: (0, k * nj + j))

    def gate_b(k):
        return pl.BlockSpec((1, tn), lambda j, i: (0, k * nj + j))

    def cols(width):
        return pl.BlockSpec((width, tn), lambda j, i: (0, j))

    return pl.pallas_call(
        _gate_mix_kernel,
        grid=(nj, T // tm),
        in_specs=[rows(D_MODEL), rows(GROUP_WIDTH), rows(LRU_WIDTH), rows(MEM_WIDTH),
                  gate_w(0), gate_w(1), gate_w(2), gate_b(0), gate_b(1), gate_b(2),
                  cols(GROUP_WIDTH), cols(LRU_WIDTH), cols(MEM_WIDTH),
                  slab(up_rows, D_FF), slab(down_rows, D_MODEL)],
        out_specs=[pl.BlockSpec((tm, tn), lambda j, i: (i, j)), slab(up_rows, D_FF), slab(down_rows, D_MODEL)],
        out_shape=[jax.ShapeDtypeStruct((T, D_MODEL), bf16),
                   jax.ShapeDtypeStruct((D_MODEL, D_FF), bf16),
                   jax.ShapeDtypeStruct((D_FF, D_MODEL), bf16)],
        compiler_params=_cparams("arbitrary", "arbitrary"),
        name="gate_mix",
    )(h, y_a, y_lru, y_c, w_gate, w_gate, w_gate, b_gate, b_gate, b_gate, w_o_attn, w_o_lru, w_o_mem,
      w_up, w_down)


def _mlp_kernel(x_ref, mix_ref, wo_ref, g_ref, gf_ref, wu_ref, wd_ref, out_ref, h_scr):
    j = pl.program_id(1)

    @pl.when(j == 0)
    def _():
        x = x_ref[...] + jnp.dot(mix_ref[...], wo_ref[...], preferred_element_type=f32)
        h_scr[...] = _rms(x, g_ref[...]).astype(bf16)
        out_ref[...] = x

    u = jnp.maximum(jnp.dot(h_scr[...], wu_ref[...], preferred_element_type=f32), 0.0)
    out_ref[...] += jnp.dot((u * u).astype(bf16), wd_ref[...], preferred_element_type=f32)

    @pl.when(j == pl.num_programs(1) - 1)
    def _():
        out_ref[...] = _rms(out_ref[...], gf_ref[...])


def _mlp(x2, mixed, w_out, gain, gain_final, w_up, w_down, tm=512, tf=1024):
    T = x2.shape[0]
    return pl.pallas_call(
        _mlp_kernel,
        grid=(T // tm, D_FF // tf),
        in_specs=[pl.BlockSpec((tm, D_MODEL), lambda i, j: (i, 0)),
                  pl.BlockSpec((tm, D_MODEL), lambda i, j: (i, 0)),
                  pl.BlockSpec((D_MODEL, D_MODEL), lambda i, j: (0, 0)),
                  pl.BlockSpec((1, D_MODEL), lambda i, j: (0, 0)),
                  pl.BlockSpec((1, D_MODEL), lambda i, j: (0, 0)),
                  pl.BlockSpec((D_MODEL, tf), lambda i, j: (0, j)),
                  pl.BlockSpec((tf, D_MODEL), lambda i, j: (j, 0))],
        out_specs=pl.BlockSpec((tm, D_MODEL), lambda i, j: (i, 0)),
        out_shape=jax.ShapeDtypeStruct((T, D_MODEL), f32),
        scratch_shapes=[pltpu.VMEM((tm, D_MODEL), bf16)],
        compiler_params=_cparams("parallel", "arbitrary"),
        name="mlp",
    )(x2, mixed, w_out, gain, gain_final, w_up, w_down)


def _query_scale():
    scale = np.ones((1, N_IN), np.float32)
    scale[:, :WIDTH_A] = 1.0 / math.sqrt(HEAD_DIM_A)
    scale[:, COL_QC:] = 1.0 / math.sqrt(MEM_HEAD_DIM)
    return scale


def kernel(x, mem, rel_bias, norm_mix, norm_mem, norm_mlp, norm_final, w_in, w_gate, b_gate, conv_w, conv_b,
           lru_wa, lru_ba, lru_wi, lru_bi, lru_lambda, w_mem_kv, w_o_attn, w_o_lru, w_o_mem, w_out, w_up, w_down):
    B, S, D = x.shape
    T = B * S
    depth = w_in.shape[0]
    assert depth == 1, "the final RMSNorm is fused into the (single) layer's MLP kernel"
    x2 = x.reshape(T, D)
    mem2 = mem.reshape(B * N_MEM, D)
    for l in range(depth):
        w_qkv = (w_in[l] * _query_scale()).astype(bf16)
        h, proj3 = _rest_proj(x2, norm_mix[l].reshape(1, D), w_qkv, B, S)

        side_casts = ((w_gate[l],), (w_out[l], w_o_lru[l]), (w_o_attn[l], w_o_mem[l], w_mem_kv[l]))
        casted = []
        attn = []
        for g in range(len(ATTN_GROUPS)):
            d = ATTN_GROUPS[g][1]
            qkv, *bf_copies = _qkv_proj(h, w_qkv, g, B, S, cast=side_casts[g])
            casted.append(bf_copies)
            o, lse = _attn_group(qkv.reshape(B * d, S // d, QKV_W), rel_bias, g)
            if g == 0:
                attn.append((o.reshape(T, GROUP_WIDTH), lse.reshape(T, LSE_LANES)))
            else:
                attn.append((o.reshape(B, d, S // d, GROUP_WIDTH), lse.reshape(B, d, S // d, LSE_LANES)))

        w_gates = 0.5 * jnp.concatenate([lru_wa[l, 0], lru_wi[l, 0], lru_wa[l, 1], lru_wi[l, 1]], axis=-1)
        b_gates = 0.5 * jnp.concatenate([lru_ba[l, 0], lru_bi[l, 0], lru_ba[l, 1], lru_bi[l, 1]], axis=-1)
        y_lru = _lru(proj3, conv_w[l], conv_b[l].reshape(1, LRU_WIDTH), _pack_lru_gates(w_gates, b_gates),
                     lru_lambda[l])

        (w_gate_bf,), (w_out_bf, w_o_lru_bf), (w_o_attn_bf, w_o_mem_bf, w_mem_kv_bf) = casted
        kv = _mem_kv(mem2, norm_mem[l].reshape(1, D), w_mem_kv_bf)
        y_c = _xattn(proj3, kv.reshape(B, N_MEM, 2 * MEM_WIDTH))

        y_a = _combine([a[0] for a in attn], [a[1] for a in attn], S)
        mixed, w_up_bf, w_down_bf = _gate_mix(
            h, y_a, y_lru.reshape(T, LRU_WIDTH), y_c.reshape(T, MEM_WIDTH), w_gate_bf, b_gate[l].reshape(1, 3 * D),
            w_o_attn_bf, w_o_lru_bf, w_o_mem_bf, w_up[l], w_down[l])
        x2 = _mlp(x2, mixed, w_out_bf, norm_mlp[l].reshape(1, D), norm_final.reshape(1, D),
                  w_up_bf, w_down_bf)
    return x2.reshape(B, S, D)
```

```python
import functools
import math

import jax
import jax.numpy as jnp
import numpy as np
from jax import lax
from jax.experimental import pallas as pl
from jax.experimental.pallas import tpu as pltpu

D_MODEL = 2048
HEAD_DIM_A = 128
ATTN_GROUPS = ((128, 1), (512, 4), (2048, 16))
HEADS_PER_GROUP = 4
GROUP_WIDTH = HEADS_PER_GROUP * HEAD_DIM_A
WIDTH_A = len(ATTN_GROUPS) * GROUP_WIDTH
ATTN_RADIUS = 64
N_BUCKETS = 32
MAX_DISTANCE = 1024
LRU_WIDTH = 1536
LRU_BLOCKS = 12
LRU_BW = 128
LRU_C = 8.0
N_MEM = 256
MEM_HEADS = 4
MEM_HEAD_DIM = 256
MEM_WIDTH = MEM_HEADS * MEM_HEAD_DIM
D_FF = 4 * D_MODEL
EPS = 1e-6
N_IN = 3 * WIDTH_A + 2 * LRU_WIDTH + MEM_WIDTH
COL_K = WIDTH_A
COL_V = 2 * WIDTH_A
COL_XB = 3 * WIDTH_A
COL_YB = 3 * WIDTH_A + LRU_WIDTH
COL_QC = 3 * WIDTH_A + 2 * LRU_WIDTH
NEG_INF = -1e30

ATTN_ROWS_PER_STEP = 2048
SUB_Q = 128
SUB_K = SUB_Q + 2 * ATTN_RADIUS
LSE_LANES = 128
LSE_REP = LSE_LANES // HEADS_PER_GROUP

VMEM_LIMIT = 56 * 1024 * 1024

f32 = jnp.float32
bf16 = jnp.bfloat16


def _cparams(*sem):
    return pltpu.CompilerParams(dimension_semantics=sem, vmem_limit_bytes=VMEM_LIMIT)


def _rms(x, gain):
    return x * lax.rsqrt(jnp.mean(x * x, axis=-1, keepdims=True) + EPS) * gain


QKV_W = 3 * GROUP_WIDTH
REST_W = 2 * LRU_WIDTH + MEM_WIDTH
REST_YB = LRU_WIDTH
REST_QC = 2 * LRU_WIDTH
PROJ_TN = GROUP_WIDTH
LANES = 128
SLABS = PROJ_TN // LANES
PROJ_ROW_BLOCKS = 2
DEINTERLEAVE_STEP = 4
N_STAGE = 2


def _qkv_kernel(h_ref, wq_ref, wk_ref, wv_ref, *refs, d, tm, n_cast):
    cast_in, o_ref = refs[:n_cast], refs[n_cast]
    cast_out = refs[n_cast + 1:2 * n_cast + 1]
    res_scr, tmp_scr = refs[2 * n_cast + 1:]
    for src_ref, dst_ref in zip(cast_in, cast_out):
        dst_ref[...] = src_ref[...].astype(bf16)
    mb = tm // PROJ_ROW_BLOCKS
    n = 0
    for k in range(PROJ_ROW_BLOCKS):
        hk = h_ref[k * mb:(k + 1) * mb, :]
        for t, w_ref in enumerate((wq_ref, wk_ref, wv_ref)):
            res = jnp.dot(hk, w_ref[...], preferred_element_type=f32)
            col = t * PROJ_TN
            if d == 1:
                o_ref[0, k * mb:(k + 1) * mb, col:col + PROJ_TN] = res.astype(bf16)
                continue
            buf = n % N_STAGE
            n += 1
            for c in range(SLABS):
                res_scr[buf, c] = res[:, c * LANES:(c + 1) * LANES]
            src, step = res_scr, d
            if d == DEINTERLEAVE_STEP ** 2:
                step = DEINTERLEAVE_STEP
                for r in range(step):
                    for c in range(SLABS):
                        tmp_scr[buf, c, r * (mb // step):(r + 1) * (mb // step), :] = (
                            res_scr[buf, c, pl.ds(r, mb // step, stride=step), :])
                src = tmp_scr
            for r in range(d):
                start = r if src is res_scr else (r % step) * (mb // step) + r // step
                for c in range(SLABS):
                    o_ref[0, r, k * (mb // d):(k + 1) * (mb // d), col + c * LANES:col + (c + 1) * LANES] = (
                        src[buf, c, pl.ds(start, mb // d, stride=step), :].astype(bf16))


def _qkv_proj(h, w_qkv, g, batch, seq, cast=(), tm=1024):
    T = h.shape[0]
    d = ATTN_GROUPS[g][1]
    nt = seq // tm
    n_groups = len(ATTN_GROUPS)
    mb = tm // PROJ_ROW_BLOCKS
    steps = T // tm

    def w_spec(which):
        return pl.BlockSpec((D_MODEL, PROJ_TN), lambda i: (0, which * n_groups + g))

    cast_specs = [pl.BlockSpec((w.shape[0] // steps, w.shape[1]), lambda i: (i, 0)) for w in cast]

    if d == 1:
        out_spec = pl.BlockSpec((1, tm, QKV_W), lambda i: (i // nt, i % nt, 0))
        out_shape = jax.ShapeDtypeStruct((batch, seq, QKV_W), bf16)
    else:
        out_spec = pl.BlockSpec((1, d, tm // d, QKV_W), lambda i: (i // nt, 0, i % nt, 0))
        out_shape = jax.ShapeDtypeStruct((batch, d, seq // d, QKV_W), bf16)
    return pl.pallas_call(
        functools.partial(_qkv_kernel, d=d, tm=tm, n_cast=len(cast)),
        grid=(steps,),
        in_specs=[pl.BlockSpec((tm, D_MODEL), lambda i: (i, 0)), w_spec(0), w_spec(1), w_spec(2)] + cast_specs,
        out_specs=[out_spec] + cast_specs,
        out_shape=[out_shape] + [jax.ShapeDtypeStruct(w.shape, bf16) for w in cast],
        scratch_shapes=[pltpu.VMEM((N_STAGE, SLABS, mb, LANES), f32)] * 2,
        compiler_params=_cparams("parallel"),
        name=f"qkv_g{g}",
    )(h, w_qkv, w_qkv, w_qkv, *cast)


REST_TILES = REST_W // PROJ_TN
YB_TILES = range(REST_YB // PROJ_TN, REST_QC // PROJ_TN)


def _gelu_tanh(y):
    return y * (0.5 * (1.0 + jnp.tanh(math.sqrt(2.0 / math.pi) * (y + 0.044715 * (y * y * y)))))


def _cast_cols_kernel(w_ref, s_ref, o_ref):
    o_ref[...] = (w_ref[...] * s_ref[...]).astype(bf16)


def _cast_rest_cols(w_in, scale):
    first = COL_XB // PROJ_TN
    return pl.pallas_call(
        _cast_cols_kernel,
        grid=(REST_TILES,),
        in_specs=[pl.BlockSpec((D_MODEL, PROJ_TN), lambda c: (0, first + c)),
                  pl.BlockSpec((1, PROJ_TN), lambda c: (0, first + c))],
        out_specs=pl.BlockSpec((D_MODEL, PROJ_TN), lambda c: (0, c)),
        out_shape=jax.ShapeDtypeStruct((D_MODEL, REST_W), bf16),
        compiler_params=_cparams("parallel"),
        name="cast_rest_cols",
    )(w_in, scale)


def _rest_kernel(x_ref, g_ref, *refs):
    w_refs = refs[:REST_TILES]
    wq_ref, sc_ref, h_ref, o_ref, wq_o_ref = refs[REST_TILES:]
    wq_o_ref[...] = (wq_ref[...] * sc_ref[...]).astype(bf16)
    mb = x_ref.shape[0] // PROJ_ROW_BLOCKS
    for k in range(PROJ_ROW_BLOCKS):
        rows = slice(k * mb, (k + 1) * mb)
        h = _rms(x_ref[rows, :], g_ref[...]).astype(bf16)
        h_ref[rows, :] = h
        for c, w_ref in enumerate(w_refs):
            res = jnp.dot(h, w_ref[...], preferred_element_type=f32)
            if c in YB_TILES:
                res = _gelu_tanh(res)
            res = res.astype(bf16)
            for s in range(SLABS):
                o_ref[0, c * SLABS + s, rows, :] = res[:, s * LANES:(s + 1) * LANES]


def _rest_proj(x2, gain, w_rest, w_in, scale, batch, seq, tm=512):
    T = x2.shape[0]
    steps = T // tm
    nt = seq // tm
    slab_rows = D_MODEL // steps

    def w_spec(c):
        return pl.BlockSpec((D_MODEL, PROJ_TN), lambda i: (0, c), pipeline_mode=pl.Buffered(1))

    return pl.pallas_call(
        _rest_kernel,
        grid=(steps,),
        in_specs=[pl.BlockSpec((tm, D_MODEL), lambda i: (i, 0)),
                  pl.BlockSpec((1, D_MODEL), lambda i: (0, 0))]
        + [w_spec(c) for c in range(REST_TILES)]
        + [pl.BlockSpec((slab_rows, COL_XB), lambda i: (i, 0)),
           pl.BlockSpec((1, COL_XB), lambda i: (0, 0))],
        out_specs=[pl.BlockSpec((tm, D_MODEL), lambda i: (i, 0)),
                   pl.BlockSpec((1, REST_W // LANES, tm, LANES), lambda i: (i // nt, 0, i % nt, 0)),
                   pl.BlockSpec((slab_rows, COL_XB), lambda i: (i, 0))],
        out_shape=[jax.ShapeDtypeStruct((T, D_MODEL), bf16),
                   jax.ShapeDtypeStruct((batch, REST_W // LANES, seq, LANES), bf16),
                   jax.ShapeDtypeStruct((D_MODEL, COL_XB), bf16)],
        compiler_params=_cparams("arbitrary"),
        name="rest_proj",
    )(x2, gain, *([w_rest] * REST_TILES), w_in, scale)


def _t5_bucket(rel):
    nb = N_BUCKETS // 2
    max_exact = nb // 2
    sign = (rel > 0).astype(np.int32) * nb
    n = np.abs(rel)
    large = max_exact + (np.log(np.maximum(n, 1) / max_exact)
                         / np.log(MAX_DISTANCE / max_exact) * (nb - max_exact)).astype(np.int32)
    large = np.minimum(large, nb - 1)
    return (sign + np.where(n < max_exact, n, large)).astype(np.int32)


def _band_bias(rel_bias_g, dilation):
    qq = np.arange(SUB_Q)[:, None]
    kk = np.arange(SUB_K)[None, :]
    rel = kk - ATTN_RADIUS - qq
    onehot = (_t5_bucket(rel * dilation)[None] == np.arange(N_BUCKETS)[:, None, None]).astype(np.float32)
    bias = jnp.einsum('nh,nqk->hqk', rel_bias_g.astype(f32), onehot, precision=lax.Precision.HIGHEST)
    return bias + np.where(np.abs(rel) <= ATTN_RADIUS, 0.0, NEG_INF).astype(np.float32)[None]


def _attn_kernel(q_ref, kp_ref, km_ref, kn_ref, vp_ref, vm_ref, vn_ref, bias_ref,
                 o_ref, lse_ref, kbuf, vbuf, *, tq, seq, n_seq):
    R = ATTN_RADIUS
    q0 = pl.program_id(1) * tq
    lane = lax.broadcasted_iota(jnp.int32, (SUB_Q, LSE_LANES), 1)
    n_sub = tq // SUB_Q
    for i in range(n_seq):
        kbuf[i, 0:R] = kp_ref[i]
        kbuf[i, R:R + tq] = km_ref[i]
        kbuf[i, R + tq:] = kn_ref[i]
        vbuf[i, 0:R] = vp_ref[i]
        vbuf[i, R:R + tq] = vm_ref[i]
        vbuf[i, R + tq:] = vn_ref[i]
        for s in range(n_sub):
            r0 = s * SUB_Q
            edge = None
            if s == 0 or s == n_sub - 1:
                pos = q0 + (r0 - R) + lax.broadcasted_iota(jnp.int32, (1, SUB_K), 1)
                edge = jnp.where(pos >= 0, jnp.where(pos < seq, 0.0, NEG_INF), NEG_INF)
            m_tile = s_tile = None
            for h in range(HEADS_PER_GROUP):
                c0 = h * HEAD_DIM_A
                q = q_ref[i, r0:r0 + SUB_Q, c0:c0 + HEAD_DIM_A]
                k = kbuf[i, r0:r0 + SUB_K, c0:c0 + HEAD_DIM_A]
                v = vbuf[i, r0:r0 + SUB_K, c0:c0 + HEAD_DIM_A]
                logits = lax.dot_general(q, k, (((1,), (1,)), ((), ())), preferred_element_type=f32) + bias_ref[h]
                if edge is not None:
                    logits = logits + edge
                m = jnp.max(logits, axis=-1, keepdims=True)
                p = jnp.exp(logits - m)
                ssum = jnp.sum(p, axis=-1, keepdims=True)
                o = jnp.dot(p.astype(bf16), v, preferred_element_type=f32) * (1.0 / ssum)
                o_ref[i, r0:r0 + SUB_Q, c0:c0 + HEAD_DIM_A] = o.astype(o_ref.dtype)
                m_tile = m if m_tile is None else jnp.where(lane >= h * LSE_REP, m, m_tile)
                s_tile = ssum if s_tile is None else jnp.where(lane >= h * LSE_REP, ssum, s_tile)
            lse_ref[i, r0:r0 + SUB_Q, :] = m_tile + jnp.log(s_tile)


def _attn_group(qkv, rel_bias, g):
    _, d = ATTN_GROUPS[g]
    n, L, _ = qkv.shape
    tq = min(ATTN_ROWS_PER_STEP, L)
    ns = ATTN_ROWS_PER_STEP // tq
    R = ATTN_RADIUS
    bias = _band_bias(rel_bias[:, g * HEADS_PER_GROUP:(g + 1) * HEADS_PER_GROUP], d)
    rb = tq // R
    last_rb = L // R - 1

    def main(col, width=GROUP_WIDTH):
        return pl.BlockSpec((ns, tq, width), lambda b, t: (b, t, col))

    def prev(col):
        return pl.BlockSpec((ns, R, GROUP_WIDTH), lambda b, t: (b, jnp.maximum(t * rb - 1, 0), col))

    def nxt(col):
        return pl.BlockSpec((ns, R, GROUP_WIDTH), lambda b, t: (b, jnp.minimum((t + 1) * rb, last_rb), col))

    return pl.pallas_call(
        functools.partial(_attn_kernel, tq=tq, seq=L, n_seq=ns),
        grid=(n // ns, L // tq),
        in_specs=[main(0), prev(1), main(1), nxt(1), prev(2), main(2), nxt(2),
                  pl.BlockSpec((HEADS_PER_GROUP, SUB_Q, SUB_K), lambda b, t: (0, 0, 0))],
        out_specs=[main(0), main(0, LSE_LANES)],
        out_shape=[jax.ShapeDtypeStruct((n, L, GROUP_WIDTH), bf16),
                   jax.ShapeDtypeStruct((n, L, LSE_LANES), f32)],
        scratch_shapes=[pltpu.VMEM((ns, tq + 2 * R, GROUP_WIDTH), bf16),
                        pltpu.VMEM((ns, tq + 2 * R, GROUP_WIDTH), bf16)],
        compiler_params=_cparams("parallel", "arbitrary"),
        name=f"attn_g{g}",
    )(qkv, qkv, qkv, qkv, qkv, qkv, qkv, bias)


LRU_CHUNK = 256
LRU_PAD = 8
LRU_FINISH_ROWS = 512
GATE_BIAS_ROWS = 3
LRU_SEGS = 8
SEG_GAP = 4


def _lru_kernel(xb_ref, yb_ref, cw_ref, cb_ref, w_ref, lam_ref, o_ref,
                xpad, af, bf, ab, bb, htf, ptf, htb, ptb, cf_scr, cb_scr, *, seq):
    R = LRU_CHUNK
    P = LRU_PAD
    seg_len = seq // LRU_SEGS
    pitch = seg_len + SEG_GAP
    chunks_per_seg = seg_len // R
    n_chunks = seq // R
    xpad[0:P] = jnp.zeros((P, LRU_BW), f32)
    xpad[P + seq:] = jnp.zeros((P, LRU_BW), f32)
    xpad[P:P + seq] = xb_ref[0, 0].astype(f32)
    lam = lam_ref[...]
    log_a_unit = -LRU_C * (jnp.maximum(-lam, 0.0) + jnp.log1p(jnp.exp(-jnp.abs(lam))))
    cw = cw_ref[...]
    cb = cb_ref[...]
    row = lax.broadcasted_iota(jnp.int32, (R, LRU_BW), 0)
    lane = lax.broadcasted_iota(jnp.int32, (R, LRU_BW), 1)
    bias_cols = jnp.where(lane < GATE_BIAS_ROWS, 1.0, 0.0).astype(bf16)

    def chunk(ci, first=False, last=False):
        c0 = ci * R
        dst = (ci // chunks_per_seg) * pitch + (ci % chunks_per_seg) * R
        xc = (cw[0:1] * xpad[pl.ds(c0 + (P - 1), R), :] + cw[1:2] * xpad[pl.ds(c0 + P, R), :]
              + cw[2:3] * xpad[pl.ds(c0 + (P + 1), R), :] + cw[3:4] * xpad[pl.ds(c0 + (P + 2), R), :]) + cb
        lhs = jnp.concatenate([xc.astype(bf16), bias_cols], axis=1)
        th = jnp.tanh(jnp.dot(lhs, w_ref[0], preferred_element_type=f32))
        half_xc = 0.5 * xc
        for direction, (a_scr, b_scr) in enumerate(((af, bf), (ab, bb))):
            base = direction * 2 * LRU_BW
            half_log2_a = (0.5 * math.log2(math.e)) * log_a_unit[direction:direction + 1]
            a = jnp.exp2(half_log2_a * th[:, base:base + LRU_BW] + half_log2_a)
            gated_x = half_xc * th[:, base + LRU_BW:base + 2 * LRU_BW] + half_xc
            y = 1.0 - a * a
            mult = y * lax.rsqrt(jnp.maximum(y, 1e-30))
            if direction == 0 and first:
                mult = jnp.where(row == 0, 1.0, mult)
            if direction == 1 and last:
                mult = jnp.where(row == R - 1, 1.0, mult)
            a_scr[pl.ds(dst, R), :] = a
            b_scr[pl.ds(dst, R), :] = mult * gated_x

    for ci in range(n_chunks):
        chunk(ci, first=ci == 0, last=ci == n_chunks - 1)

    def scan(i, carry):
        hf, pf, hb, pb = carry
        rows = pl.ds(i, LRU_SEGS, stride=pitch)
        a = af[rows, :]
        hf = a * hf + bf[rows, :]
        pf = a * pf
        htf[rows, :] = hf
        ptf[rows, :] = pf
        rows = pl.ds(seg_len - 1 - i, LRU_SEGS, stride=pitch)
        a = ab[rows, :]
        hb = a * hb + bb[rows, :]
        pb = a * pb
        htb[rows, :] = hb
        ptb[rows, :] = pb
        return hf, pf, hb, pb

    zero = jnp.zeros((LRU_SEGS, LRU_BW), f32)
    one = jnp.ones((LRU_SEGS, LRU_BW), f32)
    hf, pf, hb, pb = lax.fori_loop(0, seg_len, scan, (zero, one, zero, one), unroll=8)

    c = jnp.zeros((1, LRU_BW), f32)
    cf_scr[0:1] = c
    for j in range(1, LRU_SEGS):
        c = hf[j - 1:j] + pf[j - 1:j] * c
        cf_scr[j:j + 1] = c
    c = jnp.zeros((1, LRU_BW), f32)
    cb_scr[LRU_SEGS - 1:LRU_SEGS] = c
    for j in range(LRU_SEGS - 2, -1, -1):
        c = hb[j + 1:j + 2] + pb[j + 1:j + 2] * c
        cb_scr[j:j + 1] = c

    F = LRU_FINISH_ROWS
    finish_per_seg = seg_len // F

    for ci in range(seq // F):
        c0 = ci * F
        seg = ci // finish_per_seg
        rows = pl.ds(seg * pitch + (ci % finish_per_seg) * F, F)
        h = (htf[rows, :] + ptf[rows, :] * cf_scr[seg:seg + 1, :]
             + htb[rows, :] + ptb[rows, :] * cb_scr[seg:seg + 1, :])
        o_ref[0, c0:c0 + F, :] = (h * yb_ref[0, 0, c0:c0 + F, :].astype(f32)).astype(o_ref.dtype)


def _pack_lru_gates(w, b):
    rows, rest = [], b
    for _ in range(GATE_BIAS_ROWS):
        piece = rest.astype(bf16)
        rows.append(piece)
        rest = rest - piece.astype(f32)
    bias_rows = jnp.pad(jnp.stack(rows, axis=1), ((0, 0), (0, LRU_BW - GATE_BIAS_ROWS), (0, 0)))
    return jnp.concatenate([w.astype(bf16), bias_rows], axis=1)


def _lru(proj3, conv_w, conv_b, w_gates, lam):
    B, _, S, _ = proj3.shape
    yb0 = REST_YB // LRU_BW
    return pl.pallas_call(
        functools.partial(_lru_kernel, seq=S),
        grid=(B, LRU_BLOCKS),
        in_specs=[
            pl.BlockSpec((1, 1, S, LRU_BW), lambda b, n: (b, n, 0, 0)),
            pl.BlockSpec((1, 1, S, LRU_BW), lambda b, n: (b, yb0 + n, 0, 0)),
            pl.BlockSpec((4, LRU_BW), lambda b, n: (0, n)),
            pl.BlockSpec((1, LRU_BW), lambda b, n: (0, n)),
            pl.BlockSpec((1, 2 * LRU_BW, 4 * LRU_BW), lambda b, n: (n, 0, 0)),
            pl.BlockSpec((2, LRU_BW), lambda b, n: (0, n)),
        ],
        out_specs=pl.BlockSpec((1, S, LRU_BW), lambda b, n: (b, 0, n)),
        out_shape=jax.ShapeDtypeStruct((B, S, LRU_WIDTH), bf16),
        scratch_shapes=([pltpu.VMEM((S + 2 * LRU_PAD, LRU_BW), f32)]
                        + [pltpu.VMEM((S + LRU_SEGS * SEG_GAP, LRU_BW), f32)] * 8
                        + [pltpu.VMEM((LRU_SEGS, LRU_BW), f32)] * 2),
        compiler_params=_cparams("parallel", "parallel"),
        name="lru",
    )(proj3, proj3, conv_w, conv_b, w_gates, lam)


def _mem_kv_kernel(m_ref, g_ref, w_ref, o_ref, h_scr):
    @pl.when(pl.program_id(0) == 0)
    def _():
        h_scr[...] = _rms(m_ref[...], g_ref[...]).astype(bf16)

    o_ref[...] = jnp.dot(h_scr[...], w_ref[...], preferred_element_type=f32).astype(o_ref.dtype)


def _mem_kv(mem2, gain, w, tn=512):
    M = mem2.shape[0]
    N = w.shape[1]
    return pl.pallas_call(
        _mem_kv_kernel,
        grid=(N // tn,),
        in_specs=[pl.BlockSpec((M, D_MODEL), lambda j: (0, 0)),
                  pl.BlockSpec((1, D_MODEL), lambda j: (0, 0)),
                  pl.BlockSpec((D_MODEL, tn), lambda j: (0, j))],
        out_specs=pl.BlockSpec((M, tn), lambda j: (0, j)),
        out_shape=jax.ShapeDtypeStruct((M, N), bf16),
        scratch_shapes=[pltpu.VMEM((M, D_MODEL), bf16)],
        compiler_params=_cparams("arbitrary"),
        name="mem_kv",
    )(mem2, gain, w)


def _xattn_kernel(q_ref, kv_ref, o_ref):
    per_head = MEM_HEAD_DIM // LANES
    for h in range(MEM_HEADS):
        c0 = h * MEM_HEAD_DIM
        k = kv_ref[0, :, c0:c0 + MEM_HEAD_DIM]
        v = kv_ref[0, :, MEM_WIDTH + c0:MEM_WIDTH + c0 + MEM_HEAD_DIM]
        q = jnp.concatenate([q_ref[0, h * per_head + s] for s in range(per_head)], axis=1)
        logits = lax.dot_general(q, k, (((1,), (1,)), ((), ())), preferred_element_type=f32)
        m = jnp.max(logits, axis=-1, keepdims=True)
        p = jnp.exp(logits - m)
        ssum = jnp.sum(p, axis=-1, keepdims=True)
        o = jnp.dot(p.astype(bf16), v, preferred_element_type=f32) * (1.0 / ssum)
        o_ref[0, :, c0:c0 + MEM_HEAD_DIM] = o.astype(o_ref.dtype)


def _xattn(proj3, kv3, tq=2048):
    B, _, S, _ = proj3.shape
    q_slabs = MEM_WIDTH // LANES
    return pl.pallas_call(
        _xattn_kernel,
        grid=(B, S // tq),
        in_specs=[pl.BlockSpec((1, q_slabs, tq, LANES), lambda b, t: (b, REST_QC // MEM_WIDTH, t, 0)),
                  pl.BlockSpec((1, N_MEM, 2 * MEM_WIDTH), lambda b, t: (b, 0, 0))],
        out_specs=pl.BlockSpec((1, tq, MEM_WIDTH), lambda b, t: (b, t, 0)),
        out_shape=jax.ShapeDtypeStruct((B, S, MEM_WIDTH), bf16),
        compiler_params=_cparams("parallel", "parallel"),
        name="xattn",
    )(proj3, kv3)


def _combine_kernel(o0_ref, o1_ref, o2_ref, l0_ref, l1_ref, l2_ref, ya_ref,
                    o1_scr, o2_scr, l1_scr, l2_scr, tmp_scr, *, tm):
    step = DEINTERLEAVE_STEP
    for g, o_ref, l_ref, o_scr, l_scr in ((1, o1_ref, l1_ref, o1_scr, l1_scr),
                                          (2, o2_ref, l2_ref, o2_scr, l2_scr)):
        d = ATTN_GROUPS[g][1]
        slabs = [(l_scr, lambda r: l_ref[0, r])]
        slabs += [(o_scr.at[h], lambda r, h=h: o_ref[0, r, :, h * HEAD_DIM_A:(h + 1) * HEAD_DIM_A].astype(f32))
                  for h in range(HEADS_PER_GROUP)]
        for k, (dst, rows_of) in enumerate(slabs):
            if d == step:
                for r in range(d):
                    dst[pl.ds(r, tm // d, stride=d), :] = rows_of(r)
                continue
            tmp = tmp_scr.at[k]
            for r in range(d):
                tmp[pl.ds((r % step) * (tm // step) + r // step, tm // d, stride=step), :] = rows_of(r)
            for lo in range(step):
                dst[pl.ds(lo, tm // step, stride=step), :] = tmp[lo * (tm // step):(lo + 1) * (tm // step), :]
    l0, l1, l2 = l0_ref[...], l1_scr[...], l2_scr[...]
    m = jnp.maximum(jnp.maximum(l0, l1), l2)
    e0, e1, e2 = jnp.exp(l0 - m), jnp.exp(l1 - m), jnp.exp(l2 - m)
    inv = 1.0 / (e0 + e1 + e2)
    for h in range(HEADS_PER_GROUP):
        c0 = h * HEAD_DIM_A
        lane = slice(h * LSE_REP, h * LSE_REP + 1)
        y = ((e0 * inv)[:, lane] * o0_ref[:, c0:c0 + HEAD_DIM_A].astype(f32)
             + (e1 * inv)[:, lane] * o1_scr[h] + (e2 * inv)[:, lane] * o2_scr[h])
        ya_ref[:, c0:c0 + HEAD_DIM_A] = y.astype(bf16)


def _combine(o_groups, lse_groups, seq, tm=1024):
    T = o_groups[0].shape[0]
    nt = seq // tm
    d1, d2 = ATTN_GROUPS[1][1], ATTN_GROUPS[2][1]

    def rows(width):
        return pl.BlockSpec((tm, width), lambda i: (i, 0))

    def strided_rows(d, width):
        return pl.BlockSpec((1, d, tm // d, width), lambda i: (i // nt, 0, i % nt, 0))

    return pl.pallas_call(
        functools.partial(_combine_kernel, tm=tm),
        grid=(T // tm,),
        in_specs=[rows(GROUP_WIDTH), strided_rows(d1, GROUP_WIDTH), strided_rows(d2, GROUP_WIDTH),
                  rows(LSE_LANES), strided_rows(d1, LSE_LANES), strided_rows(d2, LSE_LANES)],
        out_specs=rows(GROUP_WIDTH),
        out_shape=jax.ShapeDtypeStruct((T, GROUP_WIDTH), bf16),
        scratch_shapes=[pltpu.VMEM((HEADS_PER_GROUP, tm, HEAD_DIM_A), f32),
                        pltpu.VMEM((HEADS_PER_GROUP, tm, HEAD_DIM_A), f32),
                        pltpu.VMEM((tm, LSE_LANES), f32), pltpu.VMEM((tm, LSE_LANES), f32),
                        pltpu.VMEM((1 + HEADS_PER_GROUP, tm, LANES), f32)],
        compiler_params=_cparams("parallel"),
        name="combine",
    )(*o_groups, *lse_groups)


def _gate_mix_kernel(h_ref, ya_ref, yl_ref, yc_ref, wga_ref, wgb_ref, wgc_ref, bga_ref, bgb_ref, bgc_ref,
                     woa_ref, wol_ref, wom_ref, wu_ref, wd_ref, mix_ref, wu_o_ref, wd_o_ref):
    wu_o_ref[...] = wu_ref[...].astype(bf16)
    wd_o_ref[...] = wd_ref[...].astype(bf16)
    mb = h_ref.shape[0] // GATE_ROW_BLOCKS
    for k in range(GATE_ROW_BLOCKS):
        rows = slice(k * mb, (k + 1) * mb)
        h = h_ref[rows, :]

        def gate(w_ref, b_ref):
            return jax.nn.sigmoid(jnp.dot(h, w_ref[...], preferred_element_type=f32) + b_ref[...])

        mixed = (gate(wga_ref, bga_ref) * jnp.dot(ya_ref[rows, :], woa_ref[...], preferred_element_type=f32)
                 + gate(wgb_ref, bgb_ref) * jnp.dot(yl_ref[rows, :], wol_ref[...], preferred_element_type=f32)
                 + gate(wgc_ref, bgc_ref) * jnp.dot(yc_ref[rows, :], wom_ref[...], preferred_element_type=f32))
        mix_ref[rows, :] = mixed.astype(mix_ref.dtype)


GATE_ROW_BLOCKS = 2


def _gate_mix(h, y_a, y_lru, y_c, w_gate, b_gate, w_o_attn, w_o_lru, w_o_mem, w_up, w_down, tm=1024, tn=512):
    T = h.shape[0]
    nj = D_MODEL // tn
    ni = T // tm
    up_rows, down_rows = D_MODEL // (nj * ni), D_FF // (nj * ni)

    def slab(rows_per_step, width):
        return pl.BlockSpec((rows_per_step, width), lambda j, i: (j * ni + i, 0))

    def rows(width):
        return pl.BlockSpec((tm, width), lambda j, i: (i, 0))

    def gate_w(k):
        return pl.BlockSpec((D_MODEL, tn), lambda j, i: (0, k * nj + j))

    def gate_b(k):
        return pl.BlockSpec((1, tn), lambda j, i: (0, k * nj + j))

    def cols(width):
        return pl.BlockSpec((width, tn), lambda j, i: (0, j))

    return pl.pallas_call(
        _gate_mix_kernel,
        grid=(nj, T // tm),
        in_specs=[rows(D_MODEL), rows(GROUP_WIDTH), rows(LRU_WIDTH), rows(MEM_WIDTH),
                  gate_w(0), gate_w(1), gate_w(2), gate_b(0), gate_b(1), gate_b(2),
                  cols(GROUP_WIDTH), cols(LRU_WIDTH), cols(MEM_WIDTH),
                  slab(up_rows, D_FF), slab(down_rows, D_MODEL)],
        out_specs=[pl.BlockSpec((tm, tn), lambda j, i: (i, j)), slab(up_rows, D_FF), slab(down_rows, D_MODEL)],
        out_shape=[jax.ShapeDtypeStruct((T, D_MODEL), bf16),
                   jax.ShapeDtypeStruct((D_MODEL, D_FF), bf16),
                   jax.ShapeDtypeStruct((D_FF, D_MODEL), bf16)],
        compiler_params=_cparams("arbitrary", "arbitrary"),
        name="gate_mix",
    )(h, y_a, y_lru, y_c, w_gate, w_gate, w_gate, b_gate, b_gate, b_gate, w_o_attn, w_o_lru, w_o_mem,
      w_up, w_down)


def _mlp_kernel(x_ref, mix_ref, wo_ref, g_ref, gf_ref, wu_ref, wd_ref, out_ref, h_scr):
    j = pl.program_id(1)

    @pl.when(j == 0)
    def _():
        x = x_ref[...] + jnp.dot(mix_ref[...], wo_ref[...], preferred_element_type=f32)
        h_scr[...] = _rms(x, g_ref[...]).astype(bf16)
        out_ref[...] = x

    u = jnp.maximum(jnp.dot(h_scr[...], wu_ref[...], preferred_element_type=f32), 0.0)
    out_ref[...] += jnp.dot((u * u).astype(bf16), wd_ref[...], preferred_element_type=f32)

    @pl.when(j == pl.num_programs(1) - 1)
    def _():
        out_ref[...] = _rms(out_ref[...], gf_ref[...])


def _mlp(x2, mixed, w_out, gain, gain_final, w_up, w_down, tm=512, tf=1024):
    T = x2.shape[0]
    return pl.pallas_call(
        _mlp_kernel,
        grid=(T // tm, D_FF // tf),
        in_specs=[pl.BlockSpec((tm, D_MODEL), lambda i, j: (i, 0)),
                  pl.BlockSpec((tm, D_MODEL), lambda i, j: (i, 0)),
                  pl.BlockSpec((D_MODEL, D_MODEL), lambda i, j: (0, 0)),
                  pl.BlockSpec((1, D_MODEL), lambda i, j: (0, 0)),
                  pl.BlockSpec((1, D_MODEL), lambda i, j: (0, 0)),
                  pl.BlockSpec((D_MODEL, tf), lambda i, j: (0, j)),
                  pl.BlockSpec((tf, D_MODEL), lambda i, j: (j, 0))],
        out_specs=pl.BlockSpec((tm, D_MODEL), lambda i, j: (i, 0)),
        out_shape=jax.ShapeDtypeStruct((T, D_MODEL), f32),
        scratch_shapes=[pltpu.VMEM((tm, D_MODEL), bf16)],
        compiler_params=_cparams("parallel", "arbitrary"),
        name="mlp",
    )(x2, mixed, w_out, gain, gain_final, w_up, w_down)


def _query_scale():
    scale = np.ones((1, N_IN), np.float32)
    scale[:, :WIDTH_A] = 1.0 / math.sqrt(HEAD_DIM_A)
    scale[:, COL_QC:] = 1.0 / math.sqrt(MEM_HEAD_DIM)
    return scale


def kernel(x, mem, rel_bias, norm_mix, norm_mem, norm_mlp, norm_final, w_in, w_gate, b_gate, conv_w, conv_b,
           lru_wa, lru_ba, lru_wi, lru_bi, lru_lambda, w_mem_kv, w_o_attn, w_o_lru, w_o_mem, w_out, w_up, w_down):
    B, S, D = x.shape
    T = B * S
    depth = w_in.shape[0]
    assert depth == 1, "the final RMSNorm is fused into the (single) layer's MLP kernel"
    x2 = x.reshape(T, D)
    mem2 = mem.reshape(B * N_MEM, D)
    for l in range(depth):
        scale = jnp.asarray(_query_scale())
        w_rest = _cast_rest_cols(w_in[l], scale)
        h, proj3, w_qkv = _rest_proj(x2, norm_mix[l].reshape(1, D), w_rest, w_in[l], scale, B, S)

        side_casts = ((w_gate[l],), (w_out[l], w_o_lru[l]), (w_o_attn[l], w_o_mem[l], w_mem_kv[l]))
        casted = []
        attn = []
        for g in range(len(ATTN_GROUPS)):
            d = ATTN_GROUPS[g][1]
            qkv, *bf_copies = _qkv_proj(h, w_qkv, g, B, S, cast=side_casts[g])
            casted.append(bf_copies)
            o, lse = _attn_group(qkv.reshape(B * d, S // d, QKV_W), rel_bias, g)
            if g == 0:
                attn.append((o.reshape(T, GROUP_WIDTH), lse.reshape(T, LSE_LANES)))
            else:
                attn.append((o.reshape(B, d, S // d, GROUP_WIDTH), lse.reshape(B, d, S // d, LSE_LANES)))

        w_gates = 0.5 * jnp.concatenate([lru_wa[l, 0], lru_wi[l, 0], lru_wa[l, 1], lru_wi[l, 1]], axis=-1)
        b_gates = 0.5 * jnp.concatenate([lru_ba[l, 0], lru_bi[l, 0], lru_ba[l, 1], lru_bi[l, 1]], axis=-1)
        y_lru = _lru(proj3, conv_w[l], conv_b[l].reshape(1, LRU_WIDTH), _pack_lru_gates(w_gates, b_gates),
                     lru_lambda[l])

        (w_gate_bf,), (w_out_bf, w_o_lru_bf), (w_o_attn_bf, w_o_mem_bf, w_mem_kv_bf) = casted
        kv = _mem_kv(mem2, norm_mem[l].reshape(1, D), w_mem_kv_bf)
        y_c = _xattn(proj3, kv.reshape(B, N_MEM, 2 * MEM_WIDTH))

        y_a = _combine([a[0] for a in attn], [a[1] for a in attn], S)
        mixed, w_up_bf, w_down_bf = _gate_mix(
            h, y_a, y_lru.reshape(T, LRU_WIDTH), y_c.reshape(T, MEM_WIDTH), w_gate_bf, b_gate[l].reshape(1, 3 * D),
            w_o_attn_bf, w_o_lru_bf, w_o_mem_bf, w_up[l], w_down[l])
        x2 = _mlp(x2, mixed, w_out_bf, norm_mlp[l].reshape(1, D), norm_final.reshape(1, D),
                  w_up_bf, w_down_bf)
    return x2.reshape(B, S, D)
```

```python
import functools
import math

import jax
import jax.numpy as jnp
import numpy as np
from jax import lax
from jax.experimental import pallas as pl
from jax.experimental.pallas import tpu as pltpu

D_MODEL = 2048
HEAD_DIM_A = 128
ATTN_GROUPS = ((128, 1), (512, 4), (2048, 16))
HEADS_PER_GROUP = 4
GROUP_WIDTH = HEADS_PER_GROUP * HEAD_DIM_A
WIDTH_A = len(ATTN_GROUPS) * GROUP_WIDTH
ATTN_RADIUS = 64
N_BUCKETS = 32
MAX_DISTANCE = 1024
LRU_WIDTH = 1536
LRU_BLOCKS = 12
LRU_BW = 128
LRU_C = 8.0
N_MEM = 256
MEM_HEADS = 4
MEM_HEAD_DIM = 256
MEM_WIDTH = MEM_HEADS * MEM_HEAD_DIM
D_FF = 4 * D_MODEL
EPS = 1e-6
N_IN = 3 * WIDTH_A + 2 * LRU_WIDTH + MEM_WIDTH
COL_XB = 3 * WIDTH_A
COL_QC = 3 * WIDTH_A + 2 * LRU_WIDTH
NEG_INF = -1e30

ATTN_ROWS_PER_STEP = 2048
SUB_Q = 128
SUB_K = SUB_Q + 2 * ATTN_RADIUS
LSE_LANES = 128
LSE_REP = LSE_LANES // HEADS_PER_GROUP

VMEM_LIMIT = 56 * 1024 * 1024

f32 = jnp.float32
bf16 = jnp.bfloat16


def _cparams(*sem):
    return pltpu.CompilerParams(dimension_semantics=sem, vmem_limit_bytes=VMEM_LIMIT)


def _rms(x, gain):
    return x * lax.rsqrt(jnp.mean(x * x, axis=-1, keepdims=True) + EPS) * gain


QKV_W = 3 * GROUP_WIDTH
REST_W = 2 * LRU_WIDTH + MEM_WIDTH
REST_YB = LRU_WIDTH
REST_QC = 2 * LRU_WIDTH
PROJ_TN = GROUP_WIDTH
LANES = 128
SLABS = PROJ_TN // LANES
PROJ_ROW_BLOCKS = 2
DEINTERLEAVE_STEP = 4
N_STAGE = 2


def _qkv_kernel(h_ref, wq_ref, wk_ref, wv_ref, *refs, d, tm, n_cast):
    cast_in, o_ref = refs[:n_cast], refs[n_cast]
    cast_out = refs[n_cast + 1:2 * n_cast + 1]
    res_scr, tmp_scr = refs[2 * n_cast + 1:]
    for src_ref, dst_ref in zip(cast_in, cast_out):
        dst_ref[...] = src_ref[...].astype(bf16)
    mb = tm // PROJ_ROW_BLOCKS
    n = 0
    for k in range(PROJ_ROW_BLOCKS):
        hk = h_ref[k * mb:(k + 1) * mb, :]
        for t, w_ref in enumerate((wq_ref, wk_ref, wv_ref)):
            res = jnp.dot(hk, w_ref[...], preferred_element_type=f32)
            col = t * PROJ_TN
            if d == 1:
                o_ref[0, k * mb:(k + 1) * mb, col:col + PROJ_TN] = res.astype(bf16)
                continue
            buf = n % N_STAGE
            n += 1
            for c in range(SLABS):
                res_scr[buf, c] = res[:, c * LANES:(c + 1) * LANES]
            src, step = res_scr, d
            if d == DEINTERLEAVE_STEP ** 2:
                step = DEINTERLEAVE_STEP
                for r in range(step):
                    for c in range(SLABS):
                        tmp_scr[buf, c, r * (mb // step):(r + 1) * (mb // step), :] = (
                            res_scr[buf, c, pl.ds(r, mb // step, stride=step), :])
                src = tmp_scr
            for r in range(d):
                start = r if src is res_scr else (r % step) * (mb // step) + r // step
                for c in range(SLABS):
                    o_ref[0, r, k * (mb // d):(k + 1) * (mb // d), col + c * LANES:col + (c + 1) * LANES] = (
                        src[buf, c, pl.ds(start, mb // d, stride=step), :].astype(bf16))


def _qkv_proj(h, w_qkv, g, batch, seq, cast=(), tm=1024):
    T = h.shape[0]
    d = ATTN_GROUPS[g][1]
    nt = seq // tm
    n_groups = len(ATTN_GROUPS)
    mb = tm // PROJ_ROW_BLOCKS
    steps = T // tm

    def w_spec(which):
        return pl.BlockSpec((D_MODEL, PROJ_TN), lambda i: (0, which * n_groups + g))

    cast_specs = [pl.BlockSpec((w.shape[0] // steps, w.shape[1]), lambda i: (i, 0)) for w in cast]

    if d == 1:
        out_spec = pl.BlockSpec((1, tm, QKV_W), lambda i: (i // nt, i % nt, 0))
        out_shape = jax.ShapeDtypeStruct((batch, seq, QKV_W), bf16)
    else:
        out_spec = pl.BlockSpec((1, d, tm // d, QKV_W), lambda i: (i // nt, 0, i % nt, 0))
        out_shape = jax.ShapeDtypeStruct((batch, d, seq // d, QKV_W), bf16)
    return pl.pallas_call(
        functools.partial(_qkv_kernel, d=d, tm=tm, n_cast=len(cast)),
        grid=(steps,),
        in_specs=[pl.BlockSpec((tm, D_MODEL), lambda i: (i, 0)), w_spec(0), w_spec(1), w_spec(2)] + cast_specs,
        out_specs=[out_spec] + cast_specs,
        out_shape=[out_shape] + [jax.ShapeDtypeStruct(w.shape, bf16) for w in cast],
        scratch_shapes=[pltpu.VMEM((N_STAGE, SLABS, mb, LANES), f32)] * 2,
        compiler_params=_cparams("parallel"),
        name=f"qkv_g{g}",
    )(h, w_qkv, w_qkv, w_qkv, *cast)


REST_TILES = REST_W // PROJ_TN
YB_TILES = range(REST_YB // PROJ_TN, REST_QC // PROJ_TN)


def _gelu_tanh(y):
    return y * (0.5 * (1.0 + jnp.tanh(math.sqrt(2.0 / math.pi) * (y + 0.044715 * (y * y * y)))))


def _cast_cols_kernel(w_ref, s_ref, o_ref):
    o_ref[...] = (w_ref[...] * s_ref[...]).astype(bf16)


def _cast_rest_cols(w_in, scale):
    first = COL_XB // PROJ_TN
    return pl.pallas_call(
        _cast_cols_kernel,
        grid=(REST_TILES,),
        in_specs=[pl.BlockSpec((D_MODEL, PROJ_TN), lambda c: (0, first + c)),
                  pl.BlockSpec((1, PROJ_TN), lambda c: (0, first + c))],
        out_specs=pl.BlockSpec((D_MODEL, PROJ_TN), lambda c: (0, c)),
        out_shape=jax.ShapeDtypeStruct((D_MODEL, REST_W), bf16),
        compiler_params=_cparams("parallel"),
        name="cast_rest_cols",
    )(w_in, scale)


def _rest_kernel(x_ref, g_ref, *refs):
    w_refs = refs[:REST_TILES]
    wq_ref, sc_ref, h_ref, o_ref, wq_o_ref = refs[REST_TILES:]
    wq_o_ref[...] = (wq_ref[...] * sc_ref[...]).astype(bf16)
    mb = x_ref.shape[0] // PROJ_ROW_BLOCKS
    for k in range(PROJ_ROW_BLOCKS):
        rows = slice(k * mb, (k + 1) * mb)
        h = _rms(x_ref[rows, :], g_ref[...]).astype(bf16)
        h_ref[rows, :] = h
        for c, w_ref in enumerate(w_refs):
            res = jnp.dot(h, w_ref[...], preferred_element_type=f32)
            if c in YB_TILES:
                res = _gelu_tanh(res)
            res = res.astype(bf16)
            for s in range(SLABS):
                o_ref[0, c * SLABS + s, rows, :] = res[:, s * LANES:(s + 1) * LANES]


def _rest_proj(x2, gain, w_rest, w_in, scale, batch, seq, tm=512):
    T = x2.shape[0]
    steps = T // tm
    nt = seq // tm
    slab_rows = D_MODEL // steps

    def w_spec(c):
        return pl.BlockSpec((D_MODEL, PROJ_TN), lambda i: (0, c), pipeline_mode=pl.Buffered(1))

    return pl.pallas_call(
        _rest_kernel,
        grid=(steps,),
        in_specs=[pl.BlockSpec((tm, D_MODEL), lambda i: (i, 0)),
                  pl.BlockSpec((1, D_MODEL), lambda i: (0, 0))]
        + [w_spec(c) for c in range(REST_TILES)]
        + [pl.BlockSpec((slab_rows, COL_XB), lambda i: (i, 0)),
           pl.BlockSpec((1, COL_XB), lambda i: (0, 0))],
        out_specs=[pl.BlockSpec((tm, D_MODEL), lambda i: (i, 0)),
                   pl.BlockSpec((1, REST_W // LANES, tm, LANES), lambda i: (i // nt, 0, i % nt, 0)),
                   pl.BlockSpec((slab_rows, COL_XB), lambda i: (i, 0))],
        out_shape=[jax.ShapeDtypeStruct((T, D_MODEL), bf16),
                   jax.ShapeDtypeStruct((batch, REST_W // LANES, seq, LANES), bf16),
                   jax.ShapeDtypeStruct((D_MODEL, COL_XB), bf16)],
        compiler_params=_cparams("arbitrary"),
        name="rest_proj",
    )(x2, gain, *([w_rest] * REST_TILES), w_in, scale)


def _t5_bucket(rel):
    nb = N_BUCKETS // 2
    max_exact = nb // 2
    sign = (rel > 0).astype(np.int32) * nb
    n = np.abs(rel)
    large = max_exact + (np.log(np.maximum(n, 1) / max_exact)
                         / np.log(MAX_DISTANCE / max_exact) * (nb - max_exact)).astype(np.int32)
    large = np.minimum(large, nb - 1)
    return (sign + np.where(n < max_exact, n, large)).astype(np.int32)


def _band_bias(rel_bias):
    qq = np.arange(SUB_Q)[:, None]
    kk = np.arange(SUB_K)[None, :]
    rel = kk - ATTN_RADIUS - qq
    onehot = np.stack([(_t5_bucket(rel * d)[None] == np.arange(N_BUCKETS)[:, None, None]).astype(np.float32)
                       for _, d in ATTN_GROUPS])
    table = rel_bias.astype(f32).reshape(N_BUCKETS, len(ATTN_GROUPS), HEADS_PER_GROUP)
    bias = jnp.einsum('ngh,gnqk->ghqk', table, onehot, precision=lax.Precision.HIGHEST)
    return bias + np.where(np.abs(rel) <= ATTN_RADIUS, 0.0, NEG_INF).astype(np.float32)[None, None]


def _attn_kernel(q_ref, kp_ref, km_ref, kn_ref, vp_ref, vm_ref, vn_ref, bias_ref,
                 o_ref, lse_ref, kbuf, vbuf, *, tq, seq, n_seq):
    R = ATTN_RADIUS
    q0 = pl.program_id(1) * tq
    lane = lax.broadcasted_iota(jnp.int32, (SUB_Q, LSE_LANES), 1)
    n_sub = tq // SUB_Q
    for i in range(n_seq):
        kbuf[i, 0:R] = kp_ref[i]
        kbuf[i, R:R + tq] = km_ref[i]
        kbuf[i, R + tq:] = kn_ref[i]
        vbuf[i, 0:R] = vp_ref[i]
        vbuf[i, R:R + tq] = vm_ref[i]
        vbuf[i, R + tq:] = vn_ref[i]
        for s in range(n_sub):
            r0 = s * SUB_Q
            edge = None
            if s == 0 or s == n_sub - 1:
                pos = q0 + (r0 - R) + lax.broadcasted_iota(jnp.int32, (1, SUB_K), 1)
                edge = jnp.where(pos >= 0, jnp.where(pos < seq, 0.0, NEG_INF), NEG_INF)
            m_tile = s_tile = None
            for h in range(HEADS_PER_GROUP):
                c0 = h * HEAD_DIM_A
                q = q_ref[i, r0:r0 + SUB_Q, c0:c0 + HEAD_DIM_A]
                k = kbuf[i, r0:r0 + SUB_K, c0:c0 + HEAD_DIM_A]
                v = vbuf[i, r0:r0 + SUB_K, c0:c0 + HEAD_DIM_A]
                logits = lax.dot_general(q, k, (((1,), (1,)), ((), ())), preferred_element_type=f32) + bias_ref[0, h]
                if edge is not None:
                    logits = logits + edge
                m = jnp.max(logits, axis=-1, keepdims=True)
                p = jnp.exp(logits - m)
                ssum = jnp.sum(p, axis=-1, keepdims=True)
                o = jnp.dot(p.astype(bf16), v, preferred_element_type=f32) * (1.0 / ssum)
                o_ref[i, r0:r0 + SUB_Q, c0:c0 + HEAD_DIM_A] = o.astype(o_ref.dtype)
                m_tile = m if m_tile is None else jnp.where(lane >= h * LSE_REP, m, m_tile)
                s_tile = ssum if s_tile is None else jnp.where(lane >= h * LSE_REP, ssum, s_tile)
            lse_ref[i, r0:r0 + SUB_Q, :] = m_tile + jnp.log(s_tile)


def _attn_group(qkv, bias, g):
    n, L, _ = qkv.shape
    tq = min(ATTN_ROWS_PER_STEP, L)
    ns = ATTN_ROWS_PER_STEP // tq
    R = ATTN_RADIUS
    rb = tq // R
    last_rb = L // R - 1

    def main(col, width=GROUP_WIDTH):
        return pl.BlockSpec((ns, tq, width), lambda b, t: (b, t, col))

    def prev(col):
        return pl.BlockSpec((ns, R, GROUP_WIDTH), lambda b, t: (b, jnp.maximum(t * rb - 1, 0), col))

    def nxt(col):
        return pl.BlockSpec((ns, R, GROUP_WIDTH), lambda b, t: (b, jnp.minimum((t + 1) * rb, last_rb), col))

    return pl.pallas_call(
        functools.partial(_attn_kernel, tq=tq, seq=L, n_seq=ns),
        grid=(n // ns, L // tq),
        in_specs=[main(0), prev(1), main(1), nxt(1), prev(2), main(2), nxt(2),
                  pl.BlockSpec((1, HEADS_PER_GROUP, SUB_Q, SUB_K), lambda b, t: (g, 0, 0, 0))],
        out_specs=[main(0), main(0, LSE_LANES)],
        out_shape=[jax.ShapeDtypeStruct((n, L, GROUP_WIDTH), bf16),
                   jax.ShapeDtypeStruct((n, L, LSE_LANES), f32)],
        scratch_shapes=[pltpu.VMEM((ns, tq + 2 * R, GROUP_WIDTH), bf16),
                        pltpu.VMEM((ns, tq + 2 * R, GROUP_WIDTH), bf16)],
        compiler_params=_cparams("parallel", "arbitrary"),
        name=f"attn_g{g}",
    )(qkv, qkv, qkv, qkv, qkv, qkv, qkv, bias)


LRU_CHUNK = 256
LRU_PAD = 8
LRU_FINISH_ROWS = 512
GATE_BIAS_ROWS = 3
LRU_SEGS = 8
SEG_GAP = 4


def _lru_kernel(xb_ref, yb_ref, cw_ref, cb_ref, w_ref, lam_ref, o_ref,
                xpad, af, bf, ab, bb, htf, ptf, htb, ptb, cf_scr, cb_scr, *, seq):
    R = LRU_CHUNK
    P = LRU_PAD
    seg_len = seq // LRU_SEGS
    pitch = seg_len + SEG_GAP
    chunks_per_seg = seg_len // R
    n_chunks = seq // R
    xpad[0:P] = jnp.zeros((P, LRU_BW), f32)
    xpad[P + seq:] = jnp.zeros((P, LRU_BW), f32)
    xpad[P:P + seq] = xb_ref[0, 0].astype(f32)
    lam = lam_ref[...]
    log_a_unit = -LRU_C * (jnp.maximum(-lam, 0.0) + jnp.log1p(jnp.exp(-jnp.abs(lam))))
    cw = cw_ref[...]
    cb = cb_ref[...]
    row = lax.broadcasted_iota(jnp.int32, (R, LRU_BW), 0)
    lane = lax.broadcasted_iota(jnp.int32, (R, LRU_BW), 1)
    bias_cols = jnp.where(lane < GATE_BIAS_ROWS, 1.0, 0.0).astype(bf16)

    def chunk(ci, first=False, last=False):
        c0 = ci * R
        dst = (ci // chunks_per_seg) * pitch + (ci % chunks_per_seg) * R
        xc = (cw[0:1] * xpad[pl.ds(c0 + (P - 1), R), :] + cw[1:2] * xpad[pl.ds(c0 + P, R), :]
              + cw[2:3] * xpad[pl.ds(c0 + (P + 1), R), :] + cw[3:4] * xpad[pl.ds(c0 + (P + 2), R), :]) + cb
        lhs = jnp.concatenate([xc.astype(bf16), bias_cols], axis=1)
        th = jnp.tanh(jnp.dot(lhs, w_ref[0], preferred_element_type=f32))
        half_xc = 0.5 * xc
        for direction, (a_scr, b_scr) in enumerate(((af, bf), (ab, bb))):
            base = direction * 2 * LRU_BW
            half_log2_a = (0.5 * math.log2(math.e)) * log_a_unit[direction:direction + 1]
            a = jnp.exp2(half_log2_a * th[:, base:base + LRU_BW] + half_log2_a)
            gated_x = half_xc * th[:, base + LRU_BW:base + 2 * LRU_BW] + half_xc
            y = 1.0 - a * a
            mult = y * lax.rsqrt(jnp.maximum(y, 1e-30))
            if direction == 0 and first:
                mult = jnp.where(row == 0, 1.0, mult)
            if direction == 1 and last:
                mult = jnp.where(row == R - 1, 1.0, mult)
            a_scr[pl.ds(dst, R), :] = a
            b_scr[pl.ds(dst, R), :] = mult * gated_x

    for ci in range(n_chunks):
        chunk(ci, first=ci == 0, last=ci == n_chunks - 1)

    def scan(i, carry):
        hf, pf, hb, pb = carry
        rows = pl.ds(i, LRU_SEGS, stride=pitch)
        a = af[rows, :]
        hf = a * hf + bf[rows, :]
        pf = a * pf
        htf[rows, :] = hf
        ptf[rows, :] = pf
        rows = pl.ds(seg_len - 1 - i, LRU_SEGS, stride=pitch)
        a = ab[rows, :]
        hb = a * hb + bb[rows, :]
        pb = a * pb
        htb[rows, :] = hb
        ptb[rows, :] = pb
        return hf, pf, hb, pb

    zero = jnp.zeros((LRU_SEGS, LRU_BW), f32)
    one = jnp.ones((LRU_SEGS, LRU_BW), f32)
    hf, pf, hb, pb = lax.fori_loop(0, seg_len, scan, (zero, one, zero, one), unroll=8)

    c = jnp.zeros((1, LRU_BW), f32)
    cf_scr[0:1] = c
    for j in range(1, LRU_SEGS):
        c = hf[j - 1:j] + pf[j - 1:j] * c
        cf_scr[j:j + 1] = c
    c = jnp.zeros((1, LRU_BW), f32)
    cb_scr[LRU_SEGS - 1:LRU_SEGS] = c
    for j in range(LRU_SEGS - 2, -1, -1):
        c = hb[j + 1:j + 2] + pb[j + 1:j + 2] * c
        cb_scr[j:j + 1] = c

    F = LRU_FINISH_ROWS
    finish_per_seg = seg_len // F

    for ci in range(seq // F):
        c0 = ci * F
        seg = ci // finish_per_seg
        rows = pl.ds(seg * pitch + (ci % finish_per_seg) * F, F)
        h = (htf[rows, :] + ptf[rows, :] * cf_scr[seg:seg + 1, :]
             + htb[rows, :] + ptb[rows, :] * cb_scr[seg:seg + 1, :])
        o_ref[0, c0:c0 + F, :] = (h * yb_ref[0, 0, c0:c0 + F, :].astype(f32)).astype(o_ref.dtype)


def _pack_lru_gates(w, b):
    rows, rest = [], b
    for _ in range(GATE_BIAS_ROWS):
        piece = rest.astype(bf16)
        rows.append(piece)
        rest = rest - piece.astype(f32)
    bias_rows = jnp.pad(jnp.stack(rows, axis=1), ((0, 0), (0, LRU_BW - GATE_BIAS_ROWS), (0, 0)))
    return jnp.concatenate([w.astype(bf16), bias_rows], axis=1)


def _lru(proj3, conv_w, conv_b, w_gates, lam):
    B, _, S, _ = proj3.shape
    yb0 = REST_YB // LRU_BW
    return pl.pallas_call(
        functools.partial(_lru_kernel, seq=S),
        grid=(B, LRU_BLOCKS),
        in_specs=[
            pl.BlockSpec((1, 1, S, LRU_BW), lambda b, n: (b, n, 0, 0)),
            pl.BlockSpec((1, 1, S, LRU_BW), lambda b, n: (b, yb0 + n, 0, 0)),
            pl.BlockSpec((4, LRU_BW), lambda b, n: (0, n)),
            pl.BlockSpec((1, LRU_BW), lambda b, n: (0, n)),
            pl.BlockSpec((1, 2 * LRU_BW, 4 * LRU_BW), lambda b, n: (n, 0, 0)),
            pl.BlockSpec((2, LRU_BW), lambda b, n: (0, n)),
        ],
        out_specs=pl.BlockSpec((1, S, LRU_BW), lambda b, n: (b, 0, n)),
        out_shape=jax.ShapeDtypeStruct((B, S, LRU_WIDTH), bf16),
        scratch_shapes=([pltpu.VMEM((S + 2 * LRU_PAD, LRU_BW), f32)]
                        + [pltpu.VMEM((S + LRU_SEGS * SEG_GAP, LRU_BW), f32)] * 8
                        + [pltpu.VMEM((LRU_SEGS, LRU_BW), f32)] * 2),
        compiler_params=_cparams("parallel", "parallel"),
        name="lru",
    )(proj3, proj3, conv_w, conv_b, w_gates, lam)


def _mem_kv_kernel(m_ref, g_ref, w_ref, o_ref, h_scr):
    @pl.when(pl.program_id(0) == 0)
    def _():
        h_scr[...] = _rms(m_ref[...], g_ref[...]).astype(bf16)

    o_ref[...] = jnp.dot(h_scr[...], w_ref[...], preferred_element_type=f32).astype(o_ref.dtype)


def _mem_kv(mem2, gain, w, tn=512):
    M = mem2.shape[0]
    N = w.shape[1]
    return pl.pallas_call(
        _mem_kv_kernel,
        grid=(N // tn,),
        in_specs=[pl.BlockSpec((M, D_MODEL), lambda j: (0, 0)),
                  pl.BlockSpec((1, D_MODEL), lambda j: (0, 0)),
                  pl.BlockSpec((D_MODEL, tn), lambda j: (0, j))],
        out_specs=pl.BlockSpec((M, tn), lambda j: (0, j)),
        out_shape=jax.ShapeDtypeStruct((M, N), bf16),
        scratch_shapes=[pltpu.VMEM((M, D_MODEL), bf16)],
        compiler_params=_cparams("arbitrary"),
        name="mem_kv",
    )(mem2, gain, w)


def _xattn_kernel(q_ref, kv_ref, o_ref):
    per_head = MEM_HEAD_DIM // LANES
    for h in range(MEM_HEADS):
        c0 = h * MEM_HEAD_DIM
        k = kv_ref[0, :, c0:c0 + MEM_HEAD_DIM]
        v = kv_ref[0, :, MEM_WIDTH + c0:MEM_WIDTH + c0 + MEM_HEAD_DIM]
        q = jnp.concatenate([q_ref[0, h * per_head + s] for s in range(per_head)], axis=1)
        logits = lax.dot_general(q, k, (((1,), (1,)), ((), ())), preferred_element_type=f32)
        m = jnp.max(logits, axis=-1, keepdims=True)
        p = jnp.exp(logits - m)
        ssum = jnp.sum(p, axis=-1, keepdims=True)
        o = jnp.dot(p.astype(bf16), v, preferred_element_type=f32) * (1.0 / ssum)
        o_ref[0, :, c0:c0 + MEM_HEAD_DIM] = o.astype(o_ref.dtype)


def _xattn(proj3, kv3, tq=2048):
    B, _, S, _ = proj3.shape
    q_slabs = MEM_WIDTH // LANES
    return pl.pallas_call(
        _xattn_kernel,
        grid=(B, S // tq),
        in_specs=[pl.BlockSpec((1, q_slabs, tq, LANES), lambda b, t: (b, REST_QC // MEM_WIDTH, t, 0)),
                  pl.BlockSpec((1, N_MEM, 2 * MEM_WIDTH), lambda b, t: (b, 0, 0))],
        out_specs=pl.BlockSpec((1, tq, MEM_WIDTH), lambda b, t: (b, t, 0)),
        out_shape=jax.ShapeDtypeStruct((B, S, MEM_WIDTH), bf16),
        compiler_params=_cparams("parallel", "parallel"),
        name="xattn",
    )(proj3, kv3)


def _combine_kernel(o0_ref, o1_ref, o2_ref, l0_ref, l1_ref, l2_ref, ya_ref,
                    o1_scr, o2_scr, l1_scr, l2_scr, tmp_scr, *, tm):
    step = DEINTERLEAVE_STEP
    for g, o_ref, l_ref, o_scr, l_scr in ((1, o1_ref, l1_ref, o1_scr, l1_scr),
                                          (2, o2_ref, l2_ref, o2_scr, l2_scr)):
        d = ATTN_GROUPS[g][1]
        slabs = [(l_scr, lambda r: l_ref[0, r])]
        slabs += [(o_scr.at[h], lambda r, h=h: o_ref[0, r, :, h * HEAD_DIM_A:(h + 1) * HEAD_DIM_A].astype(f32))
                  for h in range(HEADS_PER_GROUP)]
        for k, (dst, rows_of) in enumerate(slabs):
            if d == step:
                for r in range(d):
                    dst[pl.ds(r, tm // d, stride=d), :] = rows_of(r)
                continue
            tmp = tmp_scr.at[k]
            for r in range(d):
                tmp[pl.ds((r % step) * (tm // step) + r // step, tm // d, stride=step), :] = rows_of(r)
            for lo in range(step):
                dst[pl.ds(lo, tm // step, stride=step), :] = tmp[lo * (tm // step):(lo + 1) * (tm // step), :]
    l0, l1, l2 = l0_ref[...], l1_scr[...], l2_scr[...]
    m = jnp.maximum(jnp.maximum(l0, l1), l2)
    e0, e1, e2 = jnp.exp(l0 - m), jnp.exp(l1 - m), jnp.exp(l2 - m)
    inv = 1.0 / (e0 + e1 + e2)
    for h in range(HEADS_PER_GROUP):
        c0 = h * HEAD_DIM_A
        lane = slice(h * LSE_REP, h * LSE_REP + 1)
        y = ((e0 * inv)[:, lane] * o0_ref[:, c0:c0 + HEAD_DIM_A].astype(f32)
             + (e1 * inv)[:, lane] * o1_scr[h] + (e2 * inv)[:, lane] * o2_scr[h])
        ya_ref[:, c0:c0 + HEAD_DIM_A] = y.astype(bf16)


def _combine(o_groups, lse_groups, seq, tm=1024):
    T = o_groups[0].shape[0]
    nt = seq // tm
    d1, d2 = ATTN_GROUPS[1][1], ATTN_GROUPS[2][1]

    def rows(width):
        return pl.BlockSpec((tm, width), lambda i: (i, 0))

    def strided_rows(d, width):
        return pl.BlockSpec((1, d, tm // d, width), lambda i: (i // nt, 0, i % nt, 0))

    return pl.pallas_call(
        functools.partial(_combine_kernel, tm=tm),
        grid=(T // tm,),
        in_specs=[rows(GROUP_WIDTH), strided_rows(d1, GROUP_WIDTH), strided_rows(d2, GROUP_WIDTH),
                  rows(LSE_LANES), strided_rows(d1, LSE_LANES), strided_rows(d2, LSE_LANES)],
        out_specs=rows(GROUP_WIDTH),
        out_shape=jax.ShapeDtypeStruct((T, GROUP_WIDTH), bf16),
        scratch_shapes=[pltpu.VMEM((HEADS_PER_GROUP, tm, HEAD_DIM_A), f32),
                        pltpu.VMEM((HEADS_PER_GROUP, tm, HEAD_DIM_A), f32),
                        pltpu.VMEM((tm, LSE_LANES), f32), pltpu.VMEM((tm, LSE_LANES), f32),
                        pltpu.VMEM((1 + HEADS_PER_GROUP, tm, LANES), f32)],
        compiler_params=_cparams("parallel"),
        name="combine",
    )(*o_groups, *lse_groups)


GATE_ROW_BLOCKS = 2


def _gate_mix_kernel(h_ref, ya_ref, yl_ref, yc_ref, wga_ref, wgb_ref, wgc_ref, bga_ref, bgb_ref, bgc_ref,
                     woa_ref, wol_ref, wom_ref, wu_ref, wd_ref, mix_ref, wu_o_ref, wd_o_ref):
    wu_o_ref[...] = wu_ref[...].astype(bf16)
    wd_o_ref[...] = wd_ref[...].astype(bf16)
    mb = h_ref.shape[0] // GATE_ROW_BLOCKS
    for k in range(GATE_ROW_BLOCKS):
        rows = slice(k * mb, (k + 1) * mb)
        h = h_ref[rows, :]

        def gate(w_ref, b_ref):
            return jax.nn.sigmoid(jnp.dot(h, w_ref[...], preferred_element_type=f32) + b_ref[...])

        mixed = (gate(wga_ref, bga_ref) * jnp.dot(ya_ref[rows, :], woa_ref[...], preferred_element_type=f32)
                 + gate(wgb_ref, bgb_ref) * jnp.dot(yl_ref[rows, :], wol_ref[...], preferred_element_type=f32)
                 + gate(wgc_ref, bgc_ref) * jnp.dot(yc_ref[rows, :], wom_ref[...], preferred_element_type=f32))
        mix_ref[rows, :] = mixed.astype(mix_ref.dtype)


def _gate_mix(h, y_a, y_lru, y_c, w_gate, b_gate, w_o_attn, w_o_lru, w_o_mem, w_up, w_down, tm=1024, tn=512):
    T = h.shape[0]
    nj = D_MODEL // tn
    ni = T // tm
    up_rows, down_rows = D_MODEL // (nj * ni), D_FF // (nj * ni)

    def slab(rows_per_step, width):
        return pl.BlockSpec((rows_per_step, width), lambda j, i: (j * ni + i, 0))

    def rows(width):
        return pl.BlockSpec((tm, width), lambda j, i: (i, 0))

    def gate_w(k):
        return pl.BlockSpec((D_MODEL, tn), lambda j, i: (0, k * nj + j))

    def gate_b(k):
        return pl.BlockSpec((1, tn), lambda j, i: (0, k * nj + j))

    def cols(width):
        return pl.BlockSpec((width, tn), lambda j, i: (0, j))

    return pl.pallas_call(
        _gate_mix_kernel,
        grid=(nj, T // tm),
        in_specs=[rows(D_MODEL), rows(GROUP_WIDTH), rows(LRU_WIDTH), rows(MEM_WIDTH),
                  gate_w(0), gate_w(1), gate_w(2), gate_b(0), gate_b(1), gate_b(2),
                  cols(GROUP_WIDTH), cols(LRU_WIDTH), cols(MEM_WIDTH),
                  slab(up_rows, D_FF), slab(down_rows, D_MODEL)],
        out_specs=[pl.BlockSpec((tm, tn), lambda j, i: (i, j)), slab(up_rows, D_FF), slab(down_rows, D_MODEL)],
        out_shape=[jax.ShapeDtypeStruct((T, D_MODEL), bf16),
                   jax.ShapeDtypeStruct((D_MODEL, D_FF), bf16),
                   jax.ShapeDtypeStruct((D_FF, D_MODEL), bf16)],
        compiler_params=_cparams("arbitrary", "arbitrary"),
        name="gate_mix",
    )(h, y_a, y_lru, y_c, w_gate, w_gate, w_gate, b_gate, b_gate, b_gate, w_o_attn, w_o_lru, w_o_mem,
      w_up, w_down)


def _mlp_kernel(x_ref, mix_ref, wo_ref, g_ref, gf_ref, wu_ref, wd_ref, out_ref, h_scr):
    j = pl.program_id(1)

    @pl.when(j == 0)
    def _():
        x = x_ref[...] + jnp.dot(mix_ref[...], wo_ref[...], preferred_element_type=f32)
        h_scr[...] = _rms(x, g_ref[...]).astype(bf16)
        out_ref[...] = x

    u = jnp.maximum(jnp.dot(h_scr[...], wu_ref[...], preferred_element_type=f32), 0.0)
    out_ref[...] += jnp.dot((u * u).astype(bf16), wd_ref[...], preferred_element_type=f32)

    @pl.when(j == pl.num_programs(1) - 1)
    def _():
        out_ref[...] = _rms(out_ref[...], gf_ref[...])


def _mlp(x2, mixed, w_out, gain, gain_final, w_up, w_down, tm=512, tf=1024):
    T = x2.shape[0]
    return pl.pallas_call(
        _mlp_kernel,
        grid=(T // tm, D_FF // tf),
        in_specs=[pl.BlockSpec((tm, D_MODEL), lambda i, j: (i, 0)),
                  pl.BlockSpec((tm, D_MODEL), lambda i, j: (i, 0)),
                  pl.BlockSpec((D_MODEL, D_MODEL), lambda i, j: (0, 0)),
                  pl.BlockSpec((1, D_MODEL), lambda i, j: (0, 0)),
                  pl.BlockSpec((1, D_MODEL), lambda i, j: (0, 0)),
                  pl.BlockSpec((D_MODEL, tf), lambda i, j: (0, j)),
                  pl.BlockSpec((tf, D_MODEL), lambda i, j: (j, 0))],
        out_specs=pl.BlockSpec((tm, D_MODEL), lambda i, j: (i, 0)),
        out_shape=jax.ShapeDtypeStruct((T, D_MODEL), f32),
        scratch_shapes=[pltpu.VMEM((tm, D_MODEL), bf16)],
        compiler_params=_cparams("parallel", "arbitrary"),
        name="mlp",
    )(x2, mixed, w_out, gain, gain_final, w_up, w_down)


def _query_scale():
    scale = np.ones((1, N_IN), np.float32)
    scale[:, :WIDTH_A] = 1.0 / math.sqrt(HEAD_DIM_A)
    scale[:, COL_QC:] = 1.0 / math.sqrt(MEM_HEAD_DIM)
    return scale


def kernel(x, mem, rel_bias, norm_mix, norm_mem, norm_mlp, norm_final, w_in, w_gate, b_gate, conv_w, conv_b,
           lru_wa, lru_ba, lru_wi, lru_bi, lru_lambda, w_mem_kv, w_o_attn, w_o_lru, w_o_mem, w_out, w_up, w_down):
    B, S, D = x.shape
    T = B * S
    depth = w_in.shape[0]
    assert depth == 1, "the final RMSNorm is fused into the (single) layer's MLP kernel"
    x2 = x.reshape(T, D)
    mem2 = mem.reshape(B * N_MEM, D)
    for l in range(depth):
        scale = jnp.asarray(_query_scale())
        w_rest = _cast_rest_cols(w_in[l], scale)
        h, proj3, w_qkv = _rest_proj(x2, norm_mix[l].reshape(1, D), w_rest, w_in[l], scale, B, S)

        side_casts = ((w_gate[l],), (w_out[l], w_o_lru[l]), (w_o_attn[l], w_o_mem[l], w_mem_kv[l]))
        casted = []
        attn = []
        band_bias = _band_bias(rel_bias)
        for g in range(len(ATTN_GROUPS)):
            d = ATTN_GROUPS[g][1]
            qkv, *bf_copies = _qkv_proj(h, w_qkv, g, B, S, cast=side_casts[g])
            casted.append(bf_copies)
            o, lse = _attn_group(qkv.reshape(B * d, S // d, QKV_W), band_bias, g)
            if g == 0:
                attn.append((o.reshape(T, GROUP_WIDTH), lse.reshape(T, LSE_LANES)))
            else:
                attn.append((o.reshape(B, d, S // d, GROUP_WIDTH), lse.reshape(B, d, S // d, LSE_LANES)))

        w_gates = 0.5 * jnp.concatenate([lru_wa[l, 0], lru_wi[l, 0], lru_wa[l, 1], lru_wi[l, 1]], axis=-1)
        b_gates = 0.5 * jnp.concatenate([lru_ba[l, 0], lru_bi[l, 0], lru_ba[l, 1], lru_bi[l, 1]], axis=-1)
        y_lru = _lru(proj3, conv_w[l], conv_b[l].reshape(1, LRU_WIDTH), _pack_lru_gates(w_gates, b_gates),
                     lru_lambda[l])

        (w_gate_bf,), (w_out_bf, w_o_lru_bf), (w_o_attn_bf, w_o_mem_bf, w_mem_kv_bf) = casted
        kv = _mem_kv(mem2, norm_mem[l].reshape(1, D), w_mem_kv_bf)
        y_c = _xattn(proj3, kv.reshape(B, N_MEM, 2 * MEM_WIDTH))

        y_a = _combine([a[0] for a in attn], [a[1] for a in attn], S)
        mixed, w_up_bf, w_down_bf = _gate_mix(
            h, y_a, y_lru.reshape(T, LRU_WIDTH), y_c.reshape(T, MEM_WIDTH), w_gate_bf, b_gate[l].reshape(1, 3 * D),
            w_o_attn_bf, w_o_lru_bf, w_o_mem_bf, w_up[l], w_down[l])
        x2 = _mlp(x2, mixed, w_out_bf, norm_mlp[l].reshape(1, D), norm_final.reshape(1, D),
                  w_up_bf, w_down_bf)
    return x2.reshape(B, S, D)
```

```python
import functools
import math

import jax
import jax.numpy as jnp
import numpy as np
from jax import lax
from jax.experimental import pallas as pl
from jax.experimental.pallas import tpu as pltpu

D_MODEL = 2048
HEAD_DIM_A = 128
ATTN_GROUPS = ((128, 1), (512, 4), (2048, 16))
HEADS_PER_GROUP = 4
GROUP_WIDTH = HEADS_PER_GROUP * HEAD_DIM_A
WIDTH_A = len(ATTN_GROUPS) * GROUP_WIDTH
ATTN_RADIUS = 64
N_BUCKETS = 32
MAX_DISTANCE = 1024
LRU_WIDTH = 1536
LRU_BLOCKS = 12
LRU_BW = 128
LRU_C = 8.0
N_MEM = 256
MEM_HEADS = 4
MEM_HEAD_DIM = 256
MEM_WIDTH = MEM_HEADS * MEM_HEAD_DIM
D_FF = 4 * D_MODEL
EPS = 1e-6
N_IN = 3 * WIDTH_A + 2 * LRU_WIDTH + MEM_WIDTH
COL_XB = 3 * WIDTH_A
COL_QC = 3 * WIDTH_A + 2 * LRU_WIDTH
NEG_INF = -1e30

ATTN_ROWS_PER_STEP = 2048
SUB_Q = 128
SUB_K = SUB_Q + 2 * ATTN_RADIUS
LSE_LANES = 128
LSE_REP = LSE_LANES // HEADS_PER_GROUP

VMEM_LIMIT = 56 * 1024 * 1024

f32 = jnp.float32
bf16 = jnp.bfloat16


def _cparams(*sem):
    return pltpu.CompilerParams(dimension_semantics=sem, vmem_limit_bytes=VMEM_LIMIT)


def _rms(x, gain):
    return x * lax.rsqrt(jnp.mean(x * x, axis=-1, keepdims=True) + EPS) * gain


QKV_W = 3 * GROUP_WIDTH
REST_W = 2 * LRU_WIDTH + MEM_WIDTH
REST_YB = LRU_WIDTH
REST_QC = 2 * LRU_WIDTH
PROJ_TN = GROUP_WIDTH
LANES = 128
SLABS = PROJ_TN // LANES
PROJ_ROW_BLOCKS = 2
DEINTERLEAVE_STEP = 4
N_STAGE = 2


def _qkv_kernel(h_ref, wq_ref, wk_ref, wv_ref, *refs, d, tm, n_cast):
    cast_in, o_ref = refs[:n_cast], refs[n_cast]
    cast_out = refs[n_cast + 1:2 * n_cast + 1]
    res_scr, tmp_scr = refs[2 * n_cast + 1:]
    for src_ref, dst_ref in zip(cast_in, cast_out):
        dst_ref[...] = src_ref[...].astype(bf16)
    mb = tm // PROJ_ROW_BLOCKS
    n = 0
    for k in range(PROJ_ROW_BLOCKS):
        hk = h_ref[k * mb:(k + 1) * mb, :]
        for t, w_ref in enumerate((wq_ref, wk_ref, wv_ref)):
            res = jnp.dot(hk, w_ref[...], preferred_element_type=f32)
            col = t * PROJ_TN
            if d == 1:
                o_ref[0, k * mb:(k + 1) * mb, col:col + PROJ_TN] = res.astype(bf16)
                continue
            buf = n % N_STAGE
            n += 1
            for c in range(SLABS):
                res_scr[buf, c] = res[:, c * LANES:(c + 1) * LANES]
            src, step = res_scr, d
            if d == DEINTERLEAVE_STEP ** 2:
                step = DEINTERLEAVE_STEP
                for r in range(step):
                    for c in range(SLABS):
                        tmp_scr[buf, c, r * (mb // step):(r + 1) * (mb // step), :] = (
                            res_scr[buf, c, pl.ds(r, mb // step, stride=step), :])
                src = tmp_scr
            for r in range(d):
                start = r if src is res_scr else (r % step) * (mb // step) + r // step
                for c in range(SLABS):
                    o_ref[0, r, k * (mb // d):(k + 1) * (mb // d), col + c * LANES:col + (c + 1) * LANES] = (
                        src[buf, c, pl.ds(start, mb // d, stride=step), :].astype(bf16))


def _qkv_proj(h, w_qkv, g, batch, seq, cast=(), tm=1024):
    T = h.shape[0]
    d = ATTN_GROUPS[g][1]
    nt = seq // tm
    n_groups = len(ATTN_GROUPS)
    mb = tm // PROJ_ROW_BLOCKS
    steps = T // tm

    def w_spec(which):
        return pl.BlockSpec((D_MODEL, PROJ_TN), lambda i: (0, which * n_groups + g))

    cast_specs = [pl.BlockSpec((w.shape[0] // steps, w.shape[1]), lambda i: (i, 0)) for w in cast]

    if d == 1:
        out_spec = pl.BlockSpec((1, tm, QKV_W), lambda i: (i // nt, i % nt, 0))
        out_shape = jax.ShapeDtypeStruct((batch, seq, QKV_W), bf16)
    else:
        out_spec = pl.BlockSpec((1, d, tm // d, QKV_W), lambda i: (i // nt, 0, i % nt, 0))
        out_shape = jax.ShapeDtypeStruct((batch, d, seq // d, QKV_W), bf16)
    return pl.pallas_call(
        functools.partial(_qkv_kernel, d=d, tm=tm, n_cast=len(cast)),
        grid=(steps,),
        in_specs=[pl.BlockSpec((tm, D_MODEL), lambda i: (i, 0)), w_spec(0), w_spec(1), w_spec(2)] + cast_specs,
        out_specs=[out_spec] + cast_specs,
        out_shape=[out_shape] + [jax.ShapeDtypeStruct(w.shape, bf16) for w in cast],
        scratch_shapes=[pltpu.VMEM((N_STAGE, SLABS, mb, LANES), f32)] * 2,
        compiler_params=_cparams("parallel"),
        name=f"qkv_g{g}",
    )(h, w_qkv, w_qkv, w_qkv, *cast)


REST_TILES = REST_W // PROJ_TN
YB_TILES = range(REST_YB // PROJ_TN, REST_QC // PROJ_TN)


def _gelu_tanh(y):
    return y * (0.5 * (1.0 + jnp.tanh(math.sqrt(2.0 / math.pi) * (y + 0.044715 * (y * y * y)))))


def _cast_cols_kernel(w_ref, s_ref, o_ref):
    o_ref[...] = (w_ref[...] * s_ref[...]).astype(bf16)


def _cast_rest_cols(w_in, scale):
    first = COL_XB // PROJ_TN
    return pl.pallas_call(
        _cast_cols_kernel,
        grid=(REST_TILES,),
        in_specs=[pl.BlockSpec((D_MODEL, PROJ_TN), lambda c: (0, first + c)),
                  pl.BlockSpec((1, PROJ_TN), lambda c: (0, first + c))],
        out_specs=pl.BlockSpec((D_MODEL, PROJ_TN), lambda c: (0, c)),
        out_shape=jax.ShapeDtypeStruct((D_MODEL, REST_W), bf16),
        compiler_params=_cparams("parallel"),
        name="cast_rest_cols",
    )(w_in, scale)


def _rest_kernel(x_ref, g_ref, *refs):
    w_refs = refs[:REST_TILES]
    wq_ref, sc_ref, h_ref, o_ref, wq_o_ref = refs[REST_TILES:]
    wq_o_ref[...] = (wq_ref[...] * sc_ref[...]).astype(bf16)
    mb = x_ref.shape[0] // PROJ_ROW_BLOCKS
    for k in range(PROJ_ROW_BLOCKS):
        rows = slice(k * mb, (k + 1) * mb)
        h = _rms(x_ref[rows, :], g_ref[...]).astype(bf16)
        h_ref[rows, :] = h
        for c, w_ref in enumerate(w_refs):
            res = jnp.dot(h, w_ref[...], preferred_element_type=f32)
            if c in YB_TILES:
                res = _gelu_tanh(res)
            res = res.astype(bf16)
            for s in range(SLABS):
                o_ref[0, c * SLABS + s, rows, :] = res[:, s * LANES:(s + 1) * LANES]


def _rest_proj(x2, gain, w_rest, w_in, scale, batch, seq, tm=512):
    T = x2.shape[0]
    steps = T // tm
    nt = seq // tm
    slab_rows = D_MODEL // steps

    def w_spec(c):
        return pl.BlockSpec((D_MODEL, PROJ_TN), lambda i: (0, c), pipeline_mode=pl.Buffered(1))

    return pl.pallas_call(
        _rest_kernel,
        grid=(steps,),
        in_specs=[pl.BlockSpec((tm, D_MODEL), lambda i: (i, 0)),
                  pl.BlockSpec((1, D_MODEL), lambda i: (0, 0))]
        + [w_spec(c) for c in range(REST_TILES)]
        + [pl.BlockSpec((slab_rows, COL_XB), lambda i: (i, 0)),
           pl.BlockSpec((1, COL_XB), lambda i: (0, 0))],
        out_specs=[pl.BlockSpec((tm, D_MODEL), lambda i: (i, 0)),
                   pl.BlockSpec((1, REST_W // LANES, tm, LANES), lambda i: (i // nt, 0, i % nt, 0)),
                   pl.BlockSpec((slab_rows, COL_XB), lambda i: (i, 0))],
        out_shape=[jax.ShapeDtypeStruct((T, D_MODEL), bf16),
                   jax.ShapeDtypeStruct((batch, REST_W // LANES, seq, LANES), bf16),
                   jax.ShapeDtypeStruct((D_MODEL, COL_XB), bf16)],
        compiler_params=_cparams("arbitrary"),
        name="rest_proj",
    )(x2, gain, *([w_rest] * REST_TILES), w_in, scale)


def _t5_bucket(rel):
    nb = N_BUCKETS // 2
    max_exact = nb // 2
    sign = (rel > 0).astype(np.int32) * nb
    n = np.abs(rel)
    large = max_exact + (np.log(np.maximum(n, 1) / max_exact)
                         / np.log(MAX_DISTANCE / max_exact) * (nb - max_exact)).astype(np.int32)
    large = np.minimum(large, nb - 1)
    return (sign + np.where(n < max_exact, n, large)).astype(np.int32)


def _band_bias(rel_bias):
    qq = np.arange(SUB_Q)[:, None]
    kk = np.arange(SUB_K)[None, :]
    rel = kk - ATTN_RADIUS - qq
    onehot = np.stack([(_t5_bucket(rel * d)[None] == np.arange(N_BUCKETS)[:, None, None]).astype(np.float32)
                       for _, d in ATTN_GROUPS])
    table = rel_bias.astype(f32).reshape(N_BUCKETS, len(ATTN_GROUPS), HEADS_PER_GROUP)
    bias = jnp.einsum('ngh,gnqk->ghqk', table, onehot, precision=lax.Precision.HIGHEST)
    return bias + np.where(np.abs(rel) <= ATTN_RADIUS, 0.0, NEG_INF).astype(np.float32)[None, None]


def _attn_kernel(q_ref, kp_ref, km_ref, kn_ref, vp_ref, vm_ref, vn_ref, bias_ref,
                 o_ref, lse_ref, kbuf, vbuf, *, tq, seq, n_seq):
    R = ATTN_RADIUS
    q0 = pl.program_id(1) * tq
    lane = lax.broadcasted_iota(jnp.int32, (SUB_Q, LSE_LANES), 1)
    n_sub = tq // SUB_Q
    for i in range(n_seq):
        kbuf[i, 0:R] = kp_ref[i]
        kbuf[i, R:R + tq] = km_ref[i]
        kbuf[i, R + tq:] = kn_ref[i]
        vbuf[i, 0:R] = vp_ref[i]
        vbuf[i, R:R + tq] = vm_ref[i]
        vbuf[i, R + tq:] = vn_ref[i]
        for s in range(n_sub):
            r0 = s * SUB_Q
            edge = None
            if s == 0 or s == n_sub - 1:
                pos = q0 + (r0 - R) + lax.broadcasted_iota(jnp.int32, (1, SUB_K), 1)
                edge = jnp.where(pos >= 0, jnp.where(pos < seq, 0.0, NEG_INF), NEG_INF)
            m_tile = s_tile = None
            for h in range(HEADS_PER_GROUP):
                c0 = h * HEAD_DIM_A
                q = q_ref[i, r0:r0 + SUB_Q, c0:c0 + HEAD_DIM_A]
                k = kbuf[i, r0:r0 + SUB_K, c0:c0 + HEAD_DIM_A]
                v = vbuf[i, r0:r0 + SUB_K, c0:c0 + HEAD_DIM_A]
                logits = lax.dot_general(q, k, (((1,), (1,)), ((), ())), preferred_element_type=f32) + bias_ref[0, h]
                if edge is not None:
                    logits = logits + edge
                m = jnp.max(logits, axis=-1, keepdims=True)
                p = jnp.exp(logits - m)
                ssum = jnp.sum(p, axis=-1, keepdims=True)
                o = jnp.dot(p.astype(bf16), v, preferred_element_type=f32) * (1.0 / ssum)
                o_ref[i, r0:r0 + SUB_Q, c0:c0 + HEAD_DIM_A] = o.astype(o_ref.dtype)
                m_tile = m if m_tile is None else jnp.where(lane >= h * LSE_REP, m, m_tile)
                s_tile = ssum if s_tile is None else jnp.where(lane >= h * LSE_REP, ssum, s_tile)
            lse_ref[i, r0:r0 + SUB_Q, :] = m_tile + jnp.log(s_tile)


def _attn_group(qkv, bias, g):
    n, L, _ = qkv.shape
    tq = min(ATTN_ROWS_PER_STEP, L)
    ns = ATTN_ROWS_PER_STEP // tq
    R = ATTN_RADIUS
    rb = tq // R
    last_rb = L // R - 1

    def main(col, width=GROUP_WIDTH):
        return pl.BlockSpec((ns, tq, width), lambda b, t: (b, t, col))

    def prev(col):
        return pl.BlockSpec((ns, R, GROUP_WIDTH), lambda b, t: (b, jnp.maximum(t * rb - 1, 0), col))

    def nxt(col):
        return pl.BlockSpec((ns, R, GROUP_WIDTH), lambda b, t: (b, jnp.minimum((t + 1) * rb, last_rb), col))

    return pl.pallas_call(
        functools.partial(_attn_kernel, tq=tq, seq=L, n_seq=ns),
        grid=(n // ns, L // tq),
        in_specs=[main(0), prev(1), main(1), nxt(1), prev(2), main(2), nxt(2),
                  pl.BlockSpec((1, HEADS_PER_GROUP, SUB_Q, SUB_K), lambda b, t: (g, 0, 0, 0))],
        out_specs=[main(0), main(0, LSE_LANES)],
        out_shape=[jax.ShapeDtypeStruct((n, L, GROUP_WIDTH), bf16),
                   jax.ShapeDtypeStruct((n, L, LSE_LANES), f32)],
        scratch_shapes=[pltpu.VMEM((ns, tq + 2 * R, GROUP_WIDTH), bf16),
                        pltpu.VMEM((ns, tq + 2 * R, GROUP_WIDTH), bf16)],
        compiler_params=_cparams("parallel", "arbitrary"),
        name=f"attn_g{g}",
    )(qkv, qkv, qkv, qkv, qkv, qkv, qkv, bias)


LRU_CHUNK = 256
LRU_PAD = 8
LRU_FINISH_ROWS = 512
GATE_BIAS_ROWS = 3
LRU_SEGS = 8
SEG_GAP = 4


def _lru_kernel(xb_ref, yb_ref, cw_ref, cb_ref, w_ref, lam_ref, o_ref,
                xpad, af, bf, ab, bb, htf, ptf, htb, ptb, cf_scr, cb_scr, *, seq):
    R = LRU_CHUNK
    P = LRU_PAD
    seg_len = seq // LRU_SEGS
    pitch = seg_len + SEG_GAP
    chunks_per_seg = seg_len // R
    n_chunks = seq // R
    xpad[0:P] = jnp.zeros((P, LRU_BW), f32)
    xpad[P + seq:] = jnp.zeros((P, LRU_BW), f32)
    xpad[P:P + seq] = xb_ref[0, 0].astype(f32)
    lam = lam_ref[...]
    log_a_unit = -LRU_C * (jnp.maximum(-lam, 0.0) + jnp.log1p(jnp.exp(-jnp.abs(lam))))
    cw = cw_ref[...]
    cb = cb_ref[...]
    row = lax.broadcasted_iota(jnp.int32, (R, LRU_BW), 0)
    lane = lax.broadcasted_iota(jnp.int32, (R, LRU_BW), 1)
    bias_cols = jnp.where(lane < GATE_BIAS_ROWS, 1.0, 0.0).astype(bf16)

    def chunk(ci, first=False, last=False):
        c0 = ci * R
        dst = (ci // chunks_per_seg) * pitch + (ci % chunks_per_seg) * R
        xc = (cw[0:1] * xpad[pl.ds(c0 + (P - 1), R), :] + cw[1:2] * xpad[pl.ds(c0 + P, R), :]
              + cw[2:3] * xpad[pl.ds(c0 + (P + 1), R), :] + cw[3:4] * xpad[pl.ds(c0 + (P + 2), R), :]) + cb
        lhs = jnp.concatenate([xc.astype(bf16), bias_cols], axis=1)
        th = jnp.tanh(jnp.dot(lhs, w_ref[0], preferred_element_type=f32))
        half_xc = 0.5 * xc
        for direction, (a_scr, b_scr) in enumerate(((af, bf), (ab, bb))):
            base = direction * 2 * LRU_BW
            half_log2_a = (0.5 * math.log2(math.e)) * log_a_unit[direction:direction + 1]
            a = jnp.exp2(half_log2_a * th[:, base:base + LRU_BW] + half_log2_a)
            gated_x = half_xc * th[:, base + LRU_BW:base + 2 * LRU_BW] + half_xc
            y = 1.0 - a * a
            mult = y * lax.rsqrt(jnp.maximum(y, 1e-30))
            if direction == 0 and first:
                mult = jnp.where(row == 0, 1.0, mult)
            if direction == 1 and last:
                mult = jnp.where(row == R - 1, 1.0, mult)
            a_scr[pl.ds(dst, R), :] = a
            b_scr[pl.ds(dst, R), :] = mult * gated_x

    for ci in range(n_chunks):
        chunk(ci, first=ci == 0, last=ci == n_chunks - 1)

    def scan(i, carry):
        hf, pf, hb, pb = carry
        rows = pl.ds(i, LRU_SEGS, stride=pitch)
        a = af[rows, :]
        hf = a * hf + bf[rows, :]
        pf = a * pf
        htf[rows, :] = hf
        ptf[rows, :] = pf
        rows = pl.ds(seg_len - 1 - i, LRU_SEGS, stride=pitch)
        a = ab[rows, :]
        hb = a * hb + bb[rows, :]
        pb = a * pb
        htb[rows, :] = hb
        ptb[rows, :] = pb
        return hf, pf, hb, pb

    zero = jnp.zeros((LRU_SEGS, LRU_BW), f32)
    one = jnp.ones((LRU_SEGS, LRU_BW), f32)
    hf, pf, hb, pb = lax.fori_loop(0, seg_len, scan, (zero, one, zero, one), unroll=8)

    c = jnp.zeros((1, LRU_BW), f32)
    cf_scr[0:1] = c
    for j in range(1, LRU_SEGS):
        c = hf[j - 1:j] + pf[j - 1:j] * c
        cf_scr[j:j + 1] = c
    c = jnp.zeros((1, LRU_BW), f32)
    cb_scr[LRU_SEGS - 1:LRU_SEGS] = c
    for j in range(LRU_SEGS - 2, -1, -1):
        c = hb[j + 1:j + 2] + pb[j + 1:j + 2] * c
        cb_scr[j:j + 1] = c

    F = LRU_FINISH_ROWS
    finish_per_seg = seg_len // F

    for ci in range(seq // F):
        c0 = ci * F
        seg = ci // finish_per_seg
        rows = pl.ds(seg * pitch + (ci % finish_per_seg) * F, F)
        h = (htf[rows, :] + ptf[rows, :] * cf_scr[seg:seg + 1, :]
             + htb[rows, :] + ptb[rows, :] * cb_scr[seg:seg + 1, :])
        o_ref[0, c0:c0 + F, :] = (h * yb_ref[0, 0, c0:c0 + F, :].astype(f32)).astype(o_ref.dtype)


def _pack_lru_gates(w, b):
    rows, rest = [], b
    for _ in range(GATE_BIAS_ROWS):
        piece = rest.astype(bf16)
        rows.append(piece)
        rest = rest - piece.astype(f32)
    bias_rows = jnp.pad(jnp.stack(rows, axis=1), ((0, 0), (0, LRU_BW - GATE_BIAS_ROWS), (0, 0)))
    return jnp.concatenate([w.astype(bf16), bias_rows], axis=1)


def _lru(proj3, conv_w, conv_b, w_gates, lam):
    B, _, S, _ = proj3.shape
    yb0 = REST_YB // LRU_BW
    return pl.pallas_call(
        functools.partial(_lru_kernel, seq=S),
        grid=(B, LRU_BLOCKS),
        in_specs=[
            pl.BlockSpec((1, 1, S, LRU_BW), lambda b, n: (b, n, 0, 0)),
            pl.BlockSpec((1, 1, S, LRU_BW), lambda b, n: (b, yb0 + n, 0, 0)),
            pl.BlockSpec((4, LRU_BW), lambda b, n: (0, n)),
            pl.BlockSpec((1, LRU_BW), lambda b, n: (0, n)),
            pl.BlockSpec((1, 2 * LRU_BW, 4 * LRU_BW), lambda b, n: (n, 0, 0)),
            pl.BlockSpec((2, LRU_BW), lambda b, n: (0, n)),
        ],
        out_specs=pl.BlockSpec((1, S, LRU_BW), lambda b, n: (b, 0, n)),
        out_shape=jax.ShapeDtypeStruct((B, S, LRU_WIDTH), bf16),
        scratch_shapes=([pltpu.VMEM((S + 2 * LRU_PAD, LRU_BW), f32)]
                        + [pltpu.VMEM((S + LRU_SEGS * SEG_GAP, LRU_BW), f32)] * 8
                        + [pltpu.VMEM((LRU_SEGS, LRU_BW), f32)] * 2),
        compiler_params=_cparams("parallel", "parallel"),
        name="lru",
    )(proj3, proj3, conv_w, conv_b, w_gates, lam)


def _mem_kv_kernel(m_ref, g_ref, w_ref, o_ref, h_scr):
    @pl.when(pl.program_id(0) == 0)
    def _():
        h_scr[...] = _rms(m_ref[...], g_ref[...]).astype(bf16)

    o_ref[...] = jnp.dot(h_scr[...], w_ref[...], preferred_element_type=f32).astype(o_ref.dtype)


def _mem_kv(mem2, gain, w, tn=512):
    M = mem2.shape[0]
    N = w.shape[1]
    return pl.pallas_call(
        _mem_kv_kernel,
        grid=(N // tn,),
        in_specs=[pl.BlockSpec((M, D_MODEL), lambda j: (0, 0)),
                  pl.BlockSpec((1, D_MODEL), lambda j: (0, 0)),
                  pl.BlockSpec((D_MODEL, tn), lambda j: (0, j))],
        out_specs=pl.BlockSpec((M, tn), lambda j: (0, j)),
        out_shape=jax.ShapeDtypeStruct((M, N), bf16),
        scratch_shapes=[pltpu.VMEM((M, D_MODEL), bf16)],
        compiler_params=_cparams("arbitrary"),
        name="mem_kv",
    )(mem2, gain, w)


def _xattn_kernel(q_ref, kv_ref, o_ref):
    per_head = MEM_HEAD_DIM // LANES
    for h in range(MEM_HEADS):
        c0 = h * MEM_HEAD_DIM
        k = kv_ref[0, :, c0:c0 + MEM_HEAD_DIM]
        v = kv_ref[0, :, MEM_WIDTH + c0:MEM_WIDTH + c0 + MEM_HEAD_DIM]
        q = jnp.concatenate([q_ref[0, h * per_head + s] for s in range(per_head)], axis=1)
        logits = lax.dot_general(q, k, (((1,), (1,)), ((), ())), preferred_element_type=f32)
        m = jnp.max(logits, axis=-1, keepdims=True)
        p = jnp.exp(logits - m)
        ssum = jnp.sum(p, axis=-1, keepdims=True)
        o = jnp.dot(p.astype(bf16), v, preferred_element_type=f32) * (1.0 / ssum)
        o_ref[0, :, c0:c0 + MEM_HEAD_DIM] = o.astype(o_ref.dtype)


def _xattn(proj3, kv3, tq=2048):
    B, _, S, _ = proj3.shape
    q_slabs = MEM_WIDTH // LANES
    return pl.pallas_call(
        _xattn_kernel,
        grid=(B, S // tq),
        in_specs=[pl.BlockSpec((1, q_slabs, tq, LANES), lambda b, t: (b, REST_QC // MEM_WIDTH, t, 0)),
                  pl.BlockSpec((1, N_MEM, 2 * MEM_WIDTH), lambda b, t: (b, 0, 0))],
        out_specs=pl.BlockSpec((1, tq, MEM_WIDTH), lambda b, t: (b, t, 0)),
        out_shape=jax.ShapeDtypeStruct((B, S, MEM_WIDTH), bf16),
        compiler_params=_cparams("parallel", "parallel"),
        name="xattn",
    )(proj3, kv3)


def _combine_kernel(o0_ref, o1_ref, o2_ref, l0_ref, l1_ref, l2_ref, ya_ref,
                    o1_scr, o2_scr, l1_scr, l2_scr, tmp_scr, *, tm):
    step = DEINTERLEAVE_STEP
    for g, o_ref, l_ref, o_scr, l_scr in ((1, o1_ref, l1_ref, o1_scr, l1_scr),
                                          (2, o2_ref, l2_ref, o2_scr, l2_scr)):
        d = ATTN_GROUPS[g][1]
        slabs = [(l_scr, lambda r: l_ref[0, r])]
        slabs += [(o_scr.at[h], lambda r, h=h: o_ref[0, r, :, h * HEAD_DIM_A:(h + 1) * HEAD_DIM_A].astype(f32))
                  for h in range(HEADS_PER_GROUP)]
        for k, (dst, rows_of) in enumerate(slabs):
            if d == step:
                for r in range(d):
                    dst[pl.ds(r, tm // d, stride=d), :] = rows_of(r)
                continue
            tmp = tmp_scr.at[k]
            for r in range(d):
                tmp[pl.ds((r % step) * (tm // step) + r // step, tm // d, stride=step), :] = rows_of(r)
            for lo in range(step):
                dst[pl.ds(lo, tm // step, stride=step), :] = tmp[lo * (tm // step):(lo + 1) * (tm // step), :]
    l0, l1, l2 = l0_ref[...], l1_scr[...], l2_scr[...]
    m = jnp.maximum(jnp.maximum(l0, l1), l2)
    e0, e1, e2 = jnp.exp(l0 - m), jnp.exp(l1 - m), jnp.exp(l2 - m)
    inv = 1.0 / (e0 + e1 + e2)
    for h in range(HEADS_PER_GROUP):
        c0 = h * HEAD_DIM_A
        lane = slice(h * LSE_REP, h * LSE_REP + 1)
        y = ((e0 * inv)[:, lane] * o0_ref[:, c0:c0 + HEAD_DIM_A].astype(f32)
             + (e1 * inv)[:, lane] * o1_scr[h] + (e2 * inv)[:, lane] * o2_scr[h])
        ya_ref[:, c0:c0 + HEAD_DIM_A] = y.astype(bf16)


def _combine(o_groups, lse_groups, seq, tm=1024):
    T = o_groups[0].shape[0]
    nt = seq // tm
    d1, d2 = ATTN_GROUPS[1][1], ATTN_GROUPS[2][1]

    def rows(width):
        return pl.BlockSpec((tm, width), lambda i: (i, 0))

    def strided_rows(d, width):
        return pl.BlockSpec((1, d, tm // d, width), lambda i: (i // nt, 0, i % nt, 0))

    return pl.pallas_call(
        functools.partial(_combine_kernel, tm=tm),
        grid=(T // tm,),
        in_specs=[rows(GROUP_WIDTH), strided_rows(d1, GROUP_WIDTH), strided_rows(d2, GROUP_WIDTH),
                  rows(LSE_LANES), strided_rows(d1, LSE_LANES), strided_rows(d2, LSE_LANES)],
        out_specs=rows(GROUP_WIDTH),
        out_shape=jax.ShapeDtypeStruct((T, GROUP_WIDTH), bf16),
        scratch_shapes=[pltpu.VMEM((HEADS_PER_GROUP, tm, HEAD_DIM_A), f32),
                        pltpu.VMEM((HEADS_PER_GROUP, tm, HEAD_DIM_A), f32),
                        pltpu.VMEM((tm, LSE_LANES), f32), pltpu.VMEM((tm, LSE_LANES), f32),
                        pltpu.VMEM((1 + HEADS_PER_GROUP, tm, LANES), f32)],
        compiler_params=_cparams("parallel"),
        name="combine",
    )(*o_groups, *lse_groups)


GATE_ROW_BLOCKS = 2


def _gate_mix_kernel(h_ref, ya_ref, yl_ref, yc_ref, wga_ref, wgb_ref, wgc_ref, bga_ref, bgb_ref, bgc_ref,
                     woa_ref, wol_ref, wom_ref, wu_ref, wd_ref, mix_ref, wu_o_ref, wd_o_ref):
    wu_o_ref[...] = wu_ref[...].astype(bf16)
    wd_o_ref[...] = wd_ref[...].astype(bf16)
    mb = h_ref.shape[0] // GATE_ROW_BLOCKS
    for k in range(GATE_ROW_BLOCKS):
        rows = slice(k * mb, (k + 1) * mb)
        h = h_ref[rows, :]

        def gate(w_ref, b_ref):
            return jax.nn.sigmoid(jnp.dot(h, w_ref[...], preferred_element_type=f32) + b_ref[...])

        mixed = (gate(wga_ref, bga_ref) * jnp.dot(ya_ref[rows, :], woa_ref[...], preferred_element_type=f32)
                 + gate(wgb_ref, bgb_ref) * jnp.dot(yl_ref[rows, :], wol_ref[...], preferred_element_type=f32)
                 + gate(wgc_ref, bgc_ref) * jnp.dot(yc_ref[rows, :], wom_ref[...], preferred_element_type=f32))
        mix_ref[rows, :] = mixed.astype(mix_ref.dtype)


def _gate_mix(h, y_a, y_lru, y_c, w_gate, b_gate, w_o_attn, w_o_lru, w_o_mem, w_up, w_down, tm=1024, tn=512):
    T = h.shape[0]
    nj = D_MODEL // tn
    ni = T // tm
    up_rows, down_rows = D_MODEL // (nj * ni), D_FF // (nj * ni)

    def slab(rows_per_step, width):
        return pl.BlockSpec((rows_per_step, width), lambda j, i: (j * ni + i, 0))

    def rows(width):
        return pl.BlockSpec((tm, width), lambda j, i: (i, 0))

    def gate_w(k):
        return pl.BlockSpec((D_MODEL, tn), lambda j, i: (0, k * nj + j))

    def gate_b(k):
        return pl.BlockSpec((1, tn), lambda j, i: (0, k * nj + j))

    def cols(width):
        return pl.BlockSpec((width, tn), lambda j, i: (0, j))

    return pl.pallas_call(
        _gate_mix_kernel,
        grid=(nj, T // tm),
        in_specs=[rows(D_MODEL), rows(GROUP_WIDTH), rows(LRU_WIDTH), rows(MEM_WIDTH),
                  gate_w(0), gate_w(1), gate_w(2), gate_b(0), gate_b(1), gate_b(2),
                  cols(GROUP_WIDTH), cols(LRU_WIDTH), cols(MEM_WIDTH),
                  slab(up_rows, D_FF), slab(down_rows, D_MODEL)],
        out_specs=[pl.BlockSpec((tm, tn), lambda j, i: (i, j)), slab(up_rows, D_FF), slab(down_rows, D_MODEL)],
        out_shape=[jax.ShapeDtypeStruct((T, D_MODEL), bf16),
                   jax.ShapeDtypeStruct((D_MODEL, D_FF), bf16),
                   jax.ShapeDtypeStruct((D_FF, D_MODEL), bf16)],
        compiler_params=_cparams("arbitrary", "arbitrary"),
        name="gate_mix",
    )(h, y_a, y_lru, y_c, w_gate, w_gate, w_gate, b_gate, b_gate, b_gate, w_o_attn, w_o_lru, w_o_mem,
      w_up, w_down)


def _mlp_kernel(x_ref, mix_ref, wo_ref, g_ref, gf_ref, wu_hbm, wd_hbm, out_ref,
                h_scr, wu_buf, wd_buf, sem, *, tf):
    i = pl.program_id(0)
    nj = D_FF // tf

    def chunk_copies(j, slot):
        start = pl.multiple_of(j * tf, tf)
        return (pltpu.make_async_copy(wu_hbm.at[:, pl.ds(start, tf)], wu_buf.at[slot], sem.at[0, slot]),
                pltpu.make_async_copy(wd_hbm.at[pl.ds(start, tf), :], wd_buf.at[slot], sem.at[1, slot]))

    @pl.when(i == 0)
    def _():
        for copy in chunk_copies(0, 0):
            copy.start()

    x = x_ref[...] + jnp.dot(mix_ref[...], wo_ref[...], preferred_element_type=f32)
    h_scr[...] = _rms(x, g_ref[...]).astype(bf16)
    out_ref[...] = x

    def chunk(j, carry):
        slot = j % 2

        @pl.when((j + 1 < nj) | (i + 1 < pl.num_programs(0)))
        def _():
            for copy in chunk_copies((j + 1) % nj, 1 - slot):
                copy.start()

        for copy in chunk_copies(j, slot):
            copy.wait()
        u = jnp.maximum(jnp.dot(h_scr[...], wu_buf[slot], preferred_element_type=f32), 0.0)
        out_ref[...] += jnp.dot((u * u).astype(bf16), wd_buf[slot], preferred_element_type=f32)
        return carry

    lax.fori_loop(0, nj, chunk, 0)
    out_ref[...] = _rms(out_ref[...], gf_ref[...])


def _mlp(x2, mixed, w_out, gain, gain_final, w_up, w_down, tm=512, tf=1024):
    T = x2.shape[0]
    assert (D_FF // tf) % 2 == 0, "chunk 0 of the next tile must land in the slot the last chunk does not use"
    return pl.pallas_call(
        functools.partial(_mlp_kernel, tf=tf),
        grid=(T // tm,),
        in_specs=[pl.BlockSpec((tm, D_MODEL), lambda i: (i, 0)),
                  pl.BlockSpec((tm, D_MODEL), lambda i: (i, 0)),
                  pl.BlockSpec((D_MODEL, D_MODEL), lambda i: (0, 0)),
                  pl.BlockSpec((1, D_MODEL), lambda i: (0, 0)),
                  pl.BlockSpec((1, D_MODEL), lambda i: (0, 0)),
                  pl.BlockSpec(memory_space=pl.ANY),
                  pl.BlockSpec(memory_space=pl.ANY)],
        out_specs=pl.BlockSpec((tm, D_MODEL), lambda i: (i, 0)),
        out_shape=jax.ShapeDtypeStruct((T, D_MODEL), f32),
        scratch_shapes=[pltpu.VMEM((tm, D_MODEL), bf16),
                        pltpu.VMEM((2, D_MODEL, tf), bf16), pltpu.VMEM((2, tf, D_MODEL), bf16),
                        pltpu.SemaphoreType.DMA((2, 2))],
        compiler_params=_cparams("arbitrary"),
        name="mlp",
    )(x2, mixed, w_out, gain, gain_final, w_up, w_down)


def _query_scale():
    scale = np.ones((1, N_IN), np.float32)
    scale[:, :WIDTH_A] = 1.0 / math.sqrt(HEAD_DIM_A)
    scale[:, COL_QC:] = 1.0 / math.sqrt(MEM_HEAD_DIM)
    return scale


def kernel(x, mem, rel_bias, norm_mix, norm_mem, norm_mlp, norm_final, w_in, w_gate, b_gate, conv_w, conv_b,
           lru_wa, lru_ba, lru_wi, lru_bi, lru_lambda, w_mem_kv, w_o_attn, w_o_lru, w_o_mem, w_out, w_up, w_down):
    B, S, D = x.shape
    T = B * S
    depth = w_in.shape[0]
    assert depth == 1, "the final RMSNorm is fused into the (single) layer's MLP kernel"
    x2 = x.reshape(T, D)
    mem2 = mem.reshape(B * N_MEM, D)
    for l in range(depth):
        scale = jnp.asarray(_query_scale())
        w_rest = _cast_rest_cols(w_in[l], scale)
        h, proj3, w_qkv = _rest_proj(x2, norm_mix[l].reshape(1, D), w_rest, w_in[l], scale, B, S)

        side_casts = ((w_gate[l],), (w_out[l], w_o_lru[l]), (w_o_attn[l], w_o_mem[l], w_mem_kv[l]))
        casted = []
        attn = []
        band_bias = _band_bias(rel_bias)
        for g in range(len(ATTN_GROUPS)):
            d = ATTN_GROUPS[g][1]
            qkv, *bf_copies = _qkv_proj(h, w_qkv, g, B, S, cast=side_casts[g])
            casted.append(bf_copies)
            o, lse = _attn_group(qkv.reshape(B * d, S // d, QKV_W), band_bias, g)
            if g == 0:
                attn.append((o.reshape(T, GROUP_WIDTH), lse.reshape(T, LSE_LANES)))
            else:
                attn.append((o.reshape(B, d, S // d, GROUP_WIDTH), lse.reshape(B, d, S // d, LSE_LANES)))

        w_gates = 0.5 * jnp.concatenate([lru_wa[l, 0], lru_wi[l, 0], lru_wa[l, 1], lru_wi[l, 1]], axis=-1)
        b_gates = 0.5 * jnp.concatenate([lru_ba[l, 0], lru_bi[l, 0], lru_ba[l, 1], lru_bi[l, 1]], axis=-1)
        y_lru = _lru(proj3, conv_w[l], conv_b[l].reshape(1, LRU_WIDTH), _pack_lru_gates(w_gates, b_gates),
                     lru_lambda[l])

        (w_gate_bf,), (w_out_bf, w_o_lru_bf), (w_o_attn_bf, w_o_mem_bf, w_mem_kv_bf) = casted
        kv = _mem_kv(mem2, norm_mem[l].reshape(1, D), w_mem_kv_bf)
        y_c = _xattn(proj3, kv.reshape(B, N_MEM, 2 * MEM_WIDTH))

        y_a = _combine([a[0] for a in attn], [a[1] for a in attn], S)
        mixed, w_up_bf, w_down_bf = _gate_mix(
            h, y_a, y_lru.reshape(T, LRU_WIDTH), y_c.reshape(T, MEM_WIDTH), w_gate_bf, b_gate[l].reshape(1, 3 * D),
            w_o_attn_bf, w_o_lru_bf, w_o_mem_bf, w_up[l], w_down[l])
        x2 = _mlp(x2, mixed, w_out_bf, norm_mlp[l].reshape(1, D), norm_final.reshape(1, D),
                  w_up_bf, w_down_bf)
    return x2.reshape(B, S, D)
```

```python
import functools
import math

import jax
import jax.numpy as jnp
import numpy as np
from jax import lax
from jax.experimental import pallas as pl
from jax.experimental.pallas import tpu as pltpu

D_MODEL = 2048
HEAD_DIM_A = 128
ATTN_GROUPS = ((128, 1), (512, 4), (2048, 16))
HEADS_PER_GROUP = 4
GROUP_WIDTH = HEADS_PER_GROUP * HEAD_DIM_A
WIDTH_A = len(ATTN_GROUPS) * GROUP_WIDTH
ATTN_RADIUS = 64
N_BUCKETS = 32
MAX_DISTANCE = 1024
LRU_WIDTH = 1536
LRU_BLOCKS = 12
LRU_BW = 128
LRU_C = 8.0
N_MEM = 256
MEM_HEADS = 4
MEM_HEAD_DIM = 256
MEM_WIDTH = MEM_HEADS * MEM_HEAD_DIM
D_FF = 4 * D_MODEL
EPS = 1e-6
N_IN = 3 * WIDTH_A + 2 * LRU_WIDTH + MEM_WIDTH
COL_XB = 3 * WIDTH_A
COL_QC = 3 * WIDTH_A + 2 * LRU_WIDTH
NEG_INF = -1e30

ATTN_ROWS_PER_STEP = 2048
SUB_Q = 128
SUB_K = SUB_Q + 2 * ATTN_RADIUS
LSE_LANES = 128
LSE_REP = LSE_LANES // HEADS_PER_GROUP

VMEM_LIMIT = 56 * 1024 * 1024

f32 = jnp.float32
bf16 = jnp.bfloat16


def _cparams(*sem):
    return pltpu.CompilerParams(dimension_semantics=sem, vmem_limit_bytes=VMEM_LIMIT)


def _rms(x, gain):
    return x * lax.rsqrt(jnp.mean(x * x, axis=-1, keepdims=True) + EPS) * gain


QKV_W = 3 * GROUP_WIDTH
REST_W = 2 * LRU_WIDTH + MEM_WIDTH
REST_YB = LRU_WIDTH
REST_QC = 2 * LRU_WIDTH
PROJ_TN = GROUP_WIDTH
LANES = 128
SLABS = PROJ_TN // LANES
PROJ_ROW_BLOCKS = 2
DEINTERLEAVE_STEP = 4
N_STAGE = 2


def _qkv_kernel(h_ref, wq_ref, wk_ref, wv_ref, *refs, d, tm, n_cast):
    cast_in, o_ref = refs[:n_cast], refs[n_cast]
    cast_out = refs[n_cast + 1:2 * n_cast + 1]
    res_scr, tmp_scr = refs[2 * n_cast + 1:]
    for src_ref, dst_ref in zip(cast_in, cast_out):
        dst_ref[...] = src_ref[...].astype(bf16)
    mb = tm // PROJ_ROW_BLOCKS
    n = 0
    for k in range(PROJ_ROW_BLOCKS):
        hk = h_ref[k * mb:(k + 1) * mb, :]
        for t, w_ref in enumerate((wq_ref, wk_ref, wv_ref)):
            res = jnp.dot(hk, w_ref[...], preferred_element_type=f32)
            col = t * PROJ_TN
            if d == 1:
                o_ref[0, k * mb:(k + 1) * mb, col:col + PROJ_TN] = res.astype(bf16)
                continue
            buf = n % N_STAGE
            n += 1
            for c in range(SLABS):
                res_scr[buf, c] = res[:, c * LANES:(c + 1) * LANES]
            src, step = res_scr, d
            if d == DEINTERLEAVE_STEP ** 2:
                step = DEINTERLEAVE_STEP
                for r in range(step):
                    for c in range(SLABS):
                        tmp_scr[buf, c, r * (mb // step):(r + 1) * (mb // step), :] = (
                            res_scr[buf, c, pl.ds(r, mb // step, stride=step), :])
                src = tmp_scr
            for r in range(d):
                start = r if src is res_scr else (r % step) * (mb // step) + r // step
                for c in range(SLABS):
                    o_ref[0, r, k * (mb // d):(k + 1) * (mb // d), col + c * LANES:col + (c + 1) * LANES] = (
                        src[buf, c, pl.ds(start, mb // d, stride=step), :].astype(bf16))


def _qkv_proj(h, w_qkv, g, batch, seq, cast=(), tm=1024):
    T = h.shape[0]
    d = ATTN_GROUPS[g][1]
    nt = seq // tm
    n_groups = len(ATTN_GROUPS)
    mb = tm // PROJ_ROW_BLOCKS
    steps = T // tm

    def w_spec(which):
        return pl.BlockSpec((D_MODEL, PROJ_TN), lambda i: (0, which * n_groups + g))

    cast_specs = [pl.BlockSpec((w.shape[0] // steps, w.shape[1]), lambda i: (i, 0)) for w in cast]

    if d == 1:
        out_spec = pl.BlockSpec((1, tm, QKV_W), lambda i: (i // nt, i % nt, 0))
        out_shape = jax.ShapeDtypeStruct((batch, seq, QKV_W), bf16)
    else:
        out_spec = pl.BlockSpec((1, d, tm // d, QKV_W), lambda i: (i // nt, 0, i % nt, 0))
        out_shape = jax.ShapeDtypeStruct((batch, d, seq // d, QKV_W), bf16)
    return pl.pallas_call(
        functools.partial(_qkv_kernel, d=d, tm=tm, n_cast=len(cast)),
        grid=(steps,),
        in_specs=[pl.BlockSpec((tm, D_MODEL), lambda i: (i, 0)), w_spec(0), w_spec(1), w_spec(2)] + cast_specs,
        out_specs=[out_spec] + cast_specs,
        out_shape=[out_shape] + [jax.ShapeDtypeStruct(w.shape, bf16) for w in cast],
        scratch_shapes=[pltpu.VMEM((N_STAGE, SLABS, mb, LANES), f32)] * 2,
        compiler_params=_cparams("parallel"),
        name=f"qkv_g{g}",
    )(h, w_qkv, w_qkv, w_qkv, *cast)


REST_TILES = REST_W // PROJ_TN
YB_TILES = range(REST_YB // PROJ_TN, REST_QC // PROJ_TN)


def _gelu_tanh(y):
    return y * (0.5 * (1.0 + jnp.tanh(math.sqrt(2.0 / math.pi) * (y + 0.044715 * (y * y * y)))))


def _cast_cols_kernel(w_ref, s_ref, o_ref):
    o_ref[...] = (w_ref[...] * s_ref[...]).astype(bf16)


def _cast_rest_cols(w_in, scale):
    first = COL_XB // PROJ_TN
    return pl.pallas_call(
        _cast_cols_kernel,
        grid=(REST_TILES,),
        in_specs=[pl.BlockSpec((D_MODEL, PROJ_TN), lambda c: (0, first + c)),
                  pl.BlockSpec((1, PROJ_TN), lambda c: (0, first + c))],
        out_specs=pl.BlockSpec((D_MODEL, PROJ_TN), lambda c: (0, c)),
        out_shape=jax.ShapeDtypeStruct((D_MODEL, REST_W), bf16),
        compiler_params=_cparams("parallel"),
        name="cast_rest_cols",
    )(w_in, scale)


def _rest_kernel(x_ref, g_ref, *refs):
    w_refs = refs[:REST_TILES]
    wq_ref, sc_ref, h_ref, o_ref, wq_o_ref = refs[REST_TILES:]
    wq_o_ref[...] = (wq_ref[...] * sc_ref[...]).astype(bf16)
    mb = x_ref.shape[0] // PROJ_ROW_BLOCKS
    for k in range(PROJ_ROW_BLOCKS):
        rows = slice(k * mb, (k + 1) * mb)
        h = _rms(x_ref[rows, :], g_ref[...]).astype(bf16)
        h_ref[rows, :] = h
        for c, w_ref in enumerate(w_refs):
            res = jnp.dot(h, w_ref[...], preferred_element_type=f32)
            if c in YB_TILES:
                res = _gelu_tanh(res)
            res = res.astype(bf16)
            for s in range(SLABS):
                o_ref[0, c * SLABS + s, rows, :] = res[:, s * LANES:(s + 1) * LANES]


def _rest_proj(x2, gain, w_rest, w_in, scale, batch, seq, tm=512):
    T = x2.shape[0]
    steps = T // tm
    nt = seq // tm
    slab_rows = D_MODEL // steps

    def w_spec(c):
        return pl.BlockSpec((D_MODEL, PROJ_TN), lambda i: (0, c), pipeline_mode=pl.Buffered(1))

    return pl.pallas_call(
        _rest_kernel,
        grid=(steps,),
        in_specs=[pl.BlockSpec((tm, D_MODEL), lambda i: (i, 0)),
                  pl.BlockSpec((1, D_MODEL), lambda i: (0, 0))]
        + [w_spec(c) for c in range(REST_TILES)]
        + [pl.BlockSpec((slab_rows, COL_XB), lambda i: (i, 0)),
           pl.BlockSpec((1, COL_XB), lambda i: (0, 0))],
        out_specs=[pl.BlockSpec((tm, D_MODEL), lambda i: (i, 0)),
                   pl.BlockSpec((1, REST_W // LANES, tm, LANES), lambda i: (i // nt, 0, i % nt, 0)),
                   pl.BlockSpec((slab_rows, COL_XB), lambda i: (i, 0))],
        out_shape=[jax.ShapeDtypeStruct((T, D_MODEL), bf16),
                   jax.ShapeDtypeStruct((batch, REST_W // LANES, seq, LANES), bf16),
                   jax.ShapeDtypeStruct((D_MODEL, COL_XB), bf16)],
        compiler_params=_cparams("arbitrary"),
        name="rest_proj",
    )(x2, gain, *([w_rest] * REST_TILES), w_in, scale)


def _t5_bucket(rel):
    nb = N_BUCKETS // 2
    max_exact = nb // 2
    sign = (rel > 0).astype(np.int32) * nb
    n = np.abs(rel)
    large = max_exact + (np.log(np.maximum(n, 1) / max_exact)
                         / np.log(MAX_DISTANCE / max_exact) * (nb - max_exact)).astype(np.int32)
    large = np.minimum(large, nb - 1)
    return (sign + np.where(n < max_exact, n, large)).astype(np.int32)


def _band_bias(rel_bias):
    qq = np.arange(SUB_Q)[:, None]
    kk = np.arange(SUB_K)[None, :]
    rel = kk - ATTN_RADIUS - qq
    onehot = np.stack([(_t5_bucket(rel * d)[None] == np.arange(N_BUCKETS)[:, None, None]).astype(np.float32)
                       for _, d in ATTN_GROUPS])
    table = rel_bias.astype(f32).reshape(N_BUCKETS, len(ATTN_GROUPS), HEADS_PER_GROUP)
    bias = jnp.einsum('ngh,gnqk->ghqk', table, onehot, precision=lax.Precision.HIGHEST)
    return bias + np.where(np.abs(rel) <= ATTN_RADIUS, 0.0, NEG_INF).astype(np.float32)[None, None]


def _attn_kernel(q_ref, kp_ref, km_ref, kn_ref, vp_ref, vm_ref, vn_ref, bias_ref,
                 o_ref, lse_ref, kbuf, vbuf, *, tq, seq, n_seq):
    R = ATTN_RADIUS
    q0 = pl.program_id(1) * tq
    lane = lax.broadcasted_iota(jnp.int32, (SUB_Q, LSE_LANES), 1)
    n_sub = tq // SUB_Q
    for i in range(n_seq):
        kbuf[i, 0:R] = kp_ref[i]
        kbuf[i, R:R + tq] = km_ref[i]
        kbuf[i, R + tq:] = kn_ref[i]
        vbuf[i, 0:R] = vp_ref[i]
        vbuf[i, R:R + tq] = vm_ref[i]
        vbuf[i, R + tq:] = vn_ref[i]
        for s in range(n_sub):
            r0 = s * SUB_Q
            edge = None
            if s == 0 or s == n_sub - 1:
                pos = q0 + (r0 - R) + lax.broadcasted_iota(jnp.int32, (1, SUB_K), 1)
                edge = jnp.where(pos >= 0, jnp.where(pos < seq, 0.0, NEG_INF), NEG_INF)
            m_tile = s_tile = None
            for h in range(HEADS_PER_GROUP):
                c0 = h * HEAD_DIM_A
                q = q_ref[i, r0:r0 + SUB_Q, c0:c0 + HEAD_DIM_A]
                k = kbuf[i, r0:r0 + SUB_K, c0:c0 + HEAD_DIM_A]
                v = vbuf[i, r0:r0 + SUB_K, c0:c0 + HEAD_DIM_A]
                logits = lax.dot_general(q, k, (((1,), (1,)), ((), ())), preferred_element_type=f32) + bias_ref[0, h]
                if edge is not None:
                    logits = logits + edge
                m = jnp.max(logits, axis=-1, keepdims=True)
                p = jnp.exp(logits - m)
                ssum = jnp.sum(p, axis=-1, keepdims=True)
                o = jnp.dot(p.astype(bf16), v, preferred_element_type=f32) * (1.0 / ssum)
                o_ref[i, r0:r0 + SUB_Q, c0:c0 + HEAD_DIM_A] = o.astype(o_ref.dtype)
                m_tile = m if m_tile is None else jnp.where(lane >= h * LSE_REP, m, m_tile)
                s_tile = ssum if s_tile is None else jnp.where(lane >= h * LSE_REP, ssum, s_tile)
            lse_ref[i, r0:r0 + SUB_Q, :] = m_tile + jnp.log(s_tile)


def _attn_group(qkv, bias, g):
    n, L, _ = qkv.shape
    tq = min(ATTN_ROWS_PER_STEP, L)
    ns = ATTN_ROWS_PER_STEP // tq
    R = ATTN_RADIUS
    rb = tq // R
    last_rb = L // R - 1

    def main(col, width=GROUP_WIDTH):
        return pl.BlockSpec((ns, tq, width), lambda b, t: (b, t, col))

    def prev(col):
        return pl.BlockSpec((ns, R, GROUP_WIDTH), lambda b, t: (b, jnp.maximum(t * rb - 1, 0), col))

    def nxt(col):
        return pl.BlockSpec((ns, R, GROUP_WIDTH), lambda b, t: (b, jnp.minimum((t + 1) * rb, last_rb), col))

    return pl.pallas_call(
        functools.partial(_attn_kernel, tq=tq, seq=L, n_seq=ns),
        grid=(n // ns, L // tq),
        in_specs=[main(0), prev(1), main(1), nxt(1), prev(2), main(2), nxt(2),
                  pl.BlockSpec((1, HEADS_PER_GROUP, SUB_Q, SUB_K), lambda b, t: (g, 0, 0, 0))],
        out_specs=[main(0), main(0, LSE_LANES)],
        out_shape=[jax.ShapeDtypeStruct((n, L, GROUP_WIDTH), bf16),
                   jax.ShapeDtypeStruct((n, L, LSE_LANES), f32)],
        scratch_shapes=[pltpu.VMEM((ns, tq + 2 * R, GROUP_WIDTH), bf16),
                        pltpu.VMEM((ns, tq + 2 * R, GROUP_WIDTH), bf16)],
        compiler_params=_cparams("parallel", "arbitrary"),
        name=f"attn_g{g}",
    )(qkv, qkv, qkv, qkv, qkv, qkv, qkv, bias)


LRU_CHUNK = 256
LRU_PAD = 8
LRU_FINISH_ROWS = 512
GATE_BIAS_ROWS = 3
LRU_SEGS = 8
SEG_GAP = 4


def _lru_kernel(xb_ref, yb_ref, cw_ref, cb_ref, w_ref, lam_ref, o_ref,
                xpad, af, bf, ab, bb, htf, ptf, htb, ptb, cf_scr, cb_scr, *, seq):
    R = LRU_CHUNK
    P = LRU_PAD
    seg_len = seq // LRU_SEGS
    pitch = seg_len + SEG_GAP
    chunks_per_seg = seg_len // R
    n_chunks = seq // R
    xpad[0:P] = jnp.zeros((P, LRU_BW), f32)
    xpad[P + seq:] = jnp.zeros((P, LRU_BW), f32)
    xpad[P:P + seq] = xb_ref[0, 0].astype(f32)
    lam = lam_ref[...]
    log_a_unit = -LRU_C * (jnp.maximum(-lam, 0.0) + jnp.log1p(jnp.exp(-jnp.abs(lam))))
    cw = cw_ref[...]
    cb = cb_ref[...]
    row = lax.broadcasted_iota(jnp.int32, (R, LRU_BW), 0)
    lane = lax.broadcasted_iota(jnp.int32, (R, LRU_BW), 1)
    bias_cols = jnp.where(lane < GATE_BIAS_ROWS, 1.0, 0.0).astype(bf16)

    def chunk(ci, first=False, last=False):
        c0 = ci * R
        dst = (ci // chunks_per_seg) * pitch + (ci % chunks_per_seg) * R
        xc = (cw[0:1] * xpad[pl.ds(c0 + (P - 1), R), :] + cw[1:2] * xpad[pl.ds(c0 + P, R), :]
              + cw[2:3] * xpad[pl.ds(c0 + (P + 1), R), :] + cw[3:4] * xpad[pl.ds(c0 + (P + 2), R), :]) + cb
        lhs = jnp.concatenate([xc.astype(bf16), bias_cols], axis=1)
        th = jnp.tanh(jnp.dot(lhs, w_ref[0], preferred_element_type=f32))
        half_xc = 0.5 * xc
        for direction, (a_scr, b_scr) in enumerate(((af, bf), (ab, bb))):
            base = direction * 2 * LRU_BW
            half_log2_a = (0.5 * math.log2(math.e)) * log_a_unit[direction:direction + 1]
            a = jnp.exp2(half_log2_a * th[:, base:base + LRU_BW] + half_log2_a)
            gated_x = half_xc * th[:, base + LRU_BW:base + 2 * LRU_BW] + half_xc
            y = 1.0 - a * a
            mult = y * lax.rsqrt(jnp.maximum(y, 1e-30))
            if direction == 0 and first:
                mult = jnp.where(row == 0, 1.0, mult)
            if direction == 1 and last:
                mult = jnp.where(row == R - 1, 1.0, mult)
            a_scr[pl.ds(dst, R), :] = a
            b_scr[pl.ds(dst, R), :] = mult * gated_x

    for ci in range(n_chunks):
        chunk(ci, first=ci == 0, last=ci == n_chunks - 1)

    def scan(i, carry):
        hf, pf, hb, pb = carry
        rows = pl.ds(i, LRU_SEGS, stride=pitch)
        a = af[rows, :]
        hf = a * hf + bf[rows, :]
        pf = a * pf
        htf[rows, :] = hf
        ptf[rows, :] = pf
        rows = pl.ds(seg_len - 1 - i, LRU_SEGS, stride=pitch)
        a = ab[rows, :]
        hb = a * hb + bb[rows, :]
        pb = a * pb
        htb[rows, :] = hb
        ptb[rows, :] = pb
        return hf, pf, hb, pb

    zero = jnp.zeros((LRU_SEGS, LRU_BW), f32)
    one = jnp.ones((LRU_SEGS, LRU_BW), f32)
    hf, pf, hb, pb = lax.fori_loop(0, seg_len, scan, (zero, one, zero, one), unroll=8)

    c = jnp.zeros((1, LRU_BW), f32)
    cf_scr[0:1] = c
    for j in range(1, LRU_SEGS):
        c = hf[j - 1:j] + pf[j - 1:j] * c
        cf_scr[j:j + 1] = c
    c = jnp.zeros((1, LRU_BW), f32)
    cb_scr[LRU_SEGS - 1:LRU_SEGS] = c
    for j in range(LRU_SEGS - 2, -1, -1):
        c = hb[j + 1:j + 2] + pb[j + 1:j + 2] * c
        cb_scr[j:j + 1] = c

    F = LRU_FINISH_ROWS
    finish_per_seg = seg_len // F

    for ci in range(seq // F):
        c0 = ci * F
        seg = ci // finish_per_seg
        rows = pl.ds(seg * pitch + (ci % finish_per_seg) * F, F)
        h = (htf[rows, :] + ptf[rows, :] * cf_scr[seg:seg + 1, :]
             + htb[rows, :] + ptb[rows, :] * cb_scr[seg:seg + 1, :])
        o_ref[0, c0:c0 + F, :] = (h * yb_ref[0, 0, c0:c0 + F, :].astype(f32)).astype(o_ref.dtype)


def _pack_lru_gates(w, b):
    rows, rest = [], b
    for _ in range(GATE_BIAS_ROWS):
        piece = rest.astype(bf16)
        rows.append(piece)
        rest = rest - piece.astype(f32)
    bias_rows = jnp.pad(jnp.stack(rows, axis=1), ((0, 0), (0, LRU_BW - GATE_BIAS_ROWS), (0, 0)))
    return jnp.concatenate([w.astype(bf16), bias_rows], axis=1)


def _lru(proj3, conv_w, conv_b, w_gates, lam):
    B, _, S, _ = proj3.shape
    yb0 = REST_YB // LRU_BW
    return pl.pallas_call(
        functools.partial(_lru_kernel, seq=S),
        grid=(B, LRU_BLOCKS),
        in_specs=[
            pl.BlockSpec((1, 1, S, LRU_BW), lambda b, n: (b, n, 0, 0)),
            pl.BlockSpec((1, 1, S, LRU_BW), lambda b, n: (b, yb0 + n, 0, 0)),
            pl.BlockSpec((4, LRU_BW), lambda b, n: (0, n)),
            pl.BlockSpec((1, LRU_BW), lambda b, n: (0, n)),
            pl.BlockSpec((1, 2 * LRU_BW, 4 * LRU_BW), lambda b, n: (n, 0, 0)),
            pl.BlockSpec((2, LRU_BW), lambda b, n: (0, n)),
        ],
        out_specs=pl.BlockSpec((1, S, LRU_BW), lambda b, n: (b, 0, n)),
        out_shape=jax.ShapeDtypeStruct((B, S, LRU_WIDTH), bf16),
        scratch_shapes=([pltpu.VMEM((S + 2 * LRU_PAD, LRU_BW), f32)]
                        + [pltpu.VMEM((S + LRU_SEGS * SEG_GAP, LRU_BW), f32)] * 8
                        + [pltpu.VMEM((LRU_SEGS, LRU_BW), f32)] * 2),
        compiler_params=_cparams("parallel", "parallel"),
        name="lru",
    )(proj3, proj3, conv_w, conv_b, w_gates, lam)


def _mem_kv_kernel(m_ref, g_ref, w_ref, o_ref, h_scr):
    @pl.when(pl.program_id(0) == 0)
    def _():
        h_scr[...] = _rms(m_ref[...], g_ref[...]).astype(bf16)

    o_ref[...] = jnp.dot(h_scr[...], w_ref[...], preferred_element_type=f32).astype(o_ref.dtype)


def _mem_kv(mem2, gain, w, tn=512):
    M = mem2.shape[0]
    N = w.shape[1]
    return pl.pallas_call(
        _mem_kv_kernel,
        grid=(N // tn,),
        in_specs=[pl.BlockSpec((M, D_MODEL), lambda j: (0, 0)),
                  pl.BlockSpec((1, D_MODEL), lambda j: (0, 0)),
                  pl.BlockSpec((D_MODEL, tn), lambda j: (0, j))],
        out_specs=pl.BlockSpec((M, tn), lambda j: (0, j)),
        out_shape=jax.ShapeDtypeStruct((M, N), bf16),
        scratch_shapes=[pltpu.VMEM((M, D_MODEL), bf16)],
        compiler_params=_cparams("arbitrary"),
        name="mem_kv",
    )(mem2, gain, w)


def _xattn_kernel(q_ref, kv_ref, o_ref):
    per_head = MEM_HEAD_DIM // LANES
    for h in range(MEM_HEADS):
        c0 = h * MEM_HEAD_DIM
        k = kv_ref[0, :, c0:c0 + MEM_HEAD_DIM]
        v = kv_ref[0, :, MEM_WIDTH + c0:MEM_WIDTH + c0 + MEM_HEAD_DIM]
        q = jnp.concatenate([q_ref[0, h * per_head + s] for s in range(per_head)], axis=1)
        logits = lax.dot_general(q, k, (((1,), (1,)), ((), ())), preferred_element_type=f32)
        m = jnp.max(logits, axis=-1, keepdims=True)
        p = jnp.exp(logits - m)
        ssum = jnp.sum(p, axis=-1, keepdims=True)
        o = jnp.dot(p.astype(bf16), v, preferred_element_type=f32) * (1.0 / ssum)
        o_ref[0, :, c0:c0 + MEM_HEAD_DIM] = o.astype(o_ref.dtype)


def _xattn(proj3, kv3, tq=2048):
    B, _, S, _ = proj3.shape
    q_slabs = MEM_WIDTH // LANES
    return pl.pallas_call(
        _xattn_kernel,
        grid=(B, S // tq),
        in_specs=[pl.BlockSpec((1, q_slabs, tq, LANES), lambda b, t: (b, REST_QC // MEM_WIDTH, t, 0)),
                  pl.BlockSpec((1, N_MEM, 2 * MEM_WIDTH), lambda b, t: (b, 0, 0))],
        out_specs=pl.BlockSpec((1, tq, MEM_WIDTH), lambda b, t: (b, t, 0)),
        out_shape=jax.ShapeDtypeStruct((B, S, MEM_WIDTH), bf16),
        compiler_params=_cparams("parallel", "parallel"),
        name="xattn",
    )(proj3, kv3)


def _combine_kernel(o0_ref, o1_ref, o2_ref, l0_ref, l1_ref, l2_ref, ya_ref,
                    o1_scr, o2_scr, l1_scr, l2_scr, tmp_scr, *, tm):
    step = DEINTERLEAVE_STEP
    for g, o_ref, l_ref, o_scr, l_scr in ((1, o1_ref, l1_ref, o1_scr, l1_scr),
                                          (2, o2_ref, l2_ref, o2_scr, l2_scr)):
        d = ATTN_GROUPS[g][1]
        slabs = [(l_scr, lambda r: l_ref[0, r])]
        slabs += [(o_scr.at[h], lambda r, h=h: o_ref[0, r, :, h * HEAD_DIM_A:(h + 1) * HEAD_DIM_A].astype(f32))
                  for h in range(HEADS_PER_GROUP)]
        for k, (dst, rows_of) in enumerate(slabs):
            if d == step:
                for r in range(d):
                    dst[pl.ds(r, tm // d, stride=d), :] = rows_of(r)
                continue
            tmp = tmp_scr.at[k]
            for r in range(d):
                tmp[pl.ds((r % step) * (tm // step) + r // step, tm // d, stride=step), :] = rows_of(r)
            for lo in range(step):
                dst[pl.ds(lo, tm // step, stride=step), :] = tmp[lo * (tm // step):(lo + 1) * (tm // step), :]
    l0, l1, l2 = l0_ref[...], l1_scr[...], l2_scr[...]
    m = jnp.maximum(jnp.maximum(l0, l1), l2)
    e0, e1, e2 = jnp.exp(l0 - m), jnp.exp(l1 - m), jnp.exp(l2 - m)
    inv = 1.0 / (e0 + e1 + e2)
    for h in range(HEADS_PER_GROUP):
        c0 = h * HEAD_DIM_A
        lane = slice(h * LSE_REP, h * LSE_REP + 1)
        y = ((e0 * inv)[:, lane] * o0_ref[:, c0:c0 + HEAD_DIM_A].astype(f32)
             + (e1 * inv)[:, lane] * o1_scr[h] + (e2 * inv)[:, lane] * o2_scr[h])
        ya_ref[:, c0:c0 + HEAD_DIM_A] = y.astype(bf16)


def _combine(o_groups, lse_groups, seq, tm=1024):
    T = o_groups[0].shape[0]
    nt = seq // tm
    d1, d2 = ATTN_GROUPS[1][1], ATTN_GROUPS[2][1]

    def rows(width):
        return pl.BlockSpec((tm, width), lambda i: (i, 0))

    def strided_rows(d, width):
        return pl.BlockSpec((1, d, tm // d, width), lambda i: (i // nt, 0, i % nt, 0))

    return pl.pallas_call(
        functools.partial(_combine_kernel, tm=tm),
        grid=(T // tm,),
        in_specs=[rows(GROUP_WIDTH), strided_rows(d1, GROUP_WIDTH), strided_rows(d2, GROUP_WIDTH),
                  rows(LSE_LANES), strided_rows(d1, LSE_LANES), strided_rows(d2, LSE_LANES)],
        out_specs=rows(GROUP_WIDTH),
        out_shape=jax.ShapeDtypeStruct((T, GROUP_WIDTH), bf16),
        scratch_shapes=[pltpu.VMEM((HEADS_PER_GROUP, tm, HEAD_DIM_A), f32),
                        pltpu.VMEM((HEADS_PER_GROUP, tm, HEAD_DIM_A), f32),
                        pltpu.VMEM((tm, LSE_LANES), f32), pltpu.VMEM((tm, LSE_LANES), f32),
                        pltpu.VMEM((1 + HEADS_PER_GROUP, tm, LANES), f32)],
        compiler_params=_cparams("parallel"),
        name="combine",
    )(*o_groups, *lse_groups)


GATE_ROW_BLOCKS = 2


def _gate_mix_kernel(h_ref, ya_ref, yl_ref, yc_ref, wga_ref, wgb_ref, wgc_ref, bga_ref, bgb_ref, bgc_ref,
                     woa_ref, wol_ref, wom_ref, wu_ref, wd_ref, mix_ref, wu_o_ref, wd_o_ref):
    wu_o_ref[...] = wu_ref[...].astype(bf16)
    wd_o_ref[...] = wd_ref[...].astype(bf16)
    mb = h_ref.shape[0] // GATE_ROW_BLOCKS
    for k in range(GATE_ROW_BLOCKS):
        rows = slice(k * mb, (k + 1) * mb)
        h = h_ref[rows, :]

        def gate(w_ref, b_ref):
            return jax.nn.sigmoid(jnp.dot(h, w_ref[...], preferred_element_type=f32) + b_ref[...])

        mixed = (gate(wga_ref, bga_ref) * jnp.dot(ya_ref[rows, :], woa_ref[...], preferred_element_type=f32)
                 + gate(wgb_ref, bgb_ref) * jnp.dot(yl_ref[rows, :], wol_ref[...], preferred_element_type=f32)
                 + gate(wgc_ref, bgc_ref) * jnp.dot(yc_ref[rows, :], wom_ref[...], preferred_element_type=f32))
        mix_ref[rows, :] = mixed.astype(mix_ref.dtype)


def _gate_mix(h, y_a, y_lru, y_c, w_gate, b_gate, w_o_attn, w_o_lru, w_o_mem, w_up, w_down, tm=1024, tn=512):
    T = h.shape[0]
    nj = D_MODEL // tn
    ni = T // tm
    up_rows, down_rows = D_MODEL // (nj * ni), D_FF // (nj * ni)

    def slab(rows_per_step, width):
        return pl.BlockSpec((rows_per_step, width), lambda j, i: (j * ni + i, 0))

    def rows(width):
        return pl.BlockSpec((tm, width), lambda j, i: (i, 0))

    def gate_w(k):
        return pl.BlockSpec((D_MODEL, tn), lambda j, i: (0, k * nj + j))

    def gate_b(k):
        return pl.BlockSpec((1, tn), lambda j, i: (0, k * nj + j))

    def cols(width):
        return pl.BlockSpec((width, tn), lambda j, i: (0, j))

    return pl.pallas_call(
        _gate_mix_kernel,
        grid=(nj, T // tm),
        in_specs=[rows(D_MODEL), rows(GROUP_WIDTH), rows(LRU_WIDTH), rows(MEM_WIDTH),
                  gate_w(0), gate_w(1), gate_w(2), gate_b(0), gate_b(1), gate_b(2),
                  cols(GROUP_WIDTH), cols(LRU_WIDTH), cols(MEM_WIDTH),
                  slab(up_rows, D_FF), slab(down_rows, D_MODEL)],
        out_specs=[pl.BlockSpec((tm, tn), lambda j, i: (i, j)), slab(up_rows, D_FF), slab(down_rows, D_MODEL)],
        out_shape=[jax.ShapeDtypeStruct((T, D_MODEL), bf16),
                   jax.ShapeDtypeStruct((D_MODEL, D_FF), bf16),
                   jax.ShapeDtypeStruct((D_FF, D_MODEL), bf16)],
        compiler_params=_cparams("arbitrary", "arbitrary"),
        name="gate_mix",
    )(h, y_a, y_lru, y_c, w_gate, w_gate, w_gate, b_gate, b_gate, b_gate, w_o_attn, w_o_lru, w_o_mem,
      w_up, w_down)


def _mlp_kernel(x_ref, mix_ref, wo_ref, g_ref, gf_ref, wu_hbm, wd_hbm, out_ref,
                h_scr, wu_buf, wd_buf, sem, *, tf):
    i = pl.program_id(0)
    nj = D_FF // tf

    def chunk_copies(j, slot):
        start = pl.multiple_of(j * tf, tf)
        return (pltpu.make_async_copy(wu_hbm.at[:, pl.ds(start, tf)], wu_buf.at[slot], sem.at[0, slot]),
                pltpu.make_async_copy(wd_hbm.at[pl.ds(start, tf), :], wd_buf.at[slot], sem.at[1, slot]))

    @pl.when(i == 0)
    def _():
        for prio, copy in enumerate(chunk_copies(0, 0)):
            copy.start(priority=prio)

    x = x_ref[...] + jnp.dot(mix_ref[...], wo_ref[...], preferred_element_type=f32)
    h_scr[...] = _rms(x, g_ref[...]).astype(bf16)
    out_ref[...] = x

    def chunk(j, carry):
        slot = j % 2

        @pl.when((j + 1 < nj) | (i + 1 < pl.num_programs(0)))
        def _():
            for prio, copy in enumerate(chunk_copies((j + 1) % nj, 1 - slot)):
                copy.start(priority=prio)

        for copy in chunk_copies(j, slot):
            copy.wait()
        u = jnp.maximum(jnp.dot(h_scr[...], wu_buf[slot], preferred_element_type=f32), 0.0)
        out_ref[...] += jnp.dot((u * u).astype(bf16), wd_buf[slot], preferred_element_type=f32)
        return carry

    lax.fori_loop(0, nj, chunk, 0)
    out_ref[...] = _rms(out_ref[...], gf_ref[...])


def _mlp(x2, mixed, w_out, gain, gain_final, w_up, w_down, tm=512, tf=1024):
    T = x2.shape[0]
    assert (D_FF // tf) % 2 == 0, "chunk 0 of the next tile must land in the slot the last chunk does not use"
    return pl.pallas_call(
        functools.partial(_mlp_kernel, tf=tf),
        grid=(T // tm,),
        in_specs=[pl.BlockSpec((tm, D_MODEL), lambda i: (i, 0)),
                  pl.BlockSpec((tm, D_MODEL), lambda i: (i, 0)),
                  pl.BlockSpec((D_MODEL, D_MODEL), lambda i: (0, 0)),
                  pl.BlockSpec((1, D_MODEL), lambda i: (0, 0)),
                  pl.BlockSpec((1, D_MODEL), lambda i: (0, 0)),
                  pl.BlockSpec(memory_space=pl.ANY),
                  pl.BlockSpec(memory_space=pl.ANY)],
        out_specs=pl.BlockSpec((tm, D_MODEL), lambda i: (i, 0)),
        out_shape=jax.ShapeDtypeStruct((T, D_MODEL), f32),
        scratch_shapes=[pltpu.VMEM((tm, D_MODEL), bf16),
                        pltpu.VMEM((2, D_MODEL, tf), bf16), pltpu.VMEM((2, tf, D_MODEL), bf16),
                        pltpu.SemaphoreType.DMA((2, 2))],
        compiler_params=_cparams("arbitrary"),
        name="mlp",
    )(x2, mixed, w_out, gain, gain_final, w_up, w_down)


def _query_scale():
    scale = np.ones((1, N_IN), np.float32)
    scale[:, :WIDTH_A] = 1.0 / math.sqrt(HEAD_DIM_A)
    scale[:, COL_QC:] = 1.0 / math.sqrt(MEM_HEAD_DIM)
    return scale


def kernel(x, mem, rel_bias, norm_mix, norm_mem, norm_mlp, norm_final, w_in, w_gate, b_gate, conv_w, conv_b,
           lru_wa, lru_ba, lru_wi, lru_bi, lru_lambda, w_mem_kv, w_o_attn, w_o_lru, w_o_mem, w_out, w_up, w_down):
    B, S, D = x.shape
    T = B * S
    depth = w_in.shape[0]
    assert depth == 1, "the final RMSNorm is fused into the (single) layer's MLP kernel"
    x2 = x.reshape(T, D)
    mem2 = mem.reshape(B * N_MEM, D)
    for l in range(depth):
        scale = jnp.asarray(_query_scale())
        w_rest = _cast_rest_cols(w_in[l], scale)
        h, proj3, w_qkv = _rest_proj(x2, norm_mix[l].reshape(1, D), w_rest, w_in[l], scale, B, S)

        side_casts = ((w_gate[l],), (w_out[l], w_o_lru[l]), (w_o_attn[l], w_o_mem[l], w_mem_kv[l]))
        casted = []
        attn = []
        band_bias = _band_bias(rel_bias)
        for g in range(len(ATTN_GROUPS)):
            d = ATTN_GROUPS[g][1]
            qkv, *bf_copies = _qkv_proj(h, w_qkv, g, B, S, cast=side_casts[g])
            casted.append(bf_copies)
            o, lse = _attn_group(qkv.reshape(B * d, S // d, QKV_W), band_bias, g)
            if g == 0:
                attn.append((o.reshape(T, GROUP_WIDTH), lse.reshape(T, LSE_LANES)))
            else:
                attn.append((o.reshape(B, d, S // d, GROUP_WIDTH), lse.reshape(B, d, S // d, LSE_LANES)))

        w_gates = 0.5 * jnp.concatenate([lru_wa[l, 0], lru_wi[l, 0], lru_wa[l, 1], lru_wi[l, 1]], axis=-1)
        b_gates = 0.5 * jnp.concatenate([lru_ba[l, 0], lru_bi[l, 0], lru_ba[l, 1], lru_bi[l, 1]], axis=-1)
        y_lru = _lru(proj3, conv_w[l], conv_b[l].reshape(1, LRU_WIDTH), _pack_lru_gates(w_gates, b_gates),
                     lru_lambda[l])

        (w_gate_bf,), (w_out_bf, w_o_lru_bf), (w_o_attn_bf, w_o_mem_bf, w_mem_kv_bf) = casted
        kv = _mem_kv(mem2, norm_mem[l].reshape(1, D), w_mem_kv_bf)
        y_c = _xattn(proj3, kv.reshape(B, N_MEM, 2 * MEM_WIDTH))

        y_a = _combine([a[0] for a in attn], [a[1] for a in attn], S)
        mixed, w_up_bf, w_down_bf = _gate_mix(
            h, y_a, y_lru.reshape(T, LRU_WIDTH), y_c.reshape(T, MEM_WIDTH), w_gate_bf, b_gate[l].reshape(1, 3 * D),
            w_o_attn_bf, w_o_lru_bf, w_o_mem_bf, w_up[l], w_down[l])
        x2 = _mlp(x2, mixed, w_out_bf, norm_mlp[l].reshape(1, D), norm_final.reshape(1, D),
                  w_up_bf, w_down_bf)
    return x2.reshape(B, S, D)
```
